```python
import math
import jax
import jax.numpy as jnp
from jax import lax
import numpy as np

D_MODEL = 1024
BATCH = 8
SEQ = 2048
DEPTH = 2

N_MIXERS = 2
EPS = 1e-6

GDN_QK_HEADS = 8
GDN_V_HEADS = 16
GDN_HEAD_DIM = 128
GDN_CONV = 4
GDN_CHUNK = 64
GDN_QK_W = GDN_QK_HEADS * GDN_HEAD_DIM
GDN_V_W = GDN_V_HEADS * GDN_HEAD_DIM
GDN_CONV_W = 2 * GDN_QK_W + GDN_V_W
GDN_IN = GDN_CONV_W + GDN_V_W + 2 * GDN_V_HEADS

NSA_HEADS = 16
NSA_GROUPS = 4
NSA_HPG = NSA_HEADS // NSA_GROUPS
NSA_HEAD_DIM = 64
NSA_CMP_LEN = 32
NSA_CMP_STRIDE = 16
NSA_SLC_LEN = 64
NSA_TOP_K = 8
NSA_WINDOW = 512
NSA_QBLOCK = 128
NSA_Q_W = NSA_HEADS * NSA_HEAD_DIM
NSA_KV_W = NSA_GROUPS * NSA_HEAD_DIM
NSA_IN = 2 * NSA_Q_W + 6 * NSA_KV_W + 3 * NSA_HEADS

REL_BUCKETS = 32
REL_MAX_DIST = 128
NEG_INF = -1e30

kernel_name = 'hybrid_gdn_nsa_adaln'


def rms_norm(x, g):
    xf = x.astype(jnp.float32)
    y = xf * lax.rsqrt(jnp.mean(xf * xf, axis=-1, keepdims=True) + EPS)
    return (y * g.astype(jnp.float32)).astype(x.dtype)


def l2norm(x):
    return x * lax.rsqrt(jnp.sum(x * x, axis=-1, keepdims=True) + EPS)


def rel_bucket(dist):
    n = jnp.maximum(dist, 0)
    max_exact = REL_BUCKETS // 2
    nf = jnp.maximum(n, 1).astype(jnp.float32)
    large = max_exact + (jnp.log(nf / max_exact) / math.log(REL_MAX_DIST / max_exact)
                         * (REL_BUCKETS - max_exact)).astype(jnp.int32)
    large = jnp.minimum(large, REL_BUCKETS - 1)
    return jnp.where(n < max_exact, n, large)


def head_bias(rel_bias, dist):
    b = rel_bias[rel_bucket(dist)]
    return jnp.transpose(b, (2, 0, 1)).reshape(NSA_GROUPS, NSA_HPG, dist.shape[0], dist.shape[1])


def causal_depthwise_conv(x, w):
    kw, ch = w.shape
    return lax.conv_general_dilated(x, w[:, None, :].astype(x.dtype), window_strides=(1,),
                                    padding=[(kw - 1, 0)], dimension_numbers=('NWC', 'WIO', 'NWC'),
                                    feature_group_count=ch)


def gated_deltanet(h, w_in, conv_w, a_log, dt_bias, norm_w, w_out):
    bsz, t, _ = h.shape
    nh, dh, cs = GDN_V_HEADS, GDN_HEAD_DIM, GDN_CHUNK
    nc = t // cs
    f32 = jnp.float32
    proj = h @ w_in
    qkv, z, b_lin, a_lin = jnp.split(proj, [GDN_CONV_W, GDN_CONV_W + GDN_V_W, GDN_CONV_W + GDN_V_W + nh], axis=-1)
    qkv = jax.nn.silu(causal_depthwise_conv(qkv, conv_w)).astype(f32)
    q, k, v = jnp.split(qkv, [GDN_QK_W, 2 * GDN_QK_W], axis=-1)
    rep = nh // GDN_QK_HEADS
    q = jnp.repeat(l2norm(q.reshape(bsz, t, GDN_QK_HEADS, dh)), rep, axis=2) * (dh ** -0.5)
    k = jnp.repeat(l2norm(k.reshape(bsz, t, GDN_QK_HEADS, dh)), rep, axis=2)
    v = v.reshape(bsz, t, nh, dh)
    beta = jax.nn.sigmoid(b_lin.astype(f32))
    g = -jnp.exp(a_log.astype(f32)) * jax.nn.softplus(a_lin.astype(f32) + dt_bias.astype(f32))
    to_chunks = lambda u: u.reshape(bsz, nc, cs, nh, -1).transpose(0, 3, 1, 2, 4)
    q, k, v = to_chunks(q), to_chunks(k), to_chunks(v)
    beta = to_chunks(beta)[..., 0]
    gc = jnp.cumsum(to_chunks(g)[..., 0], axis=-1)
    idx = jnp.arange(cs)
    lower = idx[:, None] >= idx[None, :]
    strict = idx[:, None] > idx[None, :]
    decay = jnp.exp(jnp.where(lower, gc[..., :, None] - gc[..., None, :], -jnp.inf))
    kb = k * beta[..., None]
    a_mat = jnp.where(strict, jnp.einsum('bhncd,bhnsd->bhncs', kb, k) * decay, 0.0)
    eye = jnp.eye(cs, dtype=f32)
    l_mat = a_mat + eye
    t_mat = lax.linalg.triangular_solve(l_mat, jnp.broadcast_to(eye, l_mat.shape), left_side=True,
                                        lower=True, unit_diagonal=True)
    u = jnp.einsum('bhncs,bhnsd->bhncd', t_mat, v * beta[..., None])
    w = jnp.einsum('bhncs,bhnsd->bhncd', t_mat, kb * jnp.exp(gc)[..., None])
    qk = jnp.where(lower, jnp.einsum('bhncd,bhnsd->bhncs', q, k) * decay, 0.0)
    g_last = gc[..., -1]
    k_dec = k * jnp.exp(g_last[..., None] - gc)[..., None]
    q_dec = q * jnp.exp(gc)[..., None]
    xs = tuple(jnp.moveaxis(a, 2, 0) for a in (q_dec, qk, u, w, k_dec, jnp.exp(g_last)))

    def step(state, inp):
        qd, qkc, uc, wc, kd, gl = inp
        v_new = uc - jnp.einsum('bhcd,bhde->bhce', wc, state)
        o = jnp.einsum('bhcd,bhde->bhce', qd, state) + jnp.einsum('bhcs,bhse->bhce', qkc, v_new)
        state = state * gl[..., None, None] + jnp.einsum('bhcd,bhce->bhde', kd, v_new)
        return state, o

    s0 = jnp.zeros((bsz, nh, dh, dh), f32)
    _, o = lax.scan(step, s0, xs)
    o = o.transpose(1, 0, 3, 2, 4).reshape(bsz, t, nh, dh)
    o = rms_norm(o, norm_w) * jax.nn.silu(z.reshape(bsz, t, nh, dh).astype(f32))
    return o.reshape(bsz, t, nh * dh).astype(h.dtype) @ w_out


def native_sparse_attention(h, w_in, cmp_pos, cmp_w1, cmp_w2, rel_bias, w_out):
    bsz, t, _ = h.shape
    nh, ng, hpg, dh = NSA_HEADS, NSA_GROUPS, NSA_HPG, NSA_HEAD_DIM
    f32 = jnp.float32
    proj = h @ w_in
    q = proj[..., :NSA_Q_W].reshape(bsz, t, ng, hpg, dh).transpose(0, 2, 3, 1, 4) * (dh ** -0.5)
    kv = proj[..., NSA_Q_W:NSA_Q_W + 6 * NSA_KV_W].reshape(bsz, t, 6, ng, dh)
    kc, vc, ks, vs, kw, vw = [kv[:, :, i] for i in range(6)]
    off = NSA_Q_W + 6 * NSA_KV_W
    gates = jax.nn.sigmoid(proj[..., off:off + 3 * nh].astype(f32)).reshape(bsz, t, ng, hpg, 3)
    gates = gates.transpose(0, 2, 3, 1, 4)
    z = proj[..., off + 3 * nh:]

    n_cmp = (t - NSA_CMP_LEN) // NSA_CMP_STRIDE + 1
    tok = jnp.arange(n_cmp)[:, None] * NSA_CMP_STRIDE + jnp.arange(NSA_CMP_LEN)[None, :]

    def compress(xk, i):
        blk = xk[:, tok] + cmp_pos[i][None, None, :, None, :]
        blk = blk.transpose(0, 1, 3, 2, 4).reshape(bsz, n_cmp, ng, NSA_CMP_LEN * dh)
        return (jax.nn.silu(blk @ cmp_w1[i]) @ cmp_w2[i]).transpose(0, 2, 1, 3)

    k_cmp = compress(kc, 0)
    v_cmp = compress(vc, 1)
    cmp_end = jnp.arange(n_cmp) * NSA_CMP_STRIDE + NSA_CMP_LEN - 1

    n_slc = t // NSA_SLC_LEN
    k_sel = min(NSA_TOP_K, n_slc)
    c_start = jnp.arange(n_cmp)[:, None] * NSA_CMP_STRIDE
    s_start = jnp.arange(n_slc)[None, :] * NSA_SLC_LEN
    overlap = (jnp.clip(jnp.minimum(c_start + NSA_CMP_LEN, s_start + NSA_SLC_LEN) - jnp.maximum(c_start, s_start), 0)
               .astype(f32) / NSA_CMP_LEN)
    ks_blk = ks.reshape(bsz, n_slc, NSA_SLC_LEN, ng, dh).transpose(0, 3, 1, 2, 4)
    vs_blk = vs.reshape(bsz, n_slc, NSA_SLC_LEN, ng, dh).transpose(0, 3, 1, 2, 4)
    gather = jax.vmap(jax.vmap(lambda kb, ix: kb[ix]))
    tbl = rel_bias.T.reshape(ng, hpg, REL_BUCKETS)
    gi = jnp.arange(ng)[None, :, None, None, None]
    hi = jnp.arange(hpg)[None, None, :, None, None]
    blk_id = jnp.arange(n_slc)

    pad = ((0, 0), (0, 0), (NSA_WINDOW, 0), (0, 0))
    kw_p = jnp.pad(kw.transpose(0, 2, 1, 3), pad)
    vw_p = jnp.pad(vw.transpose(0, 2, 1, 3), pad)

    def query_block(bi):
        q0 = bi * NSA_QBLOCK
        qb = lax.dynamic_slice_in_dim(q, q0, NSA_QBLOCK, axis=3)
        gb = lax.dynamic_slice_in_dim(gates, q0, NSA_QBLOCK, axis=3)
        tq = q0 + jnp.arange(NSA_QBLOCK)
        dist = tq[:, None] - cmp_end[None, :]
        valid = dist >= 0
        s = jnp.einsum('bghqd,bgkd->bghqk', qb, k_cmp).astype(f32) + head_bias(rel_bias, dist)
        s = jnp.where(valid, s, NEG_INF)
        p_cmp = jax.nn.softmax(s, axis=-1) * jnp.any(valid, axis=-1)[:, None].astype(f32)
        o_cmp = jnp.einsum('bghqk,bgkd->bghqd', p_cmp.astype(v_cmp.dtype), v_cmp)
        imp = jnp.einsum('bghqk,ks->bgqs', p_cmp, overlap)
        cur = tq // NSA_SLC_LEN
        forced = (blk_id[None, :] == 0) | (blk_id[None, :] == cur[:, None]) | (blk_id[None, :] == cur[:, None] - 1)
        causal_blk = blk_id[None, :] * NSA_SLC_LEN <= tq[:, None]
        imp = jnp.where(forced, jnp.inf, jnp.where(causal_blk, imp, -jnp.inf))
        _, sel = lax.top_k(imp, k_sel)
        n_s = k_sel * NSA_SLC_LEN
        k_g = gather(ks_blk, sel).reshape(bsz, ng, NSA_QBLOCK, n_s, dh)
        v_g = gather(vs_blk, sel).reshape(bsz, ng, NSA_QBLOCK, n_s, dh)
        pos = (sel[..., None] * NSA_SLC_LEN + jnp.arange(NSA_SLC_LEN)).reshape(bsz, ng, NSA_QBLOCK, n_s)
        dist_s = tq[None, None, :, None] - pos
        s = jnp.einsum('bghqd,bgqsd->bghqs', qb, k_g).astype(f32) + tbl[gi, hi, rel_bucket(dist_s)[:, :, None]]
        s = jnp.where((dist_s >= 0)[:, :, None], s, NEG_INF)
        o_slc = jnp.einsum('bghqs,bgqsd->bghqd', jax.nn.softmax(s, axis=-1).astype(v_g.dtype), v_g)
        kwb = lax.dynamic_slice_in_dim(kw_p, q0, NSA_WINDOW + NSA_QBLOCK, axis=2)
        vwb = lax.dynamic_slice_in_dim(vw_p, q0, NSA_WINDOW + NSA_QBLOCK, axis=2)
        tk = q0 - NSA_WINDOW + jnp.arange(NSA_WINDOW + NSA_QBLOCK)
        dist_w = tq[:, None] - tk[None, :]
        valid_w = (dist_w >= 0) & (dist_w < NSA_WINDOW) & (tk[None, :] >= 0)
        s = jnp.einsum('bghqd,bgkd->bghqk', qb, kwb).astype(f32) + head_bias(rel_bias, dist_w)
        s = jnp.where(valid_w, s, NEG_INF)
        o_win = jnp.einsum('bghqk,bgkd->bghqd', jax.nn.softmax(s, axis=-1).astype(vwb.dtype), vwb)
        return (gb[..., 0:1] * o_cmp.astype(f32) + gb[..., 1:2] * o_slc.astype(f32)
                + gb[..., 2:3] * o_win.astype(f32))

    outs = lax.map(query_block, jnp.arange(t // NSA_QBLOCK))
    o = outs.transpose(1, 0, 4, 2, 3, 5).reshape(bsz, t, nh * dh)
    o = o * jax.nn.silu(z.astype(f32))
    return o.astype(h.dtype) @ w_out


def setup_inputs(seed: int = 0) -> dict:
    key = jax.random.key(seed)
    ks = jax.random.split(key, 20)
    n_a = (DEPTH + N_MIXERS - 1) // N_MIXERS
    n_b = DEPTH // N_MIXERS
    nrm = lambda k, shape, scale: jax.random.normal(k, shape, jnp.float32) * scale
    dt = jnp.exp(jax.random.uniform(ks[7], (n_a, GDN_V_HEADS), jnp.float32, math.log(1e-3), math.log(1e-1)))
    return {
        'x': nrm(ks[0], (BATCH, SEQ, D_MODEL), 1.0),
        'c': nrm(ks[1], (BATCH, D_MODEL), 1.0),
        'ada_w': nrm(ks[2], (DEPTH, D_MODEL, 3 * D_MODEL), D_MODEL ** -0.5),
        'ada_b': nrm(ks[3], (DEPTH, 3 * D_MODEL), 0.01),
        'norm_g': 1.0 + nrm(ks[4], (DEPTH, D_MODEL), 0.05),
        'gdn_w_in': nrm(ks[5], (n_a, D_MODEL, GDN_IN), D_MODEL ** -0.5),
        'gdn_conv_w': nrm(ks[6], (n_a, GDN_CONV, GDN_CONV_W), GDN_CONV ** -0.5),
        'gdn_a_log': jnp.log(jax.random.uniform(ks[8], (n_a, GDN_V_HEADS), jnp.float32, 1.0, 16.0)),
        'gdn_dt_bias': dt + jnp.log(-jnp.expm1(-dt)),
        'gdn_norm_w': 1.0 + nrm(ks[9], (n_a, GDN_HEAD_DIM), 0.05),
        'gdn_w_out': nrm(ks[10], (n_a, GDN_V_W, D_MODEL), GDN_V_W ** -0.5),
        'nsa_w_in': nrm(ks[11], (n_b, D_MODEL, NSA_IN), D_MODEL ** -0.5),
        'nsa_cmp_pos': nrm(ks[12], (n_b, 2, NSA_CMP_LEN, NSA_HEAD_DIM), 0.1),
        'nsa_cmp_w1': nrm(ks[13], (n_b, 2, NSA_CMP_LEN * NSA_HEAD_DIM, NSA_HEAD_DIM), (NSA_CMP_LEN * NSA_HEAD_DIM) ** -0.5),
        'nsa_cmp_w2': nrm(ks[14], (n_b, 2, NSA_HEAD_DIM, NSA_HEAD_DIM), NSA_HEAD_DIM ** -0.5),
        'nsa_w_out': nrm(ks[15], (n_b, NSA_Q_W, D_MODEL), NSA_Q_W ** -0.5),
        'rel_bias': nrm(ks[16], (REL_BUCKETS, NSA_HEADS), 0.5),
        'final_g': 1.0 + nrm(ks[17], (D_MODEL,), 0.05),
    }


def reference(x, c, ada_w, ada_b, norm_g, gdn_w_in, gdn_conv_w, gdn_a_log, gdn_dt_bias, gdn_norm_w, gdn_w_out,
              nsa_w_in, nsa_cmp_pos, nsa_cmp_w1, nsa_cmp_w2, nsa_w_out, rel_bias, final_g):
    cond = jax.nn.silu(c)
    for i in range(DEPTH):
        mod = cond @ ada_w[i] + ada_b[i]
        shift, scale, gate = jnp.split(mod, 3, axis=-1)
        hdn = rms_norm(x, norm_g[i]) * (1.0 + scale[:, None, :]) + shift[:, None, :]
        j = i // N_MIXERS
        if i % N_MIXERS == 0:
            y = gated_deltanet(hdn, gdn_w_in[j], gdn_conv_w[j], gdn_a_log[j], gdn_dt_bias[j], gdn_norm_w[j], gdn_w_out[j])
        else:
            y = native_sparse_attention(hdn, nsa_w_in[j], nsa_cmp_pos[j], nsa_cmp_w1[j], nsa_cmp_w2[j], rel_bias, nsa_w_out[j])
        x = x + gate[:, None, :] * y
    return rms_norm(x, final_g)
```

```python
import functools
import math

import numpy as np
import jax
import jax.numpy as jnp
from jax import lax
from jax.experimental import pallas as pl
from jax.experimental.pallas import tpu as pltpu

F32 = jnp.float32
BF16 = jnp.bfloat16
HIGHEST = lax.Precision.HIGHEST

EPS = 1e-6
NEG_INF = -1e30
LANES = 128
VMEM_LIMIT = 56 * 1024 * 1024

GDN_QK_HEADS = 8
GDN_V_HEADS = 16
GDN_HEAD_DIM = 128
GDN_CONV = 4
GDN_CHUNK = 64
GDN_QK_W = GDN_QK_HEADS * GDN_HEAD_DIM
GDN_V_W = GDN_V_HEADS * GDN_HEAD_DIM
GDN_CONV_W = 2 * GDN_QK_W + GDN_V_W
GDN_TILE = 256


def _silu(x):
    return x * jax.nn.sigmoid(x)


def _dot(a, b, **kw):
    return jnp.dot(a, b, preferred_element_type=F32, **kw)


def _dot_nt(a, b, **kw):
    return lax.dot_general(a, b, (((1,), (1,)), ((), ())), preferred_element_type=F32, **kw)


def _dot_tn(a, b, **kw):
    return lax.dot_general(a, b, (((0,), (0,)), ((), ())), preferred_element_type=F32, **kw)


def _mod_kernel(c_ref, w_ref, b_ref, o_ref):
    cond = _silu(c_ref[...])
    o_ref[0] = _dot(cond, w_ref[0], precision=HIGHEST) + b_ref[0]


def _modulation(c, ada_w, ada_b):
    depth, d, d3 = ada_w.shape
    bsz = c.shape[0]
    return pl.pallas_call(
        _mod_kernel,
        grid=(depth, d3 // d),
        in_specs=[
            pl.BlockSpec((bsz, d), lambda i, j: (0, 0)),
            pl.BlockSpec((1, d, d), lambda i, j: (i, 0, j)),
            pl.BlockSpec((1, 1, d), lambda i, j: (i, 0, j)),
        ],
        out_specs=pl.BlockSpec((1, bsz, d), lambda i, j: (i, 0, j)),
        out_shape=jax.ShapeDtypeStruct((depth, bsz, d3), F32),
        name="adaln_mod",
    )(c, ada_w, ada_b.reshape(depth, 1, d3))


def _inproj_kernel(x_ref, g_ref, mod_ref, w_ref, o_ref, *, tn):
    x = x_ref[0]
    m = mod_ref[0]
    y = x * lax.rsqrt(jnp.mean(x * x, axis=-1, keepdims=True) + EPS) * g_ref[...]
    h = (y * (1.0 + m[1:2]) + m[0:1]).astype(BF16)
    for j in range(w_ref.shape[1] // tn):
        o_ref[0, :, j * tn:(j + 1) * tn] = _dot(h, w_ref[:, j * tn:(j + 1) * tn])


def _inproj(x, g, mod, w, *, tm, tn):
    bsz, t, d = x.shape
    n = w.shape[1]
    assert t % tm == 0 and n % tn == 0
    return pl.pallas_call(
        functools.partial(_inproj_kernel, tn=tn),
        grid=(bsz, t // tm),
        in_specs=[
            pl.BlockSpec((1, tm, d), lambda b, i: (b, i, 0)),
            pl.BlockSpec((1, d), lambda b, i: (0, 0)),
            pl.BlockSpec((1, 3, d), lambda b, i: (b, 0, 0)),
            pl.BlockSpec((d, n), lambda b, i: (0, 0), pipeline_mode=pl.Buffered(1)),
        ],
        out_specs=pl.BlockSpec((1, tm, n), lambda b, i: (b, i, 0)),
        out_shape=jax.ShapeDtypeStruct((bsz, t, n), F32),
        compiler_params=pltpu.CompilerParams(
            dimension_semantics=("parallel", "parallel"), vmem_limit_bytes=VMEM_LIMIT),
        name="norm_mod_inproj",
    )(x, g.reshape(1, d), mod, w)


def _outproj_kernel(o_ref, w_ref, x_ref, mod_ref, *rest, final_norm):
    y = _dot(o_ref[0], w_ref[...])
    x = x_ref[0] + mod_ref[0][2:3] * y
    if final_norm:
        fg_ref, out_ref = rest
        x = x * lax.rsqrt(jnp.mean(x * x, axis=-1, keepdims=True) + EPS) * fg_ref[...]
    else:
        (out_ref,) = rest
    out_ref[0] = x


def _outproj(o, w, x, mod, final_g=None, *, tm):
    bsz, t, d = x.shape
    k = o.shape[-1]
    in_specs = [
        pl.BlockSpec((1, tm, k), lambda b, i: (b, i, 0)),
        pl.BlockSpec((k, d), lambda b, i: (0, 0), pipeline_mode=pl.Buffered(1)),
        pl.BlockSpec((1, tm, d), lambda b, i: (b, i, 0)),
        pl.BlockSpec((1, 3, d), lambda b, i: (b, 0, 0)),
    ]
    args = [o, w, x, mod]
    if final_g is not None:
        in_specs.append(pl.BlockSpec((1, d), lambda b, i: (0, 0)))
        args.append(final_g.reshape(1, d))
    return pl.pallas_call(
        functools.partial(_outproj_kernel, final_norm=final_g is not None),
        grid=(bsz, t // tm),
        in_specs=in_specs,
        out_specs=pl.BlockSpec((1, tm, d), lambda b, i: (b, i, 0)),
        out_shape=jax.ShapeDtypeStruct((bsz, t, d), F32),
        compiler_params=pltpu.CompilerParams(
            dimension_semantics=("parallel", "parallel"), vmem_limit_bytes=VMEM_LIMIT),
        name="outproj_residual",
    )(*args)


def _gdn_gates_kernel(ba_ref, alog_ref, dtb_ref, col_ref, row_ref):
    cs = GDN_CHUNK
    ba = ba_ref[0]
    lane = lax.broadcasted_iota(jnp.int32, ba.shape, 1)
    g = -jnp.exp(alog_ref[...]) * jax.nn.softplus(ba + dtb_ref[...])
    vals = jnp.where(lane < GDN_V_HEADS, jax.nn.sigmoid(ba), g)
    r = lax.broadcasted_iota(jnp.int32, (cs, cs), 0)
    c = lax.broadcasted_iota(jnp.int32, (cs, cs), 1)
    tri = (r >= c).astype(F32)
    is_beta = lax.broadcasted_iota(jnp.int32, (cs, LANES), 1) < GDN_V_HEADS
    for n in range(ba.shape[0] // cs):
        v = vals[n * cs:(n + 1) * cs]
        cum = _dot(tri, v, precision=HIGHEST)
        out = jnp.where(is_beta, v, cum)
        col_ref[0, n * cs:(n + 1) * cs, :] = out
        row_ref[0, n] = out.T[:2 * GDN_V_HEADS, :]


def _gdn_gates(proj, a_log, dt_bias, *, tt=512):
    bsz, t, n = proj.shape
    nc = t // GDN_CHUNK
    pad = lambda u: jnp.zeros((1, LANES), F32).at[0, GDN_V_HEADS:2 * GDN_V_HEADS].set(u)
    ba_blk = (n - LANES) // LANES
    return pl.pallas_call(
        _gdn_gates_kernel,
        grid=(bsz, t // tt),
        in_specs=[
            pl.BlockSpec((1, tt, LANES), lambda b, i: (b, i, ba_blk)),
            pl.BlockSpec((1, LANES), lambda b, i: (0, 0)),
            pl.BlockSpec((1, LANES), lambda b, i: (0, 0)),
        ],
        out_specs=[
            pl.BlockSpec((1, tt, LANES), lambda b, i: (b, i, 0)),
            pl.BlockSpec((1, tt // GDN_CHUNK, 2 * GDN_V_HEADS, GDN_CHUNK), lambda b, i: (b, i, 0, 0)),
        ],
        out_shape=[
            jax.ShapeDtypeStruct((bsz, t, LANES), F32),
            jax.ShapeDtypeStruct((bsz, nc, 2 * GDN_V_HEADS, GDN_CHUNK), F32),
        ],
        name="gdn_gates",
    )(proj, pad(a_log), pad(dt_bias))


def _tri_inverse(a):
    n = a.shape[0]
    r = lax.broadcasted_iota(jnp.int32, (n, n), 0)
    c = lax.broadcasted_iota(jnp.int32, (n, n), 1)
    p = jnp.where(r == c, 1.0, 0.0) - a
    m = a
    k = 1
    while 2 * k < n:
        m = _dot(m, m, precision=HIGHEST)
        p = p + _dot(p, m, precision=HIGHEST)
        k *= 2
    return p


def _gdn_chunk_kernel(q_ref, k_ref, v_ref, z_ref, col_ref, row_ref, wq_ref, wk_ref, wv_ref, nw_ref,
                      o_ref, s_ref, qx_ref, kx_ref, vx_ref):
    cs, dh, nh = GDN_CHUNK, GDN_HEAD_DIM, GDN_V_HEADS
    tt = q_ref.shape[1]
    hq = pl.program_id(1)
    ti = pl.program_id(2)

    @pl.when(ti == 0)
    def _():
        s_ref[...] = jnp.zeros_like(s_ref)
        qx_ref[0:8, :] = jnp.zeros((8, dh), F32)
        kx_ref[0:8, :] = jnp.zeros((8, dh), F32)
        vx_ref[0:8, :] = jnp.zeros((8, 2 * dh), F32)

    def conv_silu(x_ref, ext_ref, w_ref):
        ext_ref[8:8 + tt, :] = x_ref[0]
        w = w_ref[...]
        acc = w[0:1] * ext_ref[5:5 + tt, :]
        for j in range(1, GDN_CONV):
            acc = acc + w[j:j + 1] * ext_ref[5 + j:5 + j + tt, :]
        ext_ref[0:8, :] = ext_ref[tt:tt + 8, :]
        return _silu(acc)

    def l2n(x):
        return x * lax.rsqrt(jnp.sum(x * x, axis=-1, keepdims=True) + EPS)

    q = l2n(conv_silu(q_ref, qx_ref, wq_ref)) * (dh ** -0.5)
    k = l2n(conv_silu(k_ref, kx_ref, wk_ref))
    v = conv_silu(v_ref, vx_ref, wv_ref)

    r = lax.broadcasted_iota(jnp.int32, (cs, cs), 0)
    c = lax.broadcasted_iota(jnp.int32, (cs, cs), 1)
    lower = r >= c
    strict = r > c
    lane = lax.broadcasted_iota(jnp.int32, (cs, LANES), 1)
    nw = nw_ref[...]

    for n in range(tt // cs):
        sl = slice(n * cs, (n + 1) * cs)
        qc, kc = q[sl], k[sl]
        qb, kb16 = qc.astype(BF16), kc.astype(BF16)
        kk = _dot_nt(kb16, kb16)
        qk = _dot_nt(qb, kb16)
        colv = col_ref[0, sl, :]
        for e in range(2):
            h = 2 * hq + e
            beta = jnp.sum(jnp.where(lane == h, colv, 0.0), axis=1, keepdims=True)
            gc = jnp.sum(jnp.where(lane == nh + h, colv, 0.0), axis=1, keepdims=True)
            gc_row = row_ref[0, n, pl.ds(nh + h, 1), :]
            g_last = gc_row[:, cs - 1:cs]
            decay = jnp.exp(jnp.where(lower, gc - gc_row, -jnp.inf))
            a_mat = jnp.where(strict, beta * kk * decay, 0.0)
            t_mat = _tri_inverse(a_mat).astype(BF16)
            kbeta = kc * beta
            ve = v[sl, e * dh:(e + 1) * dh]
            rhs = jnp.concatenate([ve * beta, kbeta * jnp.exp(gc)], axis=1).astype(BF16)
            uw = _dot(t_mat, rhs)
            u, w = uw[:, :dh], uw[:, dh:]
            state = s_ref[e]
            s16 = state.astype(BF16)
            q_dec = (qc * jnp.exp(gc)).astype(BF16)
            v_new = u - _dot(w.astype(BF16), s16)
            v16 = v_new.astype(BF16)
            o = _dot(q_dec, s16) + _dot((qk * decay).astype(BF16), v16)
            k_dec = (kc * jnp.exp(g_last - gc)).astype(BF16)
            s_ref[e] = state * jnp.exp(g_last) + _dot_tn(k_dec, v16)
            o = o * lax.rsqrt(jnp.mean(o * o, axis=-1, keepdims=True) + EPS) * nw
            o = o * _silu(z_ref[0, sl, e * dh:(e + 1) * dh])
            o_ref[0, sl, e * dh:(e + 1) * dh] = o.astype(o_ref.dtype)


def _gdn_chunk(proj, col, row, conv_w, norm_w, *, tt=GDN_TILE):
    bsz, t, _ = proj.shape
    dh = GDN_HEAD_DIM
    qk_blocks = GDN_QK_W // dh
    v_blk0 = 2 * GDN_QK_W // (2 * dh)
    z_blk0 = GDN_CONV_W // (2 * dh)
    ncb = tt // GDN_CHUNK
    return pl.pallas_call(
        _gdn_chunk_kernel,
        grid=(bsz, GDN_QK_HEADS, t // tt),
        in_specs=[
            pl.BlockSpec((1, tt, dh), lambda b, h, i: (b, i, h)),
            pl.BlockSpec((1, tt, dh), lambda b, h, i: (b, i, qk_blocks + h)),
            pl.BlockSpec((1, tt, 2 * dh), lambda b, h, i: (b, i, v_blk0 + h)),
            pl.BlockSpec((1, tt, 2 * dh), lambda b, h, i: (b, i, z_blk0 + h)),
            pl.BlockSpec((1, tt, LANES), lambda b, h, i: (b, i, 0)),
            pl.BlockSpec((1, ncb, 2 * GDN_V_HEADS, GDN_CHUNK), lambda b, h, i: (b, i, 0, 0)),
            pl.BlockSpec((GDN_CONV, dh), lambda b, h, i: (0, h)),
            pl.BlockSpec((GDN_CONV, dh), lambda b, h, i: (0, qk_blocks + h)),
            pl.BlockSpec((GDN_CONV, 2 * dh), lambda b, h, i: (0, v_blk0 + h)),
            pl.BlockSpec((1, dh), lambda b, h, i: (0, 0)),
        ],
        out_specs=pl.BlockSpec((1, tt, 2 * dh), lambda b, h, i: (b, i, h)),
        out_shape=jax.ShapeDtypeStruct((bsz, t, GDN_V_W), BF16),
        scratch_shapes=[
            pltpu.VMEM((2, dh, dh), F32),
            pltpu.VMEM((tt + 8, dh), F32),
            pltpu.VMEM((tt + 8, dh), F32),
            pltpu.VMEM((tt + 8, 2 * dh), F32),
        ],
        compiler_params=pltpu.CompilerParams(
            dimension_semantics=("parallel", "parallel", "arbitrary"), vmem_limit_bytes=VMEM_LIMIT),
        name="gdn_chunk_scan",
    )(proj, proj, proj, proj, col, row, conv_w, conv_w, conv_w, norm_w.reshape(1, dh))


def _gdn_layer(x, mod, norm_g, w_in, conv_w, a_log, dt_bias, norm_w, w_out):
    n_in = w_in.shape[1]
    n_pad = -(-n_in // (7 * LANES)) * (7 * LANES)
    w_in_p = jnp.pad(w_in, ((0, 0), (0, n_pad - n_in))).astype(BF16)
    proj = _inproj(x, norm_g, mod, w_in_p, tm=256, tn=7 * LANES)
    col, row = _gdn_gates(proj, a_log, dt_bias)
    o = _gdn_chunk(proj, col, row, conv_w, norm_w)
    return _outproj(o, w_out.astype(BF16), x, mod, tm=512)


NSA_HEADS = 16
NSA_GROUPS = 4
NSA_HPG = NSA_HEADS // NSA_GROUPS
NSA_HEAD_DIM = 64
NSA_CMP_LEN = 32
NSA_CMP_STRIDE = 16
NSA_SLC_LEN = 64
NSA_TOP_K = 8
NSA_WINDOW = 512
NSA_QBLOCK = 128
NSA_Q_W = NSA_HEADS * NSA_HEAD_DIM
NSA_KV_W = NSA_GROUPS * NSA_HEAD_DIM
REL_BUCKETS = 32
REL_MAX_DIST = 128
FEAT_LANE0 = NSA_HEAD_DIM
CONST_LANE0 = FEAT_LANE0 + 32
NSA_COL_Q = 0
NSA_COL_CMP = NSA_Q_W
NSA_COL_SEL = NSA_COL_CMP + 2 * NSA_KV_W
NSA_COL_WIN = NSA_COL_SEL + 2 * NSA_KV_W
NSA_COL_Z = NSA_COL_WIN + 2 * NSA_KV_W
NSA_COL_GATE = NSA_COL_Z + NSA_Q_W
NSA_PROJ_W = NSA_COL_GATE + LANES


def _nsa_column_perm():
    g, dh = NSA_GROUPS, NSA_HEAD_DIM
    kv0 = NSA_Q_W
    cols = list(range(NSA_Q_W))
    cols += [kv0 + i for i in range(2 * NSA_KV_W)]
    for br in (1, 2):
        for gi in range(g):
            cols += [kv0 + (2 * br) * NSA_KV_W + gi * dh + d for d in range(dh)]
            cols += [kv0 + (2 * br + 1) * NSA_KV_W + gi * dh + d for d in range(dh)]
    gate0 = kv0 + 6 * NSA_KV_W
    cols += [gate0 + 3 * NSA_HEADS + i for i in range(NSA_Q_W)]
    cols += [gate0 + i for i in range(3 * NSA_HEADS)] + [-1] * (LANES - 3 * NSA_HEADS)
    assert len(cols) == NSA_PROJ_W
    return np.asarray(cols, np.int32)


def _rel_bucket_table(n):
    d = np.arange(n)
    max_exact = REL_BUCKETS // 2
    nf = np.maximum(d, 1).astype(np.float64)
    large = max_exact + (np.log(nf / max_exact) / math.log(REL_MAX_DIST / max_exact)
                         * (REL_BUCKETS - max_exact)).astype(np.int32)
    large = np.minimum(large, REL_BUCKETS - 1)
    return np.where(d < max_exact, d, large).astype(np.int32)


def _nsa_tables(rel_bias, t):
    qb = NSA_QBLOCK
    bucket = _rel_bucket_table(t)
    assert np.all(bucket[qb + 1:] == REL_BUCKETS - 1)
    bvec = rel_bias[bucket].T
    far = rel_bias[REL_BUCKETS - 1]
    far_hi = far.astype(BF16)
    far_lo = (far - far_hi.astype(F32)).astype(BF16)
    far_sum = far_hi.astype(F32) + far_lo.astype(F32)
    r = np.arange(qb)[:, None]
    c = np.arange(qb)[None, :]
    d0 = r - c
    t0 = jnp.where(d0 >= 0, bvec[:, np.maximum(d0, 0)] - far_sum[:, None, None], NEG_INF)
    t1 = bvec[:, qb + r - c] - far_sum[:, None, None]
    near = jnp.stack([t0, t1], axis=1)
    n_cmp = (t - NSA_CMP_LEN) // NSA_CMP_STRIDE + 1
    tq = np.arange(t)[:, None]
    j = np.arange(t // NSA_CMP_STRIDE)[None, :]
    dc = tq - (j * NSA_CMP_STRIDE + NSA_CMP_LEN - 1)
    ok = (dc >= 0) & (j < n_cmp)
    cmp_bias = jnp.where(ok, bvec[:, np.maximum(dc, 0)], NEG_INF)
    qconst = jnp.zeros((NSA_HEADS, 1, LANES), F32)
    qconst = qconst.at[:, 0, CONST_LANE0].set(far_hi.astype(F32)).at[:, 0, CONST_LANE0 + 1].set(far_lo.astype(F32))
    return near, cmp_bias, qconst


def _overlap_t(t):
    n_cmp = (t - NSA_CMP_LEN) // NSA_CMP_STRIDE + 1
    n_slc = t // NSA_SLC_LEN
    c_start = np.arange(n_cmp)[:, None] * NSA_CMP_STRIDE
    s_start = np.arange(n_slc)[None, :] * NSA_SLC_LEN
    ov = np.clip(np.minimum(c_start + NSA_CMP_LEN, s_start + NSA_SLC_LEN) - np.maximum(c_start, s_start), 0, None)
    ov = ov.astype(np.float32) / NSA_CMP_LEN
    out = np.zeros((LANES, t // NSA_CMP_STRIDE), np.float32)
    out[FEAT_LANE0:FEAT_LANE0 + n_slc, :n_cmp] = ov.T
    return out


def _nsa_compress_kernel(x_ref, pos_ref, w1_ref, w2_ref, o_ref, xs_ref):
    t = x_ref.shape[1]
    nb = t // NSA_CMP_STRIDE
    nlt = xs_ref.shape[0]
    for c in range(nlt):
        xs_ref[c, 0:t, :] = x_ref[0, :, c * LANES:(c + 1) * LANES]
        xs_ref[c, t:t + NSA_CMP_STRIDE, :] = jnp.zeros((NSA_CMP_STRIDE, LANES), F32)
    acc = jnp.zeros((nb, w1_ref.shape[2]), F32)
    for l in range(NSA_CMP_LEN):
        xl = jnp.concatenate([xs_ref[c, pl.ds(l, nb, stride=NSA_CMP_STRIDE), :] for c in range(nlt)], axis=1)
        xl = xl + pos_ref[l:l + 1, :]
        acc = acc + _dot(xl.astype(BF16), w1_ref[l])
    hid = _silu(acc).astype(BF16)
    res = _dot(hid, w2_ref[...])
    for g in range(NSA_GROUPS):
        o_ref[0, g] = res[:, g * LANES:(g + 1) * LANES]


def _nsa_compress(proj, cmp_pos, cmp_w1, cmp_w2):
    bsz, t, _ = proj.shape
    g, dh = NSA_GROUPS, NSA_HEAD_DIM
    nb = t // NSA_CMP_STRIDE
    w = 2 * NSA_KV_W
    eye = jnp.eye(g, dtype=F32)
    w1 = cmp_w1.reshape(2, NSA_CMP_LEN, dh, dh)
    w1c = jnp.einsum('gh,ij,ilde->ligdhje', eye, jnp.eye(2, dtype=F32), w1).reshape(NSA_CMP_LEN, w, w).astype(BF16)
    w2c = jnp.einsum('gh,ij,ide->gidhje', eye, jnp.eye(2, dtype=F32), cmp_w2).reshape(w, w).astype(BF16)
    pos = jnp.broadcast_to(cmp_pos[:, :, None, :], (2, NSA_CMP_LEN, g, dh)).transpose(1, 0, 2, 3).reshape(NSA_CMP_LEN, w)
    return pl.pallas_call(
        _nsa_compress_kernel,
        grid=(bsz,),
        in_specs=[
            pl.BlockSpec((1, t, w), lambda b: (b, 0, NSA_COL_CMP // w)),
            pl.BlockSpec((NSA_CMP_LEN, w), lambda b: (0, 0)),
            pl.BlockSpec((NSA_CMP_LEN, w, w), lambda b: (0, 0, 0), pipeline_mode=pl.Buffered(1)),
            pl.BlockSpec((w, w), lambda b: (0, 0)),
        ],
        out_specs=pl.BlockSpec((1, g, nb, LANES), lambda b: (b, 0, 0, 0)),
        out_shape=jax.ShapeDtypeStruct((bsz, g, nb, LANES), F32),
        scratch_shapes=[pltpu.VMEM((w // LANES, t + NSA_CMP_STRIDE, LANES), F32)],
        compiler_params=pltpu.CompilerParams(dimension_semantics=("parallel",), vmem_limit_bytes=VMEM_LIMIT),
        name="nsa_compress",
    )(proj, pos, w1c, w2c)


def _nsa_attn_kernel(q_ref, kvs_ref, kvw_ref, kvc_ref, gate_ref, z_ref, cb_ref, near_ref, qc_ref, ovl_ref,
                     o_ref, ks_ref, vs_ref, kw_ref, vw_ref, m_ref, l_ref, acc_ref):
    qb, dh, hpg = NSA_QBLOCK, NSA_HEAD_DIM, NSA_HPG
    t = kvs_ref.shape[1]
    nblk = t // NSA_SLC_LEN
    rows = hpg * qb
    g = pl.program_id(1)
    i = pl.program_id(2)
    lane = lax.broadcasted_iota(jnp.int32, (qb, LANES), 1)

    @pl.when(i == 0)
    def _():
        tok = lax.broadcasted_iota(jnp.int32, (t, LANES), 0)
        ln = lax.broadcasted_iota(jnp.int32, (t, LANES), 1)
        const = jnp.where((ln == CONST_LANE0) | (ln == CONST_LANE0 + 1), 1.0, 0.0)
        onehot = jnp.where(ln - FEAT_LANE0 == tok // NSA_SLC_LEN, 1.0, 0.0)
        kvs = kvs_ref[0]
        kvw = kvw_ref[0]
        ks_ref[...] = jnp.where(ln < dh, kvs, onehot + const).astype(BF16)
        kw_ref[...] = jnp.where(ln < dh, kvw, const).astype(BF16)
        vs_ref[...] = kvs.astype(BF16)
        vw_ref[...] = kvw.astype(BF16)

    qblk = q_ref[0] * (dh ** -0.5)
    qh = []
    for hh in range(hpg):
        tile = qblk[:, (hh // 2) * LANES:(hh // 2 + 1) * LANES]
        if hh % 2:
            tile = pltpu.roll(tile, dh, axis=1)
        qh.append(jnp.where(lane < dh, tile, 0.0))

    kvc = kvc_ref[0, 0].astype(BF16)
    q16 = jnp.concatenate(qh, axis=0).astype(BF16)
    s = _dot_nt(q16, kvc) + cb_ref[...].reshape(rows, cb_ref.shape[2])
    s = jnp.exp(s - jnp.max(s, axis=-1, keepdims=True))
    p = s / jnp.sum(s, axis=-1, keepdims=True)
    tq_row = i * qb + lax.broadcasted_iota(jnp.int32, (qb, 1), 0)
    p = p * jnp.concatenate([(tq_row >= NSA_CMP_LEN - 1).astype(F32)] * hpg, axis=0)
    p16 = p.astype(BF16)
    o_cmp = _dot(p16, kvc)
    ovl = ovl_ref[...].astype(BF16)
    imp_t = _dot_nt(ovl, p16[0:qb])
    for hh in range(1, hpg):
        imp_t = imp_t + _dot_nt(ovl, p16[hh * qb:(hh + 1) * qb])

    imp = imp_t[FEAT_LANE0:FEAT_LANE0 + 32, :]
    blk = lax.broadcasted_iota(jnp.int32, (32, qb), 0)
    tq = i * qb + lax.broadcasted_iota(jnp.int32, (32, qb), 1)
    cur = tq // NSA_SLC_LEN
    forced = (blk == 0) | (blk == cur) | (blk == cur - 1)
    val = jnp.where(forced, jnp.inf, jnp.where(blk * NSA_SLC_LEN <= tq, imp, -jnp.inf))
    cnt = jnp.zeros((32, qb), jnp.int32)
    for s2 in range(nblk):
        other = val[s2:s2 + 1, :]
        cnt = cnt + ((other > val) | ((other == val) & (s2 < blk))).astype(jnp.int32)
    feat_t = jnp.where((cnt < min(NSA_TOP_K, nblk)) & (blk < nblk), 0.0, NEG_INF)
    feat = jnp.concatenate([jnp.zeros((FEAT_LANE0, qb), F32), feat_t,
                            jnp.zeros((LANES - FEAT_LANE0 - 32, qb), F32)], axis=0).T

    qa = jnp.concatenate([qh[hh] + feat + qc_ref[hh] for hh in range(hpg)], axis=0).astype(BF16)

    def attend(k_ref, v_ref, kt, bias):
        ks = pl.multiple_of(kt * qb, qb)
        sc = _dot_nt(qa, k_ref[pl.ds(ks, qb), :])
        if bias is not None:
            sc = sc + bias
        m_old = m_ref[...]
        m_new = jnp.maximum(m_old, jnp.max(sc, axis=-1, keepdims=True))
        alpha = jnp.exp(m_old - m_new)
        pe = jnp.exp(sc - m_new)
        l_ref[...] = alpha * l_ref[...] + jnp.sum(pe, axis=-1, keepdims=True)
        acc_ref[...] = alpha * acc_ref[...] + _dot(pe.astype(BF16), v_ref[pl.ds(ks, qb), :])
        m_ref[...] = m_new

    def reset():
        m_ref[...] = jnp.full(m_ref.shape, NEG_INF, F32)
        l_ref[...] = jnp.zeros(l_ref.shape, F32)
        acc_ref[...] = jnp.zeros(acc_ref.shape, F32)

    def near_bias(d):
        return near_ref[:, d].reshape(rows, qb)

    reset()

    def sel_body(kt, carry):
        attend(ks_ref, vs_ref, kt, None)
        return carry

    lax.fori_loop(0, i - 1, sel_body, 0)

    @pl.when(i >= 1)
    def _():
        attend(ks_ref, vs_ref, i - 1, near_bias(1))

    attend(ks_ref, vs_ref, i, near_bias(0))
    o_slc = acc_ref[...] / l_ref[...]

    reset()
    nwt = NSA_WINDOW // qb
    rr = lax.broadcasted_iota(jnp.int32, (rows, qb), 0) % qb
    cc = lax.broadcasted_iota(jnp.int32, (rows, qb), 1)

    @pl.when(i >= nwt)
    def _():
        attend(kw_ref, vw_ref, i - nwt, jnp.where(rr < cc, 0.0, NEG_INF))

    for d in range(nwt - 1, 1, -1):
        @pl.when(i >= d)
        def _():
            attend(kw_ref, vw_ref, i - d, None)

    @pl.when(i >= 1)
    def _():
        attend(kw_ref, vw_ref, i - 1, near_bias(1))

    attend(kw_ref, vw_ref, i, near_bias(0))
    o_win = acc_ref[...] / l_ref[...]

    gates = jax.nn.sigmoid(gate_ref[0])
    zb = z_ref[0]
    outs = []
    for hh in range(hpg):
        sl = slice(hh * qb, (hh + 1) * qb)
        base = (g * hpg + hh) * 3
        gcol = [jnp.sum(jnp.where(lane == base + br, gates, 0.0), axis=1, keepdims=True) for br in range(3)]
        outs.append(gcol[0] * o_cmp[sl] + gcol[1] * o_slc[sl] + gcol[2] * o_win[sl])
    for j in range(hpg // 2):
        even = pltpu.roll(outs[2 * j], dh, axis=1)
        tile = jnp.where(lane < dh, even, outs[2 * j + 1])
        o_ref[0, :, j * LANES:(j + 1) * LANES] = (tile * _silu(zb[:, j * LANES:(j + 1) * LANES])).astype(o_ref.dtype)


def _nsa_attn(proj, kv_cmp, near, cmp_bias, qconst, ovl):
    bsz, t, _ = proj.shape
    qb, hpg = NSA_QBLOCK, NSA_HPG
    gw = hpg * NSA_HEAD_DIM
    nb = t // NSA_CMP_STRIDE
    rows = hpg * qb
    return pl.pallas_call(
        _nsa_attn_kernel,
        grid=(bsz, NSA_GROUPS, t // qb),
        in_specs=[
            pl.BlockSpec((1, qb, gw), lambda b, g, i: (b, i, NSA_COL_Q // gw + g)),
            pl.BlockSpec((1, t, LANES), lambda b, g, i: (b, 0, NSA_COL_SEL // LANES + g)),
            pl.BlockSpec((1, t, LANES), lambda b, g, i: (b, 0, NSA_COL_WIN // LANES + g)),
            pl.BlockSpec((1, 1, nb, LANES), lambda b, g, i: (b, g, 0, 0)),
            pl.BlockSpec((1, qb, LANES), lambda b, g, i: (b, i, NSA_COL_GATE // LANES)),
            pl.BlockSpec((1, qb, gw), lambda b, g, i: (b, i, NSA_COL_Z // gw + g)),
            pl.BlockSpec((hpg, qb, nb), lambda b, g, i: (g, i, 0)),
            pl.BlockSpec((hpg, 2, qb, qb), lambda b, g, i: (g, 0, 0, 0)),
            pl.BlockSpec((hpg, 1, LANES), lambda b, g, i: (g, 0, 0)),
            pl.BlockSpec((LANES, nb), lambda b, g, i: (0, 0)),
        ],
        out_specs=pl.BlockSpec((1, qb, gw), lambda b, g, i: (b, i, g)),
        out_shape=jax.ShapeDtypeStruct((bsz, t, NSA_Q_W), BF16),
        scratch_shapes=[
            pltpu.VMEM((t, LANES), BF16), pltpu.VMEM((t, LANES), BF16),
            pltpu.VMEM((t, LANES), BF16), pltpu.VMEM((t, LANES), BF16),
            pltpu.VMEM((rows, 1), F32), pltpu.VMEM((rows, 1), F32), pltpu.VMEM((rows, LANES), F32),
        ],
        compiler_params=pltpu.CompilerParams(
            dimension_semantics=("parallel", "parallel", "arbitrary"), vmem_limit_bytes=VMEM_LIMIT),
        name="nsa_attention",
    )(proj, proj, proj, kv_cmp, proj, proj, cmp_bias, near, qconst, ovl)


def _nsa_layer(x, mod, norm_g, w_in, cmp_pos, cmp_w1, cmp_w2, rel_bias, w_out, final_g):
    t = x.shape[1]
    assert t // NSA_SLC_LEN <= 32 and NSA_COL_Z % (NSA_HPG * NSA_HEAD_DIM) == 0
    perm = _nsa_column_perm()
    w_in_p = jnp.where(perm[None, :] >= 0, w_in[:, np.maximum(perm, 0)], 0.0).astype(BF16)
    proj = _inproj(x, norm_g, mod, w_in_p, tm=256, tn=NSA_PROJ_W)
    kv_cmp = _nsa_compress(proj, cmp_pos, cmp_w1, cmp_w2)
    near, cmp_bias, qconst = _nsa_tables(rel_bias, t)
    o = _nsa_attn(proj, kv_cmp, near, cmp_bias, qconst, jnp.asarray(_overlap_t(t)))
    return _outproj(o, w_out.astype(BF16), x, mod, final_g, tm=512)


def kernel(x, c, ada_w, ada_b, norm_g, gdn_w_in, gdn_conv_w, gdn_a_log, gdn_dt_bias, gdn_norm_w, gdn_w_out,
           nsa_w_in, nsa_cmp_pos, nsa_cmp_w1, nsa_cmp_w2, nsa_w_out, rel_bias, final_g):
    bsz, t, d = x.shape
    mod = _modulation(c, ada_w, ada_b).reshape(ada_w.shape[0], bsz, 3, d)
    x = _gdn_layer(x, mod[0], norm_g[0], gdn_w_in[0], gdn_conv_w[0], gdn_a_log[0], gdn_dt_bias[0],
                   gdn_norm_w[0], gdn_w_out[0])
    return _nsa_layer(x, mod[1], norm_g[1], nsa_w_in[0], nsa_cmp_pos[0], nsa_cmp_w1[0], nsa_cmp_w2[0],
                      rel_bias, nsa_w_out[0], final_g)
```

```python
import functools
import math

import numpy as np
import jax
import jax.numpy as jnp
from jax import lax
from jax.experimental import pallas as pl
from jax.experimental.pallas import tpu as pltpu

F32 = jnp.float32
BF16 = jnp.bfloat16
HIGHEST = lax.Precision.HIGHEST

EPS = 1e-6
NEG_INF = -1e30
LANES = 128
VMEM_LIMIT = 56 * 1024 * 1024

GDN_QK_HEADS = 8
GDN_V_HEADS = 16
GDN_HEAD_DIM = 128
GDN_CONV = 4
GDN_CHUNK = 64
GDN_QK_W = GDN_QK_HEADS * GDN_HEAD_DIM
GDN_V_W = GDN_V_HEADS * GDN_HEAD_DIM
GDN_CONV_W = 2 * GDN_QK_W + GDN_V_W
GDN_TILE = 256


def _silu(x):
    return x * jax.nn.sigmoid(x)


def _dot(a, b, **kw):
    return jnp.dot(a, b, preferred_element_type=F32, **kw)


def _dot_nt(a, b, **kw):
    return lax.dot_general(a, b, (((1,), (1,)), ((), ())), preferred_element_type=F32, **kw)


def _dot_tn(a, b, **kw):
    return lax.dot_general(a, b, (((0,), (0,)), ((), ())), preferred_element_type=F32, **kw)


def _mod_kernel(c_ref, w_ref, b_ref, o_ref):
    cond = _silu(c_ref[...])
    o_ref[0] = _dot(cond, w_ref[0], precision=HIGHEST) + b_ref[0]


def _modulation(c, ada_w, ada_b):
    depth, d, d3 = ada_w.shape
    bsz = c.shape[0]
    return pl.pallas_call(
        _mod_kernel,
        grid=(depth, d3 // d),
        in_specs=[
            pl.BlockSpec((bsz, d), lambda i, j: (0, 0)),
            pl.BlockSpec((1, d, d), lambda i, j: (i, 0, j)),
            pl.BlockSpec((1, 1, d), lambda i, j: (i, 0, j)),
        ],
        out_specs=pl.BlockSpec((1, bsz, d), lambda i, j: (i, 0, j)),
        out_shape=jax.ShapeDtypeStruct((depth, bsz, d3), F32),
        name="adaln_mod",
    )(c, ada_w, ada_b.reshape(depth, 1, d3))


def _inproj_kernel(x_ref, g_ref, mod_ref, w_ref, o_ref, *, tn):
    x = x_ref[0]
    m = mod_ref[0]
    y = x * lax.rsqrt(jnp.mean(x * x, axis=-1, keepdims=True) + EPS) * g_ref[...]
    h = (y * (1.0 + m[1:2]) + m[0:1]).astype(BF16)
    for j in range(w_ref.shape[1] // tn):
        o_ref[0, :, j * tn:(j + 1) * tn] = _dot(h, w_ref[:, j * tn:(j + 1) * tn])


def _inproj(x, g, mod, w, *, tm, tn):
    bsz, t, d = x.shape
    n = w.shape[1]
    assert t % tm == 0 and n % tn == 0
    return pl.pallas_call(
        functools.partial(_inproj_kernel, tn=tn),
        grid=(bsz, t // tm),
        in_specs=[
            pl.BlockSpec((1, tm, d), lambda b, i: (b, i, 0)),
            pl.BlockSpec((1, d), lambda b, i: (0, 0)),
            pl.BlockSpec((1, 3, d), lambda b, i: (b, 0, 0)),
            pl.BlockSpec((d, n), lambda b, i: (0, 0), pipeline_mode=pl.Buffered(1)),
        ],
        out_specs=pl.BlockSpec((1, tm, n), lambda b, i: (b, i, 0)),
        out_shape=jax.ShapeDtypeStruct((bsz, t, n), F32),
        compiler_params=pltpu.CompilerParams(
            dimension_semantics=("parallel", "parallel"), vmem_limit_bytes=VMEM_LIMIT),
        name="norm_mod_inproj",
    )(x, g.reshape(1, d), mod, w)


def _outproj_kernel(o_ref, w_ref, x_ref, mod_ref, *rest, final_norm):
    y = _dot(o_ref[0], w_ref[...])
    x = x_ref[0] + mod_ref[0][2:3] * y
    if final_norm:
        fg_ref, out_ref = rest
        x = x * lax.rsqrt(jnp.mean(x * x, axis=-1, keepdims=True) + EPS) * fg_ref[...]
    else:
        (out_ref,) = rest
    out_ref[0] = x


def _outproj(o, w, x, mod, final_g=None, *, tm):
    bsz, t, d = x.shape
    k = o.shape[-1]
    in_specs = [
        pl.BlockSpec((1, tm, k), lambda b, i: (b, i, 0)),
        pl.BlockSpec((k, d), lambda b, i: (0, 0), pipeline_mode=pl.Buffered(1)),
        pl.BlockSpec((1, tm, d), lambda b, i: (b, i, 0)),
        pl.BlockSpec((1, 3, d), lambda b, i: (b, 0, 0)),
    ]
    args = [o, w, x, mod]
    if final_g is not None:
        in_specs.append(pl.BlockSpec((1, d), lambda b, i: (0, 0)))
        args.append(final_g.reshape(1, d))
    return pl.pallas_call(
        functools.partial(_outproj_kernel, final_norm=final_g is not None),
        grid=(bsz, t // tm),
        in_specs=in_specs,
        out_specs=pl.BlockSpec((1, tm, d), lambda b, i: (b, i, 0)),
        out_shape=jax.ShapeDtypeStruct((bsz, t, d), F32),
        compiler_params=pltpu.CompilerParams(
            dimension_semantics=("parallel", "parallel"), vmem_limit_bytes=VMEM_LIMIT),
        name="outproj_residual",
    )(*args)


def _gdn_gates_kernel(ba_ref, alog_ref, dtb_ref, col_ref, row_ref):
    cs = GDN_CHUNK
    ba = ba_ref[0]
    lane = lax.broadcasted_iota(jnp.int32, ba.shape, 1)
    g = -jnp.exp(alog_ref[...]) * jax.nn.softplus(ba + dtb_ref[...])
    vals = jnp.where(lane < GDN_V_HEADS, jax.nn.sigmoid(ba), g)
    r = lax.broadcasted_iota(jnp.int32, (cs, cs), 0)
    c = lax.broadcasted_iota(jnp.int32, (cs, cs), 1)
    tri = (r >= c).astype(F32)
    is_beta = lax.broadcasted_iota(jnp.int32, (cs, LANES), 1) < GDN_V_HEADS
    for n in range(ba.shape[0] // cs):
        v = vals[n * cs:(n + 1) * cs]
        cum = _dot(tri, v, precision=HIGHEST)
        out = jnp.where(is_beta, v, cum)
        col_ref[0, n * cs:(n + 1) * cs, :] = out
        row_ref[0, n] = out.T[:2 * GDN_V_HEADS, :]


def _gdn_gates(proj, a_log, dt_bias, *, tt=512):
    bsz, t, n = proj.shape
    nc = t // GDN_CHUNK
    pad = lambda u: jnp.zeros((1, LANES), F32).at[0, GDN_V_HEADS:2 * GDN_V_HEADS].set(u)
    ba_blk = (n - LANES) // LANES
    return pl.pallas_call(
        _gdn_gates_kernel,
        grid=(bsz, t // tt),
        in_specs=[
            pl.BlockSpec((1, tt, LANES), lambda b, i: (b, i, ba_blk)),
            pl.BlockSpec((1, LANES), lambda b, i: (0, 0)),
            pl.BlockSpec((1, LANES), lambda b, i: (0, 0)),
        ],
        out_specs=[
            pl.BlockSpec((1, tt, LANES), lambda b, i: (b, i, 0)),
            pl.BlockSpec((1, tt // GDN_CHUNK, 2 * GDN_V_HEADS, GDN_CHUNK), lambda b, i: (b, i, 0, 0)),
        ],
        out_shape=[
            jax.ShapeDtypeStruct((bsz, t, LANES), F32),
            jax.ShapeDtypeStruct((bsz, nc, 2 * GDN_V_HEADS, GDN_CHUNK), F32),
        ],
        name="gdn_gates",
    )(proj, pad(a_log), pad(dt_bias))


TRI_BASE = 8


def _tri_inverse_many(mats):
    n = mats[0].shape[0]
    r = lax.broadcasted_iota(jnp.int32, (n, n), 0)
    c = lax.broadcasted_iota(jnp.int32, (n, n), 1)
    same = lambda s: (r // s) == (c // s)
    mm = lambda a, b: _dot(a.astype(BF16), b.astype(BF16))
    diag = [jnp.where(same(TRI_BASE), a, 0.0) for a in mats]
    inv = [jnp.where(r == c, 1.0, 0.0) - d for d in diag]
    pw = diag
    k = 1
    while 2 * k < TRI_BASE:
        pw = [mm(m, m) for m in pw]
        inv = [p + mm(p, m) for p, m in zip(inv, pw)]
        k *= 2
    s = TRI_BASE
    while s < n:
        sub = same(2 * s) & jnp.logical_not(same(s))
        left = [mm(p, jnp.where(sub, a, 0.0)) for p, a in zip(inv, mats)]
        inv = [p - mm(l, p) for p, l in zip(inv, left)]
        s *= 2
    return inv


def _gdn_chunk_kernel(q_ref, k_ref, v_ref, z_ref, col_ref, row_ref, wq_ref, wk_ref, wv_ref, nw_ref,
                      o_ref, s_ref, qx_ref, kx_ref, vx_ref):
    cs, dh, nh = GDN_CHUNK, GDN_HEAD_DIM, GDN_V_HEADS
    tt = q_ref.shape[1]
    ncb = tt // cs
    hq = pl.program_id(1)
    ti = pl.program_id(2)

    @pl.when(ti == 0)
    def _():
        s_ref[...] = jnp.zeros_like(s_ref)
        qx_ref[0:8, :] = jnp.zeros((8, dh), F32)
        kx_ref[0:8, :] = jnp.zeros((8, dh), F32)
        vx_ref[0:8, :] = jnp.zeros((8, 2 * dh), F32)

    def conv_silu(x_ref, ext_ref, w_ref):
        ext_ref[8:8 + tt, :] = x_ref[0]
        w = w_ref[...]
        acc = w[0:1] * ext_ref[5:5 + tt, :]
        for j in range(1, GDN_CONV):
            acc = acc + w[j:j + 1] * ext_ref[5 + j:5 + j + tt, :]
        ext_ref[0:8, :] = ext_ref[tt:tt + 8, :]
        return _silu(acc)

    def l2n(x):
        return x * lax.rsqrt(jnp.sum(x * x, axis=-1, keepdims=True) + EPS)

    q = l2n(conv_silu(q_ref, qx_ref, wq_ref)) * (dh ** -0.5)
    k = l2n(conv_silu(k_ref, kx_ref, wk_ref))
    v = conv_silu(v_ref, vx_ref, wv_ref)

    r = lax.broadcasted_iota(jnp.int32, (cs, cs), 0)
    c = lax.broadcasted_iota(jnp.int32, (cs, cs), 1)
    lower = r >= c
    strict = r > c
    lane = lax.broadcasted_iota(jnp.int32, (tt, LANES), 1)
    colv = col_ref[0]
    heads = [2 * hq + e for e in range(2)]
    beta = [jnp.sum(jnp.where(lane == h, colv, 0.0), axis=1, keepdims=True) for h in heads]
    gc = [jnp.sum(jnp.where(lane == nh + h, colv, 0.0), axis=1, keepdims=True) for h in heads]
    egc = [jnp.exp(x) for x in gc]
    chunks = [slice(n * cs, (n + 1) * cs) for n in range(ncb)]

    k16 = k.astype(BF16)
    q16 = q.astype(BF16)
    kk = [_dot_nt(k16[sl], k16[sl]) for sl in chunks]
    qk = [_dot_nt(q16[sl], k16[sl]) for sl in chunks]
    a_mats, qkd, glast, kdt, rhs = [], [], [], [], []
    for e in range(2):
        kbeta = k * beta[e]
        wrhs = kbeta * egc[e]
        vrhs = v[:, e * dh:(e + 1) * dh] * beta[e]
        for n, sl in enumerate(chunks):
            gc_row = row_ref[0, n, pl.ds(nh + heads[e], 1), :]
            g_last = gc_row[:, cs - 1:cs]
            decay = jnp.exp(jnp.where(lower, gc[e][sl] - gc_row, -jnp.inf))
            a_mats.append(jnp.where(strict, beta[e][sl] * kk[n] * decay, 0.0))
            qkd.append((qk[n] * decay).astype(BF16))
            glast.append(jnp.exp(g_last))
            kdt.append((k[sl] * jnp.exp(g_last - gc[e][sl])).T.astype(BF16))
            rhs.append(jnp.concatenate([vrhs[sl], wrhs[sl]], axis=1).astype(BF16))
    t_mats = _tri_inverse_many(a_mats)
    uw = [_dot(tm.astype(BF16), rh) for tm, rh in zip(t_mats, rhs)]

    nw = nw_ref[...]
    state = [s_ref[e] for e in range(2)]
    for n, sl in enumerate(chunks):
        for e in range(2):
            j = e * ncb + n
            u, w = uw[j][:, :dh], uw[j][:, dh:]
            q_dec = q[sl] * egc[e][sl]
            s16 = state[e].astype(BF16)
            ws = _dot(jnp.concatenate([w, q_dec], axis=0).astype(BF16), s16)
            v16 = (u - ws[:cs]).astype(BF16)
            o = ws[cs:] + _dot(qkd[j], v16)
            state[e] = state[e] * glast[j] + _dot(kdt[j], v16)
            o = o * lax.rsqrt(jnp.mean(o * o, axis=-1, keepdims=True) + EPS) * nw
            o = o * _silu(z_ref[0, sl, e * dh:(e + 1) * dh])
            o_ref[0, sl, e * dh:(e + 1) * dh] = o.astype(o_ref.dtype)
    for e in range(2):
        s_ref[e] = state[e]


def _gdn_chunk(proj, col, row, conv_w, norm_w, *, tt=GDN_TILE):
    bsz, t, _ = proj.shape
    dh = GDN_HEAD_DIM
    qk_blocks = GDN_QK_W // dh
    v_blk0 = 2 * GDN_QK_W // (2 * dh)
    z_blk0 = GDN_CONV_W // (2 * dh)
    ncb = tt // GDN_CHUNK
    return pl.pallas_call(
        _gdn_chunk_kernel,
        grid=(bsz, GDN_QK_HEADS, t // tt),
        in_specs=[
            pl.BlockSpec((1, tt, dh), lambda b, h, i: (b, i, h)),
            pl.BlockSpec((1, tt, dh), lambda b, h, i: (b, i, qk_blocks + h)),
            pl.BlockSpec((1, tt, 2 * dh), lambda b, h, i: (b, i, v_blk0 + h)),
            pl.BlockSpec((1, tt, 2 * dh), lambda b, h, i: (b, i, z_blk0 + h)),
            pl.BlockSpec((1, tt, LANES), lambda b, h, i: (b, i, 0)),
            pl.BlockSpec((1, ncb, 2 * GDN_V_HEADS, GDN_CHUNK), lambda b, h, i: (b, i, 0, 0)),
            pl.BlockSpec((GDN_CONV, dh), lambda b, h, i: (0, h)),
            pl.BlockSpec((GDN_CONV, dh), lambda b, h, i: (0, qk_blocks + h)),
            pl.BlockSpec((GDN_CONV, 2 * dh), lambda b, h, i: (0, v_blk0 + h)),
            pl.BlockSpec((1, dh), lambda b, h, i: (0, 0)),
        ],
        out_specs=pl.BlockSpec((1, tt, 2 * dh), lambda b, h, i: (b, i, h)),
        out_shape=jax.ShapeDtypeStruct((bsz, t, GDN_V_W), BF16),
        scratch_shapes=[
            pltpu.VMEM((2, dh, dh), F32),
            pltpu.VMEM((tt + 8, dh), F32),
            pltpu.VMEM((tt + 8, dh), F32),
            pltpu.VMEM((tt + 8, 2 * dh), F32),
        ],
        compiler_params=pltpu.CompilerParams(
            dimension_semantics=("parallel", "parallel", "arbitrary"), vmem_limit_bytes=VMEM_LIMIT),
        name="gdn_chunk_scan",
    )(proj, proj, proj, proj, col, row, conv_w, conv_w, conv_w, norm_w.reshape(1, dh))


def _gdn_layer(x, mod, norm_g, w_in, conv_w, a_log, dt_bias, norm_w, w_out):
    n_in = w_in.shape[1]
    n_pad = -(-n_in // (7 * LANES)) * (7 * LANES)
    w_in_p = jnp.pad(w_in, ((0, 0), (0, n_pad - n_in))).astype(BF16)
    proj = _inproj(x, norm_g, mod, w_in_p, tm=256, tn=7 * LANES)
    col, row = _gdn_gates(proj, a_log, dt_bias)
    o = _gdn_chunk(proj, col, row, conv_w, norm_w)
    return _outproj(o, w_out.astype(BF16), x, mod, tm=512)


NSA_HEADS = 16
NSA_GROUPS = 4
NSA_HPG = NSA_HEADS // NSA_GROUPS
NSA_HEAD_DIM = 64
NSA_CMP_LEN = 32
NSA_CMP_STRIDE = 16
NSA_SLC_LEN = 64
NSA_TOP_K = 8
NSA_WINDOW = 512
NSA_QBLOCK = 128
NSA_Q_W = NSA_HEADS * NSA_HEAD_DIM
NSA_KV_W = NSA_GROUPS * NSA_HEAD_DIM
REL_BUCKETS = 32
REL_MAX_DIST = 128
FEAT_LANE0 = NSA_HEAD_DIM
CONST_LANE0 = FEAT_LANE0 + 32
NSA_COL_Q = 0
NSA_COL_CMP = NSA_Q_W
NSA_COL_SEL = NSA_COL_CMP + 2 * NSA_KV_W
NSA_COL_WIN = NSA_COL_SEL + 2 * NSA_KV_W
NSA_COL_Z = NSA_COL_WIN + 2 * NSA_KV_W
NSA_COL_GATE = NSA_COL_Z + NSA_Q_W
NSA_PROJ_W = NSA_COL_GATE + LANES


def _nsa_column_perm():
    g, dh = NSA_GROUPS, NSA_HEAD_DIM
    kv0 = NSA_Q_W
    cols = list(range(NSA_Q_W))
    cols += [kv0 + i for i in range(2 * NSA_KV_W)]
    for br in (1, 2):
        for gi in range(g):
            cols += [kv0 + (2 * br) * NSA_KV_W + gi * dh + d for d in range(dh)]
            cols += [kv0 + (2 * br + 1) * NSA_KV_W + gi * dh + d for d in range(dh)]
    gate0 = kv0 + 6 * NSA_KV_W
    cols += [gate0 + 3 * NSA_HEADS + i for i in range(NSA_Q_W)]
    cols += [gate0 + i for i in range(3 * NSA_HEADS)] + [-1] * (LANES - 3 * NSA_HEADS)
    assert len(cols) == NSA_PROJ_W
    return np.asarray(cols, np.int32)


def _rel_bucket_table(n):
    d = np.arange(n)
    max_exact = REL_BUCKETS // 2
    nf = np.maximum(d, 1).astype(np.float64)
    large = max_exact + (np.log(nf / max_exact) / math.log(REL_MAX_DIST / max_exact)
                         * (REL_BUCKETS - max_exact)).astype(np.int32)
    large = np.minimum(large, REL_BUCKETS - 1)
    return np.where(d < max_exact, d, large).astype(np.int32)


def _nsa_tables(rel_bias, t):
    qb = NSA_QBLOCK
    bucket = _rel_bucket_table(t)
    assert np.all(bucket[qb + 1:] == REL_BUCKETS - 1)
    bvec = rel_bias[bucket].T
    far = rel_bias[REL_BUCKETS - 1]
    far_hi = far.astype(BF16)
    far_lo = (far - far_hi.astype(F32)).astype(BF16)
    far_sum = far_hi.astype(F32) + far_lo.astype(F32)
    r = np.arange(qb)[:, None]
    c = np.arange(qb)[None, :]
    d0 = r - c
    t0 = jnp.where(d0 >= 0, bvec[:, np.maximum(d0, 0)] - far_sum[:, None, None], NEG_INF)
    t1 = bvec[:, qb + r - c] - far_sum[:, None, None]
    near = jnp.stack([t0, t1], axis=1)
    nb = t // NSA_CMP_STRIDE
    per_tile = qb // NSA_CMP_STRIDE
    width = 2 * per_tile
    far_d = per_tile * NSA_CMP_STRIDE - (NSA_CMP_LEN - 1) + NSA_CMP_STRIDE
    assert np.all(bucket[far_d:] == REL_BUCKETS - 1)
    dm = r - NSA_CMP_STRIDE * np.arange(width)[None, :] + far_d
    band = jnp.where(dm >= 0, bvec[:, np.maximum(dm, 0)], NEG_INF)
    tiles = []
    for i in range(t // qb):
        j0 = per_tile * i - (per_tile + 1)
        lo, hi = max(j0, 0), min(j0 + width, nb)
        tiles.append(jnp.concatenate([
            jnp.broadcast_to(far[:, None, None], (NSA_HEADS, qb, lo)),
            band[:, :, lo - j0:hi - j0],
            jnp.full((NSA_HEADS, qb, nb - hi), NEG_INF, F32)], axis=2))
    cmp_bias = jnp.stack(tiles, axis=1).reshape(NSA_HEADS, t, nb)
    qconst = jnp.zeros((NSA_HEADS, 1, LANES), F32)
    qconst = qconst.at[:, 0, CONST_LANE0].set(far_hi.astype(F32)).at[:, 0, CONST_LANE0 + 1].set(far_lo.astype(F32))
    return near, cmp_bias, qconst


def _overlap_t(t):
    n_cmp = (t - NSA_CMP_LEN) // NSA_CMP_STRIDE + 1
    n_slc = t // NSA_SLC_LEN
    c_start = np.arange(n_cmp)[:, None] * NSA_CMP_STRIDE
    s_start = np.arange(n_slc)[None, :] * NSA_SLC_LEN
    ov = np.clip(np.minimum(c_start + NSA_CMP_LEN, s_start + NSA_SLC_LEN) - np.maximum(c_start, s_start), 0, None)
    ov = ov.astype(np.float32) / NSA_CMP_LEN
    out = np.zeros((LANES, t // NSA_CMP_STRIDE), np.float32)
    out[FEAT_LANE0:FEAT_LANE0 + n_slc, :n_cmp] = ov.T
    return out


def _nsa_compress_kernel(x_ref, pos_ref, w1_ref, w2_ref, o_ref, xs_ref):
    t = x_ref.shape[1]
    nb = t // NSA_CMP_STRIDE
    nlt = xs_ref.shape[0]
    for c in range(nlt):
        xs_ref[c, 0:t, :] = x_ref[0, :, c * LANES:(c + 1) * LANES]
        xs_ref[c, t:t + NSA_CMP_STRIDE, :] = jnp.zeros((NSA_CMP_STRIDE, LANES), F32)
    acc = jnp.zeros((nb, w1_ref.shape[2]), F32)
    for l in range(NSA_CMP_LEN):
        xl = jnp.concatenate([xs_ref[c, pl.ds(l, nb, stride=NSA_CMP_STRIDE), :] for c in range(nlt)], axis=1)
        xl = xl + pos_ref[l:l + 1, :]
        acc = acc + _dot(xl.astype(BF16), w1_ref[l])
    hid = _silu(acc).astype(BF16)
    res = _dot(hid, w2_ref[...])
    for g in range(NSA_GROUPS):
        o_ref[0, g] = res[:, g * LANES:(g + 1) * LANES]


def _nsa_compress(proj, cmp_pos, cmp_w1, cmp_w2):
    bsz, t, _ = proj.shape
    g, dh = NSA_GROUPS, NSA_HEAD_DIM
    nb = t // NSA_CMP_STRIDE
    w = 2 * NSA_KV_W
    eye = jnp.eye(g, dtype=F32)
    w1 = cmp_w1.reshape(2, NSA_CMP_LEN, dh, dh)
    w1c = jnp.einsum('gh,ij,ilde->ligdhje', eye, jnp.eye(2, dtype=F32), w1).reshape(NSA_CMP_LEN, w, w).astype(BF16)
    w2c = jnp.einsum('gh,ij,ide->gidhje', eye, jnp.eye(2, dtype=F32), cmp_w2).reshape(w, w).astype(BF16)
    pos = jnp.broadcast_to(cmp_pos[:, :, None, :], (2, NSA_CMP_LEN, g, dh)).transpose(1, 0, 2, 3).reshape(NSA_CMP_LEN, w)
    return pl.pallas_call(
        _nsa_compress_kernel,
        grid=(bsz,),
        in_specs=[
            pl.BlockSpec((1, t, w), lambda b: (b, 0, NSA_COL_CMP // w)),
            pl.BlockSpec((NSA_CMP_LEN, w), lambda b: (0, 0)),
            pl.BlockSpec((NSA_CMP_LEN, w, w), lambda b: (0, 0, 0), pipeline_mode=pl.Buffered(1)),
            pl.BlockSpec((w, w), lambda b: (0, 0)),
        ],
        out_specs=pl.BlockSpec((1, g, nb, LANES), lambda b: (b, 0, 0, 0)),
        out_shape=jax.ShapeDtypeStruct((bsz, g, nb, LANES), F32),
        scratch_shapes=[pltpu.VMEM((w // LANES, t + NSA_CMP_STRIDE, LANES), F32)],
        compiler_params=pltpu.CompilerParams(dimension_semantics=("parallel",), vmem_limit_bytes=VMEM_LIMIT),
        name="nsa_compress",
    )(proj, pos, w1c, w2c)


def _nsa_attn_kernel(q_ref, kvs_ref, kvw_ref, kvc_ref, gate_ref, z_ref, cb_ref, near_ref, qc_ref, ovl_ref,
                     o_ref, ks_ref, vs_ref, kw_ref, vw_ref, m_ref, l_ref, acc_ref):
    qb, dh, hpg = NSA_QBLOCK, NSA_HEAD_DIM, NSA_HPG
    t = kvs_ref.shape[1]
    nblk = t // NSA_SLC_LEN
    rows = hpg * qb
    g = pl.program_id(1)
    i = pl.program_id(2)
    lane = lax.broadcasted_iota(jnp.int32, (qb, LANES), 1)

    @pl.when(i == 0)
    def _():
        tok = lax.broadcasted_iota(jnp.int32, (t, LANES), 0)
        ln = lax.broadcasted_iota(jnp.int32, (t, LANES), 1)
        const = jnp.where((ln == CONST_LANE0) | (ln == CONST_LANE0 + 1), 1.0, 0.0)
        onehot = jnp.where(ln - FEAT_LANE0 == tok // NSA_SLC_LEN, 1.0, 0.0)
        kvs = kvs_ref[0]
        kvw = kvw_ref[0]
        ks_ref[...] = jnp.where(ln < dh, kvs, onehot + const).astype(BF16)
        kw_ref[...] = jnp.where(ln < dh, kvw, const).astype(BF16)
        vs_ref[...] = kvs.astype(BF16)
        vw_ref[...] = kvw.astype(BF16)

    qblk = q_ref[0] * (dh ** -0.5)
    qh = []
    for hh in range(hpg):
        tile = qblk[:, (hh // 2) * LANES:(hh // 2 + 1) * LANES]
        if hh % 2:
            tile = pltpu.roll(tile, dh, axis=1)
        qh.append(jnp.where(lane < dh, tile, 0.0))

    kvc = kvc_ref[0, 0].astype(BF16)
    q16 = jnp.concatenate(qh, axis=0).astype(BF16)
    s = _dot_nt(q16, kvc) + cb_ref[...].reshape(rows, cb_ref.shape[2])
    s = jnp.exp(s - jnp.max(s, axis=-1, keepdims=True))
    p = s / jnp.sum(s, axis=-1, keepdims=True)
    tq_row = i * qb + lax.broadcasted_iota(jnp.int32, (qb, 1), 0)
    p = p * jnp.concatenate([(tq_row >= NSA_CMP_LEN - 1).astype(F32)] * hpg, axis=0)
    p16 = p.astype(BF16)
    o_cmp = _dot(p16, kvc)
    ovl = ovl_ref[...].astype(BF16)
    imp_t = _dot_nt(ovl, p16[0:qb])
    for hh in range(1, hpg):
        imp_t = imp_t + _dot_nt(ovl, p16[hh * qb:(hh + 1) * qb])

    imp = imp_t[FEAT_LANE0:FEAT_LANE0 + 32, :]
    blk = lax.broadcasted_iota(jnp.int32, (32, qb), 0)
    tq = i * qb + lax.broadcasted_iota(jnp.int32, (32, qb), 1)
    cur = tq // NSA_SLC_LEN
    forced = (blk == 0) | (blk == cur) | (blk == cur - 1)
    val = jnp.where(forced, jnp.inf, jnp.where(blk * NSA_SLC_LEN <= tq, imp, -jnp.inf))
    cnt = jnp.zeros((32, qb), jnp.int32)
    for s2 in range(nblk):
        other = val[s2:s2 + 1, :]
        cnt = cnt + ((other > val) | ((other == val) & (s2 < blk))).astype(jnp.int32)
    feat_t = jnp.where((cnt < min(NSA_TOP_K, nblk)) & (blk < nblk), 0.0, NEG_INF)
    feat = jnp.concatenate([jnp.zeros((FEAT_LANE0, qb), F32), feat_t,
                            jnp.zeros((LANES - FEAT_LANE0 - 32, qb), F32)], axis=0).T

    qa = jnp.concatenate([qh[hh] + feat + qc_ref[hh] for hh in range(hpg)], axis=0).astype(BF16)

    def attend(k_ref, v_ref, kt, bias):
        ks = pl.multiple_of(kt * qb, qb)
        sc = _dot_nt(qa, k_ref[pl.ds(ks, qb), :])
        if bias is not None:
            sc = sc + bias
        m_old = m_ref[...]
        m_new = jnp.maximum(m_old, jnp.max(sc, axis=-1, keepdims=True))
        alpha = jnp.exp(m_old - m_new)
        pe = jnp.exp(sc - m_new)
        l_ref[...] = alpha * l_ref[...] + jnp.sum(pe, axis=-1, keepdims=True)
        acc_ref[...] = alpha * acc_ref[...] + _dot(pe.astype(BF16), v_ref[pl.ds(ks, qb), :])
        m_ref[...] = m_new

    def reset():
        m_ref[...] = jnp.full(m_ref.shape, NEG_INF, F32)
        l_ref[...] = jnp.zeros(l_ref.shape, F32)
        acc_ref[...] = jnp.zeros(acc_ref.shape, F32)

    def near_bias(d):
        return near_ref[:, d].reshape(rows, qb)

    reset()

    def sel_body(kt, carry):
        attend(ks_ref, vs_ref, kt, None)
        return carry

    lax.fori_loop(0, i - 1, sel_body, 0)

    @pl.when(i >= 1)
    def _():
        attend(ks_ref, vs_ref, i - 1, near_bias(1))

    attend(ks_ref, vs_ref, i, near_bias(0))
    o_slc = acc_ref[...] / l_ref[...]

    reset()
    nwt = NSA_WINDOW // qb
    rr = lax.broadcasted_iota(jnp.int32, (rows, qb), 0) % qb
    cc = lax.broadcasted_iota(jnp.int32, (rows, qb), 1)

    @pl.when(i >= nwt)
    def _():
        attend(kw_ref, vw_ref, i - nwt, jnp.where(rr < cc, 0.0, NEG_INF))

    for d in range(nwt - 1, 1, -1):
        @pl.when(i >= d)
        def _():
            attend(kw_ref, vw_ref, i - d, None)

    @pl.when(i >= 1)
    def _():
        attend(kw_ref, vw_ref, i - 1, near_bias(1))

    attend(kw_ref, vw_ref, i, near_bias(0))
    o_win = acc_ref[...] / l_ref[...]

    gates = jax.nn.sigmoid(gate_ref[0])
    zb = z_ref[0]
    outs = []
    for hh in range(hpg):
        sl = slice(hh * qb, (hh + 1) * qb)
        base = (g * hpg + hh) * 3
        gcol = [jnp.sum(jnp.where(lane == base + br, gates, 0.0), axis=1, keepdims=True) for br in range(3)]
        outs.append(gcol[0] * o_cmp[sl] + gcol[1] * o_slc[sl] + gcol[2] * o_win[sl])
    for j in range(hpg // 2):
        even = pltpu.roll(outs[2 * j], dh, axis=1)
        tile = jnp.where(lane < dh, even, outs[2 * j + 1])
        o_ref[0, :, j * LANES:(j + 1) * LANES] = (tile * _silu(zb[:, j * LANES:(j + 1) * LANES])).astype(o_ref.dtype)


def _nsa_attn(proj, kv_cmp, near, cmp_bias, qconst, ovl):
    bsz, t, _ = proj.shape
    qb, hpg = NSA_QBLOCK, NSA_HPG
    gw = hpg * NSA_HEAD_DIM
    nb = t // NSA_CMP_STRIDE
    rows = hpg * qb
    return pl.pallas_call(
        _nsa_attn_kernel,
        grid=(bsz, NSA_GROUPS, t // qb),
        in_specs=[
            pl.BlockSpec((1, qb, gw), lambda b, g, i: (b, i, NSA_COL_Q // gw + g)),
            pl.BlockSpec((1, t, LANES), lambda b, g, i: (b, 0, NSA_COL_SEL // LANES + g)),
            pl.BlockSpec((1, t, LANES), lambda b, g, i: (b, 0, NSA_COL_WIN // LANES + g)),
            pl.BlockSpec((1, 1, nb, LANES), lambda b, g, i: (b, g, 0, 0)),
            pl.BlockSpec((1, qb, LANES), lambda b, g, i: (b, i, NSA_COL_GATE // LANES)),
            pl.BlockSpec((1, qb, gw), lambda b, g, i: (b, i, NSA_COL_Z // gw + g)),
            pl.BlockSpec((hpg, qb, nb), lambda b, g, i: (g, i, 0)),
            pl.BlockSpec((hpg, 2, qb, qb), lambda b, g, i: (g, 0, 0, 0)),
            pl.BlockSpec((hpg, 1, LANES), lambda b, g, i: (g, 0, 0)),
            pl.BlockSpec((LANES, nb), lambda b, g, i: (0, 0)),
        ],
        out_specs=pl.BlockSpec((1, qb, gw), lambda b, g, i: (b, i, g)),
        out_shape=jax.ShapeDtypeStruct((bsz, t, NSA_Q_W), BF16),
        scratch_shapes=[
            pltpu.VMEM((t, LANES), BF16), pltpu.VMEM((t, LANES), BF16),
            pltpu.VMEM((t, LANES), BF16), pltpu.VMEM((t, LANES), BF16),
            pltpu.VMEM((rows, 1), F32), pltpu.VMEM((rows, 1), F32), pltpu.VMEM((rows, LANES), F32),
        ],
        compiler_params=pltpu.CompilerParams(
            dimension_semantics=("parallel", "parallel", "arbitrary"), vmem_limit_bytes=VMEM_LIMIT),
        name="nsa_attention",
    )(proj, proj, proj, kv_cmp, proj, proj, cmp_bias, near, qconst, ovl)


def _nsa_layer(x, mod, norm_g, w_in, cmp_pos, cmp_w1, cmp_w2, rel_bias, w_out, final_g):
    t = x.shape[1]
    assert t // NSA_SLC_LEN <= 32 and NSA_COL_Z % (NSA_HPG * NSA_HEAD_DIM) == 0
    perm = _nsa_column_perm()
    w_in_p = jnp.where(perm[None, :] >= 0, w_in[:, np.maximum(perm, 0)], 0.0).astype(BF16)
    proj = _inproj(x, norm_g, mod, w_in_p, tm=256, tn=NSA_PROJ_W)
    kv_cmp = _nsa_compress(proj, cmp_pos, cmp_w1, cmp_w2)
    near, cmp_bias, qconst = _nsa_tables(rel_bias, t)
    o = _nsa_attn(proj, kv_cmp, near, cmp_bias, qconst, jnp.asarray(_overlap_t(t)))
    return _outproj(o, w_out.astype(BF16), x, mod, final_g, tm=512)


def kernel(x, c, ada_w, ada_b, norm_g, gdn_w_in, gdn_conv_w, gdn_a_log, gdn_dt_bias, gdn_norm_w, gdn_w_out,
           nsa_w_in, nsa_cmp_pos, nsa_cmp_w1, nsa_cmp_w2, nsa_w_out, rel_bias, final_g):
    bsz, t, d = x.shape
    mod = _modulation(c, ada_w, ada_b).reshape(ada_w.shape[0], bsz, 3, d)
    x = _gdn_layer(x, mod[0], norm_g[0], gdn_w_in[0], gdn_conv_w[0], gdn_a_log[0], gdn_dt_bias[0],
                   gdn_norm_w[0], gdn_w_out[0])
    return _nsa_layer(x, mod[1], norm_g[1], nsa_w_in[0], nsa_cmp_pos[0], nsa_cmp_w1[0], nsa_cmp_w2[0],
                      rel_bias, nsa_w_out[0], final_g)
```

```python
import functools
import math

import numpy as np
import jax
import jax.numpy as jnp
from jax import lax
from jax.experimental import pallas as pl
from jax.experimental.pallas import tpu as pltpu

F32 = jnp.float32
BF16 = jnp.bfloat16
HIGHEST = lax.Precision.HIGHEST

EPS = 1e-6
NEG_INF = -1e30
LANES = 128
VMEM_LIMIT = 56 * 1024 * 1024

GDN_QK_HEADS = 8
GDN_V_HEADS = 16
GDN_HEAD_DIM = 128
GDN_CONV = 4
GDN_CHUNK = 64
GDN_QK_W = GDN_QK_HEADS * GDN_HEAD_DIM
GDN_V_W = GDN_V_HEADS * GDN_HEAD_DIM
GDN_CONV_W = 2 * GDN_QK_W + GDN_V_W
GDN_TILE = 256


def _silu(x):
    return x * jax.nn.sigmoid(x)


def _dot(a, b, **kw):
    return jnp.dot(a, b, preferred_element_type=F32, **kw)


def _dot_nt(a, b, **kw):
    return lax.dot_general(a, b, (((1,), (1,)), ((), ())), preferred_element_type=F32, **kw)


def _mod_kernel(c_ref, w_ref, b_ref, o_ref):
    cond = _silu(c_ref[...])
    o_ref[0] = _dot(cond, w_ref[0], precision=HIGHEST) + b_ref[0]


def _modulation(c, ada_w, ada_b):
    depth, d, d3 = ada_w.shape
    bsz = c.shape[0]
    return pl.pallas_call(
        _mod_kernel,
        grid=(depth, d3 // d),
        in_specs=[
            pl.BlockSpec((bsz, d), lambda i, j: (0, 0)),
            pl.BlockSpec((1, d, d), lambda i, j: (i, 0, j)),
            pl.BlockSpec((1, 1, d), lambda i, j: (i, 0, j)),
        ],
        out_specs=pl.BlockSpec((1, bsz, d), lambda i, j: (i, 0, j)),
        out_shape=jax.ShapeDtypeStruct((depth, bsz, d3), F32),
        name="adaln_mod",
    )(c, ada_w, ada_b.reshape(depth, 1, d3))


def _inproj_kernel(x_ref, g_ref, mod_ref, w_ref, o_ref, *, tn):
    x = x_ref[0]
    m = mod_ref[0]
    y = x * lax.rsqrt(jnp.mean(x * x, axis=-1, keepdims=True) + EPS) * g_ref[...]
    h = (y * (1.0 + m[1:2]) + m[0:1]).astype(BF16)
    for j in range(w_ref.shape[1] // tn):
        o_ref[0, :, j * tn:(j + 1) * tn] = _dot(h, w_ref[:, j * tn:(j + 1) * tn])


def _inproj(x, g, mod, w, *, tm, tn):
    bsz, t, d = x.shape
    n = w.shape[1]
    assert t % tm == 0 and n % tn == 0
    return pl.pallas_call(
        functools.partial(_inproj_kernel, tn=tn),
        grid=(bsz, t // tm),
        in_specs=[
            pl.BlockSpec((1, tm, d), lambda b, i: (b, i, 0)),
            pl.BlockSpec((1, d), lambda b, i: (0, 0)),
            pl.BlockSpec((1, 3, d), lambda b, i: (b, 0, 0)),
            pl.BlockSpec((d, n), lambda b, i: (0, 0), pipeline_mode=pl.Buffered(1)),
        ],
        out_specs=pl.BlockSpec((1, tm, n), lambda b, i: (b, i, 0)),
        out_shape=jax.ShapeDtypeStruct((bsz, t, n), F32),
        compiler_params=pltpu.CompilerParams(
            dimension_semantics=("parallel", "parallel"), vmem_limit_bytes=VMEM_LIMIT),
        name="norm_mod_inproj",
    )(x, g.reshape(1, d), mod, w)


def _outproj_kernel(o_ref, w_ref, x_ref, mod_ref, *rest, final_norm):
    y = _dot(o_ref[0], w_ref[...])
    x = x_ref[0] + mod_ref[0][2:3] * y
    if final_norm:
        fg_ref, out_ref = rest
        x = x * lax.rsqrt(jnp.mean(x * x, axis=-1, keepdims=True) + EPS) * fg_ref[...]
    else:
        (out_ref,) = rest
    out_ref[0] = x


def _outproj(o, w, x, mod, final_g=None, *, tm):
    bsz, t, d = x.shape
    k = o.shape[-1]
    in_specs = [
        pl.BlockSpec((1, tm, k), lambda b, i: (b, i, 0)),
        pl.BlockSpec((k, d), lambda b, i: (0, 0), pipeline_mode=pl.Buffered(1)),
        pl.BlockSpec((1, tm, d), lambda b, i: (b, i, 0)),
        pl.BlockSpec((1, 3, d), lambda b, i: (b, 0, 0)),
    ]
    args = [o, w, x, mod]
    if final_g is not None:
        in_specs.append(pl.BlockSpec((1, d), lambda b, i: (0, 0)))
        args.append(final_g.reshape(1, d))
    return pl.pallas_call(
        functools.partial(_outproj_kernel, final_norm=final_g is not None),
        grid=(bsz, t // tm),
        in_specs=in_specs,
        out_specs=pl.BlockSpec((1, tm, d), lambda b, i: (b, i, 0)),
        out_shape=jax.ShapeDtypeStruct((bsz, t, d), F32),
        compiler_params=pltpu.CompilerParams(
            dimension_semantics=("parallel", "parallel"), vmem_limit_bytes=VMEM_LIMIT),
        name="outproj_residual",
    )(*args)


def _gdn_gates_kernel(ba_ref, alog_ref, dtb_ref, col_ref, row_ref):
    cs = GDN_CHUNK
    ba = ba_ref[0]
    lane = lax.broadcasted_iota(jnp.int32, ba.shape, 1)
    g = -jnp.exp(alog_ref[...]) * jax.nn.softplus(ba + dtb_ref[...])
    vals = jnp.where(lane < GDN_V_HEADS, jax.nn.sigmoid(ba), g)
    r = lax.broadcasted_iota(jnp.int32, (cs, cs), 0)
    c = lax.broadcasted_iota(jnp.int32, (cs, cs), 1)
    tri = (r >= c).astype(F32)
    is_beta = lax.broadcasted_iota(jnp.int32, (cs, LANES), 1) < GDN_V_HEADS
    for n in range(ba.shape[0] // cs):
        v = vals[n * cs:(n + 1) * cs]
        cum = _dot(tri, v, precision=HIGHEST)
        out = jnp.where(is_beta, v, cum)
        col_ref[0, n * cs:(n + 1) * cs, :] = out
        row_ref[0, n] = out.T[:2 * GDN_V_HEADS, :]


def _gdn_gates(proj, a_log, dt_bias, *, tt=512):
    bsz, t, n = proj.shape
    nc = t // GDN_CHUNK
    pad = lambda u: jnp.zeros((1, LANES), F32).at[0, GDN_V_HEADS:2 * GDN_V_HEADS].set(u)
    ba_blk = (n - LANES) // LANES
    return pl.pallas_call(
        _gdn_gates_kernel,
        grid=(bsz, t // tt),
        in_specs=[
            pl.BlockSpec((1, tt, LANES), lambda b, i: (b, i, ba_blk)),
            pl.BlockSpec((1, LANES), lambda b, i: (0, 0)),
            pl.BlockSpec((1, LANES), lambda b, i: (0, 0)),
        ],
        out_specs=[
            pl.BlockSpec((1, tt, LANES), lambda b, i: (b, i, 0)),
            pl.BlockSpec((1, tt // GDN_CHUNK, 2 * GDN_V_HEADS, GDN_CHUNK), lambda b, i: (b, i, 0, 0)),
        ],
        out_shape=[
            jax.ShapeDtypeStruct((bsz, t, LANES), F32),
            jax.ShapeDtypeStruct((bsz, nc, 2 * GDN_V_HEADS, GDN_CHUNK), F32),
        ],
        name="gdn_gates",
    )(proj, pad(a_log), pad(dt_bias))


TRI_BASE = 8


def _tri_inverse_many(mats):
    n = mats[0].shape[0]
    r = lax.broadcasted_iota(jnp.int32, (n, n), 0)
    c = lax.broadcasted_iota(jnp.int32, (n, n), 1)
    same = lambda s: (r // s) == (c // s)
    mm = lambda a, b: _dot(a.astype(BF16), b.astype(BF16))
    diag = [jnp.where(same(TRI_BASE), a, 0.0) for a in mats]
    inv = [jnp.where(r == c, 1.0, 0.0) - d for d in diag]
    pw = diag
    k = 1
    while 2 * k < TRI_BASE:
        pw = [mm(m, m) for m in pw]
        inv = [p + mm(p, m) for p, m in zip(inv, pw)]
        k *= 2
    s = TRI_BASE
    while s < n:
        sub = same(2 * s) & jnp.logical_not(same(s))
        left = [mm(p, jnp.where(sub, a, 0.0)) for p, a in zip(inv, mats)]
        inv = [p - mm(l, p) for p, l in zip(inv, left)]
        s *= 2
    return inv


def _gdn_chunk_kernel(q_ref, k_ref, v_ref, z_ref, col_ref, row_ref, wq_ref, wk_ref, wv_ref, nw_ref,
                      o_ref, s_ref, qx_ref, kx_ref, vx_ref):
    cs, dh, nh = GDN_CHUNK, GDN_HEAD_DIM, GDN_V_HEADS
    tt = q_ref.shape[1]
    ncb = tt // cs
    hq = pl.program_id(1)
    ti = pl.program_id(2)

    @pl.when(ti == 0)
    def _():
        s_ref[...] = jnp.zeros_like(s_ref)
        qx_ref[0:8, :] = jnp.zeros((8, dh), F32)
        kx_ref[0:8, :] = jnp.zeros((8, dh), F32)
        vx_ref[0:8, :] = jnp.zeros((8, 2 * dh), F32)

    def conv_silu(x_ref, ext_ref, w_ref):
        ext_ref[8:8 + tt, :] = x_ref[0]
        w = w_ref[...]
        acc = w[0:1] * ext_ref[5:5 + tt, :]
        for j in range(1, GDN_CONV):
            acc = acc + w[j:j + 1] * ext_ref[5 + j:5 + j + tt, :]
        ext_ref[0:8, :] = ext_ref[tt:tt + 8, :]
        return _silu(acc)

    def l2n(x):
        return x * lax.rsqrt(jnp.sum(x * x, axis=-1, keepdims=True) + EPS)

    q = l2n(conv_silu(q_ref, qx_ref, wq_ref)) * (dh ** -0.5)
    k = l2n(conv_silu(k_ref, kx_ref, wk_ref))
    v = conv_silu(v_ref, vx_ref, wv_ref)

    r = lax.broadcasted_iota(jnp.int32, (cs, cs), 0)
    c = lax.broadcasted_iota(jnp.int32, (cs, cs), 1)
    lower = r >= c
    strict = r > c
    lane = lax.broadcasted_iota(jnp.int32, (tt, LANES), 1)
    colv = col_ref[0]
    heads = [2 * hq + e for e in range(2)]
    beta = [jnp.sum(jnp.where(lane == h, colv, 0.0), axis=1, keepdims=True) for h in heads]
    gc = [jnp.sum(jnp.where(lane == nh + h, colv, 0.0), axis=1, keepdims=True) for h in heads]
    egc = [jnp.exp(x) for x in gc]
    chunks = [slice(n * cs, (n + 1) * cs) for n in range(ncb)]

    k16 = k.astype(BF16)
    q16 = q.astype(BF16)
    kk = [_dot_nt(k16[sl], k16[sl]) for sl in chunks]
    qk = [_dot_nt(q16[sl], k16[sl]) for sl in chunks]
    a_mats, qkd, glast, kdt, rhs = [], [], [], [], []
    for e in range(2):
        kbeta = k * beta[e]
        wrhs = kbeta * egc[e]
        vrhs = v[:, e * dh:(e + 1) * dh] * beta[e]
        for n, sl in enumerate(chunks):
            gc_row = row_ref[0, n, pl.ds(nh + heads[e], 1), :]
            g_last = gc_row[:, cs - 1:cs]
            decay = jnp.exp(jnp.where(lower, gc[e][sl] - gc_row, -jnp.inf))
            a_mats.append(jnp.where(strict, beta[e][sl] * kk[n] * decay, 0.0))
            qkd.append((qk[n] * decay).astype(BF16))
            glast.append(jnp.exp(g_last))
            kdt.append((k[sl] * jnp.exp(g_last - gc[e][sl])).T.astype(BF16))
            rhs.append(jnp.concatenate([vrhs[sl], wrhs[sl]], axis=1).astype(BF16))
    t_mats = _tri_inverse_many(a_mats)
    uw = [_dot(tm.astype(BF16), rh) for tm, rh in zip(t_mats, rhs)]

    nw = nw_ref[...]
    state = [s_ref[e] for e in range(2)]
    for n, sl in enumerate(chunks):
        for e in range(2):
            j = e * ncb + n
            u, w = uw[j][:, :dh], uw[j][:, dh:]
            q_dec = q[sl] * egc[e][sl]
            s16 = state[e].astype(BF16)
            ws = _dot(jnp.concatenate([w, q_dec], axis=0).astype(BF16), s16)
            v16 = (u - ws[:cs]).astype(BF16)
            o = ws[cs:] + _dot(qkd[j], v16)
            state[e] = state[e] * glast[j] + _dot(kdt[j], v16)
            o = o * lax.rsqrt(jnp.mean(o * o, axis=-1, keepdims=True) + EPS) * nw
            o = o * _silu(z_ref[0, sl, e * dh:(e + 1) * dh])
            o_ref[0, sl, e * dh:(e + 1) * dh] = o.astype(o_ref.dtype)
    for e in range(2):
        s_ref[e] = state[e]


def _gdn_chunk(proj, col, row, conv_w, norm_w, *, tt=GDN_TILE):
    bsz, t, _ = proj.shape
    dh = GDN_HEAD_DIM
    qk_blocks = GDN_QK_W // dh
    v_blk0 = 2 * GDN_QK_W // (2 * dh)
    z_blk0 = GDN_CONV_W // (2 * dh)
    ncb = tt // GDN_CHUNK
    return pl.pallas_call(
        _gdn_chunk_kernel,
        grid=(bsz, GDN_QK_HEADS, t // tt),
        in_specs=[
            pl.BlockSpec((1, tt, dh), lambda b, h, i: (b, i, h)),
            pl.BlockSpec((1, tt, dh), lambda b, h, i: (b, i, qk_blocks + h)),
            pl.BlockSpec((1, tt, 2 * dh), lambda b, h, i: (b, i, v_blk0 + h)),
            pl.BlockSpec((1, tt, 2 * dh), lambda b, h, i: (b, i, z_blk0 + h)),
            pl.BlockSpec((1, tt, LANES), lambda b, h, i: (b, i, 0)),
            pl.BlockSpec((1, ncb, 2 * GDN_V_HEADS, GDN_CHUNK), lambda b, h, i: (b, i, 0, 0)),
            pl.BlockSpec((GDN_CONV, dh), lambda b, h, i: (0, h)),
            pl.BlockSpec((GDN_CONV, dh), lambda b, h, i: (0, qk_blocks + h)),
            pl.BlockSpec((GDN_CONV, 2 * dh), lambda b, h, i: (0, v_blk0 + h)),
            pl.BlockSpec((1, dh), lambda b, h, i: (0, 0)),
        ],
        out_specs=pl.BlockSpec((1, tt, 2 * dh), lambda b, h, i: (b, i, h)),
        out_shape=jax.ShapeDtypeStruct((bsz, t, GDN_V_W), BF16),
        scratch_shapes=[
            pltpu.VMEM((2, dh, dh), F32),
            pltpu.VMEM((tt + 8, dh), F32),
            pltpu.VMEM((tt + 8, dh), F32),
            pltpu.VMEM((tt + 8, 2 * dh), F32),
        ],
        compiler_params=pltpu.CompilerParams(
            dimension_semantics=("parallel", "parallel", "arbitrary"), vmem_limit_bytes=VMEM_LIMIT),
        name="gdn_chunk_scan",
    )(proj, proj, proj, proj, col, row, conv_w, conv_w, conv_w, norm_w.reshape(1, dh))


def _gdn_layer(x, mod, norm_g, w_in, conv_w, a_log, dt_bias, norm_w, w_out):
    n_in = w_in.shape[1]
    n_pad = -(-n_in // (7 * LANES)) * (7 * LANES)
    w_in_p = jnp.pad(w_in, ((0, 0), (0, n_pad - n_in))).astype(BF16)
    proj = _inproj(x, norm_g, mod, w_in_p, tm=256, tn=7 * LANES)
    col, row = _gdn_gates(proj, a_log, dt_bias)
    o = _gdn_chunk(proj, col, row, conv_w, norm_w)
    return _outproj(o, w_out.astype(BF16), x, mod, tm=512)


NSA_HEADS = 16
NSA_GROUPS = 4
NSA_HPG = NSA_HEADS // NSA_GROUPS
NSA_HEAD_DIM = 64
NSA_CMP_LEN = 32
NSA_CMP_STRIDE = 16
NSA_SLC_LEN = 64
NSA_TOP_K = 8
NSA_WINDOW = 512
NSA_QBLOCK = 128
NSA_Q_W = NSA_HEADS * NSA_HEAD_DIM
NSA_KV_W = NSA_GROUPS * NSA_HEAD_DIM
REL_BUCKETS = 32
REL_MAX_DIST = 128
FEAT_LANE0 = NSA_HEAD_DIM
CONST_LANE0 = FEAT_LANE0 + 32
NSA_COL_Q = 0
NSA_COL_CMP = NSA_Q_W
NSA_COL_SEL = NSA_COL_CMP + 2 * NSA_KV_W
NSA_COL_WIN = NSA_COL_SEL + 2 * NSA_KV_W
NSA_COL_Z = NSA_COL_WIN + 2 * NSA_KV_W
NSA_COL_GATE = NSA_COL_Z + NSA_Q_W
NSA_PROJ_W = NSA_COL_GATE + LANES


def _nsa_column_perm():
    g, dh = NSA_GROUPS, NSA_HEAD_DIM
    kv0 = NSA_Q_W
    cols = list(range(NSA_Q_W))
    cols += [kv0 + i for i in range(2 * NSA_KV_W)]
    for br in (1, 2):
        for gi in range(g):
            cols += [kv0 + (2 * br) * NSA_KV_W + gi * dh + d for d in range(dh)]
            cols += [kv0 + (2 * br + 1) * NSA_KV_W + gi * dh + d for d in range(dh)]
    gate0 = kv0 + 6 * NSA_KV_W
    cols += [gate0 + 3 * NSA_HEADS + i for i in range(NSA_Q_W)]
    cols += [gate0 + i for i in range(3 * NSA_HEADS)] + [-1] * (LANES - 3 * NSA_HEADS)
    assert len(cols) == NSA_PROJ_W
    return np.asarray(cols, np.int32)


def _rel_bucket_table(n):
    d = np.arange(n)
    max_exact = REL_BUCKETS // 2
    nf = np.maximum(d, 1).astype(np.float64)
    large = max_exact + (np.log(nf / max_exact) / math.log(REL_MAX_DIST / max_exact)
                         * (REL_BUCKETS - max_exact)).astype(np.int32)
    large = np.minimum(large, REL_BUCKETS - 1)
    return np.where(d < max_exact, d, large).astype(np.int32)


def _nsa_tables(rel_bias, t):
    qb = NSA_QBLOCK
    bucket = _rel_bucket_table(t)
    assert np.all(bucket[qb + 1:] == REL_BUCKETS - 1)
    bvec = rel_bias[bucket].T
    far = rel_bias[REL_BUCKETS - 1]
    far_hi = far.astype(BF16)
    far_lo = (far - far_hi.astype(F32)).astype(BF16)
    far_sum = far_hi.astype(F32) + far_lo.astype(F32)
    r = np.arange(qb)[:, None]
    c = np.arange(qb)[None, :]
    d0 = r - c
    t0 = jnp.where(d0 >= 0, bvec[:, np.maximum(d0, 0)] - far_sum[:, None, None], NEG_INF)
    t1 = bvec[:, qb + r - c] - far_sum[:, None, None]
    g, hpg = NSA_GROUPS, NSA_HPG
    near = jnp.stack([t1, t0], axis=1).reshape(g, hpg, 2, qb, qb).transpose(0, 2, 4, 1, 3).reshape(g, 2, qb, hpg * qb)
    nb = t // NSA_CMP_STRIDE
    per_tile = qb // NSA_CMP_STRIDE
    width = 2 * per_tile
    far_d = per_tile * NSA_CMP_STRIDE - (NSA_CMP_LEN - 1) + NSA_CMP_STRIDE
    assert np.all(bucket[far_d:] == REL_BUCKETS - 1)
    dm = r - NSA_CMP_STRIDE * np.arange(width)[None, :] + far_d
    band = jnp.where(dm >= 0, bvec[:, np.maximum(dm, 0)], NEG_INF)
    tiles = []
    for i in range(t // qb):
        j0 = per_tile * i - (per_tile + 1)
        lo, hi = max(j0, 0), min(j0 + width, nb)
        tiles.append(jnp.concatenate([
            jnp.broadcast_to(far[:, None, None], (NSA_HEADS, qb, lo)),
            band[:, :, lo - j0:hi - j0],
            jnp.full((NSA_HEADS, qb, nb - hi), NEG_INF, F32)], axis=2))
    cmp_bias = jnp.stack(tiles, axis=0).reshape(t // qb, g, hpg, qb, nb).transpose(0, 1, 4, 2, 3)
    cmp_bias = cmp_bias.reshape(t // qb, g, nb, hpg * qb)
    qconst = jnp.zeros((g, 8, hpg, qb), F32)
    qconst = qconst.at[:, 0].set(jnp.broadcast_to(far_hi.astype(F32).reshape(g, hpg, 1), (g, hpg, qb)))
    qconst = qconst.at[:, 1].set(jnp.broadcast_to(far_lo.astype(F32).reshape(g, hpg, 1), (g, hpg, qb)))
    return near, cmp_bias, qconst.reshape(g, 8, hpg * qb)


def _overlap_t(t):
    n_cmp = (t - NSA_CMP_LEN) // NSA_CMP_STRIDE + 1
    n_slc = t // NSA_SLC_LEN
    c_start = np.arange(n_cmp)[:, None] * NSA_CMP_STRIDE
    s_start = np.arange(n_slc)[None, :] * NSA_SLC_LEN
    ov = np.clip(np.minimum(c_start + NSA_CMP_LEN, s_start + NSA_SLC_LEN) - np.maximum(c_start, s_start), 0, None)
    ov = ov.astype(np.float32) / NSA_CMP_LEN
    out = np.zeros((32, t // NSA_CMP_STRIDE), np.float32)
    out[:n_slc, :n_cmp] = ov.T
    return out


def _nsa_compress_kernel(x_ref, pos_ref, w1_ref, w2_ref, o_ref, xs_ref):
    t = x_ref.shape[1]
    nb = t // NSA_CMP_STRIDE
    nlt = xs_ref.shape[0]
    for c in range(nlt):
        xs_ref[c, 0:t, :] = x_ref[0, :, c * LANES:(c + 1) * LANES]
        xs_ref[c, t:t + NSA_CMP_STRIDE, :] = jnp.zeros((NSA_CMP_STRIDE, LANES), F32)
    acc = jnp.zeros((nb, w1_ref.shape[2]), F32)
    for l in range(NSA_CMP_LEN):
        xl = jnp.concatenate([xs_ref[c, pl.ds(l, nb, stride=NSA_CMP_STRIDE), :] for c in range(nlt)], axis=1)
        xl = xl + pos_ref[l:l + 1, :]
        acc = acc + _dot(xl.astype(BF16), w1_ref[l])
    hid = _silu(acc).astype(BF16)
    res = _dot(hid, w2_ref[...])
    for g in range(NSA_GROUPS):
        o_ref[0, g] = res[:, g * LANES:(g + 1) * LANES]


def _nsa_compress(proj, cmp_pos, cmp_w1, cmp_w2):
    bsz, t, _ = proj.shape
    g, dh = NSA_GROUPS, NSA_HEAD_DIM
    nb = t // NSA_CMP_STRIDE
    w = 2 * NSA_KV_W
    eye = jnp.eye(g, dtype=F32)
    w1 = cmp_w1.reshape(2, NSA_CMP_LEN, dh, dh)
    w1c = jnp.einsum('gh,ij,ilde->ligdhje', eye, jnp.eye(2, dtype=F32), w1).reshape(NSA_CMP_LEN, w, w).astype(BF16)
    w2c = jnp.einsum('gh,ij,ide->gidhje', eye, jnp.eye(2, dtype=F32), cmp_w2).reshape(w, w).astype(BF16)
    pos = jnp.broadcast_to(cmp_pos[:, :, None, :], (2, NSA_CMP_LEN, g, dh)).transpose(1, 0, 2, 3).reshape(NSA_CMP_LEN, w)
    return pl.pallas_call(
        _nsa_compress_kernel,
        grid=(bsz,),
        in_specs=[
            pl.BlockSpec((1, t, w), lambda b: (b, 0, NSA_COL_CMP // w)),
            pl.BlockSpec((NSA_CMP_LEN, w), lambda b: (0, 0)),
            pl.BlockSpec((NSA_CMP_LEN, w, w), lambda b: (0, 0, 0), pipeline_mode=pl.Buffered(1)),
            pl.BlockSpec((w, w), lambda b: (0, 0)),
        ],
        out_specs=pl.BlockSpec((1, g, nb, LANES), lambda b: (b, 0, 0, 0)),
        out_shape=jax.ShapeDtypeStruct((bsz, g, nb, LANES), F32),
        scratch_shapes=[pltpu.VMEM((w // LANES, t + NSA_CMP_STRIDE, LANES), F32)],
        compiler_params=pltpu.CompilerParams(dimension_semantics=("parallel",), vmem_limit_bytes=VMEM_LIMIT),
        name="nsa_compress",
    )(proj, pos, w1c, w2c)


def _nsa_attn_kernel(q_ref, kvs_ref, kvw_ref, kvc_ref, gate_ref, z_ref, cb_ref, near_ref, qc_ref, ovl_ref,
                     o_ref, ks_ref, vs_ref, kw_ref, vw_ref, gt_ref, m_ref, acc_ref):
    qb, dh, hpg = NSA_QBLOCK, NSA_HEAD_DIM, NSA_HPG
    t = kvs_ref.shape[1]
    nblk = t // NSA_SLC_LEN
    cols = hpg * qb
    g = pl.program_id(1)
    i = pl.program_id(2)

    @pl.when(i == 0)
    def _():
        tok = lax.broadcasted_iota(jnp.int32, (t, LANES), 0)
        ln = lax.broadcasted_iota(jnp.int32, (t, LANES), 1)
        const = jnp.where((ln == CONST_LANE0) | (ln == CONST_LANE0 + 1), 1.0, 0.0)
        onehot = jnp.where(ln - FEAT_LANE0 == tok // NSA_SLC_LEN, 1.0, 0.0)
        row = lax.broadcasted_iota(jnp.int32, (LANES, t), 0)
        kvs = kvs_ref[0]
        kvw = kvw_ref[0]
        ks_ref[...] = jnp.where(ln < dh, kvs, onehot + const).astype(BF16)
        kw_ref[...] = jnp.where(ln < dh, kvw, const).astype(BF16)
        vs_ref[...] = jnp.where(row == 0, 1.0, kvs.T).astype(BF16)
        vw_ref[...] = jnp.where(row == 0, 1.0, kvw.T).astype(BF16)

    q_t = (q_ref[0] * (dh ** -0.5)).T
    q_heads = jnp.concatenate([q_t[hh * dh:(hh + 1) * dh] for hh in range(hpg)], axis=1)

    kvc = kvc_ref[0, 0]
    lane_k = lax.broadcasted_iota(jnp.int32, kvc.shape, 1)
    kc16 = jnp.where(lane_k < dh, kvc, 0.0).astype(BF16)
    qc16 = jnp.concatenate([q_heads, jnp.zeros((LANES - dh, cols), F32)], axis=0).astype(BF16)
    s = _dot(kc16, qc16) + cb_ref[0, 0]
    s = jnp.exp(s - jnp.max(s, axis=0, keepdims=True))
    p = s / jnp.sum(s, axis=0, keepdims=True)
    tq_lane = i * qb + lax.broadcasted_iota(jnp.int32, (1, cols), 1) % qb
    p16 = (p * (tq_lane >= NSA_CMP_LEN - 1).astype(F32)).astype(BF16)
    o_cmp = _dot(kvc.T.astype(BF16), p16)
    ovl = ovl_ref[...].astype(BF16)
    imp = _dot(ovl, p16[:, 0:qb])
    for hh in range(1, hpg):
        imp = imp + _dot(ovl, p16[:, hh * qb:(hh + 1) * qb])

    blk = lax.broadcasted_iota(jnp.int32, (32, qb), 0)
    tq = i * qb + lax.broadcasted_iota(jnp.int32, (32, qb), 1)
    cur = tq // NSA_SLC_LEN
    forced = (blk == 0) | (blk == cur) | (blk == cur - 1)
    val = jnp.where(forced, jnp.inf, jnp.where(blk * NSA_SLC_LEN <= tq, imp, -jnp.inf))
    cnt = jnp.zeros((32, qb), jnp.int32)
    for s2 in range(nblk):
        other = val[s2:s2 + 1, :]
        cnt = cnt + ((other > val) | ((other == val) & (s2 < blk))).astype(jnp.int32)
    feat = jnp.where((cnt < min(NSA_TOP_K, nblk)) & (blk < nblk), 0.0, NEG_INF)

    qa = jnp.concatenate([q_heads, jnp.concatenate([feat] * hpg, axis=1), qc_ref[0],
                          jnp.zeros((LANES - CONST_LANE0 - 8, cols), F32)], axis=0).astype(BF16)

    def attend(k_ref, vt_ref, start, nk, bias):
        start = pl.multiple_of(start, qb)
        sc = _dot(k_ref[pl.ds(start, nk), :], qa)
        if bias is not None:
            sc = sc + bias
        m_old = m_ref[...]
        m_new = jnp.maximum(m_old, jnp.max(sc, axis=0, keepdims=True))
        alpha = jnp.exp(m_old - m_new)
        pe = jnp.exp(sc - m_new).astype(BF16)
        acc_ref[...] = alpha * acc_ref[...] + _dot(vt_ref[:, pl.ds(start, nk)], pe)
        m_ref[...] = m_new

    def reset():
        m_ref[...] = jnp.full(m_ref.shape, NEG_INF, F32)
        acc_ref[...] = jnp.zeros(acc_ref.shape, F32)

    def finish():
        acc = acc_ref[...]
        return acc[dh:] / acc[0:1]

    def near_tiles(k_ref, vt_ref):
        @pl.when(i >= 1)
        def _():
            attend(k_ref, vt_ref, (i - 1) * qb, 2 * qb, near_ref[0].reshape(2 * qb, cols))

        @pl.when(i == 0)
        def _():
            attend(k_ref, vt_ref, 0, qb, near_ref[0, 1])

    reset()
    n_far = jnp.maximum(i - 1, 0)

    def sel_body(kp, carry):
        attend(ks_ref, vs_ref, kp * (2 * qb), 2 * qb, None)
        return carry

    lax.fori_loop(0, n_far // 2, sel_body, 0)

    @pl.when(n_far % 2 == 1)
    def _():
        attend(ks_ref, vs_ref, (n_far - 1) * qb, qb, None)

    near_tiles(ks_ref, vs_ref)
    o_slc = finish()

    reset()
    nwt = NSA_WINDOW // qb
    assert nwt == 4
    kk = lax.broadcasted_iota(jnp.int32, (qb, cols), 0)
    rr = lax.broadcasted_iota(jnp.int32, (qb, cols), 1) % qb

    @pl.when(i >= nwt)
    def _():
        attend(kw_ref, vw_ref, (i - nwt) * qb, qb, jnp.where(rr < kk, 0.0, NEG_INF))

    @pl.when(i >= 3)
    def _():
        attend(kw_ref, vw_ref, (i - 3) * qb, 2 * qb, None)

    @pl.when(i == 2)
    def _():
        attend(kw_ref, vw_ref, 0, qb, None)

    near_tiles(kw_ref, vw_ref)
    o_win = finish()

    gt_ref[...] = jax.nn.sigmoid(gate_ref[0]).T
    outs = []
    for hh in range(hpg):
        sl = slice(hh * qb, (hh + 1) * qb)
        base = (g * hpg + hh) * 3
        gate = [gt_ref[pl.ds(base + br, 1), :] for br in range(3)]
        outs.append(gate[0] * o_cmp[dh:, sl] + gate[1] * o_slc[:, sl] + gate[2] * o_win[:, sl])
    out = jnp.concatenate(outs, axis=0).T
    o_ref[0] = (out * _silu(z_ref[0])).astype(o_ref.dtype)


def _nsa_attn(proj, kv_cmp, near, cmp_bias, qconst, ovl):
    bsz, t, _ = proj.shape
    qb, hpg = NSA_QBLOCK, NSA_HPG
    gw = hpg * NSA_HEAD_DIM
    nb = t // NSA_CMP_STRIDE
    cols = hpg * qb
    return pl.pallas_call(
        _nsa_attn_kernel,
        grid=(bsz, NSA_GROUPS, t // qb),
        in_specs=[
            pl.BlockSpec((1, qb, gw), lambda b, g, i: (b, i, NSA_COL_Q // gw + g)),
            pl.BlockSpec((1, t, LANES), lambda b, g, i: (b, 0, NSA_COL_SEL // LANES + g)),
            pl.BlockSpec((1, t, LANES), lambda b, g, i: (b, 0, NSA_COL_WIN // LANES + g)),
            pl.BlockSpec((1, 1, nb, LANES), lambda b, g, i: (b, g, 0, 0)),
            pl.BlockSpec((1, qb, LANES), lambda b, g, i: (b, i, NSA_COL_GATE // LANES)),
            pl.BlockSpec((1, qb, gw), lambda b, g, i: (b, i, NSA_COL_Z // gw + g)),
            pl.BlockSpec((1, 1, nb, cols), lambda b, g, i: (i, g, 0, 0)),
            pl.BlockSpec((1, 2, qb, cols), lambda b, g, i: (g, 0, 0, 0)),
            pl.BlockSpec((1, 8, cols), lambda b, g, i: (g, 0, 0)),
            pl.BlockSpec((32, nb), lambda b, g, i: (0, 0)),
        ],
        out_specs=pl.BlockSpec((1, qb, gw), lambda b, g, i: (b, i, g)),
        out_shape=jax.ShapeDtypeStruct((bsz, t, NSA_Q_W), BF16),
        scratch_shapes=[
            pltpu.VMEM((t, LANES), BF16), pltpu.VMEM((LANES, t), BF16),
            pltpu.VMEM((t, LANES), BF16), pltpu.VMEM((LANES, t), BF16),
            pltpu.VMEM((LANES, qb), F32), pltpu.VMEM((1, cols), F32), pltpu.VMEM((LANES, cols), F32),
        ],
        compiler_params=pltpu.CompilerParams(
            dimension_semantics=("parallel", "parallel", "arbitrary"), vmem_limit_bytes=VMEM_LIMIT),
        name="nsa_attention",
    )(proj, proj, proj, kv_cmp, proj, proj, cmp_bias, near, qconst, ovl)


def _nsa_layer(x, mod, norm_g, w_in, cmp_pos, cmp_w1, cmp_w2, rel_bias, w_out, final_g):
    t = x.shape[1]
    assert t // NSA_SLC_LEN <= 32 and NSA_COL_Z % (NSA_HPG * NSA_HEAD_DIM) == 0
    perm = _nsa_column_perm()
    w_in_p = jnp.where(perm[None, :] >= 0, w_in[:, np.maximum(perm, 0)], 0.0).astype(BF16)
    proj = _inproj(x, norm_g, mod, w_in_p, tm=256, tn=NSA_PROJ_W)
    kv_cmp = _nsa_compress(proj, cmp_pos, cmp_w1, cmp_w2)
    near, cmp_bias, qconst = _nsa_tables(rel_bias, t)
    o = _nsa_attn(proj, kv_cmp, near, cmp_bias, qconst, jnp.asarray(_overlap_t(t)))
    return _outproj(o, w_out.astype(BF16), x, mod, final_g, tm=512)


def kernel(x, c, ada_w, ada_b, norm_g, gdn_w_in, gdn_conv_w, gdn_a_log, gdn_dt_bias, gdn_norm_w, gdn_w_out,
           nsa_w_in, nsa_cmp_pos, nsa_cmp_w1, nsa_cmp_w2, nsa_w_out, rel_bias, final_g):
    bsz, t, d = x.shape
    mod = _modulation(c, ada_w, ada_b).reshape(ada_w.shape[0], bsz, 3, d)
    x = _gdn_layer(x, mod[0], norm_g[0], gdn_w_in[0], gdn_conv_w[0], gdn_a_log[0], gdn_dt_bias[0],
                   gdn_norm_w[0], gdn_w_out[0])
    return _nsa_layer(x, mod[1], norm_g[1], nsa_w_in[0], nsa_cmp_pos[0], nsa_cmp_w1[0], nsa_cmp_w2[0],
                      rel_bias, nsa_w_out[0], final_g)
```

```python
import functools
import math

import numpy as np
import jax
import jax.numpy as jnp
from jax import lax
from jax.experimental import pallas as pl
from jax.experimental.pallas import tpu as pltpu

F32 = jnp.float32
BF16 = jnp.bfloat16
HIGHEST = lax.Precision.HIGHEST

EPS = 1e-6
NEG_INF = -1e30
LANES = 128
VMEM_LIMIT = 56 * 1024 * 1024

GDN_QK_HEADS = 8
GDN_V_HEADS = 16
GDN_HEAD_DIM = 128
GDN_CONV = 4
GDN_CHUNK = 64
GDN_QK_W = GDN_QK_HEADS * GDN_HEAD_DIM
GDN_V_W = GDN_V_HEADS * GDN_HEAD_DIM
GDN_CONV_W = 2 * GDN_QK_W + GDN_V_W
GDN_TILE = 256


def _silu(x):
    return x * jax.nn.sigmoid(x)


def _dot(a, b, **kw):
    return jnp.dot(a, b, preferred_element_type=F32, **kw)


def _dot_nt(a, b, **kw):
    return lax.dot_general(a, b, (((1,), (1,)), ((), ())), preferred_element_type=F32, **kw)


def _mod_kernel(c_ref, w_ref, b_ref, o_ref):
    cond = _silu(c_ref[...])
    o_ref[0] = _dot(cond, w_ref[0], precision=HIGHEST) + b_ref[0]


def _modulation(c, ada_w, ada_b):
    depth, d, d3 = ada_w.shape
    bsz = c.shape[0]
    return pl.pallas_call(
        _mod_kernel,
        grid=(depth, d3 // d),
        in_specs=[
            pl.BlockSpec((bsz, d), lambda i, j: (0, 0)),
            pl.BlockSpec((1, d, d), lambda i, j: (i, 0, j)),
            pl.BlockSpec((1, 1, d), lambda i, j: (i, 0, j)),
        ],
        out_specs=pl.BlockSpec((1, bsz, d), lambda i, j: (i, 0, j)),
        out_shape=jax.ShapeDtypeStruct((depth, bsz, d3), F32),
        name="adaln_mod",
    )(c, ada_w, ada_b.reshape(depth, 1, d3))


def _inproj_kernel(x_ref, g_ref, mod_ref, w_ref, o_ref, *, tn):
    x = x_ref[0]
    m = mod_ref[0]
    y = x * lax.rsqrt(jnp.mean(x * x, axis=-1, keepdims=True) + EPS) * g_ref[...]
    h = (y * (1.0 + m[1:2]) + m[0:1]).astype(BF16)
    for j in range(w_ref.shape[1] // tn):
        o_ref[0, :, j * tn:(j + 1) * tn] = _dot(h, w_ref[:, j * tn:(j + 1) * tn])


def _inproj(x, g, mod, w, *, tm, tn):
    bsz, t, d = x.shape
    n = w.shape[1]
    assert t % tm == 0 and n % tn == 0
    return pl.pallas_call(
        functools.partial(_inproj_kernel, tn=tn),
        grid=(bsz, t // tm),
        in_specs=[
            pl.BlockSpec((1, tm, d), lambda b, i: (b, i, 0)),
            pl.BlockSpec((1, d), lambda b, i: (0, 0)),
            pl.BlockSpec((1, 3, d), lambda b, i: (b, 0, 0)),
            pl.BlockSpec((d, n), lambda b, i: (0, 0), pipeline_mode=pl.Buffered(1)),
        ],
        out_specs=pl.BlockSpec((1, tm, n), lambda b, i: (b, i, 0)),
        out_shape=jax.ShapeDtypeStruct((bsz, t, n), F32),
        compiler_params=pltpu.CompilerParams(
            dimension_semantics=("parallel", "parallel"), vmem_limit_bytes=VMEM_LIMIT),
        name="norm_mod_inproj",
    )(x, g.reshape(1, d), mod, w)


def _outproj_kernel(o_ref, w_ref, x_ref, mod_ref, *rest, final_norm):
    y = _dot(o_ref[0], w_ref[...])
    x = x_ref[0] + mod_ref[0][2:3] * y
    if final_norm:
        fg_ref, out_ref = rest
        x = x * lax.rsqrt(jnp.mean(x * x, axis=-1, keepdims=True) + EPS) * fg_ref[...]
    else:
        (out_ref,) = rest
    out_ref[0] = x


def _outproj(o, w, x, mod, final_g=None, *, tm):
    bsz, t, d = x.shape
    k = o.shape[-1]
    in_specs = [
        pl.BlockSpec((1, tm, k), lambda b, i: (b, i, 0)),
        pl.BlockSpec((k, d), lambda b, i: (0, 0), pipeline_mode=pl.Buffered(1)),
        pl.BlockSpec((1, tm, d), lambda b, i: (b, i, 0)),
        pl.BlockSpec((1, 3, d), lambda b, i: (b, 0, 0)),
    ]
    args = [o, w, x, mod]
    if final_g is not None:
        in_specs.append(pl.BlockSpec((1, d), lambda b, i: (0, 0)))
        args.append(final_g.reshape(1, d))
    return pl.pallas_call(
        functools.partial(_outproj_kernel, final_norm=final_g is not None),
        grid=(bsz, t // tm),
        in_specs=in_specs,
        out_specs=pl.BlockSpec((1, tm, d), lambda b, i: (b, i, 0)),
        out_shape=jax.ShapeDtypeStruct((bsz, t, d), F32),
        compiler_params=pltpu.CompilerParams(
            dimension_semantics=("parallel", "parallel"), vmem_limit_bytes=VMEM_LIMIT),
        name="outproj_residual",
    )(*args)


def _gdn_gates_kernel(ba_ref, alog_ref, dtb_ref, col_ref, row_ref):
    cs = GDN_CHUNK
    ba = ba_ref[0]
    lane = lax.broadcasted_iota(jnp.int32, ba.shape, 1)
    g = -jnp.exp(alog_ref[...]) * jax.nn.softplus(ba + dtb_ref[...])
    vals = jnp.where(lane < GDN_V_HEADS, jax.nn.sigmoid(ba), g)
    r = lax.broadcasted_iota(jnp.int32, (cs, cs), 0)
    c = lax.broadcasted_iota(jnp.int32, (cs, cs), 1)
    tri = (r >= c).astype(F32)
    is_beta = lax.broadcasted_iota(jnp.int32, (cs, LANES), 1) < GDN_V_HEADS
    for n in range(ba.shape[0] // cs):
        v = vals[n * cs:(n + 1) * cs]
        cum = _dot(tri, v, precision=HIGHEST)
        out = jnp.where(is_beta, v, cum)
        col_ref[0, n * cs:(n + 1) * cs, :] = out
        row_ref[0, n] = out.T[:2 * GDN_V_HEADS, :]


def _gdn_gates(proj, a_log, dt_bias, *, tt=512):
    bsz, t, n = proj.shape
    nc = t // GDN_CHUNK
    pad = lambda u: jnp.zeros((1, LANES), F32).at[0, GDN_V_HEADS:2 * GDN_V_HEADS].set(u)
    ba_blk = (n - LANES) // LANES
    return pl.pallas_call(
        _gdn_gates_kernel,
        grid=(bsz, t // tt),
        in_specs=[
            pl.BlockSpec((1, tt, LANES), lambda b, i: (b, i, ba_blk)),
            pl.BlockSpec((1, LANES), lambda b, i: (0, 0)),
            pl.BlockSpec((1, LANES), lambda b, i: (0, 0)),
        ],
        out_specs=[
            pl.BlockSpec((1, tt, LANES), lambda b, i: (b, i, 0)),
            pl.BlockSpec((1, tt // GDN_CHUNK, 2 * GDN_V_HEADS, GDN_CHUNK), lambda b, i: (b, i, 0, 0)),
        ],
        out_shape=[
            jax.ShapeDtypeStruct((bsz, t, LANES), F32),
            jax.ShapeDtypeStruct((bsz, nc, 2 * GDN_V_HEADS, GDN_CHUNK), F32),
        ],
        name="gdn_gates",
    )(proj, pad(a_log), pad(dt_bias))


TRI_BASE = 8


def _tri_inverse_many(mats):
    n = mats[0].shape[0]
    r = lax.broadcasted_iota(jnp.int32, (n, n), 0)
    c = lax.broadcasted_iota(jnp.int32, (n, n), 1)
    same = lambda s: (r // s) == (c // s)
    mm = lambda a, b: _dot(a.astype(BF16), b.astype(BF16))
    diag = [jnp.where(same(TRI_BASE), a, 0.0) for a in mats]
    inv = [jnp.where(r == c, 1.0, 0.0) - d for d in diag]
    pw = diag
    k = 1
    while 2 * k < TRI_BASE:
        pw = [mm(m, m) for m in pw]
        inv = [p + mm(p, m) for p, m in zip(inv, pw)]
        k *= 2
    s = TRI_BASE
    while s < n:
        sub = same(2 * s) & jnp.logical_not(same(s))
        left = [mm(p, jnp.where(sub, a, 0.0)) for p, a in zip(inv, mats)]
        inv = [p - mm(l, p) for p, l in zip(inv, left)]
        s *= 2
    return inv


def _gdn_chunk_kernel(q_ref, k_ref, v_ref, z_ref, col_ref, row_ref, wq_ref, wk_ref, wv_ref, nw_ref,
                      o_ref, s_ref, qx_ref, kx_ref, vx_ref):
    cs, dh, nh = GDN_CHUNK, GDN_HEAD_DIM, GDN_V_HEADS
    tt = q_ref.shape[1]
    ncb = tt // cs
    hq = pl.program_id(1)
    ti = pl.program_id(2)

    @pl.when(ti == 0)
    def _():
        s_ref[...] = jnp.zeros_like(s_ref)
        qx_ref[0:8, :] = jnp.zeros((8, dh), F32)
        kx_ref[0:8, :] = jnp.zeros((8, dh), F32)
        vx_ref[0:8, :] = jnp.zeros((8, 2 * dh), F32)

    def conv_silu(x_ref, ext_ref, w_ref):
        ext_ref[8:8 + tt, :] = x_ref[0]
        w = w_ref[...]
        acc = w[0:1] * ext_ref[5:5 + tt, :]
        for j in range(1, GDN_CONV):
            acc = acc + w[j:j + 1] * ext_ref[5 + j:5 + j + tt, :]
        ext_ref[0:8, :] = ext_ref[tt:tt + 8, :]
        return _silu(acc)

    def l2n(x):
        return x * lax.rsqrt(jnp.sum(x * x, axis=-1, keepdims=True) + EPS)

    q = l2n(conv_silu(q_ref, qx_ref, wq_ref)) * (dh ** -0.5)
    k = l2n(conv_silu(k_ref, kx_ref, wk_ref))
    v = conv_silu(v_ref, vx_ref, wv_ref)

    r = lax.broadcasted_iota(jnp.int32, (cs, cs), 0)
    c = lax.broadcasted_iota(jnp.int32, (cs, cs), 1)
    lower = r >= c
    strict = r > c
    lane = lax.broadcasted_iota(jnp.int32, (tt, LANES), 1)
    colv = col_ref[0]
    heads = [2 * hq + e for e in range(2)]
    beta = [jnp.sum(jnp.where(lane == h, colv, 0.0), axis=1, keepdims=True) for h in heads]
    gc = [jnp.sum(jnp.where(lane == nh + h, colv, 0.0), axis=1, keepdims=True) for h in heads]
    egc = [jnp.exp(x) for x in gc]
    chunks = [slice(n * cs, (n + 1) * cs) for n in range(ncb)]

    k16 = k.astype(BF16)
    q16 = q.astype(BF16)
    kk = [_dot_nt(k16[sl], k16[sl]) for sl in chunks]
    qk = [_dot_nt(q16[sl], k16[sl]) for sl in chunks]
    a_mats, qkd, glast, kdt, rhs = [], [], [], [], []
    for e in range(2):
        kbeta = k * beta[e]
        wrhs = kbeta * egc[e]
        vrhs = v[:, e * dh:(e + 1) * dh] * beta[e]
        for n, sl in enumerate(chunks):
            gc_row = row_ref[0, n, pl.ds(nh + heads[e], 1), :]
            g_last = gc_row[:, cs - 1:cs]
            decay = jnp.exp(jnp.where(lower, gc[e][sl] - gc_row, -jnp.inf))
            a_mats.append(jnp.where(strict, beta[e][sl] * kk[n] * decay, 0.0))
            qkd.append((qk[n] * decay).astype(BF16))
            glast.append(jnp.exp(g_last))
            kdt.append((k[sl] * jnp.exp(g_last - gc[e][sl])).T.astype(BF16))
            rhs.append(jnp.concatenate([vrhs[sl], wrhs[sl]], axis=1).astype(BF16))
    t_mats = _tri_inverse_many(a_mats)
    uw = [_dot(tm.astype(BF16), rh) for tm, rh in zip(t_mats, rhs)]

    nw = nw_ref[...]
    state = [s_ref[e] for e in range(2)]
    for n, sl in enumerate(chunks):
        for e in range(2):
            j = e * ncb + n
            u, w = uw[j][:, :dh], uw[j][:, dh:]
            q_dec = q[sl] * egc[e][sl]
            s16 = state[e].astype(BF16)
            ws = _dot(jnp.concatenate([w, q_dec], axis=0).astype(BF16), s16)
            v16 = (u - ws[:cs]).astype(BF16)
            o = ws[cs:] + _dot(qkd[j], v16)
            state[e] = state[e] * glast[j] + _dot(kdt[j], v16)
            o = o * lax.rsqrt(jnp.mean(o * o, axis=-1, keepdims=True) + EPS) * nw
            o = o * _silu(z_ref[0, sl, e * dh:(e + 1) * dh])
            o_ref[0, sl, e * dh:(e + 1) * dh] = o.astype(o_ref.dtype)
    for e in range(2):
        s_ref[e] = state[e]


def _gdn_chunk(proj, col, row, conv_w, norm_w, *, tt=GDN_TILE):
    bsz, t, _ = proj.shape
    dh = GDN_HEAD_DIM
    qk_blocks = GDN_QK_W // dh
    v_blk0 = 2 * GDN_QK_W // (2 * dh)
    z_blk0 = GDN_CONV_W // (2 * dh)
    ncb = tt // GDN_CHUNK
    return pl.pallas_call(
        _gdn_chunk_kernel,
        grid=(bsz, GDN_QK_HEADS, t // tt),
        in_specs=[
            pl.BlockSpec((1, tt, dh), lambda b, h, i: (b, i, h)),
            pl.BlockSpec((1, tt, dh), lambda b, h, i: (b, i, qk_blocks + h)),
            pl.BlockSpec((1, tt, 2 * dh), lambda b, h, i: (b, i, v_blk0 + h)),
            pl.BlockSpec((1, tt, 2 * dh), lambda b, h, i: (b, i, z_blk0 + h)),
            pl.BlockSpec((1, tt, LANES), lambda b, h, i: (b, i, 0)),
            pl.BlockSpec((1, ncb, 2 * GDN_V_HEADS, GDN_CHUNK), lambda b, h, i: (b, i, 0, 0)),
            pl.BlockSpec((GDN_CONV, dh), lambda b, h, i: (0, h)),
            pl.BlockSpec((GDN_CONV, dh), lambda b, h, i: (0, qk_blocks + h)),
            pl.BlockSpec((GDN_CONV, 2 * dh), lambda b, h, i: (0, v_blk0 + h)),
            pl.BlockSpec((1, dh), lambda b, h, i: (0, 0)),
        ],
        out_specs=pl.BlockSpec((1, tt, 2 * dh), lambda b, h, i: (b, i, h)),
        out_shape=jax.ShapeDtypeStruct((bsz, t, GDN_V_W), BF16),
        scratch_shapes=[
            pltpu.VMEM((2, dh, dh), F32),
            pltpu.VMEM((tt + 8, dh), F32),
            pltpu.VMEM((tt + 8, dh), F32),
            pltpu.VMEM((tt + 8, 2 * dh), F32),
        ],
        compiler_params=pltpu.CompilerParams(
            dimension_semantics=("parallel", "parallel", "arbitrary"), vmem_limit_bytes=VMEM_LIMIT),
        name="gdn_chunk_scan",
    )(proj, proj, proj, proj, col, row, conv_w, conv_w, conv_w, norm_w.reshape(1, dh))


def _gdn_layer(x, mod, norm_g, w_in, conv_w, a_log, dt_bias, norm_w, w_out):
    n_in = w_in.shape[1]
    n_pad = -(-n_in // (7 * LANES)) * (7 * LANES)
    w_in_p = jnp.pad(w_in, ((0, 0), (0, n_pad - n_in))).astype(BF16)
    proj = _inproj(x, norm_g, mod, w_in_p, tm=256, tn=7 * LANES)
    col, row = _gdn_gates(proj, a_log, dt_bias)
    o = _gdn_chunk(proj, col, row, conv_w, norm_w)
    return _outproj(o, w_out.astype(BF16), x, mod, tm=512)


NSA_HEADS = 16
NSA_GROUPS = 4
NSA_HPG = NSA_HEADS // NSA_GROUPS
NSA_HEAD_DIM = 64
NSA_CMP_LEN = 32
NSA_CMP_STRIDE = 16
NSA_SLC_LEN = 64
NSA_TOP_K = 8
NSA_WINDOW = 512
NSA_QBLOCK = 128
NSA_Q_W = NSA_HEADS * NSA_HEAD_DIM
NSA_KV_W = NSA_GROUPS * NSA_HEAD_DIM
REL_BUCKETS = 32
REL_MAX_DIST = 128
FEAT_LANE0 = NSA_HEAD_DIM
CONST_LANE0 = FEAT_LANE0 + 32
NSA_COL_Q = 0
NSA_COL_CMP = NSA_Q_W
NSA_COL_SEL = NSA_COL_CMP + 2 * NSA_KV_W
NSA_COL_WIN = NSA_COL_SEL + 2 * NSA_KV_W
NSA_COL_Z = NSA_COL_WIN + 2 * NSA_KV_W
NSA_COL_GATE = NSA_COL_Z + NSA_Q_W
NSA_PROJ_W = NSA_COL_GATE + LANES


def _nsa_column_perm():
    g, dh = NSA_GROUPS, NSA_HEAD_DIM
    kv0 = NSA_Q_W
    cols = list(range(NSA_Q_W))
    cols += [kv0 + i for i in range(2 * NSA_KV_W)]
    for br in (1, 2):
        for gi in range(g):
            cols += [kv0 + (2 * br) * NSA_KV_W + gi * dh + d for d in range(dh)]
            cols += [kv0 + (2 * br + 1) * NSA_KV_W + gi * dh + d for d in range(dh)]
    gate0 = kv0 + 6 * NSA_KV_W
    cols += [gate0 + 3 * NSA_HEADS + i for i in range(NSA_Q_W)]
    cols += [gate0 + i for i in range(3 * NSA_HEADS)] + [-1] * (LANES - 3 * NSA_HEADS)
    assert len(cols) == NSA_PROJ_W
    return np.asarray(cols, np.int32)


def _rel_bucket_table(n):
    d = np.arange(n)
    max_exact = REL_BUCKETS // 2
    nf = np.maximum(d, 1).astype(np.float64)
    large = max_exact + (np.log(nf / max_exact) / math.log(REL_MAX_DIST / max_exact)
                         * (REL_BUCKETS - max_exact)).astype(np.int32)
    large = np.minimum(large, REL_BUCKETS - 1)
    return np.where(d < max_exact, d, large).astype(np.int32)


def _nsa_tables(rel_bias, t):
    qb = NSA_QBLOCK
    bucket = _rel_bucket_table(t)
    assert np.all(bucket[qb + 1:] == REL_BUCKETS - 1)
    bvec = rel_bias[bucket].T
    far = rel_bias[REL_BUCKETS - 1]
    far_hi = far.astype(BF16)
    far_lo = (far - far_hi.astype(F32)).astype(BF16)
    far_sum = far_hi.astype(F32) + far_lo.astype(F32)
    r = np.arange(qb)[:, None]
    c = np.arange(qb)[None, :]
    d0 = r - c
    t0 = jnp.where(d0 >= 0, bvec[:, np.maximum(d0, 0)] - far_sum[:, None, None], NEG_INF)
    t1 = bvec[:, qb + r - c] - far_sum[:, None, None]
    g, hpg = NSA_GROUPS, NSA_HPG
    near = jnp.stack([t1, t0], axis=1).reshape(g, hpg, 2, qb, qb).transpose(0, 2, 4, 1, 3).reshape(g, 2, qb, hpg * qb)
    nb = t // NSA_CMP_STRIDE
    per_tile = qb // NSA_CMP_STRIDE
    width = 2 * per_tile
    far_d = per_tile * NSA_CMP_STRIDE - (NSA_CMP_LEN - 1) + NSA_CMP_STRIDE
    assert np.all(bucket[far_d:] == REL_BUCKETS - 1)
    dm = r - NSA_CMP_STRIDE * np.arange(width)[None, :] + far_d
    band = jnp.where(dm >= 0, bvec[:, np.maximum(dm, 0)], NEG_INF)
    tiles = []
    for i in range(t // qb):
        j0 = per_tile * i - (per_tile + 1)
        lo, hi = max(j0, 0), min(j0 + width, nb)
        tiles.append(jnp.concatenate([
            jnp.broadcast_to(far[:, None, None], (NSA_HEADS, qb, lo)),
            band[:, :, lo - j0:hi - j0],
            jnp.full((NSA_HEADS, qb, nb - hi), NEG_INF, F32)], axis=2))
    cmp_bias = jnp.stack(tiles, axis=0).reshape(t // qb, g, hpg, qb, nb).transpose(0, 1, 4, 2, 3)
    cmp_bias = cmp_bias.reshape(t // qb, g, nb, hpg * qb)
    qconst = jnp.zeros((g, 8, hpg, qb), F32)
    qconst = qconst.at[:, 0].set(jnp.broadcast_to(far_hi.astype(F32).reshape(g, hpg, 1), (g, hpg, qb)))
    qconst = qconst.at[:, 1].set(jnp.broadcast_to(far_lo.astype(F32).reshape(g, hpg, 1), (g, hpg, qb)))
    return near, cmp_bias, qconst.reshape(g, 8, hpg * qb)


def _overlap_t(t):
    n_cmp = (t - NSA_CMP_LEN) // NSA_CMP_STRIDE + 1
    n_slc = t // NSA_SLC_LEN
    c_start = np.arange(n_cmp)[:, None] * NSA_CMP_STRIDE
    s_start = np.arange(n_slc)[None, :] * NSA_SLC_LEN
    ov = np.clip(np.minimum(c_start + NSA_CMP_LEN, s_start + NSA_SLC_LEN) - np.maximum(c_start, s_start), 0, None)
    ov = ov.astype(np.float32) / NSA_CMP_LEN
    out = np.zeros((32, t // NSA_CMP_STRIDE), np.float32)
    out[:n_slc, :n_cmp] = ov.T
    return out


def _nsa_compress_kernel(x_ref, pos_ref, w1_ref, w2_ref, o_ref, xs_ref):
    t = x_ref.shape[1]
    nb = t // NSA_CMP_STRIDE
    nlt = xs_ref.shape[0]
    for c in range(nlt):
        xs_ref[c, 0:t, :] = x_ref[0, :, c * LANES:(c + 1) * LANES]
        xs_ref[c, t:t + NSA_CMP_STRIDE, :] = jnp.zeros((NSA_CMP_STRIDE, LANES), F32)
    acc = jnp.zeros((nb, w1_ref.shape[2]), F32)
    for l in range(NSA_CMP_LEN):
        xl = jnp.concatenate([xs_ref[c, pl.ds(l, nb, stride=NSA_CMP_STRIDE), :] for c in range(nlt)], axis=1)
        xl = xl + pos_ref[l:l + 1, :]
        acc = acc + _dot(xl.astype(BF16), w1_ref[l])
    hid = _silu(acc).astype(BF16)
    res = _dot(hid, w2_ref[...])
    for g in range(NSA_GROUPS):
        o_ref[0, g] = res[:, g * LANES:(g + 1) * LANES]


def _nsa_compress(proj, cmp_pos, cmp_w1, cmp_w2):
    bsz, t, _ = proj.shape
    g, dh = NSA_GROUPS, NSA_HEAD_DIM
    nb = t // NSA_CMP_STRIDE
    w = 2 * NSA_KV_W
    eye = jnp.eye(g, dtype=F32)
    w1 = cmp_w1.reshape(2, NSA_CMP_LEN, dh, dh)
    eye2 = jnp.eye(2, dtype=F32)
    w1c = (w1.transpose(1, 0, 2, 3).astype(BF16)[:, :, None, :, None, None, :]
           * eye.astype(BF16)[None, None, :, None, :, None, None]
           * eye2.astype(BF16)[None, :, None, None, None, :, None]).reshape(NSA_CMP_LEN, w, w)
    w2c = (cmp_w2[None, :, :, None, None, :] * eye[:, None, None, :, None, None]
           * eye2[None, :, None, None, :, None]).reshape(w, w).astype(BF16)
    pos = jnp.broadcast_to(cmp_pos[:, :, None, :], (2, NSA_CMP_LEN, g, dh)).transpose(1, 0, 2, 3).reshape(NSA_CMP_LEN, w)
    return pl.pallas_call(
        _nsa_compress_kernel,
        grid=(bsz,),
        in_specs=[
            pl.BlockSpec((1, t, w), lambda b: (b, 0, NSA_COL_CMP // w)),
            pl.BlockSpec((NSA_CMP_LEN, w), lambda b: (0, 0)),
            pl.BlockSpec((NSA_CMP_LEN, w, w), lambda b: (0, 0, 0), pipeline_mode=pl.Buffered(1)),
            pl.BlockSpec((w, w), lambda b: (0, 0)),
        ],
        out_specs=pl.BlockSpec((1, g, nb, LANES), lambda b: (b, 0, 0, 0)),
        out_shape=jax.ShapeDtypeStruct((bsz, g, nb, LANES), F32),
        scratch_shapes=[pltpu.VMEM((w // LANES, t + NSA_CMP_STRIDE, LANES), F32)],
        compiler_params=pltpu.CompilerParams(dimension_semantics=("parallel",), vmem_limit_bytes=VMEM_LIMIT),
        name="nsa_compress",
    )(proj, pos, w1c, w2c)


def _nsa_attn_kernel(q_ref, kvs_ref, kvw_ref, kvc_ref, gate_ref, z_ref, cb_ref, near_ref, qc_ref, ovl_ref,
                     o_ref, ks_ref, vs_ref, kw_ref, vw_ref, gt_ref, ms_ref, accs_ref, mw_ref, accw_ref):
    qb, dh, hpg = NSA_QBLOCK, NSA_HEAD_DIM, NSA_HPG
    t = kvs_ref.shape[1]
    nblk = t // NSA_SLC_LEN
    cols = hpg * qb
    g = pl.program_id(1)
    i = pl.program_id(2)

    @pl.when(i == 0)
    def _():
        tok = lax.broadcasted_iota(jnp.int32, (t, LANES), 0)
        ln = lax.broadcasted_iota(jnp.int32, (t, LANES), 1)
        const = jnp.where((ln == CONST_LANE0) | (ln == CONST_LANE0 + 1), 1.0, 0.0)
        onehot = jnp.where(ln - FEAT_LANE0 == tok // NSA_SLC_LEN, 1.0, 0.0)
        row = lax.broadcasted_iota(jnp.int32, (LANES, t), 0)
        kvs = kvs_ref[0]
        kvw = kvw_ref[0]
        ks_ref[...] = jnp.where(ln < dh, kvs, onehot + const).astype(BF16)
        kw_ref[...] = jnp.where(ln < dh, kvw, const).astype(BF16)
        vs_ref[...] = jnp.where(row == 0, 1.0, kvs.T).astype(BF16)
        vw_ref[...] = jnp.where(row == 0, 1.0, kvw.T).astype(BF16)

    q_t = (q_ref[0] * (dh ** -0.5)).T
    q_heads = jnp.concatenate([q_t[hh * dh:(hh + 1) * dh] for hh in range(hpg)], axis=1)

    kvc = kvc_ref[0, 0]
    lane_k = lax.broadcasted_iota(jnp.int32, kvc.shape, 1)
    kc16 = jnp.where(lane_k < dh, kvc, 0.0).astype(BF16)
    qc16 = jnp.concatenate([q_heads, jnp.zeros((LANES - dh, cols), F32)], axis=0).astype(BF16)
    s = _dot(kc16, qc16) + cb_ref[0, 0]
    s = jnp.exp(s - jnp.max(s, axis=0, keepdims=True))
    p = s / jnp.sum(s, axis=0, keepdims=True)
    tq_lane = i * qb + lax.broadcasted_iota(jnp.int32, (1, cols), 1) % qb
    p16 = (p * (tq_lane >= NSA_CMP_LEN - 1).astype(F32)).astype(BF16)
    o_cmp = _dot(kvc.T.astype(BF16), p16)
    ovl = ovl_ref[...].astype(BF16)
    imp = _dot(ovl, p16[:, 0:qb])
    for hh in range(1, hpg):
        imp = imp + _dot(ovl, p16[:, hh * qb:(hh + 1) * qb])

    blk = lax.broadcasted_iota(jnp.int32, (32, qb), 0)
    tq = i * qb + lax.broadcasted_iota(jnp.int32, (32, qb), 1)
    cur = tq // NSA_SLC_LEN
    forced = (blk == 0) | (blk == cur) | (blk == cur - 1)
    val = jnp.where(forced, jnp.inf, jnp.where(blk * NSA_SLC_LEN <= tq, imp, -jnp.inf))
    cnt = jnp.zeros((32, qb), jnp.int32)
    for s2 in range(nblk):
        other = val[s2:s2 + 1, :]
        cnt = cnt + ((other > val) | ((other == val) & (s2 < blk))).astype(jnp.int32)
    feat = jnp.where((cnt < min(NSA_TOP_K, nblk)) & (blk < nblk), 0.0, NEG_INF)

    qa = jnp.concatenate([q_heads, jnp.concatenate([feat] * hpg, axis=1), qc_ref[0],
                          jnp.zeros((LANES - CONST_LANE0 - 8, cols), F32)], axis=0).astype(BF16)

    def scores(branch, start, nk, bias):
        sc = _dot(branch[0][pl.ds(pl.multiple_of(start, qb), nk), :], qa)
        return sc if bias is None else sc + bias

    def update(branch, start, nk, sc):
        _, vt_ref, m_ref, acc_ref = branch
        m_old = m_ref[...]
        m_new = jnp.maximum(m_old, jnp.max(sc, axis=0, keepdims=True))
        alpha = jnp.exp(m_old - m_new)
        pe = jnp.exp(sc - m_new).astype(BF16)
        acc_ref[...] = alpha * acc_ref[...] + _dot(vt_ref[:, pl.ds(pl.multiple_of(start, qb), nk)], pe)
        m_ref[...] = m_new

    def attend_all(*steps):
        sc = scores(*steps[0])
        for prev, nxt in zip(steps[:-1], steps[1:]):
            sc_next = scores(*nxt)
            update(*prev[:3], sc)
            sc = sc_next
        update(*steps[-1][:3], sc)

    sel = (ks_ref, vs_ref, ms_ref, accs_ref)
    win = (kw_ref, vw_ref, mw_ref, accw_ref)
    for m_ref, acc_ref in ((ms_ref, accs_ref), (mw_ref, accw_ref)):
        m_ref[...] = jnp.full(m_ref.shape, NEG_INF, F32)
        acc_ref[...] = jnp.zeros(acc_ref.shape, F32)

    n_far = jnp.maximum(i - 1, 0)
    quads = n_far // 4

    def sel_body(kq, carry):
        attend_all((sel, kq * (4 * qb), 2 * qb, None), (sel, kq * (4 * qb) + 2 * qb, 2 * qb, None))
        return carry

    lax.fori_loop(0, quads, sel_body, 0)

    @pl.when(n_far % 4 >= 2)
    def _():
        attend_all((sel, quads * (4 * qb), 2 * qb, None))

    @pl.when(n_far % 2 == 1)
    def _():
        attend_all((sel, (n_far - 1) * qb, qb, None))

    nwt = NSA_WINDOW // qb
    assert nwt == 4
    kk = lax.broadcasted_iota(jnp.int32, (qb, cols), 0)
    rr = lax.broadcasted_iota(jnp.int32, (qb, cols), 1) % qb
    near_both = lambda: near_ref[0].reshape(2 * qb, cols)

    @pl.when(i >= nwt)
    def _():
        attend_all((sel, (i - 1) * qb, 2 * qb, near_both()),
                   (win, (i - nwt) * qb, qb, jnp.where(rr < kk, 0.0, NEG_INF)),
                   (win, (i - 3) * qb, 2 * qb, None),
                   (win, (i - 1) * qb, 2 * qb, near_both()))

    @pl.when(i == 3)
    def _():
        attend_all((sel, 2 * qb, 2 * qb, near_both()), (win, 0, 2 * qb, None), (win, 2 * qb, 2 * qb, near_both()))

    @pl.when(i == 2)
    def _():
        attend_all((sel, qb, 2 * qb, near_both()), (win, 0, qb, None), (win, qb, 2 * qb, near_both()))

    @pl.when(i == 1)
    def _():
        attend_all((sel, 0, 2 * qb, near_both()), (win, 0, 2 * qb, near_both()))

    @pl.when(i == 0)
    def _():
        attend_all((sel, 0, qb, near_ref[0, 1]), (win, 0, qb, near_ref[0, 1]))

    def finish(acc_ref):
        acc = acc_ref[...]
        return acc[dh:] / acc[0:1]

    o_slc = finish(accs_ref)
    o_win = finish(accw_ref)

    gt_ref[...] = jax.nn.sigmoid(gate_ref[0]).T
    outs = []
    for hh in range(hpg):
        sl = slice(hh * qb, (hh + 1) * qb)
        base = (g * hpg + hh) * 3
        gate = [gt_ref[pl.ds(base + br, 1), :] for br in range(3)]
        outs.append(gate[0] * o_cmp[dh:, sl] + gate[1] * o_slc[:, sl] + gate[2] * o_win[:, sl])
    out = jnp.concatenate(outs, axis=0).T
    o_ref[0] = (out * _silu(z_ref[0])).astype(o_ref.dtype)


def _nsa_attn(proj, kv_cmp, near, cmp_bias, qconst, ovl):
    bsz, t, _ = proj.shape
    qb, hpg = NSA_QBLOCK, NSA_HPG
    gw = hpg * NSA_HEAD_DIM
    nb = t // NSA_CMP_STRIDE
    cols = hpg * qb
    return pl.pallas_call(
        _nsa_attn_kernel,
        grid=(bsz, NSA_GROUPS, t // qb),
        in_specs=[
            pl.BlockSpec((1, qb, gw), lambda b, g, i: (b, i, NSA_COL_Q // gw + g)),
            pl.BlockSpec((1, t, LANES), lambda b, g, i: (b, 0, NSA_COL_SEL // LANES + g)),
            pl.BlockSpec((1, t, LANES), lambda b, g, i: (b, 0, NSA_COL_WIN // LANES + g)),
            pl.BlockSpec((1, 1, nb, LANES), lambda b, g, i: (b, g, 0, 0)),
            pl.BlockSpec((1, qb, LANES), lambda b, g, i: (b, i, NSA_COL_GATE // LANES)),
            pl.BlockSpec((1, qb, gw), lambda b, g, i: (b, i, NSA_COL_Z // gw + g)),
            pl.BlockSpec((1, 1, nb, cols), lambda b, g, i: (i, g, 0, 0)),
            pl.BlockSpec((1, 2, qb, cols), lambda b, g, i: (g, 0, 0, 0)),
            pl.BlockSpec((1, 8, cols), lambda b, g, i: (g, 0, 0)),
            pl.BlockSpec((32, nb), lambda b, g, i: (0, 0)),
        ],
        out_specs=pl.BlockSpec((1, qb, gw), lambda b, g, i: (b, i, g)),
        out_shape=jax.ShapeDtypeStruct((bsz, t, NSA_Q_W), BF16),
        scratch_shapes=[
            pltpu.VMEM((t, LANES), BF16), pltpu.VMEM((LANES, t), BF16),
            pltpu.VMEM((t, LANES), BF16), pltpu.VMEM((LANES, t), BF16),
            pltpu.VMEM((LANES, qb), F32),
            pltpu.VMEM((1, cols), F32), pltpu.VMEM((LANES, cols), F32),
            pltpu.VMEM((1, cols), F32), pltpu.VMEM((LANES, cols), F32),
        ],
        compiler_params=pltpu.CompilerParams(
            dimension_semantics=("parallel", "parallel", "arbitrary"), vmem_limit_bytes=VMEM_LIMIT),
        name="nsa_attention",
    )(proj, proj, proj, kv_cmp, proj, proj, cmp_bias, near, qconst, ovl)


def _nsa_layer(x, mod, norm_g, w_in, cmp_pos, cmp_w1, cmp_w2, rel_bias, w_out, final_g):
    t = x.shape[1]
    assert t // NSA_SLC_LEN <= 32 and NSA_COL_Z % (NSA_HPG * NSA_HEAD_DIM) == 0
    perm = _nsa_column_perm()
    cuts = [0] + [j for j in range(1, len(perm)) if perm[j] != perm[j - 1] + (perm[j - 1] >= 0)] + [len(perm)]
    runs = [(int(perm[a]), b - a) for a, b in zip(cuts[:-1], cuts[1:])]
    w16 = w_in.astype(BF16)
    w_in_p = jnp.concatenate([w16[:, s:s + n] if s >= 0 else jnp.zeros((w_in.shape[0], n), BF16)
                              for s, n in runs], axis=1)
    proj = _inproj(x, norm_g, mod, w_in_p, tm=256, tn=NSA_PROJ_W)
    kv_cmp = _nsa_compress(proj, cmp_pos, cmp_w1, cmp_w2)
    near, cmp_bias, qconst = _nsa_tables(rel_bias, t)
    o = _nsa_attn(proj, kv_cmp, near, cmp_bias, qconst, jnp.asarray(_overlap_t(t)))
    return _outproj(o, w_out.astype(BF16), x, mod, final_g, tm=512)


def kernel(x, c, ada_w, ada_b, norm_g, gdn_w_in, gdn_conv_w, gdn_a_log, gdn_dt_bias, gdn_norm_w, gdn_w_out,
           nsa_w_in, nsa_cmp_pos, nsa_cmp_w1, nsa_cmp_w2, nsa_w_out, rel_bias, final_g):
    bsz, t, d = x.shape
    mod = _modulation(c, ada_w, ada_b).reshape(ada_w.shape[0], bsz, 3, d)
    x = _gdn_layer(x, mod[0], norm_g[0], gdn_w_in[0], gdn_conv_w[0], gdn_a_log[0], gdn_dt_bias[0],
                   gdn_norm_w[0], gdn_w_out[0])
    return _nsa_layer(x, mod[1], norm_g[1], nsa_w_in[0], nsa_cmp_pos[0], nsa_cmp_w1[0], nsa_cmp_w2[0],
                      rel_bias, nsa_w_out[0], final_g)
```

```python
import functools
import math

import numpy as np
import jax
import jax.numpy as jnp
from jax import lax
from jax.experimental import pallas as pl
from jax.experimental.pallas import tpu as pltpu

F32 = jnp.float32
BF16 = jnp.bfloat16
HIGHEST = lax.Precision.HIGHEST

EPS = 1e-6
NEG_INF = -1e30
LANES = 128
VMEM_LIMIT = 56 * 1024 * 1024

GDN_QK_HEADS = 8
GDN_V_HEADS = 16
GDN_HEAD_DIM = 128
GDN_CONV = 4
GDN_CHUNK = 64
GDN_QK_W = GDN_QK_HEADS * GDN_HEAD_DIM
GDN_V_W = GDN_V_HEADS * GDN_HEAD_DIM
GDN_CONV_W = 2 * GDN_QK_W + GDN_V_W
GDN_TILE = 256
GDN_HEADS_PER_STEP = 4


def _silu(x):
    return x * jax.nn.sigmoid(x)


def _dot(a, b, **kw):
    return jnp.dot(a, b, preferred_element_type=F32, **kw)


def _dot_nt(a, b, **kw):
    return lax.dot_general(a, b, (((1,), (1,)), ((), ())), preferred_element_type=F32, **kw)


def _mod_kernel(c_ref, w_ref, b_ref, o_ref):
    cond = _silu(c_ref[...])
    o_ref[0] = _dot(cond, w_ref[0], precision=HIGHEST) + b_ref[0]


def _modulation(c, ada_w, ada_b):
    depth, d, d3 = ada_w.shape
    bsz = c.shape[0]
    return pl.pallas_call(
        _mod_kernel,
        grid=(depth, d3 // d),
        in_specs=[
            pl.BlockSpec((bsz, d), lambda i, j: (0, 0)),
            pl.BlockSpec((1, d, d), lambda i, j: (i, 0, j)),
            pl.BlockSpec((1, 1, d), lambda i, j: (i, 0, j)),
        ],
        out_specs=pl.BlockSpec((1, bsz, d), lambda i, j: (i, 0, j)),
        out_shape=jax.ShapeDtypeStruct((depth, bsz, d3), F32),
        name="adaln_mod",
    )(c, ada_w, ada_b.reshape(depth, 1, d3))


def _inproj_kernel(x_ref, g_ref, mod_ref, w_ref, o_ref, *, tn):
    x = x_ref[0]
    m = mod_ref[0]
    y = x * lax.rsqrt(jnp.mean(x * x, axis=-1, keepdims=True) + EPS) * g_ref[...]
    h = (y * (1.0 + m[1:2]) + m[0:1]).astype(BF16)
    for j in range(w_ref.shape[1] // tn):
        o_ref[0, :, j * tn:(j + 1) * tn] = _dot(h, w_ref[:, j * tn:(j + 1) * tn])


def _inproj(x, g, mod, w, *, tm, tn):
    bsz, t, d = x.shape
    n = w.shape[1]
    assert t % tm == 0 and n % tn == 0
    return pl.pallas_call(
        functools.partial(_inproj_kernel, tn=tn),
        grid=(bsz, t // tm),
        in_specs=[
            pl.BlockSpec((1, tm, d), lambda b, i: (b, i, 0)),
            pl.BlockSpec((1, d), lambda b, i: (0, 0)),
            pl.BlockSpec((1, 3, d), lambda b, i: (b, 0, 0)),
            pl.BlockSpec((d, n), lambda b, i: (0, 0), pipeline_mode=pl.Buffered(1)),
        ],
        out_specs=pl.BlockSpec((1, tm, n), lambda b, i: (b, i, 0)),
        out_shape=jax.ShapeDtypeStruct((bsz, t, n), F32),
        compiler_params=pltpu.CompilerParams(
            dimension_semantics=("parallel", "parallel"), vmem_limit_bytes=VMEM_LIMIT),
        name="norm_mod_inproj",
    )(x, g.reshape(1, d), mod, w)


def _outproj_kernel(o_ref, w_ref, x_ref, mod_ref, *rest, final_norm):
    y = _dot(o_ref[0], w_ref[...])
    x = x_ref[0] + mod_ref[0][2:3] * y
    if final_norm:
        fg_ref, out_ref = rest
        x = x * lax.rsqrt(jnp.mean(x * x, axis=-1, keepdims=True) + EPS) * fg_ref[...]
    else:
        (out_ref,) = rest
    out_ref[0] = x


def _outproj(o, w, x, mod, final_g=None, *, tm):
    bsz, t, d = x.shape
    k = o.shape[-1]
    in_specs = [
        pl.BlockSpec((1, tm, k), lambda b, i: (b, i, 0)),
        pl.BlockSpec((k, d), lambda b, i: (0, 0), pipeline_mode=pl.Buffered(1)),
        pl.BlockSpec((1, tm, d), lambda b, i: (b, i, 0)),
        pl.BlockSpec((1, 3, d), lambda b, i: (b, 0, 0)),
    ]
    args = [o, w, x, mod]
    if final_g is not None:
        in_specs.append(pl.BlockSpec((1, d), lambda b, i: (0, 0)))
        args.append(final_g.reshape(1, d))
    return pl.pallas_call(
        functools.partial(_outproj_kernel, final_norm=final_g is not None),
        grid=(bsz, t // tm),
        in_specs=in_specs,
        out_specs=pl.BlockSpec((1, tm, d), lambda b, i: (b, i, 0)),
        out_shape=jax.ShapeDtypeStruct((bsz, t, d), F32),
        compiler_params=pltpu.CompilerParams(
            dimension_semantics=("parallel", "parallel"), vmem_limit_bytes=VMEM_LIMIT),
        name="outproj_residual",
    )(*args)


def _gdn_gates_kernel(ba_ref, alog_ref, dtb_ref, col_ref, row_ref):
    cs, nh = GDN_CHUNK, GDN_V_HEADS
    ba = ba_ref[0]
    lane = lax.broadcasted_iota(jnp.int32, ba.shape, 1)
    g = -jnp.exp(alog_ref[...]) * jax.nn.softplus(ba + dtb_ref[...])
    vals = jnp.where(lane < nh, jax.nn.sigmoid(ba), g)
    r = lax.broadcasted_iota(jnp.int32, (cs, cs), 0)
    c = lax.broadcasted_iota(jnp.int32, (cs, cs), 1)
    tri = (r >= c).astype(F32)
    is_beta = lax.broadcasted_iota(jnp.int32, (cs, LANES), 1) < nh
    pr = lax.broadcasted_iota(jnp.int32, (nh // 2, LANES), 0)
    pc = lax.broadcasted_iota(jnp.int32, (nh // 2, LANES), 1)
    pick_even = (pc == nh + 2 * pr).astype(F32)
    for n in range(ba.shape[0] // cs):
        v = vals[n * cs:(n + 1) * cs]
        cum = _dot(tri, v, precision=HIGHEST)
        out = jnp.where(is_beta, v, cum)
        col_ref[0, n * cs:(n + 1) * cs, :] = out
        both = jnp.concatenate([out, pltpu.roll(out, LANES - 1, axis=1)], axis=0)
        row_ref[0, n] = _dot_nt(pick_even, both, precision=HIGHEST)


def _gdn_gates(proj, a_log, dt_bias, *, tt=512):
    bsz, t, n = proj.shape
    nc = t // GDN_CHUNK
    pad = lambda u: jnp.zeros((1, LANES), F32).at[0, GDN_V_HEADS:2 * GDN_V_HEADS].set(u)
    ba_blk = (n - LANES) // LANES
    return pl.pallas_call(
        _gdn_gates_kernel,
        grid=(bsz, t // tt),
        in_specs=[
            pl.BlockSpec((1, tt, LANES), lambda b, i: (b, i, ba_blk)),
            pl.BlockSpec((1, LANES), lambda b, i: (0, 0)),
            pl.BlockSpec((1, LANES), lambda b, i: (0, 0)),
        ],
        out_specs=[
            pl.BlockSpec((1, tt, LANES), lambda b, i: (b, i, 0)),
            pl.BlockSpec((1, tt // GDN_CHUNK, GDN_V_HEADS // 2, 2 * GDN_CHUNK), lambda b, i: (b, i, 0, 0)),
        ],
        out_shape=[
            jax.ShapeDtypeStruct((bsz, t, LANES), F32),
            jax.ShapeDtypeStruct((bsz, nc, GDN_V_HEADS // 2, 2 * GDN_CHUNK), F32),
        ],
        name="gdn_gates",
    )(proj, pad(a_log), pad(dt_bias))


TRI_BASE = 8
PACK = 4


def _block_rows(p):
    n = p.shape[0]
    blk = lax.broadcasted_iota(jnp.int32, p.shape, 1) // n
    return jnp.concatenate([jnp.where(blk == j, p, 0.0) for j in range(PACK)], axis=0).astype(BF16)


def _tri_inverse_packed(mats):
    n = mats[0].shape[0]
    r = lax.broadcasted_iota(jnp.int32, mats[0].shape, 0)
    c = lax.broadcasted_iota(jnp.int32, mats[0].shape, 1) % n
    same = lambda s: (r // s) == (c // s)
    mm = lambda a, b: _dot(a.astype(BF16), _block_rows(b))
    diag = [jnp.where(same(TRI_BASE), a, 0.0) for a in mats]
    inv = [jnp.where(r == c, 1.0, 0.0) - d for d in diag]
    pw = diag
    k = 1
    while 2 * k < TRI_BASE:
        pw = [mm(m, m) for m in pw]
        inv = [p + mm(p, m) for p, m in zip(inv, pw)]
        k *= 2
    s = TRI_BASE
    while s < n:
        sub = same(2 * s) & jnp.logical_not(same(s))
        left = [mm(p, jnp.where(sub, a, 0.0)) for p, a in zip(inv, mats)]
        inv = [p - mm(l, p) for p, l in zip(inv, left)]
        s *= 2
    return inv


def _gdn_chunk_kernel(q_ref, k_ref, v_ref, z_ref, col_ref, row_ref, wq_ref, wk_ref, wv_ref, nw_ref,
                      o_ref, s_ref, qx_ref, kx_ref, vx_ref):
    cs, dh, nh = GDN_CHUNK, GDN_HEAD_DIM, GDN_V_HEADS
    tt = q_ref.shape[1]
    hps = q_ref.shape[2] // dh
    ncb = tt // cs
    assert ncb % 2 == 0 and PACK == 4
    hg = pl.program_id(1)
    ti = pl.program_id(2)

    @pl.when(ti == 0)
    def _():
        s_ref[...] = jnp.zeros_like(s_ref)
        for ext_ref in (qx_ref, kx_ref, vx_ref):
            ext_ref[0:8, :] = jnp.zeros((8, ext_ref.shape[1]), F32)

    def conv_silu(x_ref, ext_ref, w_ref):
        ext_ref[8:8 + tt, :] = x_ref[0]
        w = w_ref[...]
        acc = w[0:1] * ext_ref[5:5 + tt, :]
        for j in range(1, GDN_CONV):
            acc = acc + w[j:j + 1] * ext_ref[5 + j:5 + j + tt, :]
        ext_ref[0:8, :] = ext_ref[tt:tt + 8, :]
        return _silu(acc)

    def l2n(x):
        return x * lax.rsqrt(jnp.sum(x * x, axis=-1, keepdims=True) + EPS)

    q_all = conv_silu(q_ref, qx_ref, wq_ref)
    k_all = conv_silu(k_ref, kx_ref, wk_ref)
    v_all = conv_silu(v_ref, vx_ref, wv_ref)
    q = [l2n(q_all[:, hl * dh:(hl + 1) * dh]) * (dh ** -0.5) for hl in range(hps)]
    k = [l2n(k_all[:, hl * dh:(hl + 1) * dh]) for hl in range(hps)]
    v = [[v_all[:, (2 * hl + e) * dh:(2 * hl + e + 1) * dh] for e in range(2)] for hl in range(hps)]

    lane = lax.broadcasted_iota(jnp.int32, (tt, LANES), 1)
    colv = col_ref[0]
    column = lambda idx: jnp.sum(jnp.where(lane == idx, colv, 0.0), axis=1, keepdims=True)
    hq = [hg * hps + hl for hl in range(hps)]
    beta = [[column(2 * h + e) for e in range(2)] for h in hq]
    gc = [[column(nh + 2 * h + e) for e in range(2)] for h in hq]
    egc = [[jnp.exp(x) for x in pair] for pair in gc]
    chunks = [slice(n * cs, (n + 1) * cs) for n in range(ncb)]

    pr = lax.broadcasted_iota(jnp.int32, (cs, PACK * cs), 0)
    pb = lax.broadcasted_iota(jnp.int32, (cs, PACK * cs), 1) // cs
    pc = lax.broadcasted_iota(jnp.int32, (cs, PACK * cs), 1) % cs
    groups = [(hl, gi) for hl in range(hps) for gi in range(ncb // 2)]
    a_mats, qkd, g_rows = [], {}, {}
    for hl, gi in groups:
        k16, q16 = k[hl].astype(BF16), q[hl].astype(BF16)
        pair = (chunks[2 * gi], chunks[2 * gi + 1])
        kdup = [jnp.concatenate([k16[sl], k16[sl]], axis=0) for sl in pair]
        kk = jnp.concatenate([_dot_nt(k16[sl], kd) for sl, kd in zip(pair, kdup)], axis=1)
        qk = jnp.concatenate([_dot_nt(q16[sl], kd) for sl, kd in zip(pair, kdup)], axis=1)
        pick = lambda cols: jnp.where(pb == 0, cols[0][pair[0]], jnp.where(
            pb == 1, cols[1][pair[0]], jnp.where(pb == 2, cols[0][pair[1]], cols[1][pair[1]])))
        g_row = jnp.concatenate([row_ref[0, 2 * gi + j, pl.ds(hq[hl], 1), :] for j in range(2)], axis=1)
        decay = jnp.exp(jnp.where(pr >= pc, pick(gc[hl]) - g_row, -jnp.inf))
        a_mats.append(jnp.where(pr > pc, pick(beta[hl]) * kk * decay, 0.0))
        qkd[hl, gi] = (qk * decay).astype(BF16)
        g_rows[hl, gi] = g_row
    t_mats = _tri_inverse_packed(a_mats)

    uw, glast, kdt = {}, {}, {}
    for (hl, gi), t_mat in zip(groups, t_mats):
        rhs = []
        for n in (2 * gi, 2 * gi + 1):
            sl = chunks[n]
            for e in range(2):
                j = 2 * (n % 2) + e
                g_last = g_rows[hl, gi][:, j * cs + cs - 1:(j + 1) * cs]
                glast[hl, n, e] = jnp.exp(g_last)
                kdt[hl, n, e] = (k[hl][sl] * jnp.exp(g_last - gc[hl][e][sl])).T.astype(BF16)
                kbeta = k[hl][sl] * beta[hl][e][sl]
                rhs.append(jnp.concatenate([v[hl][e][sl] * beta[hl][e][sl], kbeta * egc[hl][e][sl]], axis=1))
        out = _dot(_block_rows(t_mat), jnp.concatenate(rhs, axis=0).astype(BF16))
        for n in (2 * gi, 2 * gi + 1):
            for e in range(2):
                j = 2 * (n % 2) + e
                uw[hl, n, e] = out[j * cs:(j + 1) * cs]

    nw = nw_ref[...]
    state = {(hl, e): s_ref[2 * hl + e] for hl in range(hps) for e in range(2)}
    for n, sl in enumerate(chunks):
        for hl in range(hps):
            for e in range(2):
                j = 2 * (n % 2) + e
                u, w = uw[hl, n, e][:, :dh], uw[hl, n, e][:, dh:]
                q_dec = q[hl][sl] * egc[hl][e][sl]
                s16 = state[hl, e].astype(BF16)
                ws = _dot(jnp.concatenate([w, q_dec], axis=0).astype(BF16), s16)
                v16 = (u - ws[:cs]).astype(BF16)
                o = ws[cs:] + _dot(qkd[hl, n // 2][:, j * cs:(j + 1) * cs], v16)
                state[hl, e] = state[hl, e] * glast[hl, n, e] + _dot(kdt[hl, n, e], v16)
                o = o * lax.rsqrt(jnp.mean(o * o, axis=-1, keepdims=True) + EPS) * nw
                lanes = slice((2 * hl + e) * dh, (2 * hl + e + 1) * dh)
                o_ref[0, sl, lanes] = (o * _silu(z_ref[0, sl, lanes])).astype(o_ref.dtype)
    for (hl, e), st in state.items():
        s_ref[2 * hl + e] = st


def _gdn_chunk(proj, col, row, conv_w, norm_w, *, tt=GDN_TILE, hps=GDN_HEADS_PER_STEP):
    bsz, t, _ = proj.shape
    dh = GDN_HEAD_DIM
    qw, vw = hps * dh, 2 * hps * dh
    k_blk0 = GDN_QK_W // qw
    v_blk0 = 2 * GDN_QK_W // vw
    z_blk0 = GDN_CONV_W // vw
    ncb = tt // GDN_CHUNK
    return pl.pallas_call(
        _gdn_chunk_kernel,
        grid=(bsz, GDN_QK_HEADS // hps, t // tt),
        in_specs=[
            pl.BlockSpec((1, tt, qw), lambda b, h, i: (b, i, h)),
            pl.BlockSpec((1, tt, qw), lambda b, h, i: (b, i, k_blk0 + h)),
            pl.BlockSpec((1, tt, vw), lambda b, h, i: (b, i, v_blk0 + h)),
            pl.BlockSpec((1, tt, vw), lambda b, h, i: (b, i, z_blk0 + h)),
            pl.BlockSpec((1, tt, LANES), lambda b, h, i: (b, i, 0)),
            pl.BlockSpec((1, ncb, GDN_V_HEADS // 2, 2 * GDN_CHUNK), lambda b, h, i: (b, i, 0, 0)),
            pl.BlockSpec((GDN_CONV, qw), lambda b, h, i: (0, h)),
            pl.BlockSpec((GDN_CONV, qw), lambda b, h, i: (0, k_blk0 + h)),
            pl.BlockSpec((GDN_CONV, vw), lambda b, h, i: (0, v_blk0 + h)),
            pl.BlockSpec((1, dh), lambda b, h, i: (0, 0)),
        ],
        out_specs=pl.BlockSpec((1, tt, vw), lambda b, h, i: (b, i, h)),
        out_shape=jax.ShapeDtypeStruct((bsz, t, GDN_V_W), BF16),
        scratch_shapes=[
            pltpu.VMEM((2 * hps, dh, dh), F32),
            pltpu.VMEM((tt + 8, qw), F32),
            pltpu.VMEM((tt + 8, qw), F32),
            pltpu.VMEM((tt + 8, vw), F32),
        ],
        compiler_params=pltpu.CompilerParams(
            dimension_semantics=("parallel", "parallel", "arbitrary"), vmem_limit_bytes=VMEM_LIMIT),
        name="gdn_chunk_scan",
    )(proj, proj, proj, proj, col, row, conv_w, conv_w, conv_w, norm_w.reshape(1, dh))


def _gdn_layer(x, mod, norm_g, w_in, conv_w, a_log, dt_bias, norm_w, w_out):
    n_in = w_in.shape[1]
    n_pad = -(-n_in // (7 * LANES)) * (7 * LANES)
    w_in_p = jnp.pad(w_in, ((0, 0), (0, n_pad - n_in))).astype(BF16)
    proj = _inproj(x, norm_g, mod, w_in_p, tm=256, tn=7 * LANES)
    col, row = _gdn_gates(proj, a_log, dt_bias)
    o = _gdn_chunk(proj, col, row, conv_w, norm_w)
    return _outproj(o, w_out.astype(BF16), x, mod, tm=512)


NSA_HEADS = 16
NSA_GROUPS = 4
NSA_HPG = NSA_HEADS // NSA_GROUPS
NSA_HEAD_DIM = 64
NSA_CMP_LEN = 32
NSA_CMP_STRIDE = 16
NSA_SLC_LEN = 64
NSA_TOP_K = 8
NSA_WINDOW = 512
NSA_QBLOCK = 128
NSA_Q_W = NSA_HEADS * NSA_HEAD_DIM
NSA_KV_W = NSA_GROUPS * NSA_HEAD_DIM
REL_BUCKETS = 32
REL_MAX_DIST = 128
FEAT_LANE0 = NSA_HEAD_DIM
CONST_LANE0 = FEAT_LANE0 + 32
NSA_COL_Q = 0
NSA_COL_CMP = NSA_Q_W
NSA_COL_SEL = NSA_COL_CMP + 2 * NSA_KV_W
NSA_COL_WIN = NSA_COL_SEL + 2 * NSA_KV_W
NSA_COL_Z = NSA_COL_WIN + 2 * NSA_KV_W
NSA_COL_GATE = NSA_COL_Z + NSA_Q_W
NSA_PROJ_W = NSA_COL_GATE + LANES


def _nsa_column_perm():
    g, dh = NSA_GROUPS, NSA_HEAD_DIM
    kv0 = NSA_Q_W
    cols = list(range(NSA_Q_W))
    cols += [kv0 + i for i in range(2 * NSA_KV_W)]
    for br in (1, 2):
        for gi in range(g):
            cols += [kv0 + (2 * br) * NSA_KV_W + gi * dh + d for d in range(dh)]
            cols += [kv0 + (2 * br + 1) * NSA_KV_W + gi * dh + d for d in range(dh)]
    gate0 = kv0 + 6 * NSA_KV_W
    cols += [gate0 + 3 * NSA_HEADS + i for i in range(NSA_Q_W)]
    cols += [gate0 + i for i in range(3 * NSA_HEADS)] + [-1] * (LANES - 3 * NSA_HEADS)
    assert len(cols) == NSA_PROJ_W
    return np.asarray(cols, np.int32)


def _rel_bucket_table(n):
    d = np.arange(n)
    max_exact = REL_BUCKETS // 2
    nf = np.maximum(d, 1).astype(np.float64)
    large = max_exact + (np.log(nf / max_exact) / math.log(REL_MAX_DIST / max_exact)
                         * (REL_BUCKETS - max_exact)).astype(np.int32)
    large = np.minimum(large, REL_BUCKETS - 1)
    return np.where(d < max_exact, d, large).astype(np.int32)


def _nsa_tables(rel_bias, t):
    qb = NSA_QBLOCK
    bucket = _rel_bucket_table(t)
    assert np.all(bucket[qb + 1:] == REL_BUCKETS - 1)
    bvec = rel_bias[bucket].T
    far = rel_bias[REL_BUCKETS - 1]
    far_hi = far.astype(BF16)
    far_lo = (far - far_hi.astype(F32)).astype(BF16)
    far_sum = far_hi.astype(F32) + far_lo.astype(F32)
    r = np.arange(qb)[:, None]
    c = np.arange(qb)[None, :]
    d0 = r - c
    t0 = jnp.where(d0 >= 0, bvec[:, np.maximum(d0, 0)] - far_sum[:, None, None], NEG_INF)
    t1 = bvec[:, qb + r - c] - far_sum[:, None, None]
    g, hpg = NSA_GROUPS, NSA_HPG
    near = jnp.stack([t1, t0], axis=1).reshape(g, hpg, 2, qb, qb).transpose(0, 2, 4, 1, 3).reshape(g, 2, qb, hpg * qb)
    nb = t // NSA_CMP_STRIDE
    per_tile = qb // NSA_CMP_STRIDE
    width = 2 * per_tile
    far_d = per_tile * NSA_CMP_STRIDE - (NSA_CMP_LEN - 1) + NSA_CMP_STRIDE
    assert np.all(bucket[far_d:] == REL_BUCKETS - 1)
    dm = r - NSA_CMP_STRIDE * np.arange(width)[None, :] + far_d
    band = jnp.where(dm >= 0, bvec[:, np.maximum(dm, 0)], NEG_INF)
    tiles = []
    for i in range(t // qb):
        j0 = per_tile * i - (per_tile + 1)
        lo, hi = max(j0, 0), min(j0 + width, nb)
        tiles.append(jnp.concatenate([
            jnp.broadcast_to(far[:, None, None], (NSA_HEADS, qb, lo)),
            band[:, :, lo - j0:hi - j0],
            jnp.full((NSA_HEADS, qb, nb - hi), NEG_INF, F32)], axis=2))
    cmp_bias = jnp.stack(tiles, axis=0).reshape(t // qb, g, hpg, qb, nb).transpose(0, 1, 4, 2, 3)
    cmp_bias = cmp_bias.reshape(t // qb, g, nb, hpg * qb)
    qconst = jnp.zeros((g, 8, hpg, qb), F32)
    qconst = qconst.at[:, 0].set(jnp.broadcast_to(far_hi.astype(F32).reshape(g, hpg, 1), (g, hpg, qb)))
    qconst = qconst.at[:, 1].set(jnp.broadcast_to(far_lo.astype(F32).reshape(g, hpg, 1), (g, hpg, qb)))
    return near, cmp_bias, qconst.reshape(g, 8, hpg * qb)


def _overlap_t(t):
    n_cmp = (t - NSA_CMP_LEN) // NSA_CMP_STRIDE + 1
    n_slc = t // NSA_SLC_LEN
    c_start = np.arange(n_cmp)[:, None] * NSA_CMP_STRIDE
    s_start = np.arange(n_slc)[None, :] * NSA_SLC_LEN
    ov = np.clip(np.minimum(c_start + NSA_CMP_LEN, s_start + NSA_SLC_LEN) - np.maximum(c_start, s_start), 0, None)
    ov = ov.astype(np.float32) / NSA_CMP_LEN
    out = np.zeros((32, t // NSA_CMP_STRIDE), np.float32)
    out[:n_slc, :n_cmp] = ov.T
    return out


def _nsa_compress_kernel(x_ref, pos_ref, w1_ref, w2_ref, o_ref, xs_ref):
    t = x_ref.shape[1]
    nb = t // NSA_CMP_STRIDE
    nlt = xs_ref.shape[0]
    for c in range(nlt):
        xs_ref[c, 0:t, :] = x_ref[0, :, c * LANES:(c + 1) * LANES]
        xs_ref[c, t:t + NSA_CMP_STRIDE, :] = jnp.zeros((NSA_CMP_STRIDE, LANES), F32)
    acc = jnp.zeros((nb, w1_ref.shape[2]), F32)
    for l in range(NSA_CMP_LEN):
        xl = jnp.concatenate([xs_ref[c, pl.ds(l, nb, stride=NSA_CMP_STRIDE), :] for c in range(nlt)], axis=1)
        xl = xl + pos_ref[l:l + 1, :]
        acc = acc + _dot(xl.astype(BF16), w1_ref[l])
    hid = _silu(acc).astype(BF16)
    res = _dot(hid, w2_ref[...])
    for g in range(NSA_GROUPS):
        o_ref[0, g] = res[:, g * LANES:(g + 1) * LANES]


def _nsa_compress(proj, cmp_pos, cmp_w1, cmp_w2):
    bsz, t, _ = proj.shape
    g, dh = NSA_GROUPS, NSA_HEAD_DIM
    nb = t // NSA_CMP_STRIDE
    w = 2 * NSA_KV_W
    eye = jnp.eye(g, dtype=F32)
    w1 = cmp_w1.reshape(2, NSA_CMP_LEN, dh, dh)
    eye2 = jnp.eye(2, dtype=F32)
    w1c = (w1.transpose(1, 0, 2, 3).astype(BF16)[:, :, None, :, None, None, :]
           * eye.astype(BF16)[None, None, :, None, :, None, None]
           * eye2.astype(BF16)[None, :, None, None, None, :, None]).reshape(NSA_CMP_LEN, w, w)
    w2c = (cmp_w2[None, :, :, None, None, :] * eye[:, None, None, :, None, None]
           * eye2[None, :, None, None, :, None]).reshape(w, w).astype(BF16)
    pos = jnp.broadcast_to(cmp_pos[:, :, None, :], (2, NSA_CMP_LEN, g, dh)).transpose(1, 0, 2, 3).reshape(NSA_CMP_LEN, w)
    return pl.pallas_call(
        _nsa_compress_kernel,
        grid=(bsz,),
        in_specs=[
            pl.BlockSpec((1, t, w), lambda b: (b, 0, NSA_COL_CMP // w)),
            pl.BlockSpec((NSA_CMP_LEN, w), lambda b: (0, 0)),
            pl.BlockSpec((NSA_CMP_LEN, w, w), lambda b: (0, 0, 0), pipeline_mode=pl.Buffered(1)),
            pl.BlockSpec((w, w), lambda b: (0, 0)),
        ],
        out_specs=pl.BlockSpec((1, g, nb, LANES), lambda b: (b, 0, 0, 0)),
        out_shape=jax.ShapeDtypeStruct((bsz, g, nb, LANES), F32),
        scratch_shapes=[pltpu.VMEM((w // LANES, t + NSA_CMP_STRIDE, LANES), F32)],
        compiler_params=pltpu.CompilerParams(dimension_semantics=("parallel",), vmem_limit_bytes=VMEM_LIMIT),
        name="nsa_compress",
    )(proj, pos, w1c, w2c)


def _nsa_attn_kernel(q_ref, kvs_ref, kvw_ref, kvc_ref, gate_ref, z_ref, cb_ref, near_ref, qc_ref, ovl_ref,
                     o_ref, ks_ref, vs_ref, kw_ref, vw_ref, gt_ref, ms_ref, accs_ref, mw_ref, accw_ref):
    qb, dh, hpg = NSA_QBLOCK, NSA_HEAD_DIM, NSA_HPG
    t = kvs_ref.shape[1]
    nblk = t // NSA_SLC_LEN
    cols = hpg * qb
    g = pl.program_id(1)
    i = pl.program_id(2)

    @pl.when(i == 0)
    def _():
        tok = lax.broadcasted_iota(jnp.int32, (t, LANES), 0)
        ln = lax.broadcasted_iota(jnp.int32, (t, LANES), 1)
        const = jnp.where((ln == CONST_LANE0) | (ln == CONST_LANE0 + 1), 1.0, 0.0)
        onehot = jnp.where(ln - FEAT_LANE0 == tok // NSA_SLC_LEN, 1.0, 0.0)
        row = lax.broadcasted_iota(jnp.int32, (LANES, t), 0)
        kvs = kvs_ref[0]
        kvw = kvw_ref[0]
        ks_ref[...] = jnp.where(ln < dh, kvs, onehot + const).astype(BF16)
        kw_ref[...] = jnp.where(ln < dh, kvw, const).astype(BF16)
        vs_ref[...] = jnp.where(row == 0, 1.0, kvs.T).astype(BF16)
        vw_ref[...] = jnp.where(row == 0, 1.0, kvw.T).astype(BF16)

    q_t = (q_ref[0] * (dh ** -0.5)).T
    q_heads = jnp.concatenate([q_t[hh * dh:(hh + 1) * dh] for hh in range(hpg)], axis=1)

    kvc = kvc_ref[0, 0]
    lane_k = lax.broadcasted_iota(jnp.int32, kvc.shape, 1)
    kc16 = jnp.where(lane_k < dh, kvc, 0.0).astype(BF16)
    qc16 = jnp.concatenate([q_heads, jnp.zeros((LANES - dh, cols), F32)], axis=0).astype(BF16)
    s = _dot(kc16, qc16) + cb_ref[0, 0]
    s = jnp.exp(s - jnp.max(s, axis=0, keepdims=True))
    p = s / jnp.sum(s, axis=0, keepdims=True)
    tq_lane = i * qb + lax.broadcasted_iota(jnp.int32, (1, cols), 1) % qb
    p16 = (p * (tq_lane >= NSA_CMP_LEN - 1).astype(F32)).astype(BF16)
    o_cmp = _dot(kvc.T.astype(BF16), p16)
    ovl = ovl_ref[...].astype(BF16)
    imp = _dot(ovl, p16[:, 0:qb])
    for hh in range(1, hpg):
        imp = imp + _dot(ovl, p16[:, hh * qb:(hh + 1) * qb])

    blk = lax.broadcasted_iota(jnp.int32, (32, qb), 0)
    tq = i * qb + lax.broadcasted_iota(jnp.int32, (32, qb), 1)
    cur = tq // NSA_SLC_LEN
    forced = (blk == 0) | (blk == cur) | (blk == cur - 1)
    val = jnp.where(forced, jnp.inf, jnp.where(blk * NSA_SLC_LEN <= tq, imp, -jnp.inf))
    cnt = jnp.zeros((32, qb), jnp.int32)
    for s2 in range(nblk):
        other = val[s2:s2 + 1, :]
        cnt = cnt + ((other > val) | ((other == val) & (s2 < blk))).astype(jnp.int32)
    feat = jnp.where((cnt < min(NSA_TOP_K, nblk)) & (blk < nblk), 0.0, NEG_INF)

    qa = jnp.concatenate([q_heads, jnp.concatenate([feat] * hpg, axis=1), qc_ref[0],
                          jnp.zeros((LANES - CONST_LANE0 - 8, cols), F32)], axis=0).astype(BF16)

    def scores(branch, start, nk, bias):
        sc = _dot(branch[0][pl.ds(pl.multiple_of(start, qb), nk), :], qa)
        return sc if bias is None else sc + bias

    def update(branch, start, nk, sc):
        _, vt_ref, m_ref, acc_ref = branch
        m_old = m_ref[...]
        m_new = jnp.maximum(m_old, jnp.max(sc, axis=0, keepdims=True))
        alpha = jnp.exp(m_old - m_new)
        pe = jnp.exp(sc - m_new).astype(BF16)
        acc_ref[...] = alpha * acc_ref[...] + _dot(vt_ref[:, pl.ds(pl.multiple_of(start, qb), nk)], pe)
        m_ref[...] = m_new

    def attend_all(*steps):
        sc = scores(*steps[0])
        for prev, nxt in zip(steps[:-1], steps[1:]):
            sc_next = scores(*nxt)
            update(*prev[:3], sc)
            sc = sc_next
        update(*steps[-1][:3], sc)

    sel = (ks_ref, vs_ref, ms_ref, accs_ref)
    win = (kw_ref, vw_ref, mw_ref, accw_ref)
    for m_ref, acc_ref in ((ms_ref, accs_ref), (mw_ref, accw_ref)):
        m_ref[...] = jnp.full(m_ref.shape, NEG_INF, F32)
        acc_ref[...] = jnp.zeros(acc_ref.shape, F32)

    n_far = jnp.maximum(i - 1, 0)
    quads = n_far // 4

    def sel_body(kq, carry):
        attend_all((sel, kq * (4 * qb), 2 * qb, None), (sel, kq * (4 * qb) + 2 * qb, 2 * qb, None))
        return carry

    lax.fori_loop(0, quads, sel_body, 0)

    @pl.when(n_far % 4 >= 2)
    def _():
        attend_all((sel, quads * (4 * qb), 2 * qb, None))

    @pl.when(n_far % 2 == 1)
    def _():
        attend_all((sel, (n_far - 1) * qb, qb, None))

    nwt = NSA_WINDOW // qb
    assert nwt == 4
    kk = lax.broadcasted_iota(jnp.int32, (qb, cols), 0)
    rr = lax.broadcasted_iota(jnp.int32, (qb, cols), 1) % qb
    near_both = lambda: near_ref[0].reshape(2 * qb, cols)

    @pl.when(i >= nwt)
    def _():
        attend_all((sel, (i - 1) * qb, 2 * qb, near_both()),
                   (win, (i - nwt) * qb, qb, jnp.where(rr < kk, 0.0, NEG_INF)),
                   (win, (i - 3) * qb, 2 * qb, None),
                   (win, (i - 1) * qb, 2 * qb, near_both()))

    @pl.when(i == 3)
    def _():
        attend_all((sel, 2 * qb, 2 * qb, near_both()), (win, 0, 2 * qb, None), (win, 2 * qb, 2 * qb, near_both()))

    @pl.when(i == 2)
    def _():
        attend_all((sel, qb, 2 * qb, near_both()), (win, 0, qb, None), (win, qb, 2 * qb, near_both()))

    @pl.when(i == 1)
    def _():
        attend_all((sel, 0, 2 * qb, near_both()), (win, 0, 2 * qb, near_both()))

    @pl.when(i == 0)
    def _():
        attend_all((sel, 0, qb, near_ref[0, 1]), (win, 0, qb, near_ref[0, 1]))

    def finish(acc_ref):
        acc = acc_ref[...]
        return acc[dh:] / acc[0:1]

    o_slc = finish(accs_ref)
    o_win = finish(accw_ref)

    gt_ref[...] = jax.nn.sigmoid(gate_ref[0]).T
    outs = []
    for hh in range(hpg):
        sl = slice(hh * qb, (hh + 1) * qb)
        base = (g * hpg + hh) * 3
        gate = [gt_ref[pl.ds(base + br, 1), :] for br in range(3)]
        outs.append(gate[0] * o_cmp[dh:, sl] + gate[1] * o_slc[:, sl] + gate[2] * o_win[:, sl])
    out = jnp.concatenate(outs, axis=0).T
    o_ref[0] = (out * _silu(z_ref[0])).astype(o_ref.dtype)


def _nsa_attn(proj, kv_cmp, near, cmp_bias, qconst, ovl):
    bsz, t, _ = proj.shape
    qb, hpg = NSA_QBLOCK, NSA_HPG
    gw = hpg * NSA_HEAD_DIM
    nb = t // NSA_CMP_STRIDE
    cols = hpg * qb
    return pl.pallas_call(
        _nsa_attn_kernel,
        grid=(bsz, NSA_GROUPS, t // qb),
        in_specs=[
            pl.BlockSpec((1, qb, gw), lambda b, g, i: (b, i, NSA_COL_Q // gw + g)),
            pl.BlockSpec((1, t, LANES), lambda b, g, i: (b, 0, NSA_COL_SEL // LANES + g)),
            pl.BlockSpec((1, t, LANES), lambda b, g, i: (b, 0, NSA_COL_WIN // LANES + g)),
            pl.BlockSpec((1, 1, nb, LANES), lambda b, g, i: (b, g, 0, 0)),
            pl.BlockSpec((1, qb, LANES), lambda b, g, i: (b, i, NSA_COL_GATE // LANES)),
            pl.BlockSpec((1, qb, gw), lambda b, g, i: (b, i, NSA_COL_Z // gw + g)),
            pl.BlockSpec((1, 1, nb, cols), lambda b, g, i: (i, g, 0, 0)),
            pl.BlockSpec((1, 2, qb, cols), lambda b, g, i: (g, 0, 0, 0)),
            pl.BlockSpec((1, 8, cols), lambda b, g, i: (g, 0, 0)),
            pl.BlockSpec((32, nb), lambda b, g, i: (0, 0)),
        ],
        out_specs=pl.BlockSpec((1, qb, gw), lambda b, g, i: (b, i, g)),
        out_shape=jax.ShapeDtypeStruct((bsz, t, NSA_Q_W), BF16),
        scratch_shapes=[
            pltpu.VMEM((t, LANES), BF16), pltpu.VMEM((LANES, t), BF16),
            pltpu.VMEM((t, LANES), BF16), pltpu.VMEM((LANES, t), BF16),
            pltpu.VMEM((LANES, qb), F32),
            pltpu.VMEM((1, cols), F32), pltpu.VMEM((LANES, cols), F32),
            pltpu.VMEM((1, cols), F32), pltpu.VMEM((LANES, cols), F32),
        ],
        compiler_params=pltpu.CompilerParams(
            dimension_semantics=("parallel", "parallel", "arbitrary"), vmem_limit_bytes=VMEM_LIMIT),
        name="nsa_attention",
    )(proj, proj, proj, kv_cmp, proj, proj, cmp_bias, near, qconst, ovl)


def _nsa_layer(x, mod, norm_g, w_in, cmp_pos, cmp_w1, cmp_w2, rel_bias, w_out, final_g):
    t = x.shape[1]
    assert t // NSA_SLC_LEN <= 32 and NSA_COL_Z % (NSA_HPG * NSA_HEAD_DIM) == 0
    perm = _nsa_column_perm()
    cuts = [0] + [j for j in range(1, len(perm)) if perm[j] != perm[j - 1] + (perm[j - 1] >= 0)] + [len(perm)]
    runs = [(int(perm[a]), b - a) for a, b in zip(cuts[:-1], cuts[1:])]
    w16 = w_in.astype(BF16)
    w_in_p = jnp.concatenate([w16[:, s:s + n] if s >= 0 else jnp.zeros((w_in.shape[0], n), BF16)
                              for s, n in runs], axis=1)
    proj = _inproj(x, norm_g, mod, w_in_p, tm=256, tn=NSA_PROJ_W)
    kv_cmp = _nsa_compress(proj, cmp_pos, cmp_w1, cmp_w2)
    near, cmp_bias, qconst = _nsa_tables(rel_bias, t)
    o = _nsa_attn(proj, kv_cmp, near, cmp_bias, qconst, jnp.asarray(_overlap_t(t)))
    return _outproj(o, w_out.astype(BF16), x, mod, final_g, tm=512)


def kernel(x, c, ada_w, ada_b, norm_g, gdn_w_in, gdn_conv_w, gdn_a_log, gdn_dt_bias, gdn_norm_w, gdn_w_out,
           nsa_w_in, nsa_cmp_pos, nsa_cmp_w1, nsa_cmp_w2, nsa_w_out, rel_bias, final_g):
    bsz, t, d = x.shape
    mod = _modulation(c, ada_w, ada_b).reshape(ada_w.shape[0], bsz, 3, d)
    x = _gdn_layer(x, mod[0], norm_g[0], gdn_w_in[0], gdn_conv_w[0], gdn_a_log[0], gdn_dt_bias[0],
                   gdn_norm_w[0], gdn_w_out[0])
    return _nsa_layer(x, mod[1], norm_g[1], nsa_w_in[0], nsa_cmp_pos[0], nsa_cmp_w1[0], nsa_cmp_w2[0],
                      rel_bias, nsa_w_out[0], final_g)
```

```python
import functools
import math

import numpy as np
import jax
import jax.numpy as jnp
from jax import lax
from jax.experimental import pallas as pl
from jax.experimental.pallas import tpu as pltpu

F32 = jnp.float32
BF16 = jnp.bfloat16
HIGHEST = lax.Precision.HIGHEST

EPS = 1e-6
NEG_INF = -1e30
LANES = 128
VMEM_LIMIT = 56 * 1024 * 1024

GDN_QK_HEADS = 8
GDN_V_HEADS = 16
GDN_HEAD_DIM = 128
GDN_CONV = 4
GDN_CHUNK = 64
GDN_QK_W = GDN_QK_HEADS * GDN_HEAD_DIM
GDN_V_W = GDN_V_HEADS * GDN_HEAD_DIM
GDN_CONV_W = 2 * GDN_QK_W + GDN_V_W
GDN_TILE = 256
GDN_HEADS_PER_STEP = 4


def _silu(x):
    return x * jax.nn.sigmoid(x)


def _dot(a, b, **kw):
    return jnp.dot(a, b, preferred_element_type=F32, **kw)


def _dot_nt(a, b, **kw):
    return lax.dot_general(a, b, (((1,), (1,)), ((), ())), preferred_element_type=F32, **kw)


def _mod_kernel(c_ref, w_ref, b_ref, o_ref):
    cond = _silu(c_ref[...])
    o_ref[0] = _dot(cond, w_ref[0], precision=HIGHEST) + b_ref[0]


def _modulation(c, ada_w, ada_b):
    depth, d, d3 = ada_w.shape
    bsz = c.shape[0]
    return pl.pallas_call(
        _mod_kernel,
        grid=(depth, d3 // d),
        in_specs=[
            pl.BlockSpec((bsz, d), lambda i, j: (0, 0)),
            pl.BlockSpec((1, d, d), lambda i, j: (i, 0, j)),
            pl.BlockSpec((1, 1, d), lambda i, j: (i, 0, j)),
        ],
        out_specs=pl.BlockSpec((1, bsz, d), lambda i, j: (i, 0, j)),
        out_shape=jax.ShapeDtypeStruct((depth, bsz, d3), F32),
        name="adaln_mod",
    )(c, ada_w, ada_b.reshape(depth, 1, d3))


def _inproj_kernel(x_ref, g_ref, mod_ref, w_ref, o_ref, *, tn):
    x = x_ref[0]
    m = mod_ref[0]
    y = x * lax.rsqrt(jnp.mean(x * x, axis=-1, keepdims=True) + EPS) * g_ref[...]
    h = (y * (1.0 + m[1:2]) + m[0:1]).astype(BF16)
    for j in range(w_ref.shape[1] // tn):
        o_ref[0, :, j * tn:(j + 1) * tn] = _dot(h, w_ref[:, j * tn:(j + 1) * tn])


def _inproj(x, g, mod, w, *, tm, tn):
    bsz, t, d = x.shape
    n = w.shape[1]
    assert t % tm == 0 and n % tn == 0
    return pl.pallas_call(
        functools.partial(_inproj_kernel, tn=tn),
        grid=(bsz, t // tm),
        in_specs=[
            pl.BlockSpec((1, tm, d), lambda b, i: (b, i, 0)),
            pl.BlockSpec((1, d), lambda b, i: (0, 0)),
            pl.BlockSpec((1, 3, d), lambda b, i: (b, 0, 0)),
            pl.BlockSpec((d, n), lambda b, i: (0, 0), pipeline_mode=pl.Buffered(1)),
        ],
        out_specs=pl.BlockSpec((1, tm, n), lambda b, i: (b, i, 0)),
        out_shape=jax.ShapeDtypeStruct((bsz, t, n), F32),
        compiler_params=pltpu.CompilerParams(
            dimension_semantics=("parallel", "parallel"), vmem_limit_bytes=VMEM_LIMIT),
        name="norm_mod_inproj",
    )(x, g.reshape(1, d), mod, w)


def _outproj_kernel(o_ref, w_ref, x_ref, mod_ref, *rest, final_norm):
    y = _dot(o_ref[0], w_ref[...])
    x = x_ref[0] + mod_ref[0][2:3] * y
    if final_norm:
        fg_ref, out_ref = rest
        x = x * lax.rsqrt(jnp.mean(x * x, axis=-1, keepdims=True) + EPS) * fg_ref[...]
    else:
        (out_ref,) = rest
    out_ref[0] = x


def _outproj(o, w, x, mod, final_g=None, *, tm):
    bsz, t, d = x.shape
    k = o.shape[-1]
    in_specs = [
        pl.BlockSpec((1, tm, k), lambda b, i: (b, i, 0)),
        pl.BlockSpec((k, d), lambda b, i: (0, 0), pipeline_mode=pl.Buffered(1)),
        pl.BlockSpec((1, tm, d), lambda b, i: (b, i, 0)),
        pl.BlockSpec((1, 3, d), lambda b, i: (b, 0, 0)),
    ]
    args = [o, w, x, mod]
    if final_g is not None:
        in_specs.append(pl.BlockSpec((1, d), lambda b, i: (0, 0)))
        args.append(final_g.reshape(1, d))
    return pl.pallas_call(
        functools.partial(_outproj_kernel, final_norm=final_g is not None),
        grid=(bsz, t // tm),
        in_specs=in_specs,
        out_specs=pl.BlockSpec((1, tm, d), lambda b, i: (b, i, 0)),
        out_shape=jax.ShapeDtypeStruct((bsz, t, d), F32),
        compiler_params=pltpu.CompilerParams(
            dimension_semantics=("parallel", "parallel"), vmem_limit_bytes=VMEM_LIMIT),
        name="outproj_residual",
    )(*args)


def _gdn_gates_kernel(ba_ref, alog_ref, dtb_ref, col_ref, row_ref):
    cs, nh = GDN_CHUNK, GDN_V_HEADS
    ba = ba_ref[0]
    lane = lax.broadcasted_iota(jnp.int32, ba.shape, 1)
    g = -jnp.exp(alog_ref[...]) * jax.nn.softplus(ba + dtb_ref[...])
    vals = jnp.where(lane < nh, jax.nn.sigmoid(ba), g)
    r = lax.broadcasted_iota(jnp.int32, (cs, cs), 0)
    c = lax.broadcasted_iota(jnp.int32, (cs, cs), 1)
    tri = (r >= c).astype(F32)
    is_beta = lax.broadcasted_iota(jnp.int32, (cs, LANES), 1) < nh
    pr = lax.broadcasted_iota(jnp.int32, (nh // 2, LANES), 0)
    pc = lax.broadcasted_iota(jnp.int32, (nh // 2, LANES), 1)
    pick_even = (pc == nh + 2 * pr).astype(F32)
    for n in range(ba.shape[0] // cs):
        v = vals[n * cs:(n + 1) * cs]
        cum = _dot(tri, v, precision=HIGHEST)
        out = jnp.where(is_beta, v, cum)
        col_ref[0, n * cs:(n + 1) * cs, :] = out
        both = jnp.concatenate([out, pltpu.roll(out, LANES - 1, axis=1)], axis=0)
        row_ref[0, n] = _dot_nt(pick_even, both, precision=HIGHEST)


def _gdn_gates(proj, a_log, dt_bias, *, tt=512):
    bsz, t, n = proj.shape
    nc = t // GDN_CHUNK
    pad = lambda u: jnp.zeros((1, LANES), F32).at[0, GDN_V_HEADS:2 * GDN_V_HEADS].set(u)
    ba_blk = (n - LANES) // LANES
    return pl.pallas_call(
        _gdn_gates_kernel,
        grid=(bsz, t // tt),
        in_specs=[
            pl.BlockSpec((1, tt, LANES), lambda b, i: (b, i, ba_blk)),
            pl.BlockSpec((1, LANES), lambda b, i: (0, 0)),
            pl.BlockSpec((1, LANES), lambda b, i: (0, 0)),
        ],
        out_specs=[
            pl.BlockSpec((1, tt, LANES), lambda b, i: (b, i, 0)),
            pl.BlockSpec((1, tt // GDN_CHUNK, GDN_V_HEADS // 2, 2 * GDN_CHUNK), lambda b, i: (b, i, 0, 0)),
        ],
        out_shape=[
            jax.ShapeDtypeStruct((bsz, t, LANES), F32),
            jax.ShapeDtypeStruct((bsz, nc, GDN_V_HEADS // 2, 2 * GDN_CHUNK), F32),
        ],
        name="gdn_gates",
    )(proj, pad(a_log), pad(dt_bias))


TRI_BASE = 8
PACK = 4


def _block_rows(p):
    n = p.shape[0]
    blk = lax.broadcasted_iota(jnp.int32, p.shape, 1) // n
    return jnp.concatenate([jnp.where(blk == j, p, 0.0) for j in range(PACK)], axis=0).astype(BF16)


def _tri_inverse_packed(mats):
    n = mats[0].shape[0]
    r = lax.broadcasted_iota(jnp.int32, mats[0].shape, 0)
    c = lax.broadcasted_iota(jnp.int32, mats[0].shape, 1) % n
    same = lambda s: (r // s) == (c // s)
    mm = lambda a, b: _dot(a.astype(BF16), _block_rows(b))
    diag = [jnp.where(same(TRI_BASE), a, 0.0) for a in mats]
    inv = [jnp.where(r == c, 1.0, 0.0) - d for d in diag]
    pw = diag
    k = 1
    while 2 * k < TRI_BASE:
        pw = [mm(m, m) for m in pw]
        inv = [p + mm(p, m) for p, m in zip(inv, pw)]
        k *= 2
    s = TRI_BASE
    while s < n:
        sub = same(2 * s) & jnp.logical_not(same(s))
        left = [mm(p, jnp.where(sub, a, 0.0)) for p, a in zip(inv, mats)]
        inv = [p - mm(l, p) for p, l in zip(inv, left)]
        s *= 2
    return inv


def _gdn_chunk_kernel(q_ref, k_ref, v_ref, z_ref, col_ref, row_ref, wq_ref, wk_ref, wv_ref, nw_ref,
                      o_ref, s_ref, qx_ref, kx_ref, vx_ref):
    cs, dh, nh = GDN_CHUNK, GDN_HEAD_DIM, GDN_V_HEADS
    tt = q_ref.shape[1]
    hps = q_ref.shape[2] // dh
    ncb = tt // cs
    assert ncb % 2 == 0 and PACK == 4
    hg = pl.program_id(1)
    ti = pl.program_id(2)

    @pl.when(ti == 0)
    def _():
        s_ref[...] = jnp.zeros_like(s_ref)
        for ext_ref in (qx_ref, kx_ref, vx_ref):
            ext_ref[0:8, :] = jnp.zeros((8, ext_ref.shape[1]), F32)

    def conv_silu(x_ref, ext_ref, w_ref):
        ext_ref[8:8 + tt, :] = x_ref[0]
        w = w_ref[...]
        acc = w[0:1] * ext_ref[5:5 + tt, :]
        for j in range(1, GDN_CONV):
            acc = acc + w[j:j + 1] * ext_ref[5 + j:5 + j + tt, :]
        ext_ref[0:8, :] = ext_ref[tt:tt + 8, :]
        return _silu(acc)

    def l2n(x):
        return x * lax.rsqrt(jnp.sum(x * x, axis=-1, keepdims=True) + EPS)

    q_all = conv_silu(q_ref, qx_ref, wq_ref)
    k_all = conv_silu(k_ref, kx_ref, wk_ref)
    v_all = conv_silu(v_ref, vx_ref, wv_ref)
    q = [l2n(q_all[:, hl * dh:(hl + 1) * dh]) * (dh ** -0.5) for hl in range(hps)]
    k = [l2n(k_all[:, hl * dh:(hl + 1) * dh]) for hl in range(hps)]
    v = [[v_all[:, (2 * hl + e) * dh:(2 * hl + e + 1) * dh] for e in range(2)] for hl in range(hps)]

    lane = lax.broadcasted_iota(jnp.int32, (tt, LANES), 1)
    colv = col_ref[0]
    column = lambda idx: jnp.sum(jnp.where(lane == idx, colv, 0.0), axis=1, keepdims=True)
    hq = [hg * hps + hl for hl in range(hps)]
    beta = [[column(2 * h + e) for e in range(2)] for h in hq]
    gc = [[column(nh + 2 * h + e) for e in range(2)] for h in hq]
    egc = [[jnp.exp(x) for x in pair] for pair in gc]
    chunks = [slice(n * cs, (n + 1) * cs) for n in range(ncb)]

    pr = lax.broadcasted_iota(jnp.int32, (cs, PACK * cs), 0)
    pb = lax.broadcasted_iota(jnp.int32, (cs, PACK * cs), 1) // cs
    pc = lax.broadcasted_iota(jnp.int32, (cs, PACK * cs), 1) % cs
    groups = [(hl, gi) for hl in range(hps) for gi in range(ncb // 2)]
    a_mats, qkd, g_rows = [], {}, {}
    for hl, gi in groups:
        k16, q16 = k[hl].astype(BF16), q[hl].astype(BF16)
        pair = (chunks[2 * gi], chunks[2 * gi + 1])
        kdup = [jnp.concatenate([k16[sl], k16[sl]], axis=0) for sl in pair]
        kk = jnp.concatenate([_dot_nt(k16[sl], kd) for sl, kd in zip(pair, kdup)], axis=1)
        qk = jnp.concatenate([_dot_nt(q16[sl], kd) for sl, kd in zip(pair, kdup)], axis=1)
        pick = lambda cols: jnp.where(pb == 0, cols[0][pair[0]], jnp.where(
            pb == 1, cols[1][pair[0]], jnp.where(pb == 2, cols[0][pair[1]], cols[1][pair[1]])))
        g_row = jnp.concatenate([row_ref[0, 2 * gi + j, pl.ds(hq[hl], 1), :] for j in range(2)], axis=1)
        decay = jnp.exp(jnp.where(pr >= pc, pick(gc[hl]) - g_row, -jnp.inf))
        a_mats.append(jnp.where(pr > pc, pick(beta[hl]) * kk * decay, 0.0))
        qkd[hl, gi] = (qk * decay).astype(BF16)
        g_rows[hl, gi] = g_row
    t_mats = _tri_inverse_packed(a_mats)

    uw, glast, kdt = {}, {}, {}
    for (hl, gi), t_mat in zip(groups, t_mats):
        rhs = []
        for n in (2 * gi, 2 * gi + 1):
            sl = chunks[n]
            for e in range(2):
                j = 2 * (n % 2) + e
                g_last = g_rows[hl, gi][:, j * cs + cs - 1:(j + 1) * cs]
                glast[hl, n, e] = jnp.exp(g_last)
                kdt[hl, n, e] = (k[hl][sl] * jnp.exp(g_last - gc[hl][e][sl])).T.astype(BF16)
                kbeta = k[hl][sl] * beta[hl][e][sl]
                rhs.append(jnp.concatenate([v[hl][e][sl] * beta[hl][e][sl], kbeta * egc[hl][e][sl]], axis=1))
        out = _dot(_block_rows(t_mat), jnp.concatenate(rhs, axis=0).astype(BF16))
        for n in (2 * gi, 2 * gi + 1):
            for e in range(2):
                j = 2 * (n % 2) + e
                uw[hl, n, e] = out[j * cs:(j + 1) * cs]

    nw = nw_ref[...]
    state = {(hl, e): s_ref[2 * hl + e] for hl in range(hps) for e in range(2)}
    for n, sl in enumerate(chunks):
        for hl in range(hps):
            for e in range(2):
                j = 2 * (n % 2) + e
                u, w = uw[hl, n, e][:, :dh], uw[hl, n, e][:, dh:]
                q_dec = q[hl][sl] * egc[hl][e][sl]
                s16 = state[hl, e].astype(BF16)
                ws = _dot(jnp.concatenate([w, q_dec], axis=0).astype(BF16), s16)
                v16 = (u - ws[:cs]).astype(BF16)
                o = ws[cs:] + _dot(qkd[hl, n // 2][:, j * cs:(j + 1) * cs], v16)
                state[hl, e] = state[hl, e] * glast[hl, n, e] + _dot(kdt[hl, n, e], v16)
                o = o * lax.rsqrt(jnp.mean(o * o, axis=-1, keepdims=True) + EPS) * nw
                lanes = slice((2 * hl + e) * dh, (2 * hl + e + 1) * dh)
                o_ref[0, sl, lanes] = (o * _silu(z_ref[0, sl, lanes])).astype(o_ref.dtype)
    for (hl, e), st in state.items():
        s_ref[2 * hl + e] = st


def _gdn_chunk(proj, col, row, conv_w, norm_w, *, tt=GDN_TILE, hps=GDN_HEADS_PER_STEP):
    bsz, t, _ = proj.shape
    dh = GDN_HEAD_DIM
    qw, vw = hps * dh, 2 * hps * dh
    k_blk0 = GDN_QK_W // qw
    v_blk0 = 2 * GDN_QK_W // vw
    z_blk0 = GDN_CONV_W // vw
    ncb = tt // GDN_CHUNK
    return pl.pallas_call(
        _gdn_chunk_kernel,
        grid=(bsz, GDN_QK_HEADS // hps, t // tt),
        in_specs=[
            pl.BlockSpec((1, tt, qw), lambda b, h, i: (b, i, h)),
            pl.BlockSpec((1, tt, qw), lambda b, h, i: (b, i, k_blk0 + h)),
            pl.BlockSpec((1, tt, vw), lambda b, h, i: (b, i, v_blk0 + h)),
            pl.BlockSpec((1, tt, vw), lambda b, h, i: (b, i, z_blk0 + h)),
            pl.BlockSpec((1, tt, LANES), lambda b, h, i: (b, i, 0)),
            pl.BlockSpec((1, ncb, GDN_V_HEADS // 2, 2 * GDN_CHUNK), lambda b, h, i: (b, i, 0, 0)),
            pl.BlockSpec((GDN_CONV, qw), lambda b, h, i: (0, h)),
            pl.BlockSpec((GDN_CONV, qw), lambda b, h, i: (0, k_blk0 + h)),
            pl.BlockSpec((GDN_CONV, vw), lambda b, h, i: (0, v_blk0 + h)),
            pl.BlockSpec((1, dh), lambda b, h, i: (0, 0)),
        ],
        out_specs=pl.BlockSpec((1, tt, vw), lambda b, h, i: (b, i, h)),
        out_shape=jax.ShapeDtypeStruct((bsz, t, GDN_V_W), BF16),
        scratch_shapes=[
            pltpu.VMEM((2 * hps, dh, dh), F32),
            pltpu.VMEM((tt + 8, qw), F32),
            pltpu.VMEM((tt + 8, qw), F32),
            pltpu.VMEM((tt + 8, vw), F32),
        ],
        compiler_params=pltpu.CompilerParams(
            dimension_semantics=("parallel", "parallel", "arbitrary"), vmem_limit_bytes=VMEM_LIMIT),
        name="gdn_chunk_scan",
    )(proj, proj, proj, proj, col, row, conv_w, conv_w, conv_w, norm_w.reshape(1, dh))


def _gdn_layer(x, mod, norm_g, w_in, conv_w, a_log, dt_bias, norm_w, w_out):
    n_in = w_in.shape[1]
    n_pad = -(-n_in // (7 * LANES)) * (7 * LANES)
    w_in_p = jnp.pad(w_in, ((0, 0), (0, n_pad - n_in))).astype(BF16)
    proj = _inproj(x, norm_g, mod, w_in_p, tm=256, tn=7 * LANES)
    col, row = _gdn_gates(proj, a_log, dt_bias)
    o = _gdn_chunk(proj, col, row, conv_w, norm_w)
    return _outproj(o, w_out.astype(BF16), x, mod, tm=512)


NSA_HEADS = 16
NSA_GROUPS = 4
NSA_HPG = NSA_HEADS // NSA_GROUPS
NSA_HEAD_DIM = 64
NSA_CMP_LEN = 32
NSA_CMP_STRIDE = 16
NSA_SLC_LEN = 64
NSA_TOP_K = 8
NSA_WINDOW = 512
NSA_QBLOCK = 128
NSA_Q_W = NSA_HEADS * NSA_HEAD_DIM
NSA_KV_W = NSA_GROUPS * NSA_HEAD_DIM
REL_BUCKETS = 32
REL_MAX_DIST = 128
FEAT_LANE0 = NSA_HEAD_DIM
CONST_LANE0 = FEAT_LANE0 + 32
NSA_COL_Q = 0
NSA_COL_CMP = NSA_Q_W
NSA_COL_SEL = NSA_COL_CMP + 2 * NSA_KV_W
NSA_COL_WIN = NSA_COL_SEL + 2 * NSA_KV_W
NSA_COL_Z = NSA_COL_WIN + 2 * NSA_KV_W
NSA_COL_GATE = NSA_COL_Z + NSA_Q_W
NSA_PROJ_W = NSA_COL_GATE + LANES


def _nsa_column_perm():
    g, dh = NSA_GROUPS, NSA_HEAD_DIM
    kv0 = NSA_Q_W
    cols = list(range(NSA_Q_W))
    cols += [kv0 + i for i in range(2 * NSA_KV_W)]
    for br in (1, 2):
        for gi in range(g):
            cols += [kv0 + (2 * br) * NSA_KV_W + gi * dh + d for d in range(dh)]
            cols += [kv0 + (2 * br + 1) * NSA_KV_W + gi * dh + d for d in range(dh)]
    gate0 = kv0 + 6 * NSA_KV_W
    cols += [gate0 + 3 * NSA_HEADS + i for i in range(NSA_Q_W)]
    cols += [gate0 + i for i in range(3 * NSA_HEADS)] + [-1] * (LANES - 3 * NSA_HEADS)
    assert len(cols) == NSA_PROJ_W
    return np.asarray(cols, np.int32)


def _rel_bucket_table(n):
    d = np.arange(n)
    max_exact = REL_BUCKETS // 2
    nf = np.maximum(d, 1).astype(np.float64)
    large = max_exact + (np.log(nf / max_exact) / math.log(REL_MAX_DIST / max_exact)
                         * (REL_BUCKETS - max_exact)).astype(np.int32)
    large = np.minimum(large, REL_BUCKETS - 1)
    return np.where(d < max_exact, d, large).astype(np.int32)


def _nsa_tables(rel_bias, t):
    qb = NSA_QBLOCK
    bucket = _rel_bucket_table(t)
    assert np.all(bucket[qb + 1:] == REL_BUCKETS - 1)
    bvec = rel_bias[bucket].T
    far = rel_bias[REL_BUCKETS - 1]
    far_hi = far.astype(BF16)
    far_lo = (far - far_hi.astype(F32)).astype(BF16)
    far_sum = far_hi.astype(F32) + far_lo.astype(F32)
    r = np.arange(qb)[:, None]
    c = np.arange(qb)[None, :]
    d0 = r - c

    def toeplitz(w):
        n = 2 * qb - 1
        ext = jnp.pad(w[:, ::-1], ((0, 0), (0, 1)))
        skew = jnp.tile(ext, (1, qb))[:, :qb * n].reshape(w.shape[0], qb, n)
        return skew[:, :, qb - 1:]

    rel = bvec[:, :2 * qb] - far_sum[:, None]
    t0 = toeplitz(jnp.concatenate([jnp.full((NSA_HEADS, qb - 1), NEG_INF, F32), rel[:, :qb]], axis=1))
    t1 = toeplitz(rel[:, 1:])
    g, hpg = NSA_GROUPS, NSA_HPG
    none = jnp.full_like(t0, NEG_INF)
    near = jnp.stack([t1, t0, t0, none], axis=1).reshape(g, hpg, 2, 2, qb, qb)
    near = near.transpose(0, 2, 3, 5, 1, 4).reshape(g, 2, 2 * qb, hpg * qb)
    nb = t // NSA_CMP_STRIDE
    per_tile = qb // NSA_CMP_STRIDE
    width = 2 * per_tile
    far_d = per_tile * NSA_CMP_STRIDE - (NSA_CMP_LEN - 1) + NSA_CMP_STRIDE
    assert np.all(bucket[far_d:] == REL_BUCKETS - 1)
    dm = r - NSA_CMP_STRIDE * np.arange(width)[None, :] + far_d
    band = jnp.where(dm >= 0, bvec[:, np.maximum(dm, 0)], NEG_INF)
    tiles = []
    for i in range(t // qb):
        j0 = per_tile * i - (per_tile + 1)
        lo, hi = max(j0, 0), min(j0 + width, nb)
        tiles.append(jnp.concatenate([
            jnp.broadcast_to(far[:, None, None], (NSA_HEADS, qb, lo)),
            band[:, :, lo - j0:hi - j0],
            jnp.full((NSA_HEADS, qb, nb - hi), NEG_INF, F32)], axis=2))
    cmp_bias = jnp.stack(tiles, axis=0).reshape(t // qb, g, hpg, qb, nb).transpose(0, 1, 4, 2, 3)
    cmp_bias = cmp_bias.reshape(t // qb, g, nb, hpg * qb)
    qconst = jnp.zeros((g, 8, hpg, qb), F32)
    qconst = qconst.at[:, 0].set(jnp.broadcast_to(far_hi.astype(F32).reshape(g, hpg, 1), (g, hpg, qb)))
    qconst = qconst.at[:, 1].set(jnp.broadcast_to(far_lo.astype(F32).reshape(g, hpg, 1), (g, hpg, qb)))
    return near, cmp_bias, qconst.reshape(g, 8, hpg * qb)


def _overlap_t(t):
    n_cmp = (t - NSA_CMP_LEN) // NSA_CMP_STRIDE + 1
    n_slc = t // NSA_SLC_LEN
    c_start = np.arange(n_cmp)[:, None] * NSA_CMP_STRIDE
    s_start = np.arange(n_slc)[None, :] * NSA_SLC_LEN
    ov = np.clip(np.minimum(c_start + NSA_CMP_LEN, s_start + NSA_SLC_LEN) - np.maximum(c_start, s_start), 0, None)
    ov = ov.astype(np.float32) / NSA_CMP_LEN
    out = np.zeros((32, t // NSA_CMP_STRIDE), np.float32)
    out[:n_slc, :n_cmp] = ov.T
    return out


def _nsa_compress_kernel(x_ref, pos_ref, w1_ref, w2_ref, o_ref, xs_ref):
    t = x_ref.shape[1]
    nb = t // NSA_CMP_STRIDE
    nlt = xs_ref.shape[0]
    for c in range(nlt):
        xs_ref[c, 0:t, :] = x_ref[0, :, c * LANES:(c + 1) * LANES]
        xs_ref[c, t:t + NSA_CMP_STRIDE, :] = jnp.zeros((NSA_CMP_STRIDE, LANES), F32)
    acc = jnp.zeros((nb, w1_ref.shape[2]), F32)
    for l in range(NSA_CMP_LEN):
        xl = jnp.concatenate([xs_ref[c, pl.ds(l, nb, stride=NSA_CMP_STRIDE), :] for c in range(nlt)], axis=1)
        xl = xl + pos_ref[l:l + 1, :]
        acc = acc + _dot(xl.astype(BF16), w1_ref[l])
    hid = _silu(acc).astype(BF16)
    res = _dot(hid, w2_ref[...])
    for g in range(NSA_GROUPS):
        o_ref[0, g] = res[:, g * LANES:(g + 1) * LANES]


def _nsa_compress(proj, cmp_pos, cmp_w1, cmp_w2):
    bsz, t, _ = proj.shape
    g, dh = NSA_GROUPS, NSA_HEAD_DIM
    nb = t // NSA_CMP_STRIDE
    w = 2 * NSA_KV_W
    w1 = cmp_w1.reshape(2, NSA_CMP_LEN, dh, dh).astype(BF16)
    w1c = jnp.zeros((NSA_CMP_LEN, w, w), BF16)
    w2c = jnp.zeros((w, w), BF16)
    for i in range(2):
        for gi in range(g):
            r0, c0 = (i * g + gi) * dh, (gi * 2 + i) * dh
            w1c = w1c.at[:, r0:r0 + dh, c0:c0 + dh].set(w1[i])
            w2c = w2c.at[c0:c0 + dh, c0:c0 + dh].set(cmp_w2[i].astype(BF16))
    pos = jnp.broadcast_to(cmp_pos[:, :, None, :], (2, NSA_CMP_LEN, g, dh)).transpose(1, 0, 2, 3).reshape(NSA_CMP_LEN, w)
    return pl.pallas_call(
        _nsa_compress_kernel,
        grid=(bsz,),
        in_specs=[
            pl.BlockSpec((1, t, w), lambda b: (b, 0, NSA_COL_CMP // w)),
            pl.BlockSpec((NSA_CMP_LEN, w), lambda b: (0, 0)),
            pl.BlockSpec((NSA_CMP_LEN, w, w), lambda b: (0, 0, 0), pipeline_mode=pl.Buffered(1)),
            pl.BlockSpec((w, w), lambda b: (0, 0)),
        ],
        out_specs=pl.BlockSpec((1, g, nb, LANES), lambda b: (b, 0, 0, 0)),
        out_shape=jax.ShapeDtypeStruct((bsz, g, nb, LANES), F32),
        scratch_shapes=[pltpu.VMEM((w // LANES, t + NSA_CMP_STRIDE, LANES), F32)],
        compiler_params=pltpu.CompilerParams(dimension_semantics=("parallel",), vmem_limit_bytes=VMEM_LIMIT),
        name="nsa_compress",
    )(proj, pos, w1c, w2c)


def _nsa_attn_kernel(q_ref, kvs_ref, kvw_ref, kvc_ref, gate_ref, z_ref, cb_ref, near_ref, qc_ref, ovl_ref,
                     o_ref, ks_ref, vs_ref, kw_ref, vw_ref, gt_ref, ms_ref, accs_ref, mw_ref, accw_ref):
    qb, dh, hpg = NSA_QBLOCK, NSA_HEAD_DIM, NSA_HPG
    t = kvs_ref.shape[1]
    nblk = t // NSA_SLC_LEN
    cols = hpg * qb
    g = pl.program_id(1)
    i = pl.program_id(2)

    @pl.when(i == 0)
    def _():
        tok = lax.broadcasted_iota(jnp.int32, (t, LANES), 0)
        ln = lax.broadcasted_iota(jnp.int32, (t, LANES), 1)
        const = jnp.where((ln == CONST_LANE0) | (ln == CONST_LANE0 + 1), 1.0, 0.0)
        onehot = jnp.where(ln - FEAT_LANE0 == tok // NSA_SLC_LEN, 1.0, 0.0)
        row = lax.broadcasted_iota(jnp.int32, (LANES, t), 0)
        kvs = kvs_ref[0]
        kvw = kvw_ref[0]
        ks_ref[...] = jnp.where(ln < dh, kvs, onehot + const).astype(BF16)
        kw_ref[...] = jnp.where(ln < dh, kvw, const).astype(BF16)
        vs_ref[...] = jnp.where(row == 0, 1.0, kvs.T).astype(BF16)
        vw_ref[...] = jnp.where(row == 0, 1.0, kvw.T).astype(BF16)

    q_t = (q_ref[0] * (dh ** -0.5)).T
    q_heads = jnp.concatenate([q_t[hh * dh:(hh + 1) * dh] for hh in range(hpg)], axis=1)

    def scores(branch, qa_t, start, nk, bias):
        sc = _dot(branch[0][pl.ds(pl.multiple_of(start, qb), nk), :], qa_t)
        return sc if bias is None else sc + bias

    def update(branch, start, nk, sc):
        _, vt_ref, m_ref, acc_ref = branch
        m_old = m_ref[...]
        m_new = jnp.maximum(m_old, jnp.max(sc, axis=0, keepdims=True))
        alpha = jnp.exp(m_old - m_new)
        pe = jnp.exp(sc - m_new).astype(BF16)
        acc_ref[...] = alpha * acc_ref[...] + _dot(vt_ref[:, pl.ds(pl.multiple_of(start, qb), nk)], pe)
        m_ref[...] = m_new

    sel = (ks_ref, vs_ref, ms_ref, accs_ref)
    win = (kw_ref, vw_ref, mw_ref, accw_ref)
    for m_ref, acc_ref in ((ms_ref, accs_ref), (mw_ref, accw_ref)):
        m_ref[...] = jnp.full(m_ref.shape, NEG_INF, F32)
        acc_ref[...] = jnp.zeros(acc_ref.shape, F32)

    nwt = NSA_WINDOW // qb
    assert nwt == 4
    pad_rows = jnp.zeros((LANES - CONST_LANE0 - 8, cols), F32)
    qa_win = jnp.concatenate([q_heads, jnp.zeros((32, cols), F32), qc_ref[0], pad_rows], axis=0).astype(BF16)

    kvc = kvc_ref[0, 0]
    lane_k = lax.broadcasted_iota(jnp.int32, kvc.shape, 1)
    kc16 = jnp.where(lane_k < dh, kvc, 0.0).astype(BF16)
    s = _dot(kc16, qa_win) + cb_ref[0, 0]

    kk = lax.broadcasted_iota(jnp.int32, (qb, cols), 0)
    rr = lax.broadcasted_iota(jnp.int32, (qb, cols), 1) % qb
    w4_start = jnp.maximum(i - nwt, 0) * qb
    sc_w4 = scores(win, qa_win, w4_start, qb, jnp.where((rr < kk) & (i >= nwt), 0.0, NEG_INF))

    s = jnp.exp(s - jnp.max(s, axis=0, keepdims=True))
    p = s / jnp.sum(s, axis=0, keepdims=True)
    tq_lane = i * qb + lax.broadcasted_iota(jnp.int32, (1, cols), 1) % qb
    p16 = (p * (tq_lane >= NSA_CMP_LEN - 1).astype(F32)).astype(BF16)
    o_cmp = _dot(kvc.T.astype(BF16), p16)
    ovl = ovl_ref[...].astype(BF16)
    imp = _dot(ovl, p16[:, 0:qb])
    for hh in range(1, hpg):
        imp = imp + _dot(ovl, p16[:, hh * qb:(hh + 1) * qb])

    update(win, w4_start, qb, sc_w4)
    w32_start = jnp.maximum(i - 3, 0) * qb
    row2 = lax.broadcasted_iota(jnp.int32, (2 * qb, cols), 0)
    sc_w32 = scores(win, qa_win, w32_start, 2 * qb,
                    jnp.where(row2 < (i - 1) * qb - w32_start, 0.0, NEG_INF))

    blk = lax.broadcasted_iota(jnp.int32, (32, qb), 0)
    tq = i * qb + lax.broadcasted_iota(jnp.int32, (32, qb), 1)
    cur = tq // NSA_SLC_LEN
    forced = (blk == 0) | (blk == cur) | (blk == cur - 1)
    val = jnp.where(forced, jnp.inf, jnp.where(blk * NSA_SLC_LEN <= tq, imp, -jnp.inf))
    cnt = jnp.zeros((32, qb), jnp.int32)
    for s2 in range(nblk):
        other = val[s2:s2 + 1, :]
        cnt = cnt + ((other > val) | ((other == val) & (s2 < blk))).astype(jnp.int32)
    feat = jnp.where((cnt < min(NSA_TOP_K, nblk)) & (blk < nblk), 0.0, NEG_INF)
    qa = jnp.concatenate([q_heads, jnp.concatenate([feat] * hpg, axis=1), qc_ref[0], pad_rows],
                         axis=0).astype(BF16)

    update(win, w32_start, 2 * qb, sc_w32)
    near_start = jnp.maximum(i - 1, 0) * qb
    near_bias = near_ref[0, jnp.where(i == 0, 1, 0)]
    sc_wn = scores(win, qa_win, near_start, 2 * qb, near_bias)
    sc_sn = scores(sel, qa, near_start, 2 * qb, near_bias)
    update(win, near_start, 2 * qb, sc_wn)
    update(sel, near_start, 2 * qb, sc_sn)

    def attend_all(*steps):
        sc = scores(sel, qa, *steps[0], None)
        for prev, nxt in zip(steps[:-1], steps[1:]):
            sc_next = scores(sel, qa, *nxt, None)
            update(sel, *prev, sc)
            sc = sc_next
        update(sel, *steps[-1], sc)

    n_far = jnp.maximum(i - 1, 0)
    quads = n_far // 4

    def sel_body(kq, carry):
        attend_all((kq * (4 * qb), 2 * qb), (kq * (4 * qb) + 2 * qb, 2 * qb))
        return carry

    lax.fori_loop(0, quads, sel_body, 0)

    @pl.when(n_far % 4 >= 2)
    def _():
        attend_all((quads * (4 * qb), 2 * qb))

    @pl.when(n_far % 2 == 1)
    def _():
        attend_all(((n_far - 1) * qb, qb))

    def finish(acc_ref):
        acc = acc_ref[...]
        return acc[dh:] / acc[0:1]

    o_slc = finish(accs_ref)
    o_win = finish(accw_ref)

    gt_ref[...] = jax.nn.sigmoid(gate_ref[0]).T
    outs = []
    for hh in range(hpg):
        sl = slice(hh * qb, (hh + 1) * qb)
        base = (g * hpg + hh) * 3
        gate = [gt_ref[pl.ds(base + br, 1), :] for br in range(3)]
        outs.append(gate[0] * o_cmp[dh:, sl] + gate[1] * o_slc[:, sl] + gate[2] * o_win[:, sl])
    out = jnp.concatenate(outs, axis=0).T
    o_ref[0] = (out * _silu(z_ref[0])).astype(o_ref.dtype)


def _nsa_attn(proj, kv_cmp, near, cmp_bias, qconst, ovl):
    bsz, t, _ = proj.shape
    qb, hpg = NSA_QBLOCK, NSA_HPG
    gw = hpg * NSA_HEAD_DIM
    nb = t // NSA_CMP_STRIDE
    cols = hpg * qb
    return pl.pallas_call(
        _nsa_attn_kernel,
        grid=(bsz, NSA_GROUPS, t // qb),
        in_specs=[
            pl.BlockSpec((1, qb, gw), lambda b, g, i: (b, i, NSA_COL_Q // gw + g)),
            pl.BlockSpec((1, t, LANES), lambda b, g, i: (b, 0, NSA_COL_SEL // LANES + g)),
            pl.BlockSpec((1, t, LANES), lambda b, g, i: (b, 0, NSA_COL_WIN // LANES + g)),
            pl.BlockSpec((1, 1, nb, LANES), lambda b, g, i: (b, g, 0, 0)),
            pl.BlockSpec((1, qb, LANES), lambda b, g, i: (b, i, NSA_COL_GATE // LANES)),
            pl.BlockSpec((1, qb, gw), lambda b, g, i: (b, i, NSA_COL_Z // gw + g)),
            pl.BlockSpec((1, 1, nb, cols), lambda b, g, i: (i, g, 0, 0)),
            pl.BlockSpec((1, 2, 2 * qb, cols), lambda b, g, i: (g, 0, 0, 0)),
            pl.BlockSpec((1, 8, cols), lambda b, g, i: (g, 0, 0)),
            pl.BlockSpec((32, nb), lambda b, g, i: (0, 0)),
        ],
        out_specs=pl.BlockSpec((1, qb, gw), lambda b, g, i: (b, i, g)),
        out_shape=jax.ShapeDtypeStruct((bsz, t, NSA_Q_W), BF16),
        scratch_shapes=[
            pltpu.VMEM((t, LANES), BF16), pltpu.VMEM((LANES, t), BF16),
            pltpu.VMEM((t, LANES), BF16), pltpu.VMEM((LANES, t), BF16),
            pltpu.VMEM((LANES, qb), F32),
            pltpu.VMEM((1, cols), F32), pltpu.VMEM((LANES, cols), F32),
            pltpu.VMEM((1, cols), F32), pltpu.VMEM((LANES, cols), F32),
        ],
        compiler_params=pltpu.CompilerParams(
            dimension_semantics=("parallel", "parallel", "arbitrary"), vmem_limit_bytes=VMEM_LIMIT),
        name="nsa_attention",
    )(proj, proj, proj, kv_cmp, proj, proj, cmp_bias, near, qconst, ovl)


def _nsa_layer(x, mod, norm_g, w_in, cmp_pos, cmp_w1, cmp_w2, rel_bias, w_out, final_g):
    t = x.shape[1]
    assert t // NSA_SLC_LEN <= 32 and NSA_COL_Z % (NSA_HPG * NSA_HEAD_DIM) == 0
    perm = _nsa_column_perm()
    cuts = [0] + [j for j in range(1, len(perm)) if perm[j] != perm[j - 1] + (perm[j - 1] >= 0)] + [len(perm)]
    runs = [(int(perm[a]), b - a) for a, b in zip(cuts[:-1], cuts[1:])]
    w16 = w_in.astype(BF16)
    w_in_p = jnp.concatenate([w16[:, s:s + n] if s >= 0 else jnp.zeros((w_in.shape[0], n), BF16)
                              for s, n in runs], axis=1)
    proj = _inproj(x, norm_g, mod, w_in_p, tm=256, tn=NSA_PROJ_W)
    kv_cmp = _nsa_compress(proj, cmp_pos, cmp_w1, cmp_w2)
    near, cmp_bias, qconst = _nsa_tables(rel_bias, t)
    o = _nsa_attn(proj, kv_cmp, near, cmp_bias, qconst, jnp.asarray(_overlap_t(t)))
    return _outproj(o, w_out.astype(BF16), x, mod, final_g, tm=512)


def kernel(x, c, ada_w, ada_b, norm_g, gdn_w_in, gdn_conv_w, gdn_a_log, gdn_dt_bias, gdn_norm_w, gdn_w_out,
           nsa_w_in, nsa_cmp_pos, nsa_cmp_w1, nsa_cmp_w2, nsa_w_out, rel_bias, final_g):
    bsz, t, d = x.shape
    mod = _modulation(c, ada_w, ada_b).reshape(ada_w.shape[0], bsz, 3, d)
    x = _gdn_layer(x, mod[0], norm_g[0], gdn_w_in[0], gdn_conv_w[0], gdn_a_log[0], gdn_dt_bias[0],
                   gdn_norm_w[0], gdn_w_out[0])
    return _nsa_layer(x, mod[1], norm_g[1], nsa_w_in[0], nsa_cmp_pos[0], nsa_cmp_w1[0], nsa_cmp_w2[0],
                      rel_bias, nsa_w_out[0], final_g)
```

```python
import functools
import math

import numpy as np
import jax
import jax.numpy as jnp
from jax import lax
from jax.experimental import pallas as pl
from jax.experimental.pallas import tpu as pltpu

F32 = jnp.float32
BF16 = jnp.bfloat16
HIGHEST = lax.Precision.HIGHEST

EPS = 1e-6
NEG_INF = -1e30
LANES = 128
VMEM_LIMIT = 56 * 1024 * 1024

GDN_QK_HEADS = 8
GDN_V_HEADS = 16
GDN_HEAD_DIM = 128
GDN_CONV = 4
GDN_CHUNK = 64
GDN_QK_W = GDN_QK_HEADS * GDN_HEAD_DIM
GDN_V_W = GDN_V_HEADS * GDN_HEAD_DIM
GDN_CONV_W = 2 * GDN_QK_W + GDN_V_W
GDN_TILE = 256
GDN_HEADS_PER_STEP = 4


def _silu(x):
    return x * jax.nn.sigmoid(x)


def _dot(a, b, **kw):
    return jnp.dot(a, b, preferred_element_type=F32, **kw)


def _dot_nt(a, b, **kw):
    return lax.dot_general(a, b, (((1,), (1,)), ((), ())), preferred_element_type=F32, **kw)


def _mod_kernel(c_ref, w_ref, b_ref, o_ref):
    cond = _silu(c_ref[...])
    o_ref[0] = _dot(cond, w_ref[0], precision=HIGHEST) + b_ref[0]


def _modulation(c, ada_w, ada_b):
    depth, d, d3 = ada_w.shape
    bsz = c.shape[0]
    return pl.pallas_call(
        _mod_kernel,
        grid=(depth, d3 // d),
        in_specs=[
            pl.BlockSpec((bsz, d), lambda i, j: (0, 0)),
            pl.BlockSpec((1, d, d), lambda i, j: (i, 0, j)),
            pl.BlockSpec((1, 1, d), lambda i, j: (i, 0, j)),
        ],
        out_specs=pl.BlockSpec((1, bsz, d), lambda i, j: (i, 0, j)),
        out_shape=jax.ShapeDtypeStruct((depth, bsz, d3), F32),
        name="adaln_mod",
    )(c, ada_w, ada_b.reshape(depth, 1, d3))


def _inproj_kernel(x_ref, g_ref, mod_ref, w_ref, o_ref, *, tn):
    x = x_ref[0]
    m = mod_ref[0]
    y = x * lax.rsqrt(jnp.mean(x * x, axis=-1, keepdims=True) + EPS) * g_ref[...]
    h = (y * (1.0 + m[1:2]) + m[0:1]).astype(BF16)
    for j in range(w_ref.shape[1] // tn):
        o_ref[0, :, j * tn:(j + 1) * tn] = _dot(h, w_ref[:, j * tn:(j + 1) * tn])


def _inproj(x, g, mod, w, *, tm, tn):
    bsz, t, d = x.shape
    n = w.shape[1]
    assert t % tm == 0 and n % tn == 0
    return pl.pallas_call(
        functools.partial(_inproj_kernel, tn=tn),
        grid=(bsz, t // tm),
        in_specs=[
            pl.BlockSpec((1, tm, d), lambda b, i: (b, i, 0)),
            pl.BlockSpec((1, d), lambda b, i: (0, 0)),
            pl.BlockSpec((1, 3, d), lambda b, i: (b, 0, 0)),
            pl.BlockSpec((d, n), lambda b, i: (0, 0), pipeline_mode=pl.Buffered(1)),
        ],
        out_specs=pl.BlockSpec((1, tm, n), lambda b, i: (b, i, 0)),
        out_shape=jax.ShapeDtypeStruct((bsz, t, n), F32),
        compiler_params=pltpu.CompilerParams(
            dimension_semantics=("parallel", "parallel"), vmem_limit_bytes=VMEM_LIMIT),
        name="norm_mod_inproj",
    )(x, g.reshape(1, d), mod, w)


def _outproj_kernel(o_ref, w_ref, x_ref, mod_ref, *rest, final_norm):
    y = _dot(o_ref[0], w_ref[...])
    x = x_ref[0] + mod_ref[0][2:3] * y
    if final_norm:
        fg_ref, out_ref = rest
        x = x * lax.rsqrt(jnp.mean(x * x, axis=-1, keepdims=True) + EPS) * fg_ref[...]
    else:
        (out_ref,) = rest
    out_ref[0] = x


def _outproj(o, w, x, mod, final_g=None, *, tm):
    bsz, t, d = x.shape
    k = o.shape[-1]
    in_specs = [
        pl.BlockSpec((1, tm, k), lambda b, i: (b, i, 0)),
        pl.BlockSpec((k, d), lambda b, i: (0, 0), pipeline_mode=pl.Buffered(1)),
        pl.BlockSpec((1, tm, d), lambda b, i: (b, i, 0)),
        pl.BlockSpec((1, 3, d), lambda b, i: (b, 0, 0)),
    ]
    args = [o, w, x, mod]
    if final_g is not None:
        in_specs.append(pl.BlockSpec((1, d), lambda b, i: (0, 0)))
        args.append(final_g.reshape(1, d))
    return pl.pallas_call(
        functools.partial(_outproj_kernel, final_norm=final_g is not None),
        grid=(bsz, t // tm),
        in_specs=in_specs,
        out_specs=pl.BlockSpec((1, tm, d), lambda b, i: (b, i, 0)),
        out_shape=jax.ShapeDtypeStruct((bsz, t, d), F32),
        compiler_params=pltpu.CompilerParams(
            dimension_semantics=("parallel", "parallel"), vmem_limit_bytes=VMEM_LIMIT),
        name="outproj_residual",
    )(*args)


def _gdn_gates_kernel(ba_ref, alog_ref, dtb_ref, col_ref, row_ref):
    cs, nh = GDN_CHUNK, GDN_V_HEADS
    ba = ba_ref[0]
    lane = lax.broadcasted_iota(jnp.int32, ba.shape, 1)
    g = -jnp.exp(alog_ref[...]) * jax.nn.softplus(ba + dtb_ref[...])
    vals = jnp.where(lane < nh, jax.nn.sigmoid(ba), g)
    r = lax.broadcasted_iota(jnp.int32, (cs, cs), 0)
    c = lax.broadcasted_iota(jnp.int32, (cs, cs), 1)
    tri = (r >= c).astype(F32)
    is_beta = lax.broadcasted_iota(jnp.int32, (cs, LANES), 1) < nh
    pr = lax.broadcasted_iota(jnp.int32, (nh // 2, LANES), 0)
    pc = lax.broadcasted_iota(jnp.int32, (nh // 2, LANES), 1)
    pick_even = (pc == nh + 2 * pr).astype(F32)
    for n in range(ba.shape[0] // cs):
        v = vals[n * cs:(n + 1) * cs]
        cum = _dot(tri, v, precision=HIGHEST)
        out = jnp.where(is_beta, v, cum)
        col_ref[0, n * cs:(n + 1) * cs, :] = out
        both = jnp.concatenate([out, pltpu.roll(out, LANES - 1, axis=1)], axis=0)
        row_ref[0, n] = _dot_nt(pick_even, both, precision=HIGHEST)


def _gdn_gates(proj, a_log, dt_bias, *, tt=512):
    bsz, t, n = proj.shape
    nc = t // GDN_CHUNK
    pad = lambda u: jnp.zeros((1, LANES), F32).at[0, GDN_V_HEADS:2 * GDN_V_HEADS].set(u)
    ba_blk = (n - LANES) // LANES
    return pl.pallas_call(
        _gdn_gates_kernel,
        grid=(bsz, t // tt),
        in_specs=[
            pl.BlockSpec((1, tt, LANES), lambda b, i: (b, i, ba_blk)),
            pl.BlockSpec((1, LANES), lambda b, i: (0, 0)),
            pl.BlockSpec((1, LANES), lambda b, i: (0, 0)),
        ],
        out_specs=[
            pl.BlockSpec((1, tt, LANES), lambda b, i: (b, i, 0)),
            pl.BlockSpec((1, tt // GDN_CHUNK, GDN_V_HEADS // 2, 2 * GDN_CHUNK), lambda b, i: (b, i, 0, 0)),
        ],
        out_shape=[
            jax.ShapeDtypeStruct((bsz, t, LANES), F32),
            jax.ShapeDtypeStruct((bsz, nc, GDN_V_HEADS // 2, 2 * GDN_CHUNK), F32),
        ],
        name="gdn_gates",
    )(proj, pad(a_log), pad(dt_bias))


TRI_BASE = 8
PACK = 4


def _block_rows(p):
    n = p.shape[0]
    blk = lax.broadcasted_iota(jnp.int32, p.shape, 1) // n
    return jnp.concatenate([jnp.where(blk == j, p, 0.0) for j in range(PACK)], axis=0).astype(BF16)


def _tri_inverse_packed(mats):
    n = mats[0].shape[0]
    r = lax.broadcasted_iota(jnp.int32, mats[0].shape, 0)
    c = lax.broadcasted_iota(jnp.int32, mats[0].shape, 1) % n
    same = lambda s: (r // s) == (c // s)
    mm = lambda a, b: _dot(a.astype(BF16), _block_rows(b))
    diag = [jnp.where(same(TRI_BASE), a, 0.0) for a in mats]
    inv = [jnp.where(r == c, 1.0, 0.0) - d for d in diag]
    pw = diag
    k = 1
    while 2 * k < TRI_BASE:
        pw = [mm(m, m) for m in pw]
        inv = [p + mm(p, m) for p, m in zip(inv, pw)]
        k *= 2
        yield None
    s = TRI_BASE
    while s < n:
        sub = same(2 * s) & jnp.logical_not(same(s))
        left = [mm(p, jnp.where(sub, a, 0.0)) for p, a in zip(inv, mats)]
        inv = [p - mm(l, p) for p, l in zip(inv, left)]
        s *= 2
        yield None
    yield inv


def _gdn_chunk_kernel(q_ref, k_ref, v_ref, z_ref, col_ref, row_ref, wq_ref, wk_ref, wv_ref, nw_ref,
                      o_ref, s_ref, qx_ref, kx_ref, vx_ref, hu_ref, hwq_ref, hqk_ref, hkd_ref, hgl_ref):
    cs, dh, nh = GDN_CHUNK, GDN_HEAD_DIM, GDN_V_HEADS
    tt = q_ref.shape[1]
    hps = q_ref.shape[2] // dh
    ncb = tt // cs
    assert ncb % 2 == 0 and PACK == 4
    hg = pl.program_id(1)
    ti = pl.program_id(2)
    wslot = ti % 2
    rslot = 1 - wslot
    jidx = lambda hl, n, e: (hl * ncb + n) * 2 + e

    @pl.when(ti == 0)
    def _():
        s_ref[...] = jnp.zeros_like(s_ref)
        for ext_ref in (qx_ref, kx_ref, vx_ref):
            ext_ref[0:8, :] = jnp.zeros((8, ext_ref.shape[1]), F32)
        for h_ref in (hu_ref, hwq_ref, hqk_ref, hkd_ref, hgl_ref):
            h_ref[1] = jnp.zeros(h_ref.shape[1:], h_ref.dtype)

    def conv_silu(x_ref, ext_ref, w_ref):
        ext_ref[8:8 + tt, :] = x_ref[0]
        w = w_ref[...]
        acc = w[0:1] * ext_ref[5:5 + tt, :]
        for j in range(1, GDN_CONV):
            acc = acc + w[j:j + 1] * ext_ref[5 + j:5 + j + tt, :]
        ext_ref[0:8, :] = ext_ref[tt:tt + 8, :]
        return _silu(acc)

    def l2n(x):
        return x * lax.rsqrt(jnp.sum(x * x, axis=-1, keepdims=True) + EPS)

    chunks = [slice(n * cs, (n + 1) * cs) for n in range(ncb)]

    def prepare():
        q_all = conv_silu(q_ref, qx_ref, wq_ref)
        yield
        k_all = conv_silu(k_ref, kx_ref, wk_ref)
        yield
        v_all = conv_silu(v_ref, vx_ref, wv_ref)
        yield
        q = [l2n(q_all[:, hl * dh:(hl + 1) * dh]) * (dh ** -0.5) for hl in range(hps)]
        k = [l2n(k_all[:, hl * dh:(hl + 1) * dh]) for hl in range(hps)]
        v = [[v_all[:, (2 * hl + e) * dh:(2 * hl + e + 1) * dh] for e in range(2)] for hl in range(hps)]
        yield
        lane = lax.broadcasted_iota(jnp.int32, (tt, LANES), 1)
        colv = col_ref[0]
        column = lambda idx: jnp.sum(jnp.where(lane == idx, colv, 0.0), axis=1, keepdims=True)
        hq = [hg * hps + hl for hl in range(hps)]
        beta = [[column(2 * h + e) for e in range(2)] for h in hq]
        gc = [[column(nh + 2 * h + e) for e in range(2)] for h in hq]
        egc = [[jnp.exp(x) for x in pair] for pair in gc]
        yield

        pr = lax.broadcasted_iota(jnp.int32, (cs, PACK * cs), 0)
        pb = lax.broadcasted_iota(jnp.int32, (cs, PACK * cs), 1) // cs
        pc = lax.broadcasted_iota(jnp.int32, (cs, PACK * cs), 1) % cs
        groups = [(hl, gi) for hl in range(hps) for gi in range(ncb // 2)]
        a_mats, g_rows = [], {}
        for hl, gi in groups:
            k16, q16 = k[hl].astype(BF16), q[hl].astype(BF16)
            pair = (chunks[2 * gi], chunks[2 * gi + 1])
            kdup = [jnp.concatenate([k16[sl], k16[sl]], axis=0) for sl in pair]
            kk = jnp.concatenate([_dot_nt(k16[sl], kd) for sl, kd in zip(pair, kdup)], axis=1)
            qk = jnp.concatenate([_dot_nt(q16[sl], kd) for sl, kd in zip(pair, kdup)], axis=1)
            pick = lambda cols: jnp.where(pb == 0, cols[0][pair[0]], jnp.where(
                pb == 1, cols[1][pair[0]], jnp.where(pb == 2, cols[0][pair[1]], cols[1][pair[1]])))
            g_row = jnp.concatenate([row_ref[0, 2 * gi + j, pl.ds(hq[hl], 1), :] for j in range(2)], axis=1)
            decay = jnp.exp(jnp.where(pr >= pc, pick(gc[hl]) - g_row, -jnp.inf))
            a_mats.append(jnp.where(pr > pc, pick(beta[hl]) * kk * decay, 0.0))
            hqk_ref[wslot, hl * (ncb // 2) + gi] = (qk * decay).astype(BF16)
            g_rows[hl, gi] = g_row
            yield
        t_mats = None
        for t_mats in _tri_inverse_packed(a_mats):
            yield

        for (hl, gi), t_mat in zip(groups, t_mats):
            rhs = []
            for n in (2 * gi, 2 * gi + 1):
                sl = chunks[n]
                for e in range(2):
                    j = 2 * (n % 2) + e
                    g_last = g_rows[hl, gi][:, j * cs + cs - 1:(j + 1) * cs]
                    hgl_ref[wslot, jidx(hl, n, e)] = jnp.broadcast_to(jnp.exp(g_last), (1, LANES))
                    hkd_ref[wslot, jidx(hl, n, e)] = (k[hl][sl] * jnp.exp(g_last - gc[hl][e][sl])).T.astype(BF16)
                    kbeta = k[hl][sl] * beta[hl][e][sl]
                    rhs.append(jnp.concatenate([v[hl][e][sl] * beta[hl][e][sl], kbeta * egc[hl][e][sl]], axis=1))
            out = _dot(_block_rows(t_mat), jnp.concatenate(rhs, axis=0).astype(BF16))
            for n in (2 * gi, 2 * gi + 1):
                for e in range(2):
                    uw = out[(2 * (n % 2) + e) * cs:(2 * (n % 2) + e + 1) * cs]
                    hu_ref[wslot, jidx(hl, n, e)] = uw[:, :dh]
                    q_dec = q[hl][chunks[n]] * egc[hl][e][chunks[n]]
                    hwq_ref[wslot, jidx(hl, n, e)] = jnp.concatenate([uw[:, dh:], q_dec], axis=0).astype(BF16)
            yield

    def recurrence():
        nw = nw_ref[...]
        heads = [(hl, e) for hl in range(hps) for e in range(2)]
        state = {he: s_ref[2 * he[0] + he[1]] for he in heads}
        for n, sl in enumerate(chunks):
            ws = {}
            for hl, e in heads:
                ws[hl, e] = _dot(hwq_ref[rslot, jidx(hl, n, e)], state[hl, e].astype(BF16))
            yield
            for hl, e in heads:
                j = jidx(hl, n, e)
                v16 = (hu_ref[rslot, j] - ws[hl, e][:cs]).astype(BF16)
                lb = 2 * (n % 2) + e
                qkd = hqk_ref[rslot, hl * (ncb // 2) + n // 2][:, lb * cs:(lb + 1) * cs]
                o = ws[hl, e][cs:] + _dot(qkd, v16)
                state[hl, e] = state[hl, e] * hgl_ref[rslot, j] + _dot(hkd_ref[rslot, j], v16)
                o = o * lax.rsqrt(jnp.mean(o * o, axis=-1, keepdims=True) + EPS) * nw
                lanes = slice((2 * hl + e) * dh, (2 * hl + e + 1) * dh)
                o_ref[0, sl, lanes] = (o * _silu(z_ref[0, sl, lanes])).astype(o_ref.dtype)
            yield
        for (hl, e), st in state.items():
            s_ref[2 * hl + e] = st

    prep, rec = prepare(), recurrence()
    n_prep = 5 + hps * (ncb // 2) * 2 + 6
    per_stage = -(-n_prep // (2 * ncb))
    prep_live = rec_live = True
    while prep_live or rec_live:
        for _ in range(per_stage):
            if prep_live:
                prep_live = next(prep, "done") != "done"
        if rec_live:
            rec_live = next(rec, "done") != "done"


def _gdn_chunk(proj, col, row, conv_w, norm_w, *, tt=GDN_TILE, hps=GDN_HEADS_PER_STEP):
    bsz, t, _ = proj.shape
    dh = GDN_HEAD_DIM
    qw, vw = hps * dh, 2 * hps * dh
    k_blk0 = GDN_QK_W // qw
    v_blk0 = 2 * GDN_QK_W // vw
    z_blk0 = GDN_CONV_W // vw
    ncb = tt // GDN_CHUNK
    nt = t // tt
    nj = hps * ncb * 2
    cur = lambda i: jnp.minimum(i, nt - 1)
    prev = lambda i: jnp.maximum(i - 1, 0)
    return pl.pallas_call(
        _gdn_chunk_kernel,
        grid=(bsz, GDN_QK_HEADS // hps, nt + 1),
        in_specs=[
            pl.BlockSpec((1, tt, qw), lambda b, h, i: (b, cur(i), h)),
            pl.BlockSpec((1, tt, qw), lambda b, h, i: (b, cur(i), k_blk0 + h)),
            pl.BlockSpec((1, tt, vw), lambda b, h, i: (b, cur(i), v_blk0 + h)),
            pl.BlockSpec((1, tt, vw), lambda b, h, i: (b, prev(i), z_blk0 + h)),
            pl.BlockSpec((1, tt, LANES), lambda b, h, i: (b, cur(i), 0)),
            pl.BlockSpec((1, ncb, GDN_V_HEADS // 2, 2 * GDN_CHUNK), lambda b, h, i: (b, cur(i), 0, 0)),
            pl.BlockSpec((GDN_CONV, qw), lambda b, h, i: (0, h)),
            pl.BlockSpec((GDN_CONV, qw), lambda b, h, i: (0, k_blk0 + h)),
            pl.BlockSpec((GDN_CONV, vw), lambda b, h, i: (0, v_blk0 + h)),
            pl.BlockSpec((1, dh), lambda b, h, i: (0, 0)),
        ],
        out_specs=pl.BlockSpec((1, tt, vw), lambda b, h, i: (b, prev(i), h)),
        out_shape=jax.ShapeDtypeStruct((bsz, t, GDN_V_W), BF16),
        scratch_shapes=[
            pltpu.VMEM((2 * hps, dh, dh), F32),
            pltpu.VMEM((tt + 8, qw), F32),
            pltpu.VMEM((tt + 8, qw), F32),
            pltpu.VMEM((tt + 8, vw), F32),
            pltpu.VMEM((2, nj, GDN_CHUNK, dh), F32),
            pltpu.VMEM((2, nj, 2 * GDN_CHUNK, dh), BF16),
            pltpu.VMEM((2, nj // PACK, GDN_CHUNK, PACK * GDN_CHUNK), BF16),
            pltpu.VMEM((2, nj, dh, GDN_CHUNK), BF16),
            pltpu.VMEM((2, nj, 1, LANES), F32),
        ],
        compiler_params=pltpu.CompilerParams(
            dimension_semantics=("parallel", "parallel", "arbitrary"), vmem_limit_bytes=VMEM_LIMIT),
        name="gdn_chunk_scan",
    )(proj, proj, proj, proj, col, row, conv_w, conv_w, conv_w, norm_w.reshape(1, dh))


def _gdn_layer(x, mod, norm_g, w_in, conv_w, a_log, dt_bias, norm_w, w_out):
    n_in = w_in.shape[1]
    n_pad = -(-n_in // (7 * LANES)) * (7 * LANES)
    w_in_p = jnp.pad(w_in, ((0, 0), (0, n_pad - n_in))).astype(BF16)
    proj = _inproj(x, norm_g, mod, w_in_p, tm=256, tn=7 * LANES)
    col, row = _gdn_gates(proj, a_log, dt_bias)
    o = _gdn_chunk(proj, col, row, conv_w, norm_w)
    return _outproj(o, w_out.astype(BF16), x, mod, tm=512)


NSA_HEADS = 16
NSA_GROUPS = 4
NSA_HPG = NSA_HEADS // NSA_GROUPS
NSA_HEAD_DIM = 64
NSA_CMP_LEN = 32
NSA_CMP_STRIDE = 16
NSA_SLC_LEN = 64
NSA_TOP_K = 8
NSA_WINDOW = 512
NSA_QBLOCK = 128
NSA_Q_W = NSA_HEADS * NSA_HEAD_DIM
NSA_KV_W = NSA_GROUPS * NSA_HEAD_DIM
REL_BUCKETS = 32
REL_MAX_DIST = 128
FEAT_LANE0 = NSA_HEAD_DIM
CONST_LANE0 = FEAT_LANE0 + 32
NSA_COL_Q = 0
NSA_COL_CMP = NSA_Q_W
NSA_COL_SEL = NSA_COL_CMP + 2 * NSA_KV_W
NSA_COL_WIN = NSA_COL_SEL + 2 * NSA_KV_W
NSA_COL_Z = NSA_COL_WIN + 2 * NSA_KV_W
NSA_COL_GATE = NSA_COL_Z + NSA_Q_W
NSA_PROJ_W = NSA_COL_GATE + LANES


def _nsa_column_perm():
    g, dh = NSA_GROUPS, NSA_HEAD_DIM
    kv0 = NSA_Q_W
    cols = list(range(NSA_Q_W))
    cols += [kv0 + i for i in range(2 * NSA_KV_W)]
    for br in (1, 2):
        for gi in range(g):
            cols += [kv0 + (2 * br) * NSA_KV_W + gi * dh + d for d in range(dh)]
            cols += [kv0 + (2 * br + 1) * NSA_KV_W + gi * dh + d for d in range(dh)]
    gate0 = kv0 + 6 * NSA_KV_W
    cols += [gate0 + 3 * NSA_HEADS + i for i in range(NSA_Q_W)]
    cols += [gate0 + i for i in range(3 * NSA_HEADS)] + [-1] * (LANES - 3 * NSA_HEADS)
    assert len(cols) == NSA_PROJ_W
    return np.asarray(cols, np.int32)


def _rel_bucket_table(n):
    d = np.arange(n)
    max_exact = REL_BUCKETS // 2
    nf = np.maximum(d, 1).astype(np.float64)
    large = max_exact + (np.log(nf / max_exact) / math.log(REL_MAX_DIST / max_exact)
                         * (REL_BUCKETS - max_exact)).astype(np.int32)
    large = np.minimum(large, REL_BUCKETS - 1)
    return np.where(d < max_exact, d, large).astype(np.int32)


def _nsa_tables(rel_bias, t):
    qb = NSA_QBLOCK
    bucket = _rel_bucket_table(t)
    assert np.all(bucket[qb + 1:] == REL_BUCKETS - 1)
    bvec = rel_bias[bucket].T
    far = rel_bias[REL_BUCKETS - 1]
    far_hi = far.astype(BF16)
    far_lo = (far - far_hi.astype(F32)).astype(BF16)
    far_sum = far_hi.astype(F32) + far_lo.astype(F32)
    r = np.arange(qb)[:, None]
    c = np.arange(qb)[None, :]
    d0 = r - c

    def toeplitz(w):
        n = 2 * qb - 1
        ext = jnp.pad(w[:, ::-1], ((0, 0), (0, 1)))
        skew = jnp.tile(ext, (1, qb))[:, :qb * n].reshape(w.shape[0], qb, n)
        return skew[:, :, qb - 1:]

    rel = bvec[:, :2 * qb] - far_sum[:, None]
    t0 = toeplitz(jnp.concatenate([jnp.full((NSA_HEADS, qb - 1), NEG_INF, F32), rel[:, :qb]], axis=1))
    t1 = toeplitz(rel[:, 1:])
    g, hpg = NSA_GROUPS, NSA_HPG
    none = jnp.full_like(t0, NEG_INF)
    near = jnp.stack([t1, t0, t0, none], axis=1).reshape(g, hpg, 2, 2, qb, qb)
    near = near.transpose(0, 2, 3, 5, 1, 4).reshape(g, 2, 2 * qb, hpg * qb)
    nb = t // NSA_CMP_STRIDE
    per_tile = qb // NSA_CMP_STRIDE
    width = 2 * per_tile
    far_d = per_tile * NSA_CMP_STRIDE - (NSA_CMP_LEN - 1) + NSA_CMP_STRIDE
    assert np.all(bucket[far_d:] == REL_BUCKETS - 1)
    dm = r - NSA_CMP_STRIDE * np.arange(width)[None, :] + far_d
    band = jnp.where(dm >= 0, bvec[:, np.maximum(dm, 0)], NEG_INF)
    tiles = []
    for i in range(t // qb):
        j0 = per_tile * i - (per_tile + 1)
        lo, hi = max(j0, 0), min(j0 + width, nb)
        tiles.append(jnp.concatenate([
            jnp.broadcast_to(far[:, None, None], (NSA_HEADS, qb, lo)),
            band[:, :, lo - j0:hi - j0],
            jnp.full((NSA_HEADS, qb, nb - hi), NEG_INF, F32)], axis=2))
    cmp_bias = jnp.stack(tiles, axis=0).reshape(t // qb, g, hpg, qb, nb).transpose(0, 1, 4, 2, 3)
    cmp_bias = cmp_bias.reshape(t // qb, g, nb, hpg * qb)
    qconst = jnp.zeros((g, 8, hpg, qb), F32)
    qconst = qconst.at[:, 0].set(jnp.broadcast_to(far_hi.astype(F32).reshape(g, hpg, 1), (g, hpg, qb)))
    qconst = qconst.at[:, 1].set(jnp.broadcast_to(far_lo.astype(F32).reshape(g, hpg, 1), (g, hpg, qb)))
    return near, cmp_bias, qconst.reshape(g, 8, hpg * qb)


def _overlap_t(t):
    n_cmp = (t - NSA_CMP_LEN) // NSA_CMP_STRIDE + 1
    n_slc = t // NSA_SLC_LEN
    c_start = np.arange(n_cmp)[:, None] * NSA_CMP_STRIDE
    s_start = np.arange(n_slc)[None, :] * NSA_SLC_LEN
    ov = np.clip(np.minimum(c_start + NSA_CMP_LEN, s_start + NSA_SLC_LEN) - np.maximum(c_start, s_start), 0, None)
    ov = ov.astype(np.float32) / NSA_CMP_LEN
    out = np.zeros((32, t // NSA_CMP_STRIDE), np.float32)
    out[:n_slc, :n_cmp] = ov.T
    return out


def _nsa_compress_kernel(x_ref, pos_ref, w1_ref, w2_ref, o_ref, xs_ref):
    t = x_ref.shape[1]
    nb = t // NSA_CMP_STRIDE
    nlt = xs_ref.shape[0]
    for c in range(nlt):
        xs_ref[c, 0:t, :] = x_ref[0, :, c * LANES:(c + 1) * LANES]
        xs_ref[c, t:t + NSA_CMP_STRIDE, :] = jnp.zeros((NSA_CMP_STRIDE, LANES), F32)
    acc = jnp.zeros((nb, w1_ref.shape[2]), F32)
    for l in range(NSA_CMP_LEN):
        xl = jnp.concatenate([xs_ref[c, pl.ds(l, nb, stride=NSA_CMP_STRIDE), :] for c in range(nlt)], axis=1)
        xl = xl + pos_ref[l:l + 1, :]
        acc = acc + _dot(xl.astype(BF16), w1_ref[l])
    hid = _silu(acc).astype(BF16)
    res = _dot(hid, w2_ref[...])
    for g in range(NSA_GROUPS):
        o_ref[0, g] = res[:, g * LANES:(g + 1) * LANES]


def _nsa_compress(proj, cmp_pos, cmp_w1, cmp_w2):
    bsz, t, _ = proj.shape
    g, dh = NSA_GROUPS, NSA_HEAD_DIM
    nb = t // NSA_CMP_STRIDE
    w = 2 * NSA_KV_W
    w1 = cmp_w1.reshape(2, NSA_CMP_LEN, dh, dh).astype(BF16)
    w2 = cmp_w2.astype(BF16)
    place = lambda blk, c0: jnp.pad(blk, [(0, 0)] * (blk.ndim - 1) + [(c0, w - dh - c0)])
    w1c = jnp.concatenate([place(w1[i], (gi * 2 + i) * dh) for i in range(2) for gi in range(g)], axis=1)
    w2c = jnp.concatenate([place(w2[i], (gi * 2 + i) * dh) for gi in range(g) for i in range(2)], axis=0)
    pos = jnp.broadcast_to(cmp_pos[:, :, None, :], (2, NSA_CMP_LEN, g, dh)).transpose(1, 0, 2, 3).reshape(NSA_CMP_LEN, w)
    return pl.pallas_call(
        _nsa_compress_kernel,
        grid=(bsz,),
        in_specs=[
            pl.BlockSpec((1, t, w), lambda b: (b, 0, NSA_COL_CMP // w)),
            pl.BlockSpec((NSA_CMP_LEN, w), lambda b: (0, 0)),
            pl.BlockSpec((NSA_CMP_LEN, w, w), lambda b: (0, 0, 0), pipeline_mode=pl.Buffered(1)),
            pl.BlockSpec((w, w), lambda b: (0, 0)),
        ],
        out_specs=pl.BlockSpec((1, g, nb, LANES), lambda b: (b, 0, 0, 0)),
        out_shape=jax.ShapeDtypeStruct((bsz, g, nb, LANES), F32),
        scratch_shapes=[pltpu.VMEM((w // LANES, t + NSA_CMP_STRIDE, LANES), F32)],
        compiler_params=pltpu.CompilerParams(dimension_semantics=("parallel",), vmem_limit_bytes=VMEM_LIMIT),
        name="nsa_compress",
    )(proj, pos, w1c, w2c)


def _nsa_attn_kernel(q_ref, kvs_ref, kvw_ref, kvc_ref, gate_ref, z_ref, cb_ref, near_ref, qc_ref, ovl_ref,
                     o_ref, ks_ref, vs_ref, kw_ref, vw_ref, gt_ref, ms_ref, accs_ref, mw_ref, accw_ref):
    qb, dh, hpg = NSA_QBLOCK, NSA_HEAD_DIM, NSA_HPG
    t = kvs_ref.shape[1]
    nblk = t // NSA_SLC_LEN
    cols = hpg * qb
    g = pl.program_id(1)
    i = pl.program_id(2)

    @pl.when(i == 0)
    def _():
        tok = lax.broadcasted_iota(jnp.int32, (t, LANES), 0)
        ln = lax.broadcasted_iota(jnp.int32, (t, LANES), 1)
        const = jnp.where((ln == CONST_LANE0) | (ln == CONST_LANE0 + 1), 1.0, 0.0)
        onehot = jnp.where(ln - FEAT_LANE0 == tok // NSA_SLC_LEN, 1.0, 0.0)
        row = lax.broadcasted_iota(jnp.int32, (LANES, t), 0)
        kvs = kvs_ref[0]
        kvw = kvw_ref[0]
        ks_ref[...] = jnp.where(ln < dh, kvs, onehot + const).astype(BF16)
        kw_ref[...] = jnp.where(ln < dh, kvw, const).astype(BF16)
        vs_ref[...] = jnp.where(row == 0, 1.0, kvs.T).astype(BF16)
        vw_ref[...] = jnp.where(row == 0, 1.0, kvw.T).astype(BF16)

    q_t = (q_ref[0] * (dh ** -0.5)).T
    q_heads = jnp.concatenate([q_t[hh * dh:(hh + 1) * dh] for hh in range(hpg)], axis=1)

    def scores(branch, qa_t, start, nk, bias):
        sc = _dot(branch[0][pl.ds(pl.multiple_of(start, qb), nk), :], qa_t)
        return sc if bias is None else sc + bias

    def update(branch, start, nk, sc):
        _, vt_ref, m_ref, acc_ref = branch
        m_old = m_ref[...]
        m_new = jnp.maximum(m_old, jnp.max(sc, axis=0, keepdims=True))
        alpha = jnp.exp(m_old - m_new)
        pe = jnp.exp(sc - m_new).astype(BF16)
        acc_ref[...] = alpha * acc_ref[...] + _dot(vt_ref[:, pl.ds(pl.multiple_of(start, qb), nk)], pe)
        m_ref[...] = m_new

    sel = (ks_ref, vs_ref, ms_ref, accs_ref)
    win = (kw_ref, vw_ref, mw_ref, accw_ref)
    for m_ref, acc_ref in ((ms_ref, accs_ref), (mw_ref, accw_ref)):
        m_ref[...] = jnp.full(m_ref.shape, NEG_INF, F32)
        acc_ref[...] = jnp.zeros(acc_ref.shape, F32)

    nwt = NSA_WINDOW // qb
    assert nwt == 4
    pad_rows = jnp.zeros((LANES - CONST_LANE0 - 8, cols), F32)
    qa_win = jnp.concatenate([q_heads, jnp.zeros((32, cols), F32), qc_ref[0], pad_rows], axis=0).astype(BF16)

    kvc = kvc_ref[0, 0]
    lane_k = lax.broadcasted_iota(jnp.int32, kvc.shape, 1)
    kc16 = jnp.where(lane_k < dh, kvc, 0.0).astype(BF16)
    s = _dot(kc16, qa_win) + cb_ref[0, 0]

    kk = lax.broadcasted_iota(jnp.int32, (qb, cols), 0)
    rr = lax.broadcasted_iota(jnp.int32, (qb, cols), 1) % qb
    w4_start = jnp.maximum(i - nwt, 0) * qb
    sc_w4 = scores(win, qa_win, w4_start, qb, jnp.where((rr < kk) & (i >= nwt), 0.0, NEG_INF))

    s = jnp.exp(s - jnp.max(s, axis=0, keepdims=True))
    p = s / jnp.sum(s, axis=0, keepdims=True)
    tq_lane = i * qb + lax.broadcasted_iota(jnp.int32, (1, cols), 1) % qb
    p16 = (p * (tq_lane >= NSA_CMP_LEN - 1).astype(F32)).astype(BF16)
    o_cmp = _dot(kvc.T.astype(BF16), p16)
    ovl = ovl_ref[...].astype(BF16)
    imp = _dot(ovl, p16[:, 0:qb])
    for hh in range(1, hpg):
        imp = imp + _dot(ovl, p16[:, hh * qb:(hh + 1) * qb])

    update(win, w4_start, qb, sc_w4)
    w32_start = jnp.maximum(i - 3, 0) * qb
    row2 = lax.broadcasted_iota(jnp.int32, (2 * qb, cols), 0)
    sc_w32 = scores(win, qa_win, w32_start, 2 * qb,
                    jnp.where(row2 < (i - 1) * qb - w32_start, 0.0, NEG_INF))

    blk = lax.broadcasted_iota(jnp.int32, (32, qb), 0)
    tq = i * qb + lax.broadcasted_iota(jnp.int32, (32, qb), 1)
    cur = tq // NSA_SLC_LEN
    forced = (blk == 0) | (blk == cur) | (blk == cur - 1)
    val = jnp.where(forced, jnp.inf, jnp.where(blk * NSA_SLC_LEN <= tq, imp, -jnp.inf))
    cnt = jnp.zeros((32, qb), jnp.int32)
    for s2 in range(nblk):
        other = val[s2:s2 + 1, :]
        cnt = cnt + ((other > val) | ((other == val) & (s2 < blk))).astype(jnp.int32)
    feat = jnp.where((cnt < min(NSA_TOP_K, nblk)) & (blk < nblk), 0.0, NEG_INF)
    qa = jnp.concatenate([q_heads, jnp.concatenate([feat] * hpg, axis=1), qc_ref[0], pad_rows],
                         axis=0).astype(BF16)

    update(win, w32_start, 2 * qb, sc_w32)
    near_start = jnp.maximum(i - 1, 0) * qb
    near_bias = near_ref[0, jnp.where(i == 0, 1, 0)]
    sc_wn = scores(win, qa_win, near_start, 2 * qb, near_bias)
    sc_sn = scores(sel, qa, near_start, 2 * qb, near_bias)
    update(win, near_start, 2 * qb, sc_wn)
    update(sel, near_start, 2 * qb, sc_sn)

    def attend_all(*steps):
        sc = scores(sel, qa, *steps[0], None)
        for prev, nxt in zip(steps[:-1], steps[1:]):
            sc_next = scores(sel, qa, *nxt, None)
            update(sel, *prev, sc)
            sc = sc_next
        update(sel, *steps[-1], sc)

    n_far = jnp.maximum(i - 1, 0)
    quads = n_far // 4

    def sel_body(kq, carry):
        attend_all((kq * (4 * qb), 2 * qb), (kq * (4 * qb) + 2 * qb, 2 * qb))
        return carry

    lax.fori_loop(0, quads, sel_body, 0)

    @pl.when(n_far % 4 >= 2)
    def _():
        attend_all((quads * (4 * qb), 2 * qb))

    @pl.when(n_far % 2 == 1)
    def _():
        attend_all(((n_far - 1) * qb, qb))

    def finish(acc_ref):
        acc = acc_ref[...]
        return acc[dh:] / acc[0:1]

    o_slc = finish(accs_ref)
    o_win = finish(accw_ref)

    gt_ref[...] = jax.nn.sigmoid(gate_ref[0]).T
    outs = []
    for hh in range(hpg):
        sl = slice(hh * qb, (hh + 1) * qb)
        base = (g * hpg + hh) * 3
        gate = [gt_ref[pl.ds(base + br, 1), :] for br in range(3)]
        outs.append(gate[0] * o_cmp[dh:, sl] + gate[1] * o_slc[:, sl] + gate[2] * o_win[:, sl])
    out = jnp.concatenate(outs, axis=0).T
    o_ref[0] = (out * _silu(z_ref[0])).astype(o_ref.dtype)


def _nsa_attn(proj, kv_cmp, near, cmp_bias, qconst, ovl):
    bsz, t, _ = proj.shape
    qb, hpg = NSA_QBLOCK, NSA_HPG
    gw = hpg * NSA_HEAD_DIM
    nb = t // NSA_CMP_STRIDE
    cols = hpg * qb
    return pl.pallas_call(
        _nsa_attn_kernel,
        grid=(bsz, NSA_GROUPS, t // qb),
        in_specs=[
            pl.BlockSpec((1, qb, gw), lambda b, g, i: (b, i, NSA_COL_Q // gw + g)),
            pl.BlockSpec((1, t, LANES), lambda b, g, i: (b, 0, NSA_COL_SEL // LANES + g)),
            pl.BlockSpec((1, t, LANES), lambda b, g, i: (b, 0, NSA_COL_WIN // LANES + g)),
            pl.BlockSpec((1, 1, nb, LANES), lambda b, g, i: (b, g, 0, 0)),
            pl.BlockSpec((1, qb, LANES), lambda b, g, i: (b, i, NSA_COL_GATE // LANES)),
            pl.BlockSpec((1, qb, gw), lambda b, g, i: (b, i, NSA_COL_Z // gw + g)),
            pl.BlockSpec((1, 1, nb, cols), lambda b, g, i: (i, g, 0, 0)),
            pl.BlockSpec((1, 2, 2 * qb, cols), lambda b, g, i: (g, 0, 0, 0)),
            pl.BlockSpec((1, 8, cols), lambda b, g, i: (g, 0, 0)),
            pl.BlockSpec((32, nb), lambda b, g, i: (0, 0)),
        ],
        out_specs=pl.BlockSpec((1, qb, gw), lambda b, g, i: (b, i, g)),
        out_shape=jax.ShapeDtypeStruct((bsz, t, NSA_Q_W), BF16),
        scratch_shapes=[
            pltpu.VMEM((t, LANES), BF16), pltpu.VMEM((LANES, t), BF16),
            pltpu.VMEM((t, LANES), BF16), pltpu.VMEM((LANES, t), BF16),
            pltpu.VMEM((LANES, qb), F32),
            pltpu.VMEM((1, cols), F32), pltpu.VMEM((LANES, cols), F32),
            pltpu.VMEM((1, cols), F32), pltpu.VMEM((LANES, cols), F32),
        ],
        compiler_params=pltpu.CompilerParams(
            dimension_semantics=("parallel", "parallel", "arbitrary"), vmem_limit_bytes=VMEM_LIMIT),
        name="nsa_attention",
    )(proj, proj, proj, kv_cmp, proj, proj, cmp_bias, near, qconst, ovl)


def _nsa_layer(x, mod, norm_g, w_in, cmp_pos, cmp_w1, cmp_w2, rel_bias, w_out, final_g):
    t = x.shape[1]
    assert t // NSA_SLC_LEN <= 32 and NSA_COL_Z % (NSA_HPG * NSA_HEAD_DIM) == 0
    perm = _nsa_column_perm()
    cuts = [0] + [j for j in range(1, len(perm)) if perm[j] != perm[j - 1] + (perm[j - 1] >= 0)] + [len(perm)]
    runs = [(int(perm[a]), b - a) for a, b in zip(cuts[:-1], cuts[1:])]
    w16 = w_in.astype(BF16)
    w_in_p = jnp.concatenate([w16[:, s:s + n] if s >= 0 else jnp.zeros((w_in.shape[0], n), BF16)
                              for s, n in runs], axis=1)
    proj = _inproj(x, norm_g, mod, w_in_p, tm=256, tn=NSA_PROJ_W)
    kv_cmp = _nsa_compress(proj, cmp_pos, cmp_w1, cmp_w2)
    near, cmp_bias, qconst = _nsa_tables(rel_bias, t)
    o = _nsa_attn(proj, kv_cmp, near, cmp_bias, qconst, jnp.asarray(_overlap_t(t)))
    return _outproj(o, w_out.astype(BF16), x, mod, final_g, tm=512)


def kernel(x, c, ada_w, ada_b, norm_g, gdn_w_in, gdn_conv_w, gdn_a_log, gdn_dt_bias, gdn_norm_w, gdn_w_out,
           nsa_w_in, nsa_cmp_pos, nsa_cmp_w1, nsa_cmp_w2, nsa_w_out, rel_bias, final_g):
    bsz, t, d = x.shape
    mod = _modulation(c, ada_w, ada_b).reshape(ada_w.shape[0], bsz, 3, d)
    x = _gdn_layer(x, mod[0], norm_g[0], gdn_w_in[0], gdn_conv_w[0], gdn_a_log[0], gdn_dt_bias[0],
                   gdn_norm_w[0], gdn_w_out[0])
    return _nsa_layer(x, mod[1], norm_g[1], nsa_w_in[0], nsa_cmp_pos[0], nsa_cmp_w1[0], nsa_cmp_w2[0],
                      rel_bias, nsa_w_out[0], final_g)
```

```python
import functools
import math

import numpy as np
import jax
import jax.numpy as jnp
from jax import lax
from jax.experimental import pallas as pl
from jax.experimental.pallas import tpu as pltpu

F32 = jnp.float32
BF16 = jnp.bfloat16
HIGHEST = lax.Precision.HIGHEST

EPS = 1e-6
NEG_INF = -1e30
LANES = 128
VMEM_LIMIT = 56 * 1024 * 1024

GDN_QK_HEADS = 8
GDN_V_HEADS = 16
GDN_HEAD_DIM = 128
GDN_CONV = 4
GDN_CHUNK = 64
GDN_QK_W = GDN_QK_HEADS * GDN_HEAD_DIM
GDN_V_W = GDN_V_HEADS * GDN_HEAD_DIM
GDN_CONV_W = 2 * GDN_QK_W + GDN_V_W
GDN_TILE = 256
GDN_HEADS_PER_STEP = 4


def _silu(x):
    return x * jax.nn.sigmoid(x)


def _dot(a, b, **kw):
    return jnp.dot(a, b, preferred_element_type=F32, **kw)


def _dot_nt(a, b, **kw):
    return lax.dot_general(a, b, (((1,), (1,)), ((), ())), preferred_element_type=F32, **kw)


def _mod_kernel(c_ref, w_ref, b_ref, o_ref):
    cond = _silu(c_ref[...])
    o_ref[0] = _dot(cond, w_ref[0], precision=HIGHEST) + b_ref[0]


def _modulation(c, ada_w, ada_b):
    depth, d, d3 = ada_w.shape
    bsz = c.shape[0]
    return pl.pallas_call(
        _mod_kernel,
        grid=(depth, d3 // d),
        in_specs=[
            pl.BlockSpec((bsz, d), lambda i, j: (0, 0)),
            pl.BlockSpec((1, d, d), lambda i, j: (i, 0, j)),
            pl.BlockSpec((1, 1, d), lambda i, j: (i, 0, j)),
        ],
        out_specs=pl.BlockSpec((1, bsz, d), lambda i, j: (i, 0, j)),
        out_shape=jax.ShapeDtypeStruct((depth, bsz, d3), F32),
        name="adaln_mod",
    )(c, ada_w, ada_b.reshape(depth, 1, d3))


def _inproj_kernel(x_ref, g_ref, mod_ref, w_ref, o_ref, *, tn):
    x = x_ref[0]
    m = mod_ref[0]
    y = x * lax.rsqrt(jnp.mean(x * x, axis=-1, keepdims=True) + EPS) * g_ref[...]
    h = (y * (1.0 + m[1:2]) + m[0:1]).astype(BF16)
    for j in range(w_ref.shape[1] // tn):
        o_ref[0, :, j * tn:(j + 1) * tn] = _dot(h, w_ref[:, j * tn:(j + 1) * tn])


def _inproj(x, g, mod, w, *, tm, tn):
    bsz, t, d = x.shape
    n = w.shape[1]
    assert t % tm == 0 and n % tn == 0
    return pl.pallas_call(
        functools.partial(_inproj_kernel, tn=tn),
        grid=(bsz, t // tm),
        in_specs=[
            pl.BlockSpec((1, tm, d), lambda b, i: (b, i, 0)),
            pl.BlockSpec((1, d), lambda b, i: (0, 0)),
            pl.BlockSpec((1, 3, d), lambda b, i: (b, 0, 0)),
            pl.BlockSpec((d, n), lambda b, i: (0, 0), pipeline_mode=pl.Buffered(1)),
        ],
        out_specs=pl.BlockSpec((1, tm, n), lambda b, i: (b, i, 0)),
        out_shape=jax.ShapeDtypeStruct((bsz, t, n), F32),
        compiler_params=pltpu.CompilerParams(
            dimension_semantics=("parallel", "parallel"), vmem_limit_bytes=VMEM_LIMIT),
        name="norm_mod_inproj",
    )(x, g.reshape(1, d), mod, w)


def _outproj_kernel(o_ref, w_ref, x_ref, mod_ref, *rest, final_norm):
    y = _dot(o_ref[0], w_ref[...])
    x = x_ref[0] + mod_ref[0][2:3] * y
    if final_norm:
        fg_ref, out_ref = rest
        x = x * lax.rsqrt(jnp.mean(x * x, axis=-1, keepdims=True) + EPS) * fg_ref[...]
    else:
        (out_ref,) = rest
    out_ref[0] = x


def _outproj(o, w, x, mod, final_g=None, *, tm):
    bsz, t, d = x.shape
    k = o.shape[-1]
    in_specs = [
        pl.BlockSpec((1, tm, k), lambda b, i: (b, i, 0)),
        pl.BlockSpec((k, d), lambda b, i: (0, 0), pipeline_mode=pl.Buffered(1)),
        pl.BlockSpec((1, tm, d), lambda b, i: (b, i, 0)),
        pl.BlockSpec((1, 3, d), lambda b, i: (b, 0, 0)),
    ]
    args = [o, w, x, mod]
    if final_g is not None:
        in_specs.append(pl.BlockSpec((1, d), lambda b, i: (0, 0)))
        args.append(final_g.reshape(1, d))
    return pl.pallas_call(
        functools.partial(_outproj_kernel, final_norm=final_g is not None),
        grid=(bsz, t // tm),
        in_specs=in_specs,
        out_specs=pl.BlockSpec((1, tm, d), lambda b, i: (b, i, 0)),
        out_shape=jax.ShapeDtypeStruct((bsz, t, d), F32),
        compiler_params=pltpu.CompilerParams(
            dimension_semantics=("parallel", "parallel"), vmem_limit_bytes=VMEM_LIMIT),
        name="outproj_residual",
    )(*args)


def _gdn_gates_kernel(ba_ref, alog_ref, dtb_ref, col_ref, row_ref):
    cs, nh = GDN_CHUNK, GDN_V_HEADS
    ba = ba_ref[0]
    lane = lax.broadcasted_iota(jnp.int32, ba.shape, 1)
    g = -jnp.exp(alog_ref[...]) * jax.nn.softplus(ba + dtb_ref[...])
    vals = jnp.where(lane < nh, jax.nn.sigmoid(ba), g)
    r = lax.broadcasted_iota(jnp.int32, (cs, cs), 0)
    c = lax.broadcasted_iota(jnp.int32, (cs, cs), 1)
    tri = (r >= c).astype(F32)
    is_beta = lax.broadcasted_iota(jnp.int32, (cs, LANES), 1) < nh
    pr = lax.broadcasted_iota(jnp.int32, (nh // 2, LANES), 0)
    pc = lax.broadcasted_iota(jnp.int32, (nh // 2, LANES), 1)
    pick_even = (pc == nh + 2 * pr).astype(F32)
    for n in range(ba.shape[0] // cs):
        v = vals[n * cs:(n + 1) * cs]
        cum = _dot(tri, v, precision=HIGHEST)
        out = jnp.where(is_beta, v, cum)
        col_ref[0, n * cs:(n + 1) * cs, :] = out
        both = jnp.concatenate([out, pltpu.roll(out, LANES - 1, axis=1)], axis=0)
        row_ref[0, n] = _dot_nt(pick_even, both, precision=HIGHEST)


def _gdn_gates(proj, a_log, dt_bias, *, tt=512):
    bsz, t, n = proj.shape
    nc = t // GDN_CHUNK
    pad = lambda u: jnp.zeros((1, LANES), F32).at[0, GDN_V_HEADS:2 * GDN_V_HEADS].set(u)
    ba_blk = (n - LANES) // LANES
    return pl.pallas_call(
        _gdn_gates_kernel,
        grid=(bsz, t // tt),
        in_specs=[
            pl.BlockSpec((1, tt, LANES), lambda b, i: (b, i, ba_blk)),
            pl.BlockSpec((1, LANES), lambda b, i: (0, 0)),
            pl.BlockSpec((1, LANES), lambda b, i: (0, 0)),
        ],
        out_specs=[
            pl.BlockSpec((1, tt, LANES), lambda b, i: (b, i, 0)),
            pl.BlockSpec((1, tt // GDN_CHUNK, GDN_V_HEADS // 2, 2 * GDN_CHUNK), lambda b, i: (b, i, 0, 0)),
        ],
        out_shape=[
            jax.ShapeDtypeStruct((bsz, t, LANES), F32),
            jax.ShapeDtypeStruct((bsz, nc, GDN_V_HEADS // 2, 2 * GDN_CHUNK), F32),
        ],
        name="gdn_gates",
    )(proj, pad(a_log), pad(dt_bias))


TRI_BASE = 8
PACK = 4


def _block_rows(p):
    n = p.shape[0]
    blk = lax.broadcasted_iota(jnp.int32, p.shape, 1) // n
    return jnp.concatenate([jnp.where(blk == j, p, 0.0) for j in range(PACK)], axis=0).astype(BF16)


def _tri_inverse_packed(mats):
    n = mats[0].shape[0]
    r = lax.broadcasted_iota(jnp.int32, mats[0].shape, 0)
    c = lax.broadcasted_iota(jnp.int32, mats[0].shape, 1) % n
    same = lambda s: (r // s) == (c // s)
    mm = lambda a, b: _dot(a.astype(BF16), _block_rows(b))
    diag = [jnp.where(same(TRI_BASE), a, 0.0) for a in mats]
    inv = [jnp.where(r == c, 1.0, 0.0) - d for d in diag]
    pw = diag
    k = 1
    while 2 * k < TRI_BASE:
        pw = [mm(m, m) for m in pw]
        inv = [p + mm(p, m) for p, m in zip(inv, pw)]
        k *= 2
        yield None
    s = TRI_BASE
    while s < n:
        sub = same(2 * s) & jnp.logical_not(same(s))
        left = [mm(p, jnp.where(sub, a, 0.0)) for p, a in zip(inv, mats)]
        inv = [p - mm(l, p) for p, l in zip(inv, left)]
        s *= 2
        yield None
    yield inv


def _gdn_chunk_kernel(q_ref, k_ref, v_ref, z_ref, col_ref, row_ref, wq_ref, wk_ref, wv_ref, nw_ref,
                      o_ref, s_ref, qx_ref, kx_ref, vx_ref, hu_ref, hwq_ref, hqk_ref, hkd_ref, hgl_ref):
    cs, dh, nh = GDN_CHUNK, GDN_HEAD_DIM, GDN_V_HEADS
    tt = q_ref.shape[1]
    hps = q_ref.shape[2] // dh
    ncb = tt // cs
    assert ncb % 2 == 0 and PACK == 4
    hg = pl.program_id(1)
    ti = pl.program_id(2)
    wslot = ti % 2
    rslot = 1 - wslot
    jidx = lambda hl, n, e: (hl * ncb + n) * 2 + e

    @pl.when(ti == 0)
    def _():
        s_ref[...] = jnp.zeros_like(s_ref)
        for ext_ref in (qx_ref, kx_ref, vx_ref):
            ext_ref[0:8, :] = jnp.zeros((8, ext_ref.shape[1]), F32)
        for h_ref in (hu_ref, hwq_ref, hqk_ref, hkd_ref, hgl_ref):
            h_ref[1] = jnp.zeros(h_ref.shape[1:], h_ref.dtype)

    def conv_silu(x_ref, ext_ref, w_ref):
        ext_ref[8:8 + tt, :] = x_ref[0]
        w = w_ref[...]
        acc = w[0:1] * ext_ref[5:5 + tt, :]
        for j in range(1, GDN_CONV):
            acc = acc + w[j:j + 1] * ext_ref[5 + j:5 + j + tt, :]
        ext_ref[0:8, :] = ext_ref[tt:tt + 8, :]
        return _silu(acc)

    def l2n(x):
        return x * lax.rsqrt(jnp.sum(x * x, axis=-1, keepdims=True) + EPS)

    chunks = [slice(n * cs, (n + 1) * cs) for n in range(ncb)]

    def prepare():
        q_all = conv_silu(q_ref, qx_ref, wq_ref)
        yield
        k_all = conv_silu(k_ref, kx_ref, wk_ref)
        yield
        v_all = conv_silu(v_ref, vx_ref, wv_ref)
        yield
        q = [l2n(q_all[:, hl * dh:(hl + 1) * dh]) * (dh ** -0.5) for hl in range(hps)]
        k = [l2n(k_all[:, hl * dh:(hl + 1) * dh]) for hl in range(hps)]
        v = [[v_all[:, (2 * hl + e) * dh:(2 * hl + e + 1) * dh] for e in range(2)] for hl in range(hps)]
        yield
        lane = lax.broadcasted_iota(jnp.int32, (tt, LANES), 1)
        colv = col_ref[0]
        column = lambda idx: jnp.sum(jnp.where(lane == idx, colv, 0.0), axis=1, keepdims=True)
        hq = [hg * hps + hl for hl in range(hps)]
        beta = [[column(2 * h + e) for e in range(2)] for h in hq]
        gc = [[column(nh + 2 * h + e) for e in range(2)] for h in hq]
        egc = [[jnp.exp(x) for x in pair] for pair in gc]
        yield

        pr = lax.broadcasted_iota(jnp.int32, (cs, PACK * cs), 0)
        pb = lax.broadcasted_iota(jnp.int32, (cs, PACK * cs), 1) // cs
        pc = lax.broadcasted_iota(jnp.int32, (cs, PACK * cs), 1) % cs
        groups = [(hl, gi) for hl in range(hps) for gi in range(ncb // 2)]
        a_mats, g_rows = [], {}
        for hl, gi in groups:
            k16, q16 = k[hl].astype(BF16), q[hl].astype(BF16)
            pair = (chunks[2 * gi], chunks[2 * gi + 1])
            kdup = [jnp.concatenate([k16[sl], k16[sl]], axis=0) for sl in pair]
            kk = jnp.concatenate([_dot_nt(k16[sl], kd) for sl, kd in zip(pair, kdup)], axis=1)
            qk = jnp.concatenate([_dot_nt(q16[sl], kd) for sl, kd in zip(pair, kdup)], axis=1)
            pick = lambda cols: jnp.where(pb == 0, cols[0][pair[0]], jnp.where(
                pb == 1, cols[1][pair[0]], jnp.where(pb == 2, cols[0][pair[1]], cols[1][pair[1]])))
            g_row = jnp.concatenate([row_ref[0, 2 * gi + j, pl.ds(hq[hl], 1), :] for j in range(2)], axis=1)
            decay = jnp.exp(jnp.where(pr >= pc, pick(gc[hl]) - g_row, -jnp.inf))
            a_mats.append(jnp.where(pr > pc, pick(beta[hl]) * kk * decay, 0.0))
            hqk_ref[wslot, hl * (ncb // 2) + gi] = (qk * decay).astype(BF16)
            g_rows[hl, gi] = g_row
            yield
        t_mats = None
        for t_mats in _tri_inverse_packed(a_mats):
            yield

        for (hl, gi), t_mat in zip(groups, t_mats):
            rhs = []
            for n in (2 * gi, 2 * gi + 1):
                sl = chunks[n]
                for e in range(2):
                    j = 2 * (n % 2) + e
                    g_last = g_rows[hl, gi][:, j * cs + cs - 1:(j + 1) * cs]
                    hgl_ref[wslot, jidx(hl, n, e)] = jnp.broadcast_to(jnp.exp(g_last), (1, LANES))
                    hkd_ref[wslot, jidx(hl, n, e)] = (k[hl][sl] * jnp.exp(g_last - gc[hl][e][sl])).T.astype(BF16)
                    kbeta = k[hl][sl] * beta[hl][e][sl]
                    rhs.append(jnp.concatenate([v[hl][e][sl] * beta[hl][e][sl], kbeta * egc[hl][e][sl]], axis=1))
            out = _dot(_block_rows(t_mat), jnp.concatenate(rhs, axis=0).astype(BF16))
            for n in (2 * gi, 2 * gi + 1):
                for e in range(2):
                    uw = out[(2 * (n % 2) + e) * cs:(2 * (n % 2) + e + 1) * cs]
                    hu_ref[wslot, jidx(hl, n, e)] = uw[:, :dh]
                    q_dec = q[hl][chunks[n]] * egc[hl][e][chunks[n]]
                    hwq_ref[wslot, jidx(hl, n, e)] = jnp.concatenate([uw[:, dh:], q_dec], axis=0).astype(BF16)
            yield

    def recurrence():
        nw = nw_ref[...]
        heads = [(hl, e) for hl in range(hps) for e in range(2)]
        state = {he: s_ref[2 * he[0] + he[1]] for he in heads}
        for n, sl in enumerate(chunks):
            ws = {}
            for hl, e in heads:
                ws[hl, e] = _dot(hwq_ref[rslot, jidx(hl, n, e)], state[hl, e].astype(BF16))
            yield
            for hl, e in heads:
                j = jidx(hl, n, e)
                v16 = (hu_ref[rslot, j] - ws[hl, e][:cs]).astype(BF16)
                lb = 2 * (n % 2) + e
                qkd = hqk_ref[rslot, hl * (ncb // 2) + n // 2][:, lb * cs:(lb + 1) * cs]
                o = ws[hl, e][cs:] + _dot(qkd, v16)
                state[hl, e] = state[hl, e] * hgl_ref[rslot, j] + _dot(hkd_ref[rslot, j], v16)
                o = o * lax.rsqrt(jnp.mean(o * o, axis=-1, keepdims=True) + EPS) * nw
                lanes = slice((2 * hl + e) * dh, (2 * hl + e + 1) * dh)
                o_ref[0, sl, lanes] = (o * _silu(z_ref[0, sl, lanes])).astype(o_ref.dtype)
            yield
        for (hl, e), st in state.items():
            s_ref[2 * hl + e] = st

    prep, rec = prepare(), recurrence()
    n_prep = 5 + hps * (ncb // 2) * 2 + 6
    per_stage = -(-n_prep // (2 * ncb))
    prep_live = rec_live = True
    while prep_live or rec_live:
        for _ in range(per_stage):
            if prep_live:
                prep_live = next(prep, "done") != "done"
        if rec_live:
            rec_live = next(rec, "done") != "done"


def _gdn_chunk(proj, col, row, conv_w, norm_w, *, tt=GDN_TILE, hps=GDN_HEADS_PER_STEP):
    bsz, t, _ = proj.shape
    dh = GDN_HEAD_DIM
    qw, vw = hps * dh, 2 * hps * dh
    k_blk0 = GDN_QK_W // qw
    v_blk0 = 2 * GDN_QK_W // vw
    z_blk0 = GDN_CONV_W // vw
    ncb = tt // GDN_CHUNK
    nt = t // tt
    nj = hps * ncb * 2
    cur = lambda i: jnp.minimum(i, nt - 1)
    prev = lambda i: jnp.maximum(i - 1, 0)
    return pl.pallas_call(
        _gdn_chunk_kernel,
        grid=(bsz, GDN_QK_HEADS // hps, nt + 1),
        in_specs=[
            pl.BlockSpec((1, tt, qw), lambda b, h, i: (b, cur(i), h)),
            pl.BlockSpec((1, tt, qw), lambda b, h, i: (b, cur(i), k_blk0 + h)),
            pl.BlockSpec((1, tt, vw), lambda b, h, i: (b, cur(i), v_blk0 + h)),
            pl.BlockSpec((1, tt, vw), lambda b, h, i: (b, prev(i), z_blk0 + h)),
            pl.BlockSpec((1, tt, LANES), lambda b, h, i: (b, cur(i), 0)),
            pl.BlockSpec((1, ncb, GDN_V_HEADS // 2, 2 * GDN_CHUNK), lambda b, h, i: (b, cur(i), 0, 0)),
            pl.BlockSpec((GDN_CONV, qw), lambda b, h, i: (0, h)),
            pl.BlockSpec((GDN_CONV, qw), lambda b, h, i: (0, k_blk0 + h)),
            pl.BlockSpec((GDN_CONV, vw), lambda b, h, i: (0, v_blk0 + h)),
            pl.BlockSpec((1, dh), lambda b, h, i: (0, 0)),
        ],
        out_specs=pl.BlockSpec((1, tt, vw), lambda b, h, i: (b, prev(i), h)),
        out_shape=jax.ShapeDtypeStruct((bsz, t, GDN_V_W), BF16),
        scratch_shapes=[
            pltpu.VMEM((2 * hps, dh, dh), F32),
            pltpu.VMEM((tt + 8, qw), F32),
            pltpu.VMEM((tt + 8, qw), F32),
            pltpu.VMEM((tt + 8, vw), F32),
            pltpu.VMEM((2, nj, GDN_CHUNK, dh), F32),
            pltpu.VMEM((2, nj, 2 * GDN_CHUNK, dh), BF16),
            pltpu.VMEM((2, nj // PACK, GDN_CHUNK, PACK * GDN_CHUNK), BF16),
            pltpu.VMEM((2, nj, dh, GDN_CHUNK), BF16),
            pltpu.VMEM((2, nj, 1, LANES), F32),
        ],
        compiler_params=pltpu.CompilerParams(
            dimension_semantics=("parallel", "parallel", "arbitrary"), vmem_limit_bytes=VMEM_LIMIT),
        name="gdn_chunk_scan",
    )(proj, proj, proj, proj, col, row, conv_w, conv_w, conv_w, norm_w.reshape(1, dh))


def _gdn_layer(x, mod, norm_g, w_in, conv_w, a_log, dt_bias, norm_w, w_out):
    n_in = w_in.shape[1]
    n_pad = -(-n_in // (7 * LANES)) * (7 * LANES)
    w_in_p = jnp.pad(w_in, ((0, 0), (0, n_pad - n_in))).astype(BF16)
    proj = _inproj(x, norm_g, mod, w_in_p, tm=256, tn=7 * LANES)
    col, row = _gdn_gates(proj, a_log, dt_bias)
    o = _gdn_chunk(proj, col, row, conv_w, norm_w)
    return _outproj(o, w_out.astype(BF16), x, mod, tm=512)


NSA_HEADS = 16
NSA_GROUPS = 4
NSA_HPG = NSA_HEADS // NSA_GROUPS
NSA_HEAD_DIM = 64
NSA_CMP_LEN = 32
NSA_CMP_STRIDE = 16
NSA_SLC_LEN = 64
NSA_TOP_K = 8
NSA_WINDOW = 512
NSA_QBLOCK = 128
NSA_Q_W = NSA_HEADS * NSA_HEAD_DIM
NSA_KV_W = NSA_GROUPS * NSA_HEAD_DIM
REL_BUCKETS = 32
REL_MAX_DIST = 128
FEAT_LANE0 = NSA_HEAD_DIM
CONST_LANE0 = FEAT_LANE0 + 32
NSA_COL_Q = 0
NSA_COL_CMP = NSA_Q_W
NSA_COL_SEL = NSA_COL_CMP + 2 * NSA_KV_W
NSA_COL_WIN = NSA_COL_SEL + 2 * NSA_KV_W
NSA_COL_Z = NSA_COL_WIN + 2 * NSA_KV_W
NSA_COL_GATE = NSA_COL_Z + NSA_Q_W
NSA_PROJ_W = NSA_COL_GATE + LANES


def _nsa_column_perm():
    g, dh = NSA_GROUPS, NSA_HEAD_DIM
    kv0 = NSA_Q_W
    cols = list(range(NSA_Q_W))
    cols += [kv0 + i for i in range(2 * NSA_KV_W)]
    for br in (1, 2):
        for gi in range(g):
            cols += [kv0 + (2 * br) * NSA_KV_W + gi * dh + d for d in range(dh)]
            cols += [kv0 + (2 * br + 1) * NSA_KV_W + gi * dh + d for d in range(dh)]
    gate0 = kv0 + 6 * NSA_KV_W
    cols += [gate0 + 3 * NSA_HEADS + i for i in range(NSA_Q_W)]
    cols += [gate0 + i for i in range(3 * NSA_HEADS)] + [-1] * (LANES - 3 * NSA_HEADS)
    assert len(cols) == NSA_PROJ_W
    return np.asarray(cols, np.int32)


def _rel_bucket_table(n):
    d = np.arange(n)
    max_exact = REL_BUCKETS // 2
    nf = np.maximum(d, 1).astype(np.float64)
    large = max_exact + (np.log(nf / max_exact) / math.log(REL_MAX_DIST / max_exact)
                         * (REL_BUCKETS - max_exact)).astype(np.int32)
    large = np.minimum(large, REL_BUCKETS - 1)
    return np.where(d < max_exact, d, large).astype(np.int32)


def _nsa_tables(rel_bias, t):
    qb = NSA_QBLOCK
    bucket = _rel_bucket_table(t)
    assert np.all(bucket[qb + 1:] == REL_BUCKETS - 1)
    bvec = rel_bias[bucket].T
    far = rel_bias[REL_BUCKETS - 1]
    far_hi = far.astype(BF16)
    far_lo = (far - far_hi.astype(F32)).astype(BF16)
    far_sum = far_hi.astype(F32) + far_lo.astype(F32)
    r = np.arange(qb)[:, None]
    c = np.arange(qb)[None, :]
    d0 = r - c

    def toeplitz(w):
        n = 2 * qb - 1
        ext = jnp.pad(w[:, ::-1], ((0, 0), (0, 1)))
        skew = jnp.tile(ext, (1, qb))[:, :qb * n].reshape(w.shape[0], qb, n)
        return skew[:, :, qb - 1:]

    rel = bvec[:, :2 * qb] - far_sum[:, None]
    t0 = toeplitz(jnp.concatenate([jnp.full((NSA_HEADS, qb - 1), NEG_INF, F32), rel[:, :qb]], axis=1))
    t1 = toeplitz(rel[:, 1:])
    g, hpg = NSA_GROUPS, NSA_HPG
    none = jnp.full_like(t0, NEG_INF)
    near = jnp.stack([t1, t0, t0, none], axis=1).reshape(g, hpg, 2, 2, qb, qb)
    near = near.transpose(0, 2, 3, 5, 1, 4).reshape(g, 2, 2 * qb, hpg * qb)
    nb = t // NSA_CMP_STRIDE
    per_tile = qb // NSA_CMP_STRIDE
    width = 2 * per_tile
    far_d = per_tile * NSA_CMP_STRIDE - (NSA_CMP_LEN - 1) + NSA_CMP_STRIDE
    assert np.all(bucket[far_d:] == REL_BUCKETS - 1)
    dm = r - NSA_CMP_STRIDE * np.arange(width)[None, :] + far_d
    band = jnp.where(dm >= 0, bvec[:, np.maximum(dm, 0)], NEG_INF)
    tiles = []
    for i in range(t // qb):
        j0 = per_tile * i - (per_tile + 1)
        lo, hi = max(j0, 0), min(j0 + width, nb)
        tiles.append(jnp.concatenate([
            jnp.broadcast_to(far[:, None, None], (NSA_HEADS, qb, lo)),
            band[:, :, lo - j0:hi - j0],
            jnp.full((NSA_HEADS, qb, nb - hi), NEG_INF, F32)], axis=2))
    cmp_bias = jnp.stack(tiles, axis=0).reshape(t // qb, g, hpg, qb, nb).transpose(0, 1, 4, 2, 3)
    cmp_bias = cmp_bias.reshape(t // qb, g, nb, hpg * qb)
    qconst = jnp.zeros((g, 8, hpg, qb), F32)
    qconst = qconst.at[:, 0].set(jnp.broadcast_to(far_hi.astype(F32).reshape(g, hpg, 1), (g, hpg, qb)))
    qconst = qconst.at[:, 1].set(jnp.broadcast_to(far_lo.astype(F32).reshape(g, hpg, 1), (g, hpg, qb)))
    return near, cmp_bias, qconst.reshape(g, 8, hpg * qb)


def _overlap_t(t):
    n_cmp = (t - NSA_CMP_LEN) // NSA_CMP_STRIDE + 1
    n_slc = t // NSA_SLC_LEN
    c_start = np.arange(n_cmp)[:, None] * NSA_CMP_STRIDE
    s_start = np.arange(n_slc)[None, :] * NSA_SLC_LEN
    ov = np.clip(np.minimum(c_start + NSA_CMP_LEN, s_start + NSA_SLC_LEN) - np.maximum(c_start, s_start), 0, None)
    ov = ov.astype(np.float32) / NSA_CMP_LEN
    out = np.zeros((32, t // NSA_CMP_STRIDE), np.float32)
    out[:n_slc, :n_cmp] = ov.T
    return out


def _nsa_compress_kernel(x_ref, pos_ref, w1_ref, w2_ref, o_ref, xs_ref):
    t = x_ref.shape[1]
    nb = t // NSA_CMP_STRIDE
    nlt = xs_ref.shape[0]
    for c in range(nlt):
        xs_ref[c, 0:t, :] = x_ref[0, :, c * LANES:(c + 1) * LANES]
        xs_ref[c, t:t + NSA_CMP_STRIDE, :] = jnp.zeros((NSA_CMP_STRIDE, LANES), F32)
    acc = jnp.zeros((nb, w1_ref.shape[2]), F32)
    for l in range(NSA_CMP_LEN):
        xl = jnp.concatenate([xs_ref[c, pl.ds(l, nb, stride=NSA_CMP_STRIDE), :] for c in range(nlt)], axis=1)
        xl = xl + pos_ref[l:l + 1, :]
        acc = acc + _dot(xl.astype(BF16), w1_ref[l])
    hid = _silu(acc).astype(BF16)
    res = _dot(hid, w2_ref[...])
    for g in range(NSA_GROUPS):
        o_ref[0, g] = res[:, g * LANES:(g + 1) * LANES]


def _nsa_compress(proj, cmp_pos, cmp_w1, cmp_w2):
    bsz, t, _ = proj.shape
    g, dh = NSA_GROUPS, NSA_HEAD_DIM
    nb = t // NSA_CMP_STRIDE
    w = 2 * NSA_KV_W
    w1 = cmp_w1.reshape(2, NSA_CMP_LEN, dh, dh).astype(BF16)
    w2 = cmp_w2.astype(BF16)
    place = lambda blk, c0: jnp.pad(blk, [(0, 0)] * (blk.ndim - 1) + [(c0, w - dh - c0)])
    w1c = jnp.concatenate([place(w1[i], (gi * 2 + i) * dh) for i in range(2) for gi in range(g)], axis=1)
    w2c = jnp.concatenate([place(w2[i], (gi * 2 + i) * dh) for gi in range(g) for i in range(2)], axis=0)
    pos = jnp.broadcast_to(cmp_pos[:, :, None, :], (2, NSA_CMP_LEN, g, dh)).transpose(1, 0, 2, 3).reshape(NSA_CMP_LEN, w)
    return pl.pallas_call(
        _nsa_compress_kernel,
        grid=(bsz,),
        in_specs=[
            pl.BlockSpec((1, t, w), lambda b: (b, 0, NSA_COL_CMP // w)),
            pl.BlockSpec((NSA_CMP_LEN, w), lambda b: (0, 0)),
            pl.BlockSpec((NSA_CMP_LEN, w, w), lambda b: (0, 0, 0), pipeline_mode=pl.Buffered(1)),
            pl.BlockSpec((w, w), lambda b: (0, 0)),
        ],
        out_specs=pl.BlockSpec((1, g, nb, LANES), lambda b: (b, 0, 0, 0)),
        out_shape=jax.ShapeDtypeStruct((bsz, g, nb, LANES), F32),
        scratch_shapes=[pltpu.VMEM((w // LANES, t + NSA_CMP_STRIDE, LANES), F32)],
        compiler_params=pltpu.CompilerParams(dimension_semantics=("parallel",), vmem_limit_bytes=VMEM_LIMIT),
        name="nsa_compress",
    )(proj, pos, w1c, w2c)


def _nsa_attn_kernel(q_ref, kvs_ref, kvw_ref, kvc_ref, gate_ref, z_ref, cb_ref, near_ref, qc_ref, ovl_ref,
                     o_ref, ks_ref, vs_ref, kw_ref, vw_ref, gt_ref, ms_ref, accs_ref, mw_ref, accw_ref, sc_ref):
    qb, dh, hpg = NSA_QBLOCK, NSA_HEAD_DIM, NSA_HPG
    t = kvs_ref.shape[1]
    nblk = t // NSA_SLC_LEN
    cols = hpg * qb
    g = pl.program_id(1)
    i = pl.program_id(2)

    @pl.when(i == 0)
    def _():
        tok = lax.broadcasted_iota(jnp.int32, (t, LANES), 0)
        ln = lax.broadcasted_iota(jnp.int32, (t, LANES), 1)
        const = jnp.where((ln == CONST_LANE0) | (ln == CONST_LANE0 + 1), 1.0, 0.0)
        onehot = jnp.where(ln - FEAT_LANE0 == tok // NSA_SLC_LEN, 1.0, 0.0)
        row = lax.broadcasted_iota(jnp.int32, (LANES, t), 0)
        kvs = kvs_ref[0]
        kvw = kvw_ref[0]
        ks_ref[...] = jnp.where(ln < dh, kvs, onehot + const).astype(BF16)
        kw_ref[...] = jnp.where(ln < dh, kvw, const).astype(BF16)
        vs_ref[...] = jnp.where(row == 0, 1.0, kvs.T).astype(BF16)
        vw_ref[...] = jnp.where(row == 0, 1.0, kvw.T).astype(BF16)

    q_t = (q_ref[0] * (dh ** -0.5)).T
    q_heads = jnp.concatenate([q_t[hh * dh:(hh + 1) * dh] for hh in range(hpg)], axis=1)

    def scores(branch, qa_t, start, nk, bias):
        sc = _dot(branch[0][pl.ds(pl.multiple_of(start, qb), nk), :], qa_t)
        return sc if bias is None else sc + bias

    def update(branch, start, nk, sc):
        _, vt_ref, m_ref, acc_ref = branch
        m_old = m_ref[...]
        m_new = jnp.maximum(m_old, jnp.max(sc, axis=0, keepdims=True))
        alpha = jnp.exp(m_old - m_new)
        pe = jnp.exp(sc - m_new).astype(BF16)
        acc_ref[...] = alpha * acc_ref[...] + _dot(vt_ref[:, pl.ds(pl.multiple_of(start, qb), nk)], pe)
        m_ref[...] = m_new

    sel = (ks_ref, vs_ref, ms_ref, accs_ref)
    win = (kw_ref, vw_ref, mw_ref, accw_ref)
    for m_ref, acc_ref in ((ms_ref, accs_ref), (mw_ref, accw_ref)):
        m_ref[...] = jnp.full(m_ref.shape, NEG_INF, F32)
        acc_ref[...] = jnp.zeros(acc_ref.shape, F32)

    nwt = NSA_WINDOW // qb
    assert nwt == 4
    pad_rows = jnp.zeros((LANES - CONST_LANE0 - 8, cols), F32)
    qa_win = jnp.concatenate([q_heads, jnp.zeros((32, cols), F32), qc_ref[0], pad_rows], axis=0).astype(BF16)

    kvc = kvc_ref[0, 0]
    lane_k = lax.broadcasted_iota(jnp.int32, kvc.shape, 1)
    kc16 = jnp.where(lane_k < dh, kvc, 0.0).astype(BF16)
    s = _dot(kc16, qa_win) + cb_ref[0, 0]

    kk = lax.broadcasted_iota(jnp.int32, (qb, cols), 0)
    rr = lax.broadcasted_iota(jnp.int32, (qb, cols), 1) % qb
    w4_start = jnp.maximum(i - nwt, 0) * qb
    sc_w4 = scores(win, qa_win, w4_start, qb, jnp.where((rr < kk) & (i >= nwt), 0.0, NEG_INF))

    w32_start = jnp.maximum(i - 3, 0) * qb
    row2 = lax.broadcasted_iota(jnp.int32, (2 * qb, cols), 0)
    sc_w32 = scores(win, qa_win, w32_start, 2 * qb,
                    jnp.where(row2 < (i - 1) * qb - w32_start, 0.0, NEG_INF))
    near_start = jnp.maximum(i - 1, 0) * qb
    near_bias = near_ref[0, jnp.where(i == 0, 1, 0)]
    sc_wn = scores(win, qa_win, near_start, 2 * qb, near_bias)

    s = jnp.exp(s - jnp.max(s, axis=0, keepdims=True))
    p = s / jnp.sum(s, axis=0, keepdims=True)
    tq_lane = i * qb + lax.broadcasted_iota(jnp.int32, (1, cols), 1) % qb
    p16 = (p * (tq_lane >= NSA_CMP_LEN - 1).astype(F32)).astype(BF16)
    o_cmp = _dot(kvc.T.astype(BF16), p16)
    ovl = ovl_ref[...].astype(BF16)
    imp = _dot(ovl, p16[:, 0:qb])
    for hh in range(1, hpg):
        imp = imp + _dot(ovl, p16[:, hh * qb:(hh + 1) * qb])

    update(win, w4_start, qb, sc_w4)

    blk = lax.broadcasted_iota(jnp.int32, (32, qb), 0)
    tq = i * qb + lax.broadcasted_iota(jnp.int32, (32, qb), 1)
    cur = tq // NSA_SLC_LEN
    forced = (blk == 0) | (blk == cur) | (blk == cur - 1)
    val = jnp.where(forced, jnp.inf, jnp.where(blk * NSA_SLC_LEN <= tq, imp, -jnp.inf))
    cnt = jnp.zeros((32, qb), jnp.int32)
    for s2 in range(nblk):
        other = val[s2:s2 + 1, :]
        cnt = cnt + ((other > val) | ((other == val) & (s2 < blk))).astype(jnp.int32)
    feat = jnp.where((cnt < min(NSA_TOP_K, nblk)) & (blk < nblk), 0.0, NEG_INF)
    qa = jnp.concatenate([q_heads, jnp.concatenate([feat] * hpg, axis=1), qc_ref[0], pad_rows],
                         axis=0).astype(BF16)

    update(win, w32_start, 2 * qb, sc_w32)
    sc_sn = scores(sel, qa, near_start, 2 * qb, near_bias)
    update(win, near_start, 2 * qb, sc_wn)
    update(sel, near_start, 2 * qb, sc_sn)

    n_far = jnp.maximum(i - 1, 0)
    n_pairs = (n_far + 1) // 2
    pair_start = lambda p: jnp.maximum(2 * p - n_far % 2, 0) * qb
    last_pair = jnp.maximum(n_pairs - 1, 0)
    sc_ref[...] = scores(sel, qa, 0, 2 * qb, jnp.where(row2 < (2 - n_far % 2) * qb, 0.0, NEG_INF))

    def sel_body(k, carry):
        sc_odd = scores(sel, qa, pair_start(2 * k + 1), 2 * qb, None)
        update(sel, pair_start(2 * k), 2 * qb, sc_ref[...])
        sc_ref[...] = scores(sel, qa, pair_start(jnp.minimum(2 * k + 2, last_pair)), 2 * qb, None)
        update(sel, pair_start(2 * k + 1), 2 * qb, sc_odd)
        return carry

    lax.fori_loop(0, n_pairs // 2, sel_body, 0)

    @pl.when(n_pairs % 2 == 1)
    def _():
        update(sel, pair_start(n_pairs - 1), 2 * qb, sc_ref[...])

    def finish(acc_ref):
        acc = acc_ref[...]
        return acc[dh:] / acc[0:1]

    o_slc = finish(accs_ref)
    o_win = finish(accw_ref)

    gt_ref[...] = jax.nn.sigmoid(gate_ref[0]).T
    outs = []
    for hh in range(hpg):
        sl = slice(hh * qb, (hh + 1) * qb)
        base = (g * hpg + hh) * 3
        gate = [gt_ref[pl.ds(base + br, 1), :] for br in range(3)]
        outs.append(gate[0] * o_cmp[dh:, sl] + gate[1] * o_slc[:, sl] + gate[2] * o_win[:, sl])
    out = jnp.concatenate(outs, axis=0).T
    o_ref[0] = (out * _silu(z_ref[0])).astype(o_ref.dtype)


def _nsa_attn(proj, kv_cmp, near, cmp_bias, qconst, ovl):
    bsz, t, _ = proj.shape
    qb, hpg = NSA_QBLOCK, NSA_HPG
    gw = hpg * NSA_HEAD_DIM
    nb = t // NSA_CMP_STRIDE
    cols = hpg * qb
    return pl.pallas_call(
        _nsa_attn_kernel,
        grid=(bsz, NSA_GROUPS, t // qb),
        in_specs=[
            pl.BlockSpec((1, qb, gw), lambda b, g, i: (b, i, NSA_COL_Q // gw + g)),
            pl.BlockSpec((1, t, LANES), lambda b, g, i: (b, 0, NSA_COL_SEL // LANES + g)),
            pl.BlockSpec((1, t, LANES), lambda b, g, i: (b, 0, NSA_COL_WIN // LANES + g)),
            pl.BlockSpec((1, 1, nb, LANES), lambda b, g, i: (b, g, 0, 0)),
            pl.BlockSpec((1, qb, LANES), lambda b, g, i: (b, i, NSA_COL_GATE // LANES)),
            pl.BlockSpec((1, qb, gw), lambda b, g, i: (b, i, NSA_COL_Z // gw + g)),
            pl.BlockSpec((1, 1, nb, cols), lambda b, g, i: (i, g, 0, 0)),
            pl.BlockSpec((1, 2, 2 * qb, cols), lambda b, g, i: (g, 0, 0, 0)),
            pl.BlockSpec((1, 8, cols), lambda b, g, i: (g, 0, 0)),
            pl.BlockSpec((32, nb), lambda b, g, i: (0, 0)),
        ],
        out_specs=pl.BlockSpec((1, qb, gw), lambda b, g, i: (b, i, g)),
        out_shape=jax.ShapeDtypeStruct((bsz, t, NSA_Q_W), BF16),
        scratch_shapes=[
            pltpu.VMEM((t, LANES), BF16), pltpu.VMEM((LANES, t), BF16),
            pltpu.VMEM((t, LANES), BF16), pltpu.VMEM((LANES, t), BF16),
            pltpu.VMEM((LANES, qb), F32),
            pltpu.VMEM((1, cols), F32), pltpu.VMEM((LANES, cols), F32),
            pltpu.VMEM((1, cols), F32), pltpu.VMEM((LANES, cols), F32),
            pltpu.VMEM((2 * qb, cols), F32),
        ],
        compiler_params=pltpu.CompilerParams(
            dimension_semantics=("parallel", "parallel", "arbitrary"), vmem_limit_bytes=VMEM_LIMIT),
        name="nsa_attention",
    )(proj, proj, proj, kv_cmp, proj, proj, cmp_bias, near, qconst, ovl)


def _nsa_layer(x, mod, norm_g, w_in, cmp_pos, cmp_w1, cmp_w2, rel_bias, w_out, final_g):
    t = x.shape[1]
    assert t // NSA_SLC_LEN <= 32 and NSA_COL_Z % (NSA_HPG * NSA_HEAD_DIM) == 0
    perm = _nsa_column_perm()
    cuts = [0] + [j for j in range(1, len(perm)) if perm[j] != perm[j - 1] + (perm[j - 1] >= 0)] + [len(perm)]
    runs = [(int(perm[a]), b - a) for a, b in zip(cuts[:-1], cuts[1:])]
    w16 = w_in.astype(BF16)
    w_in_p = jnp.concatenate([w16[:, s:s + n] if s >= 0 else jnp.zeros((w_in.shape[0], n), BF16)
                              for s, n in runs], axis=1)
    proj = _inproj(x, norm_g, mod, w_in_p, tm=256, tn=NSA_PROJ_W)
    kv_cmp = _nsa_compress(proj, cmp_pos, cmp_w1, cmp_w2)
    near, cmp_bias, qconst = _nsa_tables(rel_bias, t)
    o = _nsa_attn(proj, kv_cmp, near, cmp_bias, qconst, jnp.asarray(_overlap_t(t)))
    return _outproj(o, w_out.astype(BF16), x, mod, final_g, tm=512)


def kernel(x, c, ada_w, ada_b, norm_g, gdn_w_in, gdn_conv_w, gdn_a_log, gdn_dt_bias, gdn_norm_w, gdn_w_out,
           nsa_w_in, nsa_cmp_pos, nsa_cmp_w1, nsa_cmp_w2, nsa_w_out, rel_bias, final_g):
    bsz, t, d = x.shape
    mod = _modulation(c, ada_w, ada_b).reshape(ada_w.shape[0], bsz, 3, d)
    x = _gdn_layer(x, mod[0], norm_g[0], gdn_w_in[0], gdn_conv_w[0], gdn_a_log[0], gdn_dt_bias[0],
                   gdn_norm_w[0], gdn_w_out[0])
    return _nsa_layer(x, mod[1], norm_g[1], nsa_w_in[0], nsa_cmp_pos[0], nsa_cmp_w1[0], nsa_cmp_w2[0],
                      rel_bias, nsa_w_out[0], final_g)
```

```python
import functools
import math

import numpy as np
import jax
import jax.numpy as jnp
from jax import lax
from jax.experimental import pallas as pl
from jax.experimental.pallas import tpu as pltpu

F32 = jnp.float32
BF16 = jnp.bfloat16
HIGHEST = lax.Precision.HIGHEST

EPS = 1e-6
NEG_INF = -1e30
LANES = 128
VMEM_LIMIT = 56 * 1024 * 1024

GDN_QK_HEADS = 8
GDN_V_HEADS = 16
GDN_HEAD_DIM = 128
GDN_CONV = 4
GDN_CHUNK = 64
GDN_QK_W = GDN_QK_HEADS * GDN_HEAD_DIM
GDN_V_W = GDN_V_HEADS * GDN_HEAD_DIM
GDN_CONV_W = 2 * GDN_QK_W + GDN_V_W
GDN_TILE = 256
GDN_HEADS_PER_STEP = 4


def _silu(x):
    return x * jax.nn.sigmoid(x)


def _dot(a, b, **kw):
    return jnp.dot(a, b, preferred_element_type=F32, **kw)


def _dot_nt(a, b, **kw):
    return lax.dot_general(a, b, (((1,), (1,)), ((), ())), preferred_element_type=F32, **kw)


def _mod_kernel(c_ref, w_ref, b_ref, o_ref):
    cond = _silu(c_ref[...])
    o_ref[0] = _dot(cond, w_ref[0], precision=HIGHEST) + b_ref[0]


def _modulation(c, ada_w, ada_b):
    depth, d, d3 = ada_w.shape
    bsz = c.shape[0]
    return pl.pallas_call(
        _mod_kernel,
        grid=(depth, d3 // d),
        in_specs=[
            pl.BlockSpec((bsz, d), lambda i, j: (0, 0)),
            pl.BlockSpec((1, d, d), lambda i, j: (i, 0, j)),
            pl.BlockSpec((1, 1, d), lambda i, j: (i, 0, j)),
        ],
        out_specs=pl.BlockSpec((1, bsz, d), lambda i, j: (i, 0, j)),
        out_shape=jax.ShapeDtypeStruct((depth, bsz, d3), F32),
        name="adaln_mod",
    )(c, ada_w, ada_b.reshape(depth, 1, d3))


def _inproj_kernel(x_ref, g_ref, mod_ref, w_ref, o_ref, *, tn):
    x = x_ref[0]
    m = mod_ref[0]
    y = x * lax.rsqrt(jnp.mean(x * x, axis=-1, keepdims=True) + EPS) * g_ref[...]
    h = (y * (1.0 + m[1:2]) + m[0:1]).astype(BF16)
    for j in range(w_ref.shape[1] // tn):
        o_ref[0, :, j * tn:(j + 1) * tn] = _dot(h, w_ref[:, j * tn:(j + 1) * tn])


def _inproj(x, g, mod, w, *, tm, tn):
    bsz, t, d = x.shape
    n = w.shape[1]
    assert t % tm == 0 and n % tn == 0
    return pl.pallas_call(
        functools.partial(_inproj_kernel, tn=tn),
        grid=(bsz, t // tm),
        in_specs=[
            pl.BlockSpec((1, tm, d), lambda b, i: (b, i, 0)),
            pl.BlockSpec((1, d), lambda b, i: (0, 0)),
            pl.BlockSpec((1, 3, d), lambda b, i: (b, 0, 0)),
            pl.BlockSpec((d, n), lambda b, i: (0, 0), pipeline_mode=pl.Buffered(1)),
        ],
        out_specs=pl.BlockSpec((1, tm, n), lambda b, i: (b, i, 0)),
        out_shape=jax.ShapeDtypeStruct((bsz, t, n), F32),
        compiler_params=pltpu.CompilerParams(
            dimension_semantics=("parallel", "parallel"), vmem_limit_bytes=VMEM_LIMIT),
        name="norm_mod_inproj",
    )(x, g.reshape(1, d), mod, w)


def _gdn_inproj_kernel(x_ref, g_ref, mod_ref, w_ref, cw_ref, o_ref, ext_ref, tail_ref, *, tn):
    dh = GDN_HEAD_DIM
    tm = x_ref.shape[1]
    x = x_ref[0]
    m = mod_ref[0]
    y = x * lax.rsqrt(jnp.mean(x * x, axis=-1, keepdims=True) + EPS) * g_ref[...]
    h = (y * (1.0 + m[1:2]) + m[0:1]).astype(BF16)

    @pl.when(pl.program_id(1) == 0)
    def _():
        tail_ref[...] = jnp.zeros_like(tail_ref)

    def conv_tile(j):
        cols = slice(j * tn, (j + 1) * tn)
        ext_ref[0:8, :] = tail_ref[:, cols]
        ext_ref[8:8 + tm, :] = o_ref[0, :, cols]
        tail_ref[:, cols] = ext_ref[tm:tm + 8, :]
        w = cw_ref[:, cols]
        acc = w[0:1] * ext_ref[5:5 + tm, :]
        for tap in range(1, GDN_CONV):
            acc = acc + w[tap:tap + 1] * ext_ref[5 + tap:5 + tap + tm, :]
        act = _silu(acc)
        if j * tn < 2 * GDN_QK_W:
            scale = dh ** -0.5 if j * tn < GDN_QK_W else 1.0
            heads = [act[:, c * dh:(c + 1) * dh] for c in range(tn // dh)]
            heads = [hd * (lax.rsqrt(jnp.sum(hd * hd, axis=-1, keepdims=True) + EPS) * scale) for hd in heads]
            act = jnp.concatenate(heads, axis=1)
        o_ref[0, :, cols] = act

    n = w_ref.shape[1]
    n_conv = GDN_CONV_W // tn
    bounds = [(j * tn, min((j + 1) * tn, n)) for j in range(-(-n // tn))]
    for j, (lo, hi) in enumerate(bounds):
        o_ref[0, :, lo:hi] = _dot(h, w_ref[:, lo:hi])
        if 1 <= j <= n_conv:
            conv_tile(j - 1)


def _gdn_inproj(x, g, mod, w, conv_w, *, tm, tn):
    bsz, t, d = x.shape
    n = w.shape[1]
    assert t % tm == 0 and GDN_CONV_W % tn == 0 and GDN_QK_W % tn == 0 and n > GDN_CONV_W
    return pl.pallas_call(
        functools.partial(_gdn_inproj_kernel, tn=tn),
        grid=(bsz, t // tm),
        in_specs=[
            pl.BlockSpec((1, tm, d), lambda b, i: (b, i, 0)),
            pl.BlockSpec((1, d), lambda b, i: (0, 0)),
            pl.BlockSpec((1, 3, d), lambda b, i: (b, 0, 0)),
            pl.BlockSpec((d, n), lambda b, i: (0, 0), pipeline_mode=pl.Buffered(1)),
            pl.BlockSpec((GDN_CONV, GDN_CONV_W), lambda b, i: (0, 0)),
        ],
        out_specs=pl.BlockSpec((1, tm, n), lambda b, i: (b, i, 0)),
        out_shape=jax.ShapeDtypeStruct((bsz, t, n), F32),
        scratch_shapes=[pltpu.VMEM((tm + 8, tn), F32), pltpu.VMEM((8, GDN_CONV_W), F32)],
        compiler_params=pltpu.CompilerParams(
            dimension_semantics=("parallel", "arbitrary"), vmem_limit_bytes=VMEM_LIMIT),
        name="gdn_norm_inproj_conv",
    )(x, g.reshape(1, d), mod, w, conv_w)


def _outproj_kernel(o_ref, w_ref, x_ref, mod_ref, *rest, final_norm):
    y = _dot(o_ref[0], w_ref[...])
    x = x_ref[0] + mod_ref[0][2:3] * y
    if final_norm:
        fg_ref, out_ref = rest
        x = x * lax.rsqrt(jnp.mean(x * x, axis=-1, keepdims=True) + EPS) * fg_ref[...]
    else:
        (out_ref,) = rest
    out_ref[0] = x


def _outproj(o, w, x, mod, final_g=None, *, tm):
    bsz, t, d = x.shape
    k = o.shape[-1]
    in_specs = [
        pl.BlockSpec((1, tm, k), lambda b, i: (b, i, 0)),
        pl.BlockSpec((k, d), lambda b, i: (0, 0), pipeline_mode=pl.Buffered(1)),
        pl.BlockSpec((1, tm, d), lambda b, i: (b, i, 0)),
        pl.BlockSpec((1, 3, d), lambda b, i: (b, 0, 0)),
    ]
    args = [o, w, x, mod]
    if final_g is not None:
        in_specs.append(pl.BlockSpec((1, d), lambda b, i: (0, 0)))
        args.append(final_g.reshape(1, d))
    return pl.pallas_call(
        functools.partial(_outproj_kernel, final_norm=final_g is not None),
        grid=(bsz, t // tm),
        in_specs=in_specs,
        out_specs=pl.BlockSpec((1, tm, d), lambda b, i: (b, i, 0)),
        out_shape=jax.ShapeDtypeStruct((bsz, t, d), F32),
        compiler_params=pltpu.CompilerParams(
            dimension_semantics=("parallel", "parallel"), vmem_limit_bytes=VMEM_LIMIT),
        name="outproj_residual",
    )(*args)


def _gdn_gates_kernel(ba_ref, alog_ref, dtb_ref, col_ref, row_ref):
    cs, nh = GDN_CHUNK, GDN_V_HEADS
    ba = ba_ref[0]
    lane = lax.broadcasted_iota(jnp.int32, ba.shape, 1)
    g = -jnp.exp(alog_ref[...]) * jax.nn.softplus(ba + dtb_ref[...])
    vals = jnp.where(lane < nh, jax.nn.sigmoid(ba), g)
    r = lax.broadcasted_iota(jnp.int32, (cs, cs), 0)
    c = lax.broadcasted_iota(jnp.int32, (cs, cs), 1)
    tri = (r >= c).astype(F32)
    is_beta = lax.broadcasted_iota(jnp.int32, (cs, LANES), 1) < nh
    pr = lax.broadcasted_iota(jnp.int32, (nh // 2, LANES), 0)
    pc = lax.broadcasted_iota(jnp.int32, (nh // 2, LANES), 1)
    pick_even = (pc == nh + 2 * pr).astype(F32)
    for n in range(ba.shape[0] // cs):
        v = vals[n * cs:(n + 1) * cs]
        cum = _dot(tri, v, precision=HIGHEST)
        out = jnp.where(is_beta, v, cum)
        col_ref[0, n * cs:(n + 1) * cs, :] = out
        both = jnp.concatenate([out, pltpu.roll(out, LANES - 1, axis=1)], axis=0)
        row_ref[0, n] = _dot_nt(pick_even, both, precision=HIGHEST)


def _gdn_gates(proj, a_log, dt_bias, *, tt=512):
    bsz, t, n = proj.shape
    nc = t // GDN_CHUNK
    pad = lambda u: jnp.zeros((1, LANES), F32).at[0, GDN_V_HEADS:2 * GDN_V_HEADS].set(u)
    ba_blk = (n - LANES) // LANES
    return pl.pallas_call(
        _gdn_gates_kernel,
        grid=(bsz, t // tt),
        in_specs=[
            pl.BlockSpec((1, tt, LANES), lambda b, i: (b, i, ba_blk)),
            pl.BlockSpec((1, LANES), lambda b, i: (0, 0)),
            pl.BlockSpec((1, LANES), lambda b, i: (0, 0)),
        ],
        out_specs=[
            pl.BlockSpec((1, tt, LANES), lambda b, i: (b, i, 0)),
            pl.BlockSpec((1, tt // GDN_CHUNK, GDN_V_HEADS // 2, 2 * GDN_CHUNK), lambda b, i: (b, i, 0, 0)),
        ],
        out_shape=[
            jax.ShapeDtypeStruct((bsz, t, LANES), F32),
            jax.ShapeDtypeStruct((bsz, nc, GDN_V_HEADS // 2, 2 * GDN_CHUNK), F32),
        ],
        name="gdn_gates",
    )(proj, pad(a_log), pad(dt_bias))


TRI_BASE = 8
PACK = 4


def _block_rows(p):
    n = p.shape[0]
    blk = lax.broadcasted_iota(jnp.int32, p.shape, 1) // n
    return jnp.concatenate([jnp.where(blk == j, p, 0.0) for j in range(PACK)], axis=0).astype(BF16)


def _tri_inverse_packed(mats):
    n = mats[0].shape[0]
    r = lax.broadcasted_iota(jnp.int32, mats[0].shape, 0)
    c = lax.broadcasted_iota(jnp.int32, mats[0].shape, 1) % n
    same = lambda s: (r // s) == (c // s)
    mm = lambda a, b: _dot(a.astype(BF16), _block_rows(b))
    diag = [jnp.where(same(TRI_BASE), a, 0.0) for a in mats]
    inv = [jnp.where(r == c, 1.0, 0.0) - d for d in diag]
    pw = diag
    k = 1
    while 2 * k < TRI_BASE:
        pw = [mm(m, m) for m in pw]
        inv = [p + mm(p, m) for p, m in zip(inv, pw)]
        k *= 2
        yield None
    s = TRI_BASE
    while s < n:
        sub = same(2 * s) & jnp.logical_not(same(s))
        left = [mm(p, jnp.where(sub, a, 0.0)) for p, a in zip(inv, mats)]
        inv = [p - mm(l, p) for p, l in zip(inv, left)]
        s *= 2
        yield None
    yield inv


def _gdn_chunk_kernel(q_ref, k_ref, v_ref, z_ref, col_ref, row_ref, nw_ref,
                      o_ref, s_ref, hu_ref, hwq_ref, hqk_ref, hkd_ref, hgl_ref):
    cs, dh, nh = GDN_CHUNK, GDN_HEAD_DIM, GDN_V_HEADS
    tt = q_ref.shape[1]
    hps = q_ref.shape[2] // dh
    ncb = tt // cs
    assert ncb % 2 == 0 and PACK == 4
    hg = pl.program_id(1)
    ti = pl.program_id(2)
    wslot = ti % 2
    rslot = 1 - wslot
    jidx = lambda hl, n, e: (hl * ncb + n) * 2 + e

    @pl.when(ti == 0)
    def _():
        s_ref[...] = jnp.zeros_like(s_ref)
        for h_ref in (hu_ref, hwq_ref, hqk_ref, hkd_ref, hgl_ref):
            h_ref[1] = jnp.zeros(h_ref.shape[1:], h_ref.dtype)

    chunks = [slice(n * cs, (n + 1) * cs) for n in range(ncb)]

    def prepare():
        q = [q_ref[0, :, hl * dh:(hl + 1) * dh] for hl in range(hps)]
        k = [k_ref[0, :, hl * dh:(hl + 1) * dh] for hl in range(hps)]
        v = [[v_ref[0, :, (2 * hl + e) * dh:(2 * hl + e + 1) * dh] for e in range(2)] for hl in range(hps)]
        lane = lax.broadcasted_iota(jnp.int32, (tt, LANES), 1)
        colv = col_ref[0]
        column = lambda idx: jnp.sum(jnp.where(lane == idx, colv, 0.0), axis=1, keepdims=True)
        hq = [hg * hps + hl for hl in range(hps)]
        beta = [[column(2 * h + e) for e in range(2)] for h in hq]
        gc = [[column(nh + 2 * h + e) for e in range(2)] for h in hq]
        egc = [[jnp.exp(x) for x in pair] for pair in gc]
        yield

        pr = lax.broadcasted_iota(jnp.int32, (cs, PACK * cs), 0)
        pb = lax.broadcasted_iota(jnp.int32, (cs, PACK * cs), 1) // cs
        pc = lax.broadcasted_iota(jnp.int32, (cs, PACK * cs), 1) % cs
        groups = [(hl, gi) for hl in range(hps) for gi in range(ncb // 2)]
        a_mats, g_rows = [], {}
        for hl, gi in groups:
            k16, q16 = k[hl].astype(BF16), q[hl].astype(BF16)
            pair = (chunks[2 * gi], chunks[2 * gi + 1])
            kdup = [jnp.concatenate([k16[sl], k16[sl]], axis=0) for sl in pair]
            kk = jnp.concatenate([_dot_nt(k16[sl], kd) for sl, kd in zip(pair, kdup)], axis=1)
            qk = jnp.concatenate([_dot_nt(q16[sl], kd) for sl, kd in zip(pair, kdup)], axis=1)
            pick = lambda cols: jnp.where(pb == 0, cols[0][pair[0]], jnp.where(
                pb == 1, cols[1][pair[0]], jnp.where(pb == 2, cols[0][pair[1]], cols[1][pair[1]])))
            g_row = jnp.concatenate([row_ref[0, 2 * gi + j, pl.ds(hq[hl], 1), :] for j in range(2)], axis=1)
            decay = jnp.exp(jnp.where(pr >= pc, pick(gc[hl]) - g_row, -jnp.inf))
            a_mats.append(jnp.where(pr > pc, pick(beta[hl]) * kk * decay, 0.0))
            hqk_ref[wslot, hl * (ncb // 2) + gi] = (qk * decay).astype(BF16)
            g_rows[hl, gi] = g_row
            yield
        t_mats = None
        for t_mats in _tri_inverse_packed(a_mats):
            yield

        for (hl, gi), t_mat in zip(groups, t_mats):
            rhs = []
            for n in (2 * gi, 2 * gi + 1):
                sl = chunks[n]
                for e in range(2):
                    j = 2 * (n % 2) + e
                    g_last = g_rows[hl, gi][:, j * cs + cs - 1:(j + 1) * cs]
                    hgl_ref[wslot, jidx(hl, n, e)] = jnp.broadcast_to(jnp.exp(g_last), (1, LANES))
                    hkd_ref[wslot, jidx(hl, n, e)] = (k[hl][sl] * jnp.exp(g_last - gc[hl][e][sl])).T.astype(BF16)
                    kbeta = k[hl][sl] * beta[hl][e][sl]
                    rhs.append(jnp.concatenate([v[hl][e][sl] * beta[hl][e][sl], kbeta * egc[hl][e][sl]], axis=1))
            out = _dot(_block_rows(t_mat), jnp.concatenate(rhs, axis=0).astype(BF16))
            for n in (2 * gi, 2 * gi + 1):
                for e in range(2):
                    uw = out[(2 * (n % 2) + e) * cs:(2 * (n % 2) + e + 1) * cs]
                    hu_ref[wslot, jidx(hl, n, e)] = uw[:, :dh]
                    q_dec = q[hl][chunks[n]] * egc[hl][e][chunks[n]]
                    hwq_ref[wslot, jidx(hl, n, e)] = jnp.concatenate([uw[:, dh:], q_dec], axis=0).astype(BF16)
            yield

    def recurrence():
        nw = nw_ref[...]
        heads = [(hl, e) for hl in range(hps) for e in range(2)]
        state = {he: s_ref[2 * he[0] + he[1]] for he in heads}
        for n, sl in enumerate(chunks):
            ws = {}
            for hl, e in heads:
                ws[hl, e] = _dot(hwq_ref[rslot, jidx(hl, n, e)], state[hl, e].astype(BF16))
            yield
            for hl, e in heads:
                j = jidx(hl, n, e)
                v16 = (hu_ref[rslot, j] - ws[hl, e][:cs]).astype(BF16)
                lb = 2 * (n % 2) + e
                qkd = hqk_ref[rslot, hl * (ncb // 2) + n // 2][:, lb * cs:(lb + 1) * cs]
                o = ws[hl, e][cs:] + _dot(qkd, v16)
                state[hl, e] = state[hl, e] * hgl_ref[rslot, j] + _dot(hkd_ref[rslot, j], v16)
                o = o * lax.rsqrt(jnp.mean(o * o, axis=-1, keepdims=True) + EPS) * nw
                lanes = slice((2 * hl + e) * dh, (2 * hl + e + 1) * dh)
                o_ref[0, sl, lanes] = (o * _silu(z_ref[0, sl, lanes])).astype(o_ref.dtype)
            yield
        for (hl, e), st in state.items():
            s_ref[2 * hl + e] = st

    prep, rec = prepare(), recurrence()
    n_prep = 1 + hps * (ncb // 2) * 2 + 6
    per_stage = -(-n_prep // (2 * ncb))
    prep_live = rec_live = True
    while prep_live or rec_live:
        for _ in range(per_stage):
            if prep_live:
                prep_live = next(prep, "done") != "done"
        if rec_live:
            rec_live = next(rec, "done") != "done"


def _gdn_chunk(proj, col, row, norm_w, *, tt=GDN_TILE, hps=GDN_HEADS_PER_STEP):
    bsz, t, _ = proj.shape
    dh = GDN_HEAD_DIM
    qw, vw = hps * dh, 2 * hps * dh
    k_blk0 = GDN_QK_W // qw
    v_blk0 = 2 * GDN_QK_W // vw
    z_blk0 = GDN_CONV_W // vw
    ncb = tt // GDN_CHUNK
    nt = t // tt
    nj = hps * ncb * 2
    cur = lambda i: jnp.minimum(i, nt - 1)
    prev = lambda i: jnp.maximum(i - 1, 0)
    return pl.pallas_call(
        _gdn_chunk_kernel,
        grid=(bsz, GDN_QK_HEADS // hps, nt + 1),
        in_specs=[
            pl.BlockSpec((1, tt, qw), lambda b, h, i: (b, cur(i), h)),
            pl.BlockSpec((1, tt, qw), lambda b, h, i: (b, cur(i), k_blk0 + h)),
            pl.BlockSpec((1, tt, vw), lambda b, h, i: (b, cur(i), v_blk0 + h)),
            pl.BlockSpec((1, tt, vw), lambda b, h, i: (b, prev(i), z_blk0 + h)),
            pl.BlockSpec((1, tt, LANES), lambda b, h, i: (b, cur(i), 0)),
            pl.BlockSpec((1, ncb, GDN_V_HEADS // 2, 2 * GDN_CHUNK), lambda b, h, i: (b, cur(i), 0, 0)),
            pl.BlockSpec((1, dh), lambda b, h, i: (0, 0)),
        ],
        out_specs=pl.BlockSpec((1, tt, vw), lambda b, h, i: (b, prev(i), h)),
        out_shape=jax.ShapeDtypeStruct((bsz, t, GDN_V_W), BF16),
        scratch_shapes=[
            pltpu.VMEM((2 * hps, dh, dh), F32),
            pltpu.VMEM((2, nj, GDN_CHUNK, dh), F32),
            pltpu.VMEM((2, nj, 2 * GDN_CHUNK, dh), BF16),
            pltpu.VMEM((2, nj // PACK, GDN_CHUNK, PACK * GDN_CHUNK), BF16),
            pltpu.VMEM((2, nj, dh, GDN_CHUNK), BF16),
            pltpu.VMEM((2, nj, 1, LANES), F32),
        ],
        compiler_params=pltpu.CompilerParams(
            dimension_semantics=("parallel", "parallel", "arbitrary"), vmem_limit_bytes=VMEM_LIMIT),
        name="gdn_chunk_scan",
    )(proj, proj, proj, proj, col, row, norm_w.reshape(1, dh))


def _gdn_layer(x, mod, norm_g, w_in, conv_w, a_log, dt_bias, norm_w, w_out):
    n_in = w_in.shape[1]
    n_pad = -(-n_in // (7 * LANES)) * (7 * LANES)
    w_in_p = jnp.pad(w_in, ((0, 0), (0, n_pad - n_in))).astype(BF16)
    proj = _gdn_inproj(x, norm_g, mod, w_in_p, conv_w, tm=256, tn=4 * LANES)
    col, row = _gdn_gates(proj, a_log, dt_bias)
    o = _gdn_chunk(proj, col, row, norm_w)
    return _outproj(o, w_out.astype(BF16), x, mod, tm=512)


NSA_HEADS = 16
NSA_GROUPS = 4
NSA_HPG = NSA_HEADS // NSA_GROUPS
NSA_HEAD_DIM = 64
NSA_CMP_LEN = 32
NSA_CMP_STRIDE = 16
NSA_SLC_LEN = 64
NSA_TOP_K = 8
NSA_WINDOW = 512
NSA_QBLOCK = 128
NSA_Q_W = NSA_HEADS * NSA_HEAD_DIM
NSA_KV_W = NSA_GROUPS * NSA_HEAD_DIM
REL_BUCKETS = 32
REL_MAX_DIST = 128
FEAT_LANE0 = NSA_HEAD_DIM
CONST_LANE0 = FEAT_LANE0 + 32
NSA_COL_Q = 0
NSA_COL_CMP = NSA_Q_W
NSA_COL_SEL = NSA_COL_CMP + 2 * NSA_KV_W
NSA_COL_WIN = NSA_COL_SEL + 2 * NSA_KV_W
NSA_COL_Z = NSA_COL_WIN + 2 * NSA_KV_W
NSA_COL_GATE = NSA_COL_Z + NSA_Q_W
NSA_PROJ_W = NSA_COL_GATE + LANES


def _nsa_column_perm():
    g, dh = NSA_GROUPS, NSA_HEAD_DIM
    kv0 = NSA_Q_W
    cols = list(range(NSA_Q_W))
    cols += [kv0 + i for i in range(2 * NSA_KV_W)]
    for br in (1, 2):
        for gi in range(g):
            cols += [kv0 + (2 * br) * NSA_KV_W + gi * dh + d for d in range(dh)]
            cols += [kv0 + (2 * br + 1) * NSA_KV_W + gi * dh + d for d in range(dh)]
    gate0 = kv0 + 6 * NSA_KV_W
    cols += [gate0 + 3 * NSA_HEADS + i for i in range(NSA_Q_W)]
    cols += [gate0 + i for i in range(3 * NSA_HEADS)] + [-1] * (LANES - 3 * NSA_HEADS)
    assert len(cols) == NSA_PROJ_W
    return np.asarray(cols, np.int32)


def _rel_bucket_table(n):
    d = np.arange(n)
    max_exact = REL_BUCKETS // 2
    nf = np.maximum(d, 1).astype(np.float64)
    large = max_exact + (np.log(nf / max_exact) / math.log(REL_MAX_DIST / max_exact)
                         * (REL_BUCKETS - max_exact)).astype(np.int32)
    large = np.minimum(large, REL_BUCKETS - 1)
    return np.where(d < max_exact, d, large).astype(np.int32)


def _nsa_tables(rel_bias, t):
    qb = NSA_QBLOCK
    bucket = _rel_bucket_table(t)
    assert np.all(bucket[qb + 1:] == REL_BUCKETS - 1)
    bvec = rel_bias[bucket].T
    far = rel_bias[REL_BUCKETS - 1]
    far_hi = far.astype(BF16)
    far_lo = (far - far_hi.astype(F32)).astype(BF16)
    far_sum = far_hi.astype(F32) + far_lo.astype(F32)
    r = np.arange(qb)[:, None]
    c = np.arange(qb)[None, :]
    d0 = r - c

    def toeplitz(w):
        n = 2 * qb - 1
        ext = jnp.pad(w[:, ::-1], ((0, 0), (0, 1)))
        skew = jnp.tile(ext, (1, qb))[:, :qb * n].reshape(w.shape[0], qb, n)
        return skew[:, :, qb - 1:]

    rel = bvec[:, :2 * qb] - far_sum[:, None]
    t0 = toeplitz(jnp.concatenate([jnp.full((NSA_HEADS, qb - 1), NEG_INF, F32), rel[:, :qb]], axis=1))
    t1 = toeplitz(rel[:, 1:])
    g, hpg = NSA_GROUPS, NSA_HPG
    none = jnp.full_like(t0, NEG_INF)
    near = jnp.stack([t1, t0, t0, none], axis=1).reshape(g, hpg, 2, 2, qb, qb)
    near = near.transpose(0, 2, 3, 5, 1, 4).reshape(g, 2, 2 * qb, hpg * qb)
    nb = t // NSA_CMP_STRIDE
    per_tile = qb // NSA_CMP_STRIDE
    width = 2 * per_tile
    far_d = per_tile * NSA_CMP_STRIDE - (NSA_CMP_LEN - 1) + NSA_CMP_STRIDE
    assert np.all(bucket[far_d:] == REL_BUCKETS - 1)
    dm = r - NSA_CMP_STRIDE * np.arange(width)[None, :] + far_d
    band = jnp.where(dm >= 0, bvec[:, np.maximum(dm, 0)], NEG_INF)
    tiles = []
    for i in range(t // qb):
        j0 = per_tile * i - (per_tile + 1)
        lo, hi = max(j0, 0), min(j0 + width, nb)
        tiles.append(jnp.concatenate([
            jnp.broadcast_to(far[:, None, None], (NSA_HEADS, qb, lo)),
            band[:, :, lo - j0:hi - j0],
            jnp.full((NSA_HEADS, qb, nb - hi), NEG_INF, F32)], axis=2))
    cmp_bias = jnp.stack(tiles, axis=0).reshape(t // qb, g, hpg, qb, nb).transpose(0, 1, 4, 2, 3)
    cmp_bias = cmp_bias.reshape(t // qb, g, nb, hpg * qb)
    qconst = jnp.zeros((g, 8, hpg, qb), F32)
    qconst = qconst.at[:, 0].set(jnp.broadcast_to(far_hi.astype(F32).reshape(g, hpg, 1), (g, hpg, qb)))
    qconst = qconst.at[:, 1].set(jnp.broadcast_to(far_lo.astype(F32).reshape(g, hpg, 1), (g, hpg, qb)))
    return near, cmp_bias, qconst.reshape(g, 8, hpg * qb)


def _overlap_t(t):
    n_cmp = (t - NSA_CMP_LEN) // NSA_CMP_STRIDE + 1
    n_slc = t // NSA_SLC_LEN
    c_start = np.arange(n_cmp)[:, None] * NSA_CMP_STRIDE
    s_start = np.arange(n_slc)[None, :] * NSA_SLC_LEN
    ov = np.clip(np.minimum(c_start + NSA_CMP_LEN, s_start + NSA_SLC_LEN) - np.maximum(c_start, s_start), 0, None)
    ov = ov.astype(np.float32) / NSA_CMP_LEN
    out = np.zeros((32, t // NSA_CMP_STRIDE), np.float32)
    out[:n_slc, :n_cmp] = ov.T
    return out


def _nsa_compress_kernel(x_ref, pos_ref, w1_ref, w2_ref, o_ref, xs_ref):
    t = x_ref.shape[1]
    nb = t // NSA_CMP_STRIDE
    nlt = xs_ref.shape[0]
    for c in range(nlt):
        xs_ref[c, 0:t, :] = x_ref[0, :, c * LANES:(c + 1) * LANES]
        xs_ref[c, t:t + NSA_CMP_STRIDE, :] = jnp.zeros((NSA_CMP_STRIDE, LANES), F32)
    acc = jnp.zeros((nb, w1_ref.shape[2]), F32)
    for l in range(NSA_CMP_LEN):
        xl = jnp.concatenate([xs_ref[c, pl.ds(l, nb, stride=NSA_CMP_STRIDE), :] for c in range(nlt)], axis=1)
        xl = xl + pos_ref[l:l + 1, :]
        acc = acc + _dot(xl.astype(BF16), w1_ref[l])
    hid = _silu(acc).astype(BF16)
    res = _dot(hid, w2_ref[...])
    for g in range(NSA_GROUPS):
        o_ref[0, g] = res[:, g * LANES:(g + 1) * LANES]


def _nsa_compress(proj, cmp_pos, cmp_w1, cmp_w2):
    bsz, t, _ = proj.shape
    g, dh = NSA_GROUPS, NSA_HEAD_DIM
    nb = t // NSA_CMP_STRIDE
    w = 2 * NSA_KV_W
    w1 = cmp_w1.reshape(2, NSA_CMP_LEN, dh, dh).astype(BF16)
    w2 = cmp_w2.astype(BF16)
    place = lambda blk, c0: jnp.pad(blk, [(0, 0)] * (blk.ndim - 1) + [(c0, w - dh - c0)])
    w1c = jnp.concatenate([place(w1[i], (gi * 2 + i) * dh) for i in range(2) for gi in range(g)], axis=1)
    w2c = jnp.concatenate([place(w2[i], (gi * 2 + i) * dh) for gi in range(g) for i in range(2)], axis=0)
    pos = jnp.broadcast_to(cmp_pos[:, :, None, :], (2, NSA_CMP_LEN, g, dh)).transpose(1, 0, 2, 3).reshape(NSA_CMP_LEN, w)
    return pl.pallas_call(
        _nsa_compress_kernel,
        grid=(bsz,),
        in_specs=[
            pl.BlockSpec((1, t, w), lambda b: (b, 0, NSA_COL_CMP // w)),
            pl.BlockSpec((NSA_CMP_LEN, w), lambda b: (0, 0)),
            pl.BlockSpec((NSA_CMP_LEN, w, w), lambda b: (0, 0, 0), pipeline_mode=pl.Buffered(1)),
            pl.BlockSpec((w, w), lambda b: (0, 0)),
        ],
        out_specs=pl.BlockSpec((1, g, nb, LANES), lambda b: (b, 0, 0, 0)),
        out_shape=jax.ShapeDtypeStruct((bsz, g, nb, LANES), F32),
        scratch_shapes=[pltpu.VMEM((w // LANES, t + NSA_CMP_STRIDE, LANES), F32)],
        compiler_params=pltpu.CompilerParams(dimension_semantics=("parallel",), vmem_limit_bytes=VMEM_LIMIT),
        name="nsa_compress",
    )(proj, pos, w1c, w2c)


def _nsa_attn_kernel(q_ref, kvs_ref, kvw_ref, kvc_ref, gate_ref, z_ref, cb_ref, near_ref, qc_ref, ovl_ref,
                     o_ref, ks_ref, vs_ref, kw_ref, vw_ref, gt_ref, ms_ref, accs_ref, mw_ref, accw_ref, sc_ref):
    qb, dh, hpg = NSA_QBLOCK, NSA_HEAD_DIM, NSA_HPG
    t = kvs_ref.shape[1]
    nblk = t // NSA_SLC_LEN
    cols = hpg * qb
    g = pl.program_id(1)
    i = pl.program_id(2)

    @pl.when(i == 0)
    def _():
        tok = lax.broadcasted_iota(jnp.int32, (t, LANES), 0)
        ln = lax.broadcasted_iota(jnp.int32, (t, LANES), 1)
        const = jnp.where((ln == CONST_LANE0) | (ln == CONST_LANE0 + 1), 1.0, 0.0)
        onehot = jnp.where(ln - FEAT_LANE0 == tok // NSA_SLC_LEN, 1.0, 0.0)
        row = lax.broadcasted_iota(jnp.int32, (LANES, t), 0)
        kvs = kvs_ref[0]
        kvw = kvw_ref[0]
        ks_ref[...] = jnp.where(ln < dh, kvs, onehot + const).astype(BF16)
        kw_ref[...] = jnp.where(ln < dh, kvw, const).astype(BF16)
        vs_ref[...] = jnp.where(row == 0, 1.0, kvs.T).astype(BF16)
        vw_ref[...] = jnp.where(row == 0, 1.0, kvw.T).astype(BF16)

    q_t = (q_ref[0] * (dh ** -0.5)).T
    q_heads = jnp.concatenate([q_t[hh * dh:(hh + 1) * dh] for hh in range(hpg)], axis=1)

    def scores(branch, qa_t, start, nk, bias):
        sc = _dot(branch[0][pl.ds(pl.multiple_of(start, qb), nk), :], qa_t)
        return sc if bias is None else sc + bias

    def update(branch, start, nk, sc):
        _, vt_ref, m_ref, acc_ref = branch
        m_old = m_ref[...]
        m_new = jnp.maximum(m_old, jnp.max(sc, axis=0, keepdims=True))
        alpha = jnp.exp(m_old - m_new)
        pe = jnp.exp(sc - m_new).astype(BF16)
        acc_ref[...] = alpha * acc_ref[...] + _dot(vt_ref[:, pl.ds(pl.multiple_of(start, qb), nk)], pe)
        m_ref[...] = m_new

    sel = (ks_ref, vs_ref, ms_ref, accs_ref)
    win = (kw_ref, vw_ref, mw_ref, accw_ref)
    for m_ref, acc_ref in ((ms_ref, accs_ref), (mw_ref, accw_ref)):
        m_ref[...] = jnp.full(m_ref.shape, NEG_INF, F32)
        acc_ref[...] = jnp.zeros(acc_ref.shape, F32)

    nwt = NSA_WINDOW // qb
    assert nwt == 4
    pad_rows = jnp.zeros((LANES - CONST_LANE0 - 8, cols), F32)
    qa_win = jnp.concatenate([q_heads, jnp.zeros((32, cols), F32), qc_ref[0], pad_rows], axis=0).astype(BF16)

    kvc = kvc_ref[0, 0]
    lane_k = lax.broadcasted_iota(jnp.int32, kvc.shape, 1)
    kc16 = jnp.where(lane_k < dh, kvc, 0.0).astype(BF16)
    s = _dot(kc16, qa_win) + cb_ref[0, 0]

    kk = lax.broadcasted_iota(jnp.int32, (qb, cols), 0)
    rr = lax.broadcasted_iota(jnp.int32, (qb, cols), 1) % qb
    w4_start = jnp.maximum(i - nwt, 0) * qb
    sc_w4 = scores(win, qa_win, w4_start, qb, jnp.where((rr < kk) & (i >= nwt), 0.0, NEG_INF))

    w32_start = jnp.maximum(i - 3, 0) * qb
    row2 = lax.broadcasted_iota(jnp.int32, (2 * qb, cols), 0)
    sc_w32 = scores(win, qa_win, w32_start, 2 * qb,
                    jnp.where(row2 < (i - 1) * qb - w32_start, 0.0, NEG_INF))
    near_start = jnp.maximum(i - 1, 0) * qb
    near_bias = near_ref[0, jnp.where(i == 0, 1, 0)]
    sc_wn = scores(win, qa_win, near_start, 2 * qb, near_bias)

    s = jnp.exp(s - jnp.max(s, axis=0, keepdims=True))
    p = s / jnp.sum(s, axis=0, keepdims=True)
    tq_lane = i * qb + lax.broadcasted_iota(jnp.int32, (1, cols), 1) % qb
    p16 = (p * (tq_lane >= NSA_CMP_LEN - 1).astype(F32)).astype(BF16)
    o_cmp = _dot(kvc.T.astype(BF16), p16)
    ovl = ovl_ref[...].astype(BF16)
    imp = _dot(ovl, p16[:, 0:qb])
    for hh in range(1, hpg):
        imp = imp + _dot(ovl, p16[:, hh * qb:(hh + 1) * qb])

    update(win, w4_start, qb, sc_w4)

    blk = lax.broadcasted_iota(jnp.int32, (32, qb), 0)
    tq = i * qb + lax.broadcasted_iota(jnp.int32, (32, qb), 1)
    cur = tq // NSA_SLC_LEN
    forced = (blk == 0) | (blk == cur) | (blk == cur - 1)
    val = jnp.where(forced, jnp.inf, jnp.where(blk * NSA_SLC_LEN <= tq, imp, -jnp.inf))
    cnt = jnp.zeros((32, qb), jnp.int32)
    for s2 in range(nblk):
        other = val[s2:s2 + 1, :]
        cnt = cnt + ((other > val) | ((other == val) & (s2 < blk))).astype(jnp.int32)
    feat = jnp.where((cnt < min(NSA_TOP_K, nblk)) & (blk < nblk), 0.0, NEG_INF)
    qa = jnp.concatenate([q_heads, jnp.concatenate([feat] * hpg, axis=1), qc_ref[0], pad_rows],
                         axis=0).astype(BF16)

    update(win, w32_start, 2 * qb, sc_w32)
    sc_sn = scores(sel, qa, near_start, 2 * qb, near_bias)
    update(win, near_start, 2 * qb, sc_wn)
    update(sel, near_start, 2 * qb, sc_sn)

    n_far = jnp.maximum(i - 1, 0)
    n_pairs = (n_far + 1) // 2
    pair_start = lambda p: jnp.maximum(2 * p - n_far % 2, 0) * qb
    last_pair = jnp.maximum(n_pairs - 1, 0)
    sc_ref[...] = scores(sel, qa, 0, 2 * qb, jnp.where(row2 < (2 - n_far % 2) * qb, 0.0, NEG_INF))

    def sel_body(k, carry):
        sc_odd = scores(sel, qa, pair_start(2 * k + 1), 2 * qb, None)
        update(sel, pair_start(2 * k), 2 * qb, sc_ref[...])
        sc_ref[...] = scores(sel, qa, pair_start(jnp.minimum(2 * k + 2, last_pair)), 2 * qb, None)
        update(sel, pair_start(2 * k + 1), 2 * qb, sc_odd)
        return carry

    lax.fori_loop(0, n_pairs // 2, sel_body, 0)

    @pl.when(n_pairs % 2 == 1)
    def _():
        update(sel, pair_start(n_pairs - 1), 2 * qb, sc_ref[...])

    def finish(acc_ref):
        acc = acc_ref[...]
        return acc[dh:] / acc[0:1]

    o_slc = finish(accs_ref)
    o_win = finish(accw_ref)

    gt_ref[...] = jax.nn.sigmoid(gate_ref[0]).T
    outs = []
    for hh in range(hpg):
        sl = slice(hh * qb, (hh + 1) * qb)
        base = (g * hpg + hh) * 3
        gate = [gt_ref[pl.ds(base + br, 1), :] for br in range(3)]
        outs.append(gate[0] * o_cmp[dh:, sl] + gate[1] * o_slc[:, sl] + gate[2] * o_win[:, sl])
    out = jnp.concatenate(outs, axis=0).T
    o_ref[0] = (out * _silu(z_ref[0])).astype(o_ref.dtype)


def _nsa_attn(proj, kv_cmp, near, cmp_bias, qconst, ovl):
    bsz, t, _ = proj.shape
    qb, hpg = NSA_QBLOCK, NSA_HPG
    gw = hpg * NSA_HEAD_DIM
    nb = t // NSA_CMP_STRIDE
    cols = hpg * qb
    return pl.pallas_call(
        _nsa_attn_kernel,
        grid=(bsz, NSA_GROUPS, t // qb),
        in_specs=[
            pl.BlockSpec((1, qb, gw), lambda b, g, i: (b, i, NSA_COL_Q // gw + g)),
            pl.BlockSpec((1, t, LANES), lambda b, g, i: (b, 0, NSA_COL_SEL // LANES + g)),
            pl.BlockSpec((1, t, LANES), lambda b, g, i: (b, 0, NSA_COL_WIN // LANES + g)),
            pl.BlockSpec((1, 1, nb, LANES), lambda b, g, i: (b, g, 0, 0)),
            pl.BlockSpec((1, qb, LANES), lambda b, g, i: (b, i, NSA_COL_GATE // LANES)),
            pl.BlockSpec((1, qb, gw), lambda b, g, i: (b, i, NSA_COL_Z // gw + g)),
            pl.BlockSpec((1, 1, nb, cols), lambda b, g, i: (i, g, 0, 0)),
            pl.BlockSpec((1, 2, 2 * qb, cols), lambda b, g, i: (g, 0, 0, 0)),
            pl.BlockSpec((1, 8, cols), lambda b, g, i: (g, 0, 0)),
            pl.BlockSpec((32, nb), lambda b, g, i: (0, 0)),
        ],
        out_specs=pl.BlockSpec((1, qb, gw), lambda b, g, i: (b, i, g)),
        out_shape=jax.ShapeDtypeStruct((bsz, t, NSA_Q_W), BF16),
        scratch_shapes=[
            pltpu.VMEM((t, LANES), BF16), pltpu.VMEM((LANES, t), BF16),
            pltpu.VMEM((t, LANES), BF16), pltpu.VMEM((LANES, t), BF16),
            pltpu.VMEM((LANES, qb), F32),
            pltpu.VMEM((1, cols), F32), pltpu.VMEM((LANES, cols), F32),
            pltpu.VMEM((1, cols), F32), pltpu.VMEM((LANES, cols), F32),
            pltpu.VMEM((2 * qb, cols), F32),
        ],
        compiler_params=pltpu.CompilerParams(
            dimension_semantics=("parallel", "parallel", "arbitrary"), vmem_limit_bytes=VMEM_LIMIT),
        name="nsa_attention",
    )(proj, proj, proj, kv_cmp, proj, proj, cmp_bias, near, qconst, ovl)


def _nsa_layer(x, mod, norm_g, w_in, cmp_pos, cmp_w1, cmp_w2, rel_bias, w_out, final_g):
    t = x.shape[1]
    assert t // NSA_SLC_LEN <= 32 and NSA_COL_Z % (NSA_HPG * NSA_HEAD_DIM) == 0
    perm = _nsa_column_perm()
    cuts = [0] + [j for j in range(1, len(perm)) if perm[j] != perm[j - 1] + (perm[j - 1] >= 0)] + [len(perm)]
    runs = [(int(perm[a]), b - a) for a, b in zip(cuts[:-1], cuts[1:])]
    w16 = w_in.astype(BF16)
    w_in_p = jnp.concatenate([w16[:, s:s + n] if s >= 0 else jnp.zeros((w_in.shape[0], n), BF16)
                              for s, n in runs], axis=1)
    proj = _inproj(x, norm_g, mod, w_in_p, tm=256, tn=NSA_PROJ_W)
    kv_cmp = _nsa_compress(proj, cmp_pos, cmp_w1, cmp_w2)
    near, cmp_bias, qconst = _nsa_tables(rel_bias, t)
    o = _nsa_attn(proj, kv_cmp, near, cmp_bias, qconst, jnp.asarray(_overlap_t(t)))
    return _outproj(o, w_out.astype(BF16), x, mod, final_g, tm=512)


def kernel(x, c, ada_w, ada_b, norm_g, gdn_w_in, gdn_conv_w, gdn_a_log, gdn_dt_bias, gdn_norm_w, gdn_w_out,
           nsa_w_in, nsa_cmp_pos, nsa_cmp_w1, nsa_cmp_w2, nsa_w_out, rel_bias, final_g):
    bsz, t, d = x.shape
    mod = _modulation(c, ada_w, ada_b).reshape(ada_w.shape[0], bsz, 3, d)
    x = _gdn_layer(x, mod[0], norm_g[0], gdn_w_in[0], gdn_conv_w[0], gdn_a_log[0], gdn_dt_bias[0],
                   gdn_norm_w[0], gdn_w_out[0])
    return _nsa_layer(x, mod[1], norm_g[1], nsa_w_in[0], nsa_cmp_pos[0], nsa_cmp_w1[0], nsa_cmp_w2[0],
                      rel_bias, nsa_w_out[0], final_g)
```

```python
import functools
import math

import numpy as np
import jax
import jax.numpy as jnp
from jax import lax
from jax.experimental import pallas as pl
from jax.experimental.pallas import tpu as pltpu

F32 = jnp.float32
BF16 = jnp.bfloat16
HIGHEST = lax.Precision.HIGHEST

EPS = 1e-6
NEG_INF = -1e30
LANES = 128
VMEM_LIMIT = 56 * 1024 * 1024

GDN_QK_HEADS = 8
GDN_V_HEADS = 16
GDN_HEAD_DIM = 128
GDN_CONV = 4
GDN_CHUNK = 64
GDN_QK_W = GDN_QK_HEADS * GDN_HEAD_DIM
GDN_V_W = GDN_V_HEADS * GDN_HEAD_DIM
GDN_CONV_W = 2 * GDN_QK_W + GDN_V_W
GDN_TILE = 256
GDN_HEADS_PER_STEP = 4


def _silu(x):
    return x * jax.nn.sigmoid(x)


def _dot(a, b, **kw):
    return jnp.dot(a, b, preferred_element_type=F32, **kw)


def _dot_nt(a, b, **kw):
    return lax.dot_general(a, b, (((1,), (1,)), ((), ())), preferred_element_type=F32, **kw)


def _mod_kernel(c_ref, w_ref, b_ref, o_ref):
    cond = _silu(c_ref[...])
    o_ref[0] = _dot(cond, w_ref[0], precision=HIGHEST) + b_ref[0]


def _modulation(c, ada_w, ada_b):
    depth, d, d3 = ada_w.shape
    bsz = c.shape[0]
    return pl.pallas_call(
        _mod_kernel,
        grid=(depth, d3 // d),
        in_specs=[
            pl.BlockSpec((bsz, d), lambda i, j: (0, 0)),
            pl.BlockSpec((1, d, d), lambda i, j: (i, 0, j)),
            pl.BlockSpec((1, 1, d), lambda i, j: (i, 0, j)),
        ],
        out_specs=pl.BlockSpec((1, bsz, d), lambda i, j: (i, 0, j)),
        out_shape=jax.ShapeDtypeStruct((depth, bsz, d3), F32),
        name="adaln_mod",
    )(c, ada_w, ada_b.reshape(depth, 1, d3))


def _inproj_kernel(x_ref, g_ref, mod_ref, w_ref, o_ref, *, tn):
    x = x_ref[0]
    m = mod_ref[0]
    y = x * lax.rsqrt(jnp.mean(x * x, axis=-1, keepdims=True) + EPS) * g_ref[...]
    h = (y * (1.0 + m[1:2]) + m[0:1]).astype(BF16)
    for j in range(w_ref.shape[1] // tn):
        o_ref[0, :, j * tn:(j + 1) * tn] = _dot(h, w_ref[:, j * tn:(j + 1) * tn])


def _inproj(x, g, mod, w, *, tm, tn):
    bsz, t, d = x.shape
    n = w.shape[1]
    assert t % tm == 0 and n % tn == 0
    return pl.pallas_call(
        functools.partial(_inproj_kernel, tn=tn),
        grid=(bsz, t // tm),
        in_specs=[
            pl.BlockSpec((1, tm, d), lambda b, i: (b, i, 0)),
            pl.BlockSpec((1, d), lambda b, i: (0, 0)),
            pl.BlockSpec((1, 3, d), lambda b, i: (b, 0, 0)),
            pl.BlockSpec((d, n), lambda b, i: (0, 0), pipeline_mode=pl.Buffered(1)),
        ],
        out_specs=pl.BlockSpec((1, tm, n), lambda b, i: (b, i, 0)),
        out_shape=jax.ShapeDtypeStruct((bsz, t, n), F32),
        compiler_params=pltpu.CompilerParams(
            dimension_semantics=("parallel", "parallel"), vmem_limit_bytes=VMEM_LIMIT),
        name="norm_mod_inproj",
    )(x, g.reshape(1, d), mod, w)


def _gdn_inproj_kernel(x_ref, g_ref, mod_ref, w_ref, cw_ref, o_ref, ext_ref, tail_ref, *, tn):
    dh = GDN_HEAD_DIM
    tm = x_ref.shape[1]
    x = x_ref[0]
    m = mod_ref[0]
    y = x * lax.rsqrt(jnp.mean(x * x, axis=-1, keepdims=True) + EPS) * g_ref[...]
    h = (y * (1.0 + m[1:2]) + m[0:1]).astype(BF16)

    @pl.when(pl.program_id(1) == 0)
    def _():
        tail_ref[...] = jnp.zeros_like(tail_ref)

    def conv_tile(j):
        cols = slice(j * tn, (j + 1) * tn)
        ext_ref[0:8, :] = tail_ref[:, cols]
        ext_ref[8:8 + tm, :] = o_ref[0, :, cols]
        tail_ref[:, cols] = ext_ref[tm:tm + 8, :]
        w = cw_ref[:, cols]
        acc = w[0:1] * ext_ref[5:5 + tm, :]
        for tap in range(1, GDN_CONV):
            acc = acc + w[tap:tap + 1] * ext_ref[5 + tap:5 + tap + tm, :]
        act = _silu(acc)
        if j * tn < 2 * GDN_QK_W:
            scale = dh ** -0.5 if j * tn < GDN_QK_W else 1.0
            heads = [act[:, c * dh:(c + 1) * dh] for c in range(tn // dh)]
            heads = [hd * (lax.rsqrt(jnp.sum(hd * hd, axis=-1, keepdims=True) + EPS) * scale) for hd in heads]
            act = jnp.concatenate(heads, axis=1)
        o_ref[0, :, cols] = act

    n = w_ref.shape[1]
    n_conv = GDN_CONV_W // tn
    bounds = [(j * tn, min((j + 1) * tn, n)) for j in range(-(-n // tn))]
    for j, (lo, hi) in enumerate(bounds):
        o_ref[0, :, lo:hi] = _dot(h, w_ref[:, lo:hi])
        if 1 <= j <= n_conv:
            conv_tile(j - 1)


def _gdn_inproj(x, g, mod, w, conv_w, *, tm, tn):
    bsz, t, d = x.shape
    n = w.shape[1]
    assert t % tm == 0 and GDN_CONV_W % tn == 0 and GDN_QK_W % tn == 0 and n > GDN_CONV_W
    return pl.pallas_call(
        functools.partial(_gdn_inproj_kernel, tn=tn),
        grid=(bsz, t // tm),
        in_specs=[
            pl.BlockSpec((1, tm, d), lambda b, i: (b, i, 0)),
            pl.BlockSpec((1, d), lambda b, i: (0, 0)),
            pl.BlockSpec((1, 3, d), lambda b, i: (b, 0, 0)),
            pl.BlockSpec((d, n), lambda b, i: (0, 0), pipeline_mode=pl.Buffered(1)),
            pl.BlockSpec((GDN_CONV, GDN_CONV_W), lambda b, i: (0, 0)),
        ],
        out_specs=pl.BlockSpec((1, tm, n), lambda b, i: (b, i, 0)),
        out_shape=jax.ShapeDtypeStruct((bsz, t, n), F32),
        scratch_shapes=[pltpu.VMEM((tm + 8, tn), F32), pltpu.VMEM((8, GDN_CONV_W), F32)],
        compiler_params=pltpu.CompilerParams(
            dimension_semantics=("parallel", "arbitrary"), vmem_limit_bytes=VMEM_LIMIT),
        name="gdn_norm_inproj_conv",
    )(x, g.reshape(1, d), mod, w, conv_w)


def _outproj_kernel(o_ref, w_ref, x_ref, mod_ref, *rest, final_norm):
    y = _dot(o_ref[0], w_ref[...])
    x = x_ref[0] + mod_ref[0][2:3] * y
    if final_norm:
        fg_ref, out_ref = rest
        x = x * lax.rsqrt(jnp.mean(x * x, axis=-1, keepdims=True) + EPS) * fg_ref[...]
    else:
        (out_ref,) = rest
    out_ref[0] = x


def _outproj(o, w, x, mod, final_g=None, *, tm):
    bsz, t, d = x.shape
    k = o.shape[-1]
    in_specs = [
        pl.BlockSpec((1, tm, k), lambda b, i: (b, i, 0)),
        pl.BlockSpec((k, d), lambda b, i: (0, 0), pipeline_mode=pl.Buffered(1)),
        pl.BlockSpec((1, tm, d), lambda b, i: (b, i, 0)),
        pl.BlockSpec((1, 3, d), lambda b, i: (b, 0, 0)),
    ]
    args = [o, w, x, mod]
    if final_g is not None:
        in_specs.append(pl.BlockSpec((1, d), lambda b, i: (0, 0)))
        args.append(final_g.reshape(1, d))
    return pl.pallas_call(
        functools.partial(_outproj_kernel, final_norm=final_g is not None),
        grid=(bsz, t // tm),
        in_specs=in_specs,
        out_specs=pl.BlockSpec((1, tm, d), lambda b, i: (b, i, 0)),
        out_shape=jax.ShapeDtypeStruct((bsz, t, d), F32),
        compiler_params=pltpu.CompilerParams(
            dimension_semantics=("parallel", "parallel"), vmem_limit_bytes=VMEM_LIMIT),
        name="outproj_residual",
    )(*args)


def _gdn_gates_kernel(ba_ref, alog_ref, dtb_ref, col_ref, row_ref):
    cs, nh = GDN_CHUNK, GDN_V_HEADS
    ba = ba_ref[0]
    lane = lax.broadcasted_iota(jnp.int32, ba.shape, 1)
    g = -jnp.exp(alog_ref[...]) * jax.nn.softplus(ba + dtb_ref[...])
    vals = jnp.where(lane < nh, jax.nn.sigmoid(ba), g)
    r = lax.broadcasted_iota(jnp.int32, (cs, cs), 0)
    c = lax.broadcasted_iota(jnp.int32, (cs, cs), 1)
    tri = (r >= c).astype(F32)
    is_beta = lax.broadcasted_iota(jnp.int32, (cs, LANES), 1) < nh
    pr = lax.broadcasted_iota(jnp.int32, (nh // 2, LANES), 0)
    pc = lax.broadcasted_iota(jnp.int32, (nh // 2, LANES), 1)
    pick_even = (pc == nh + 2 * pr).astype(F32)
    for n in range(ba.shape[0] // cs):
        v = vals[n * cs:(n + 1) * cs]
        cum = _dot(tri, v, precision=HIGHEST)
        out = jnp.where(is_beta, v, cum)
        col_ref[0, n * cs:(n + 1) * cs, :] = out
        both = jnp.concatenate([out, pltpu.roll(out, LANES - 1, axis=1)], axis=0)
        row_ref[0, n] = _dot_nt(pick_even, both, precision=HIGHEST)


def _gdn_gates(proj, a_log, dt_bias, *, tt=512):
    bsz, t, n = proj.shape
    nc = t // GDN_CHUNK
    pad = lambda u: jnp.zeros((1, LANES), F32).at[0, GDN_V_HEADS:2 * GDN_V_HEADS].set(u)
    ba_blk = (n - LANES) // LANES
    return pl.pallas_call(
        _gdn_gates_kernel,
        grid=(bsz, t // tt),
        in_specs=[
            pl.BlockSpec((1, tt, LANES), lambda b, i: (b, i, ba_blk)),
            pl.BlockSpec((1, LANES), lambda b, i: (0, 0)),
            pl.BlockSpec((1, LANES), lambda b, i: (0, 0)),
        ],
        out_specs=[
            pl.BlockSpec((1, tt, LANES), lambda b, i: (b, i, 0)),
            pl.BlockSpec((1, tt // GDN_CHUNK, GDN_V_HEADS // 2, 2 * GDN_CHUNK), lambda b, i: (b, i, 0, 0)),
        ],
        out_shape=[
            jax.ShapeDtypeStruct((bsz, t, LANES), F32),
            jax.ShapeDtypeStruct((bsz, nc, GDN_V_HEADS // 2, 2 * GDN_CHUNK), F32),
        ],
        name="gdn_gates",
    )(proj, pad(a_log), pad(dt_bias))


TRI_BASE = 8
PACK = 4


def _block_rows(p):
    n = p.shape[0]
    blk = lax.broadcasted_iota(jnp.int32, p.shape, 1) // n
    return jnp.concatenate([jnp.where(blk == j, p, 0.0) for j in range(PACK)], axis=0).astype(BF16)


def _tri_inverse_packed(mats):
    n = mats[0].shape[0]
    r = lax.broadcasted_iota(jnp.int32, mats[0].shape, 0)
    c = lax.broadcasted_iota(jnp.int32, mats[0].shape, 1) % n
    same = lambda s: (r // s) == (c // s)
    mm = lambda a, b: _dot(a.astype(BF16), _block_rows(b))
    diag = [jnp.where(same(TRI_BASE), a, 0.0) for a in mats]
    inv = [jnp.where(r == c, 1.0, 0.0) - d for d in diag]
    pw = diag
    k = 1
    while 2 * k < TRI_BASE:
        pw = [mm(m, m) for m in pw]
        inv = [p + mm(p, m) for p, m in zip(inv, pw)]
        k *= 2
        yield None
    s = TRI_BASE
    while s < n:
        sub = same(2 * s) & jnp.logical_not(same(s))
        left = [mm(p, jnp.where(sub, a, 0.0)) for p, a in zip(inv, mats)]
        inv = [p - mm(l, p) for p, l in zip(inv, left)]
        s *= 2
        yield None
    yield inv


def _gdn_chunk_kernel(q_ref, k_ref, v_ref, z_ref, col_ref, row_ref, nw_ref,
                      o_ref, s_ref, hu_ref, hwq_ref, hqk_ref, hkd_ref, hgl_ref):
    cs, dh, nh = GDN_CHUNK, GDN_HEAD_DIM, GDN_V_HEADS
    tt = q_ref.shape[1]
    hps = q_ref.shape[2] // dh
    ncb = tt // cs
    assert ncb % 2 == 0 and PACK == 4
    hg = pl.program_id(1)
    ti = pl.program_id(2)
    wslot = ti % 2
    rslot = 1 - wslot
    jidx = lambda hl, n, e: (hl * ncb + n) * 2 + e

    @pl.when(ti == 0)
    def _():
        s_ref[...] = jnp.zeros_like(s_ref)
        for h_ref in (hu_ref, hwq_ref, hqk_ref, hkd_ref, hgl_ref):
            h_ref[1] = jnp.zeros(h_ref.shape[1:], h_ref.dtype)

    chunks = [slice(n * cs, (n + 1) * cs) for n in range(ncb)]

    def prepare():
        q = [q_ref[0, :, hl * dh:(hl + 1) * dh] for hl in range(hps)]
        k = [k_ref[0, :, hl * dh:(hl + 1) * dh] for hl in range(hps)]
        v = [[v_ref[0, :, (2 * hl + e) * dh:(2 * hl + e + 1) * dh] for e in range(2)] for hl in range(hps)]
        lane = lax.broadcasted_iota(jnp.int32, (tt, LANES), 1)
        colv = col_ref[0]
        column = lambda idx: jnp.sum(jnp.where(lane == idx, colv, 0.0), axis=1, keepdims=True)
        hq = [hg * hps + hl for hl in range(hps)]
        beta = [[column(2 * h + e) for e in range(2)] for h in hq]
        gc = [[column(nh + 2 * h + e) for e in range(2)] for h in hq]
        egc = [[jnp.exp(x) for x in pair] for pair in gc]
        yield

        pr = lax.broadcasted_iota(jnp.int32, (cs, PACK * cs), 0)
        pb = lax.broadcasted_iota(jnp.int32, (cs, PACK * cs), 1) // cs
        pc = lax.broadcasted_iota(jnp.int32, (cs, PACK * cs), 1) % cs
        groups = [(hl, gi) for hl in range(hps) for gi in range(ncb // 2)]
        a_mats, g_rows = [], {}
        for hl, gi in groups:
            k16, q16 = k[hl].astype(BF16), q[hl].astype(BF16)
            pair = (chunks[2 * gi], chunks[2 * gi + 1])
            kdup = [jnp.concatenate([k16[sl], k16[sl]], axis=0) for sl in pair]
            kk = jnp.concatenate([_dot_nt(k16[sl], kd) for sl, kd in zip(pair, kdup)], axis=1)
            qk = jnp.concatenate([_dot_nt(q16[sl], kd) for sl, kd in zip(pair, kdup)], axis=1)
            pick = lambda cols: jnp.where(pb == 0, cols[0][pair[0]], jnp.where(
                pb == 1, cols[1][pair[0]], jnp.where(pb == 2, cols[0][pair[1]], cols[1][pair[1]])))
            g_row = jnp.concatenate([row_ref[0, 2 * gi + j, pl.ds(hq[hl], 1), :] for j in range(2)], axis=1)
            decay = jnp.exp(jnp.where(pr >= pc, pick(gc[hl]) - g_row, -jnp.inf))
            a_mats.append(jnp.where(pr > pc, pick(beta[hl]) * kk * decay, 0.0))
            hqk_ref[wslot, hl * (ncb // 2) + gi] = (qk * decay).astype(BF16)
            g_rows[hl, gi] = g_row
            yield
        t_mats = None
        for t_mats in _tri_inverse_packed(a_mats):
            yield

        for (hl, gi), t_mat in zip(groups, t_mats):
            rhs = []
            for n in (2 * gi, 2 * gi + 1):
                sl = chunks[n]
                for e in range(2):
                    j = 2 * (n % 2) + e
                    g_last = g_rows[hl, gi][:, j * cs + cs - 1:(j + 1) * cs]
                    hgl_ref[wslot, jidx(hl, n, e)] = jnp.broadcast_to(jnp.exp(g_last), (1, LANES))
                    hkd_ref[wslot, jidx(hl, n, e)] = (k[hl][sl] * jnp.exp(g_last - gc[hl][e][sl])).T.astype(BF16)
                    kbeta = k[hl][sl] * beta[hl][e][sl]
                    rhs.append(jnp.concatenate([v[hl][e][sl] * beta[hl][e][sl], kbeta * egc[hl][e][sl]], axis=1))
            out = _dot(_block_rows(t_mat), jnp.concatenate(rhs, axis=0).astype(BF16))
            for n in (2 * gi, 2 * gi + 1):
                for e in range(2):
                    uw = out[(2 * (n % 2) + e) * cs:(2 * (n % 2) + e + 1) * cs]
                    hu_ref[wslot, jidx(hl, n, e)] = uw[:, :dh]
                    q_dec = q[hl][chunks[n]] * egc[hl][e][chunks[n]]
                    hwq_ref[wslot, jidx(hl, n, e)] = jnp.concatenate([uw[:, dh:], q_dec], axis=0).astype(BF16)
            yield

    def recurrence():
        nw = nw_ref[...]
        heads = [(hl, e) for hl in range(hps) for e in range(2)]
        state = {he: s_ref[2 * he[0] + he[1]] for he in heads}
        for n, sl in enumerate(chunks):
            ws = {}
            for hl, e in heads:
                ws[hl, e] = _dot(hwq_ref[rslot, jidx(hl, n, e)], state[hl, e].astype(BF16))
            yield
            for hl, e in heads:
                j = jidx(hl, n, e)
                v16 = (hu_ref[rslot, j] - ws[hl, e][:cs]).astype(BF16)
                lb = 2 * (n % 2) + e
                qkd = hqk_ref[rslot, hl * (ncb // 2) + n // 2][:, lb * cs:(lb + 1) * cs]
                o = ws[hl, e][cs:] + _dot(qkd, v16)
                state[hl, e] = state[hl, e] * hgl_ref[rslot, j] + _dot(hkd_ref[rslot, j], v16)
                o = o * lax.rsqrt(jnp.mean(o * o, axis=-1, keepdims=True) + EPS) * nw
                lanes = slice((2 * hl + e) * dh, (2 * hl + e + 1) * dh)
                o_ref[0, sl, lanes] = (o * _silu(z_ref[0, sl, lanes])).astype(o_ref.dtype)
            yield
        for (hl, e), st in state.items():
            s_ref[2 * hl + e] = st

    prep, rec = prepare(), recurrence()
    n_prep = 1 + hps * (ncb // 2) * 2 + 6
    per_stage = -(-n_prep // (2 * ncb))
    prep_live = rec_live = True
    while prep_live or rec_live:
        for _ in range(per_stage):
            if prep_live:
                prep_live = next(prep, "done") != "done"
        if rec_live:
            rec_live = next(rec, "done") != "done"


def _gdn_chunk(proj, col, row, norm_w, *, tt=GDN_TILE, hps=GDN_HEADS_PER_STEP):
    bsz, t, _ = proj.shape
    dh = GDN_HEAD_DIM
    qw, vw = hps * dh, 2 * hps * dh
    k_blk0 = GDN_QK_W // qw
    v_blk0 = 2 * GDN_QK_W // vw
    z_blk0 = GDN_CONV_W // vw
    ncb = tt // GDN_CHUNK
    nt = t // tt
    nj = hps * ncb * 2
    cur = lambda i: jnp.minimum(i, nt - 1)
    prev = lambda i: jnp.maximum(i - 1, 0)
    return pl.pallas_call(
        _gdn_chunk_kernel,
        grid=(bsz, GDN_QK_HEADS // hps, nt + 1),
        in_specs=[
            pl.BlockSpec((1, tt, qw), lambda b, h, i: (b, cur(i), h)),
            pl.BlockSpec((1, tt, qw), lambda b, h, i: (b, cur(i), k_blk0 + h)),
            pl.BlockSpec((1, tt, vw), lambda b, h, i: (b, cur(i), v_blk0 + h)),
            pl.BlockSpec((1, tt, vw), lambda b, h, i: (b, prev(i), z_blk0 + h)),
            pl.BlockSpec((1, tt, LANES), lambda b, h, i: (b, cur(i), 0)),
            pl.BlockSpec((1, ncb, GDN_V_HEADS // 2, 2 * GDN_CHUNK), lambda b, h, i: (b, cur(i), 0, 0)),
            pl.BlockSpec((1, dh), lambda b, h, i: (0, 0)),
        ],
        out_specs=pl.BlockSpec((1, tt, vw), lambda b, h, i: (b, prev(i), h)),
        out_shape=jax.ShapeDtypeStruct((bsz, t, GDN_V_W), BF16),
        scratch_shapes=[
            pltpu.VMEM((2 * hps, dh, dh), F32),
            pltpu.VMEM((2, nj, GDN_CHUNK, dh), F32),
            pltpu.VMEM((2, nj, 2 * GDN_CHUNK, dh), BF16),
            pltpu.VMEM((2, nj // PACK, GDN_CHUNK, PACK * GDN_CHUNK), BF16),
            pltpu.VMEM((2, nj, dh, GDN_CHUNK), BF16),
            pltpu.VMEM((2, nj, 1, LANES), F32),
        ],
        compiler_params=pltpu.CompilerParams(
            dimension_semantics=("parallel", "parallel", "arbitrary"), vmem_limit_bytes=VMEM_LIMIT),
        name="gdn_chunk_scan",
    )(proj, proj, proj, proj, col, row, norm_w.reshape(1, dh))


def _gdn_layer(x, mod, norm_g, w_in, conv_w, a_log, dt_bias, norm_w, w_out):
    n_in = w_in.shape[1]
    n_pad = -(-n_in // (7 * LANES)) * (7 * LANES)
    w_in_p = jnp.pad(w_in, ((0, 0), (0, n_pad - n_in))).astype(BF16)
    proj = _gdn_inproj(x, norm_g, mod, w_in_p, conv_w, tm=256, tn=4 * LANES)
    col, row = _gdn_gates(proj, a_log, dt_bias)
    o = _gdn_chunk(proj, col, row, norm_w)
    return _outproj(o, w_out.astype(BF16), x, mod, tm=512)


NSA_HEADS = 16
NSA_GROUPS = 4
NSA_HPG = NSA_HEADS // NSA_GROUPS
NSA_HEAD_DIM = 64
NSA_CMP_LEN = 32
NSA_CMP_STRIDE = 16
NSA_SLC_LEN = 64
NSA_TOP_K = 8
NSA_WINDOW = 512
NSA_QTILE = 256
NSA_Q_W = NSA_HEADS * NSA_HEAD_DIM
NSA_KV_W = NSA_GROUPS * NSA_HEAD_DIM
REL_BUCKETS = 32
REL_MAX_DIST = 128
FEAT_LANE0 = NSA_HEAD_DIM
CONST_LANE0 = FEAT_LANE0 + 32
NSA_COL_Q = 0
NSA_COL_CMP = NSA_Q_W
NSA_COL_SEL = NSA_COL_CMP + 2 * NSA_KV_W
NSA_COL_WIN = NSA_COL_SEL + 2 * NSA_KV_W
NSA_COL_Z = NSA_COL_WIN + 2 * NSA_KV_W
NSA_COL_GATE = NSA_COL_Z + NSA_Q_W
NSA_PROJ_W = NSA_COL_GATE + LANES


def _nsa_column_perm():
    g, dh = NSA_GROUPS, NSA_HEAD_DIM
    kv0 = NSA_Q_W
    cols = list(range(NSA_Q_W))
    cols += [kv0 + i for i in range(2 * NSA_KV_W)]
    for br in (1, 2):
        for gi in range(g):
            cols += [kv0 + (2 * br) * NSA_KV_W + gi * dh + d for d in range(dh)]
            cols += [kv0 + (2 * br + 1) * NSA_KV_W + gi * dh + d for d in range(dh)]
    gate0 = kv0 + 6 * NSA_KV_W
    cols += [gate0 + 3 * NSA_HEADS + i for i in range(NSA_Q_W)]
    cols += [gate0 + i for i in range(3 * NSA_HEADS)] + [-1] * (LANES - 3 * NSA_HEADS)
    assert len(cols) == NSA_PROJ_W
    return np.asarray(cols, np.int32)


def _rel_bucket_table(n):
    d = np.arange(n)
    max_exact = REL_BUCKETS // 2
    nf = np.maximum(d, 1).astype(np.float64)
    large = max_exact + (np.log(nf / max_exact) / math.log(REL_MAX_DIST / max_exact)
                         * (REL_BUCKETS - max_exact)).astype(np.int32)
    large = np.minimum(large, REL_BUCKETS - 1)
    return np.where(d < max_exact, d, large).astype(np.int32)


def _nsa_tables(rel_bias, t):
    qb = NSA_QTILE
    bucket = _rel_bucket_table(t)
    assert np.all(bucket[qb + 1:] == REL_BUCKETS - 1)
    bvec = rel_bias[bucket].T
    far = rel_bias[REL_BUCKETS - 1]
    far_hi = far.astype(BF16)
    far_lo = (far - far_hi.astype(F32)).astype(BF16)
    far_sum = far_hi.astype(F32) + far_lo.astype(F32)
    r = np.arange(qb)[:, None]
    c = np.arange(qb)[None, :]
    d0 = r - c

    def toeplitz(w):
        n = 2 * qb - 1
        ext = jnp.pad(w[:, ::-1], ((0, 0), (0, 1)))
        skew = jnp.tile(ext, (1, qb))[:, :qb * n].reshape(w.shape[0], qb, n)
        return skew[:, :, qb - 1:]

    rel = bvec[:, :2 * qb] - far_sum[:, None]
    t0 = toeplitz(jnp.concatenate([jnp.full((NSA_HEADS, qb - 1), NEG_INF, F32), rel[:, :qb]], axis=1))
    t1 = toeplitz(rel[:, 1:])
    g, hpg = NSA_GROUPS, NSA_HPG
    none = jnp.full_like(t0, NEG_INF)
    near = jnp.stack([t1, t0, t0, none], axis=1).reshape(g, hpg, 2, 2, qb, qb)
    near = near.transpose(0, 2, 3, 5, 1, 4).reshape(g, 2, 2 * qb, hpg * qb)
    nb = t // NSA_CMP_STRIDE
    per_tile = qb // NSA_CMP_STRIDE
    back = 9
    far_d = back * NSA_CMP_STRIDE - (NSA_CMP_LEN - 1)
    assert np.all(bucket[far_d:] == REL_BUCKETS - 1)
    width = (qb - 1 + far_d) // NSA_CMP_STRIDE + 1
    dm = r - NSA_CMP_STRIDE * np.arange(width)[None, :] + far_d
    band = jnp.where(dm >= 0, bvec[:, np.maximum(dm, 0)], NEG_INF)
    tiles = []
    for i in range(t // qb):
        j0 = per_tile * i - back
        lo, hi = max(j0, 0), min(j0 + width, nb)
        tiles.append(jnp.concatenate([
            jnp.broadcast_to(far[:, None, None], (NSA_HEADS, qb, lo)),
            band[:, :, lo - j0:hi - j0],
            jnp.full((NSA_HEADS, qb, nb - hi), NEG_INF, F32)], axis=2))
    cmp_bias = jnp.stack(tiles, axis=0).reshape(t // qb, g, hpg, qb, nb).transpose(0, 1, 4, 2, 3)
    cmp_bias = cmp_bias.reshape(t // qb, g, nb, hpg * qb)
    qconst = jnp.zeros((g, 8, hpg, qb), F32)
    qconst = qconst.at[:, 0].set(jnp.broadcast_to(far_hi.astype(F32).reshape(g, hpg, 1), (g, hpg, qb)))
    qconst = qconst.at[:, 1].set(jnp.broadcast_to(far_lo.astype(F32).reshape(g, hpg, 1), (g, hpg, qb)))
    return near, cmp_bias, qconst.reshape(g, 8, hpg * qb)


def _overlap_t(t):
    n_cmp = (t - NSA_CMP_LEN) // NSA_CMP_STRIDE + 1
    n_slc = t // NSA_SLC_LEN
    c_start = np.arange(n_cmp)[:, None] * NSA_CMP_STRIDE
    s_start = np.arange(n_slc)[None, :] * NSA_SLC_LEN
    ov = np.clip(np.minimum(c_start + NSA_CMP_LEN, s_start + NSA_SLC_LEN) - np.maximum(c_start, s_start), 0, None)
    ov = ov.astype(np.float32) / NSA_CMP_LEN
    out = np.zeros((32, t // NSA_CMP_STRIDE), np.float32)
    out[:n_slc, :n_cmp] = ov.T
    return out


def _nsa_compress_kernel(x_ref, pos_ref, w1_ref, w2_ref, o_ref, xs_ref):
    t = x_ref.shape[1]
    nb = t // NSA_CMP_STRIDE
    nlt = xs_ref.shape[0]
    for c in range(nlt):
        xs_ref[c, 0:t, :] = x_ref[0, :, c * LANES:(c + 1) * LANES]
        xs_ref[c, t:t + NSA_CMP_STRIDE, :] = jnp.zeros((NSA_CMP_STRIDE, LANES), F32)
    acc = jnp.zeros((nb, w1_ref.shape[2]), F32)
    for l in range(NSA_CMP_LEN):
        xl = jnp.concatenate([xs_ref[c, pl.ds(l, nb, stride=NSA_CMP_STRIDE), :] for c in range(nlt)], axis=1)
        xl = xl + pos_ref[l:l + 1, :]
        acc = acc + _dot(xl.astype(BF16), w1_ref[l])
    hid = _silu(acc).astype(BF16)
    res = _dot(hid, w2_ref[...])
    for g in range(NSA_GROUPS):
        o_ref[0, g] = res[:, g * LANES:(g + 1) * LANES]


def _nsa_compress(proj, cmp_pos, cmp_w1, cmp_w2):
    bsz, t, _ = proj.shape
    g, dh = NSA_GROUPS, NSA_HEAD_DIM
    nb = t // NSA_CMP_STRIDE
    w = 2 * NSA_KV_W
    w1 = cmp_w1.reshape(2, NSA_CMP_LEN, dh, dh).astype(BF16)
    w2 = cmp_w2.astype(BF16)
    place = lambda blk, c0: jnp.pad(blk, [(0, 0)] * (blk.ndim - 1) + [(c0, w - dh - c0)])
    w1c = jnp.concatenate([place(w1[i], (gi * 2 + i) * dh) for i in range(2) for gi in range(g)], axis=1)
    w2c = jnp.concatenate([place(w2[i], (gi * 2 + i) * dh) for gi in range(g) for i in range(2)], axis=0)
    pos = jnp.broadcast_to(cmp_pos[:, :, None, :], (2, NSA_CMP_LEN, g, dh)).transpose(1, 0, 2, 3).reshape(NSA_CMP_LEN, w)
    return pl.pallas_call(
        _nsa_compress_kernel,
        grid=(bsz,),
        in_specs=[
            pl.BlockSpec((1, t, w), lambda b: (b, 0, NSA_COL_CMP // w)),
            pl.BlockSpec((NSA_CMP_LEN, w), lambda b: (0, 0)),
            pl.BlockSpec((NSA_CMP_LEN, w, w), lambda b: (0, 0, 0), pipeline_mode=pl.Buffered(1)),
            pl.BlockSpec((w, w), lambda b: (0, 0)),
        ],
        out_specs=pl.BlockSpec((1, g, nb, LANES), lambda b: (b, 0, 0, 0)),
        out_shape=jax.ShapeDtypeStruct((bsz, g, nb, LANES), F32),
        scratch_shapes=[pltpu.VMEM((w // LANES, t + NSA_CMP_STRIDE, LANES), F32)],
        compiler_params=pltpu.CompilerParams(dimension_semantics=("parallel",), vmem_limit_bytes=VMEM_LIMIT),
        name="nsa_compress",
    )(proj, pos, w1c, w2c)


def _nsa_attn_kernel(q_ref, kvs_ref, kvw_ref, kvc_ref, gate_ref, z_ref, cb_ref, near_ref, qc_ref, ovl_ref,
                     o_ref, ks_ref, vs_ref, kw_ref, vw_ref, gt_ref, ms_ref, accs_ref, mw_ref, accw_ref, sc_ref):
    qb, dh, hpg = NSA_QTILE, NSA_HEAD_DIM, NSA_HPG
    t = kvs_ref.shape[1]
    nblk = t // NSA_SLC_LEN
    cols = hpg * qb
    g = pl.program_id(1)
    i = pl.program_id(2)

    @pl.when(i == 0)
    def _():
        tok = lax.broadcasted_iota(jnp.int32, (t, LANES), 0)
        ln = lax.broadcasted_iota(jnp.int32, (t, LANES), 1)
        const = jnp.where((ln == CONST_LANE0) | (ln == CONST_LANE0 + 1), 1.0, 0.0)
        onehot = jnp.where(ln - FEAT_LANE0 == tok // NSA_SLC_LEN, 1.0, 0.0)
        row = lax.broadcasted_iota(jnp.int32, (LANES, t), 0)
        kvs = kvs_ref[0]
        kvw = kvw_ref[0]
        ks_ref[...] = jnp.where(ln < dh, kvs, onehot + const).astype(BF16)
        kw_ref[...] = jnp.where(ln < dh, kvw, const).astype(BF16)
        vs_ref[...] = jnp.where(row == 0, 1.0, kvs.T).astype(BF16)
        vw_ref[...] = jnp.where(row == 0, 1.0, kvw.T).astype(BF16)

    q_t = (q_ref[0] * (dh ** -0.5)).T
    q_heads = jnp.concatenate([q_t[hh * dh:(hh + 1) * dh] for hh in range(hpg)], axis=1)

    def scores(branch, qa_t, start, nk, bias):
        sc = _dot(branch[0][pl.ds(pl.multiple_of(start, qb), nk), :], qa_t)
        return sc if bias is None else sc + bias

    def update(branch, start, nk, sc):
        _, vt_ref, m_ref, acc_ref = branch
        m_old = m_ref[...]
        m_new = jnp.maximum(m_old, jnp.max(sc, axis=0, keepdims=True))
        alpha = jnp.exp(m_old - m_new)
        pe = jnp.exp(sc - m_new).astype(BF16)
        acc_ref[...] = alpha * acc_ref[...] + _dot(vt_ref[:, pl.ds(pl.multiple_of(start, qb), nk)], pe)
        m_ref[...] = m_new

    sel = (ks_ref, vs_ref, ms_ref, accs_ref)
    win = (kw_ref, vw_ref, mw_ref, accw_ref)
    for m_ref, acc_ref in ((ms_ref, accs_ref), (mw_ref, accw_ref)):
        m_ref[...] = jnp.full(m_ref.shape, NEG_INF, F32)
        acc_ref[...] = jnp.zeros(acc_ref.shape, F32)

    nwt = NSA_WINDOW // qb
    assert nwt in (2, 4)
    pad_rows = jnp.zeros((LANES - CONST_LANE0 - 8, cols), F32)
    qa_win = jnp.concatenate([q_heads, jnp.zeros((32, cols), F32), qc_ref[0], pad_rows], axis=0).astype(BF16)

    kvc = kvc_ref[0, 0]
    lane_k = lax.broadcasted_iota(jnp.int32, kvc.shape, 1)
    kc16 = jnp.where(lane_k < dh, kvc, 0.0).astype(BF16)
    s = _dot(kc16, qa_win) + cb_ref[0, 0]

    kk = lax.broadcasted_iota(jnp.int32, (qb, cols), 0)
    rr = lax.broadcasted_iota(jnp.int32, (qb, cols), 1) % qb
    w4_start = jnp.maximum(i - nwt, 0) * qb
    sc_w4 = scores(win, qa_win, w4_start, qb, jnp.where((rr < kk) & (i >= nwt), 0.0, NEG_INF))

    row2 = lax.broadcasted_iota(jnp.int32, (2 * qb, cols), 0)
    if nwt == 4:
        w32_start = jnp.maximum(i - 3, 0) * qb
        sc_w32 = scores(win, qa_win, w32_start, 2 * qb,
                        jnp.where(row2 < (i - 1) * qb - w32_start, 0.0, NEG_INF))
    near_start = jnp.maximum(i - 1, 0) * qb
    near_bias = near_ref[0, jnp.where(i == 0, 1, 0)]
    sc_wn = scores(win, qa_win, near_start, 2 * qb, near_bias)

    s = jnp.exp(s - jnp.max(s, axis=0, keepdims=True))
    p = s / jnp.sum(s, axis=0, keepdims=True)
    tq_lane = i * qb + lax.broadcasted_iota(jnp.int32, (1, cols), 1) % qb
    p16 = (p * (tq_lane >= NSA_CMP_LEN - 1).astype(F32)).astype(BF16)
    o_cmp = _dot(kvc.T.astype(BF16), p16)
    ovl = ovl_ref[...].astype(BF16)
    imp = _dot(ovl, p16[:, 0:qb])
    for hh in range(1, hpg):
        imp = imp + _dot(ovl, p16[:, hh * qb:(hh + 1) * qb])

    update(win, w4_start, qb, sc_w4)

    blk = lax.broadcasted_iota(jnp.int32, (32, qb), 0)
    tq = i * qb + lax.broadcasted_iota(jnp.int32, (32, qb), 1)
    cur = tq // NSA_SLC_LEN
    forced = (blk == 0) | (blk == cur) | (blk == cur - 1)
    val = jnp.where(forced, jnp.inf, jnp.where(blk * NSA_SLC_LEN <= tq, imp, -jnp.inf))
    cnt = jnp.zeros((32, qb), jnp.int32)
    for s2 in range(nblk):
        other = val[s2:s2 + 1, :]
        cnt = cnt + ((other > val) | ((other == val) & (s2 < blk))).astype(jnp.int32)
    feat = jnp.where((cnt < min(NSA_TOP_K, nblk)) & (blk < nblk), 0.0, NEG_INF)
    qa = jnp.concatenate([q_heads, jnp.concatenate([feat] * hpg, axis=1), qc_ref[0], pad_rows],
                         axis=0).astype(BF16)

    if nwt == 4:
        update(win, w32_start, 2 * qb, sc_w32)
    sc_sn = scores(sel, qa, near_start, 2 * qb, near_bias)
    update(win, near_start, 2 * qb, sc_wn)
    update(sel, near_start, 2 * qb, sc_sn)

    n_far = jnp.maximum(i - 1, 0)
    n_pairs = (n_far + 1) // 2
    pair_start = lambda p: jnp.maximum(2 * p - n_far % 2, 0) * qb
    last_pair = jnp.maximum(n_pairs - 1, 0)
    sc_ref[...] = scores(sel, qa, 0, 2 * qb, jnp.where(row2 < (2 - n_far % 2) * qb, 0.0, NEG_INF))

    def sel_body(k, carry):
        sc_odd = scores(sel, qa, pair_start(2 * k + 1), 2 * qb, None)
        update(sel, pair_start(2 * k), 2 * qb, sc_ref[...])
        sc_ref[...] = scores(sel, qa, pair_start(jnp.minimum(2 * k + 2, last_pair)), 2 * qb, None)
        update(sel, pair_start(2 * k + 1), 2 * qb, sc_odd)
        return carry

    lax.fori_loop(0, n_pairs // 2, sel_body, 0)

    @pl.when(n_pairs % 2 == 1)
    def _():
        update(sel, pair_start(n_pairs - 1), 2 * qb, sc_ref[...])

    def finish(acc_ref):
        acc = acc_ref[...]
        return acc[dh:] / acc[0:1]

    o_slc = finish(accs_ref)
    o_win = finish(accw_ref)

    gt_ref[...] = jax.nn.sigmoid(gate_ref[0]).T
    outs = []
    for hh in range(hpg):
        sl = slice(hh * qb, (hh + 1) * qb)
        base = (g * hpg + hh) * 3
        gate = [gt_ref[pl.ds(base + br, 1), :] for br in range(3)]
        outs.append(gate[0] * o_cmp[dh:, sl] + gate[1] * o_slc[:, sl] + gate[2] * o_win[:, sl])
    out = jnp.concatenate(outs, axis=0).T
    o_ref[0] = (out * _silu(z_ref[0])).astype(o_ref.dtype)


def _nsa_attn(proj, kv_cmp, near, cmp_bias, qconst, ovl):
    bsz, t, _ = proj.shape
    qb, hpg = NSA_QTILE, NSA_HPG
    gw = hpg * NSA_HEAD_DIM
    nb = t // NSA_CMP_STRIDE
    cols = hpg * qb
    return pl.pallas_call(
        _nsa_attn_kernel,
        grid=(bsz, NSA_GROUPS, t // qb),
        in_specs=[
            pl.BlockSpec((1, qb, gw), lambda b, g, i: (b, i, NSA_COL_Q // gw + g)),
            pl.BlockSpec((1, t, LANES), lambda b, g, i: (b, 0, NSA_COL_SEL // LANES + g)),
            pl.BlockSpec((1, t, LANES), lambda b, g, i: (b, 0, NSA_COL_WIN // LANES + g)),
            pl.BlockSpec((1, 1, nb, LANES), lambda b, g, i: (b, g, 0, 0)),
            pl.BlockSpec((1, qb, LANES), lambda b, g, i: (b, i, NSA_COL_GATE // LANES)),
            pl.BlockSpec((1, qb, gw), lambda b, g, i: (b, i, NSA_COL_Z // gw + g)),
            pl.BlockSpec((1, 1, nb, cols), lambda b, g, i: (i, g, 0, 0)),
            pl.BlockSpec((1, 2, 2 * qb, cols), lambda b, g, i: (g, 0, 0, 0)),
            pl.BlockSpec((1, 8, cols), lambda b, g, i: (g, 0, 0)),
            pl.BlockSpec((32, nb), lambda b, g, i: (0, 0)),
        ],
        out_specs=pl.BlockSpec((1, qb, gw), lambda b, g, i: (b, i, g)),
        out_shape=jax.ShapeDtypeStruct((bsz, t, NSA_Q_W), BF16),
        scratch_shapes=[
            pltpu.VMEM((t, LANES), BF16), pltpu.VMEM((LANES, t), BF16),
            pltpu.VMEM((t, LANES), BF16), pltpu.VMEM((LANES, t), BF16),
            pltpu.VMEM((LANES, qb), F32),
            pltpu.VMEM((1, cols), F32), pltpu.VMEM((LANES, cols), F32),
            pltpu.VMEM((1, cols), F32), pltpu.VMEM((LANES, cols), F32),
            pltpu.VMEM((2 * qb, cols), F32),
        ],
        compiler_params=pltpu.CompilerParams(
            dimension_semantics=("parallel", "parallel", "arbitrary"), vmem_limit_bytes=VMEM_LIMIT),
        name="nsa_attention",
    )(proj, proj, proj, kv_cmp, proj, proj, cmp_bias, near, qconst, ovl)


def _nsa_layer(x, mod, norm_g, w_in, cmp_pos, cmp_w1, cmp_w2, rel_bias, w_out, final_g):
    t = x.shape[1]
    assert t // NSA_SLC_LEN <= 32 and NSA_COL_Z % (NSA_HPG * NSA_HEAD_DIM) == 0
    perm = _nsa_column_perm()
    cuts = [0] + [j for j in range(1, len(perm)) if perm[j] != perm[j - 1] + (perm[j - 1] >= 0)] + [len(perm)]
    runs = [(int(perm[a]), b - a) for a, b in zip(cuts[:-1], cuts[1:])]
    w16 = w_in.astype(BF16)
    w_in_p = jnp.concatenate([w16[:, s:s + n] if s >= 0 else jnp.zeros((w_in.shape[0], n), BF16)
                              for s, n in runs], axis=1)
    proj = _inproj(x, norm_g, mod, w_in_p, tm=256, tn=NSA_PROJ_W)
    kv_cmp = _nsa_compress(proj, cmp_pos, cmp_w1, cmp_w2)
    near, cmp_bias, qconst = _nsa_tables(rel_bias, t)
    o = _nsa_attn(proj, kv_cmp, near, cmp_bias, qconst, jnp.asarray(_overlap_t(t)))
    return _outproj(o, w_out.astype(BF16), x, mod, final_g, tm=512)


def kernel(x, c, ada_w, ada_b, norm_g, gdn_w_in, gdn_conv_w, gdn_a_log, gdn_dt_bias, gdn_norm_w, gdn_w_out,
           nsa_w_in, nsa_cmp_pos, nsa_cmp_w1, nsa_cmp_w2, nsa_w_out, rel_bias, final_g):
    bsz, t, d = x.shape
    mod = _modulation(c, ada_w, ada_b).reshape(ada_w.shape[0], bsz, 3, d)
    x = _gdn_layer(x, mod[0], norm_g[0], gdn_w_in[0], gdn_conv_w[0], gdn_a_log[0], gdn_dt_bias[0],
                   gdn_norm_w[0], gdn_w_out[0])
    return _nsa_layer(x, mod[1], norm_g[1], nsa_w_in[0], nsa_cmp_pos[0], nsa_cmp_w1[0], nsa_cmp_w2[0],
                      rel_bias, nsa_w_out[0], final_g)
```

```python
import functools
import math

import numpy as np
import jax
import jax.numpy as jnp
from jax import lax
from jax.experimental import pallas as pl
from jax.experimental.pallas import tpu as pltpu

F32 = jnp.float32
BF16 = jnp.bfloat16
HIGHEST = lax.Precision.HIGHEST

EPS = 1e-6
NEG_INF = -1e30
LANES = 128
VMEM_LIMIT = 56 * 1024 * 1024

GDN_QK_HEADS = 8
GDN_V_HEADS = 16
GDN_HEAD_DIM = 128
GDN_CONV = 4
GDN_CHUNK = 64
GDN_QK_W = GDN_QK_HEADS * GDN_HEAD_DIM
GDN_V_W = GDN_V_HEADS * GDN_HEAD_DIM
GDN_CONV_W = 2 * GDN_QK_W + GDN_V_W
GDN_TILE = 256
GDN_HEADS_PER_STEP = 4


def _silu(x):
    return x * jax.nn.sigmoid(x)


def _dot(a, b, **kw):
    return jnp.dot(a, b, preferred_element_type=F32, **kw)


def _dot_nt(a, b, **kw):
    return lax.dot_general(a, b, (((1,), (1,)), ((), ())), preferred_element_type=F32, **kw)


def _mod_kernel(c_ref, w_ref, b_ref, o_ref):
    cond = _silu(c_ref[...])
    o_ref[0] = _dot(cond, w_ref[0], precision=HIGHEST) + b_ref[0]


def _modulation(c, ada_w, ada_b):
    depth, d, d3 = ada_w.shape
    bsz = c.shape[0]
    return pl.pallas_call(
        _mod_kernel,
        grid=(depth, d3 // d),
        in_specs=[
            pl.BlockSpec((bsz, d), lambda i, j: (0, 0)),
            pl.BlockSpec((1, d, d), lambda i, j: (i, 0, j)),
            pl.BlockSpec((1, 1, d), lambda i, j: (i, 0, j)),
        ],
        out_specs=pl.BlockSpec((1, bsz, d), lambda i, j: (i, 0, j)),
        out_shape=jax.ShapeDtypeStruct((depth, bsz, d3), F32),
        name="adaln_mod",
    )(c, ada_w, ada_b.reshape(depth, 1, d3))


def _inproj_kernel(x_ref, g_ref, mod_ref, w_ref, o_ref, *, tn):
    x = x_ref[0]
    m = mod_ref[0]
    y = x * lax.rsqrt(jnp.mean(x * x, axis=-1, keepdims=True) + EPS) * g_ref[...]
    h = (y * (1.0 + m[1:2]) + m[0:1]).astype(BF16)
    for j in range(w_ref.shape[1] // tn):
        o_ref[0, :, j * tn:(j + 1) * tn] = _dot(h, w_ref[:, j * tn:(j + 1) * tn])


def _inproj(x, g, mod, w, *, tm, tn):
    bsz, t, d = x.shape
    n = w.shape[1]
    assert t % tm == 0 and n % tn == 0
    return pl.pallas_call(
        functools.partial(_inproj_kernel, tn=tn),
        grid=(bsz, t // tm),
        in_specs=[
            pl.BlockSpec((1, tm, d), lambda b, i: (b, i, 0)),
            pl.BlockSpec((1, d), lambda b, i: (0, 0)),
            pl.BlockSpec((1, 3, d), lambda b, i: (b, 0, 0)),
            pl.BlockSpec((d, n), lambda b, i: (0, 0), pipeline_mode=pl.Buffered(1)),
        ],
        out_specs=pl.BlockSpec((1, tm, n), lambda b, i: (b, i, 0)),
        out_shape=jax.ShapeDtypeStruct((bsz, t, n), F32),
        compiler_params=pltpu.CompilerParams(
            dimension_semantics=("parallel", "parallel"), vmem_limit_bytes=VMEM_LIMIT),
        name="norm_mod_inproj",
    )(x, g.reshape(1, d), mod, w)


def _gdn_inproj_kernel(x_ref, g_ref, mod_ref, w_ref, cw_ref, alog_ref, dtb_ref, o_ref, col_ref, row_ref,
                        ext_ref, tail_ref, *, tn):
    dh = GDN_HEAD_DIM
    tm = x_ref.shape[1]
    x = x_ref[0]
    m = mod_ref[0]
    y = x * lax.rsqrt(jnp.mean(x * x, axis=-1, keepdims=True) + EPS) * g_ref[...]
    h = (y * (1.0 + m[1:2]) + m[0:1]).astype(BF16)

    @pl.when(pl.program_id(1) == 0)
    def _():
        tail_ref[...] = jnp.zeros_like(tail_ref)

    def conv_tile(j):
        cols = slice(j * tn, (j + 1) * tn)
        ext_ref[0:8, :] = tail_ref[:, cols]
        ext_ref[8:8 + tm, :] = o_ref[0, :, cols]
        tail_ref[:, cols] = ext_ref[tm:tm + 8, :]
        w = cw_ref[:, cols]
        ext = ext_ref[...]
        acc = w[0:1] * ext
        for tap in range(1, GDN_CONV):
            acc = pltpu.roll(acc, 1, axis=0) + w[tap:tap + 1] * ext
        act = _silu(acc[8:8 + tm])
        if j * tn < 2 * GDN_QK_W:
            scale = dh ** -0.5 if j * tn < GDN_QK_W else 1.0
            heads = [act[:, c * dh:(c + 1) * dh] for c in range(tn // dh)]
            heads = [hd * (lax.rsqrt(jnp.sum(hd * hd, axis=-1, keepdims=True) + EPS) * scale) for hd in heads]
            act = jnp.concatenate(heads, axis=1)
        o_ref[0, :, cols] = act

    n = w_ref.shape[1]
    n_conv = GDN_CONV_W // tn
    bounds = [(j * tn, min((j + 1) * tn, n)) for j in range(-(-n // tn))]
    for j, (lo, hi) in enumerate(bounds):
        o_ref[0, :, lo:hi] = _dot(h, w_ref[:, lo:hi])
        if 1 <= j <= n_conv:
            conv_tile(j - 1)
    _gdn_gate_scalars(o_ref[0, :, n - LANES:n], alog_ref[...], dtb_ref[...], col_ref, row_ref)


def _gdn_inproj(x, g, mod, w, conv_w, a_log, dt_bias, *, tm, tn):
    bsz, t, d = x.shape
    n = w.shape[1]
    assert t % tm == 0 and GDN_CONV_W % tn == 0 and GDN_QK_W % tn == 0 and n > GDN_CONV_W
    pad = lambda u: jnp.zeros((1, LANES), F32).at[0, GDN_V_HEADS:2 * GDN_V_HEADS].set(u)
    ncb = tm // GDN_CHUNK
    return pl.pallas_call(
        functools.partial(_gdn_inproj_kernel, tn=tn),
        grid=(bsz, t // tm),
        in_specs=[
            pl.BlockSpec((1, tm, d), lambda b, i: (b, i, 0)),
            pl.BlockSpec((1, d), lambda b, i: (0, 0)),
            pl.BlockSpec((1, 3, d), lambda b, i: (b, 0, 0)),
            pl.BlockSpec((d, n), lambda b, i: (0, 0), pipeline_mode=pl.Buffered(1)),
            pl.BlockSpec((GDN_CONV, GDN_CONV_W), lambda b, i: (0, 0)),
            pl.BlockSpec((1, LANES), lambda b, i: (0, 0)),
            pl.BlockSpec((1, LANES), lambda b, i: (0, 0)),
        ],
        out_specs=[
            pl.BlockSpec((1, tm, n), lambda b, i: (b, i, 0)),
            pl.BlockSpec((1, tm, LANES), lambda b, i: (b, i, 0)),
            pl.BlockSpec((1, ncb, GDN_V_HEADS // 2, 2 * GDN_CHUNK), lambda b, i: (b, i, 0, 0)),
        ],
        out_shape=[
            jax.ShapeDtypeStruct((bsz, t, n), F32),
            jax.ShapeDtypeStruct((bsz, t, LANES), F32),
            jax.ShapeDtypeStruct((bsz, t // GDN_CHUNK, GDN_V_HEADS // 2, 2 * GDN_CHUNK), F32),
        ],
        scratch_shapes=[pltpu.VMEM((tm + 8, tn), F32), pltpu.VMEM((8, GDN_CONV_W), F32)],
        compiler_params=pltpu.CompilerParams(
            dimension_semantics=("parallel", "arbitrary"), vmem_limit_bytes=VMEM_LIMIT),
        name="gdn_norm_inproj_conv",
    )(x, g.reshape(1, d), mod, w, conv_w, pad(a_log), pad(dt_bias))


def _outproj_kernel(o_ref, w_ref, x_ref, mod_ref, *rest, final_norm):
    y = _dot(o_ref[0], w_ref[...])
    x = x_ref[0] + mod_ref[0][2:3] * y
    if final_norm:
        fg_ref, out_ref = rest
        x = x * lax.rsqrt(jnp.mean(x * x, axis=-1, keepdims=True) + EPS) * fg_ref[...]
    else:
        (out_ref,) = rest
    out_ref[0] = x


def _outproj(o, w, x, mod, final_g=None, *, tm):
    bsz, t, d = x.shape
    k = o.shape[-1]
    in_specs = [
        pl.BlockSpec((1, tm, k), lambda b, i: (b, i, 0)),
        pl.BlockSpec((k, d), lambda b, i: (0, 0), pipeline_mode=pl.Buffered(1)),
        pl.BlockSpec((1, tm, d), lambda b, i: (b, i, 0)),
        pl.BlockSpec((1, 3, d), lambda b, i: (b, 0, 0)),
    ]
    args = [o, w, x, mod]
    if final_g is not None:
        in_specs.append(pl.BlockSpec((1, d), lambda b, i: (0, 0)))
        args.append(final_g.reshape(1, d))
    return pl.pallas_call(
        functools.partial(_outproj_kernel, final_norm=final_g is not None),
        grid=(bsz, t // tm),
        in_specs=in_specs,
        out_specs=pl.BlockSpec((1, tm, d), lambda b, i: (b, i, 0)),
        out_shape=jax.ShapeDtypeStruct((bsz, t, d), F32),
        compiler_params=pltpu.CompilerParams(
            dimension_semantics=("parallel", "parallel"), vmem_limit_bytes=VMEM_LIMIT),
        name="outproj_residual",
    )(*args)


def _gdn_gate_scalars(ba, alog, dtb, col_ref, row_ref):
    cs, nh = GDN_CHUNK, GDN_V_HEADS
    lane = lax.broadcasted_iota(jnp.int32, ba.shape, 1)
    g = -jnp.exp(alog) * jax.nn.softplus(ba + dtb)
    vals = jnp.where(lane < nh, jax.nn.sigmoid(ba), g)
    r = lax.broadcasted_iota(jnp.int32, (cs, cs), 0)
    c = lax.broadcasted_iota(jnp.int32, (cs, cs), 1)
    tri = (r >= c).astype(F32)
    is_beta = lax.broadcasted_iota(jnp.int32, (cs, LANES), 1) < nh
    pr = lax.broadcasted_iota(jnp.int32, (nh // 2, LANES), 0)
    pc = lax.broadcasted_iota(jnp.int32, (nh // 2, LANES), 1)
    pick_even = (pc == nh + 2 * pr).astype(F32)
    for n in range(ba.shape[0] // cs):
        v = vals[n * cs:(n + 1) * cs]
        cum = _dot(tri, v, precision=HIGHEST)
        out = jnp.where(is_beta, v, cum)
        col_ref[0, n * cs:(n + 1) * cs, :] = out
        both = jnp.concatenate([out, pltpu.roll(out, LANES - 1, axis=1)], axis=0)
        row_ref[0, n] = _dot_nt(pick_even, both, precision=HIGHEST)


TRI_BASE = 8
PACK = 4


def _block_rows(p):
    n = p.shape[0]
    blk = lax.broadcasted_iota(jnp.int32, p.shape, 1) // n
    return jnp.concatenate([jnp.where(blk == j, p, 0.0) for j in range(PACK)], axis=0).astype(BF16)


def _tri_inverse_packed(mats):
    n = mats[0].shape[0]
    r = lax.broadcasted_iota(jnp.int32, mats[0].shape, 0)
    c = lax.broadcasted_iota(jnp.int32, mats[0].shape, 1) % n
    same = lambda s: (r // s) == (c // s)
    mm = lambda a, b: _dot(a.astype(BF16), _block_rows(b))
    diag = [jnp.where(same(TRI_BASE), a, 0.0) for a in mats]
    inv = [jnp.where(r == c, 1.0, 0.0) - d for d in diag]
    pw = diag
    k = 1
    while 2 * k < TRI_BASE:
        pw = [mm(m, m) for m in pw]
        inv = [p + mm(p, m) for p, m in zip(inv, pw)]
        k *= 2
        yield None
    s = TRI_BASE
    while s < n:
        sub = same(2 * s) & jnp.logical_not(same(s))
        left = [mm(p, jnp.where(sub, a, 0.0)) for p, a in zip(inv, mats)]
        inv = [p - mm(l, p) for p, l in zip(inv, left)]
        s *= 2
        yield None
    yield inv


def _gdn_chunk_kernel(q_ref, k_ref, v_ref, z_ref, col_ref, row_ref, nw_ref,
                      o_ref, s_ref, hu_ref, hwq_ref, hqk_ref, hkd_ref, hgl_ref):
    cs, dh, nh = GDN_CHUNK, GDN_HEAD_DIM, GDN_V_HEADS
    tt = q_ref.shape[1]
    hps = q_ref.shape[2] // dh
    ncb = tt // cs
    assert ncb % 2 == 0 and PACK == 4
    hg = pl.program_id(1)
    ti = pl.program_id(2)
    wslot = ti % 2
    rslot = 1 - wslot
    jidx = lambda hl, n, e: (hl * ncb + n) * 2 + e

    @pl.when(ti == 0)
    def _():
        s_ref[...] = jnp.zeros_like(s_ref)
        for h_ref in (hu_ref, hwq_ref, hqk_ref, hkd_ref, hgl_ref):
            h_ref[1] = jnp.zeros(h_ref.shape[1:], h_ref.dtype)

    chunks = [slice(n * cs, (n + 1) * cs) for n in range(ncb)]

    def prepare():
        q = [q_ref[0, :, hl * dh:(hl + 1) * dh] for hl in range(hps)]
        k = [k_ref[0, :, hl * dh:(hl + 1) * dh] for hl in range(hps)]
        v = [[v_ref[0, :, (2 * hl + e) * dh:(2 * hl + e + 1) * dh] for e in range(2)] for hl in range(hps)]
        lane = lax.broadcasted_iota(jnp.int32, (tt, LANES), 1)
        colv = col_ref[0]
        column = lambda idx: jnp.sum(jnp.where(lane == idx, colv, 0.0), axis=1, keepdims=True)
        hq = [hg * hps + hl for hl in range(hps)]
        beta = [[column(2 * h + e) for e in range(2)] for h in hq]
        gc = [[column(nh + 2 * h + e) for e in range(2)] for h in hq]
        egc = [[jnp.exp(x) for x in pair] for pair in gc]
        yield

        pr = lax.broadcasted_iota(jnp.int32, (cs, PACK * cs), 0)
        pb = lax.broadcasted_iota(jnp.int32, (cs, PACK * cs), 1) // cs
        pc = lax.broadcasted_iota(jnp.int32, (cs, PACK * cs), 1) % cs
        groups = [(hl, gi) for hl in range(hps) for gi in range(ncb // 2)]
        a_mats, g_rows = [], {}
        for hl, gi in groups:
            k16, q16 = k[hl].astype(BF16), q[hl].astype(BF16)
            pair = (chunks[2 * gi], chunks[2 * gi + 1])
            kdup = [jnp.concatenate([k16[sl], k16[sl]], axis=0) for sl in pair]
            kk = jnp.concatenate([_dot_nt(k16[sl], kd) for sl, kd in zip(pair, kdup)], axis=1)
            qk = jnp.concatenate([_dot_nt(q16[sl], kd) for sl, kd in zip(pair, kdup)], axis=1)
            pick = lambda cols: jnp.where(pb == 0, cols[0][pair[0]], jnp.where(
                pb == 1, cols[1][pair[0]], jnp.where(pb == 2, cols[0][pair[1]], cols[1][pair[1]])))
            g_row = jnp.concatenate([row_ref[0, 2 * gi + j, pl.ds(hq[hl], 1), :] for j in range(2)], axis=1)
            decay = jnp.exp(jnp.where(pr >= pc, pick(gc[hl]) - g_row, -jnp.inf))
            a_mats.append(jnp.where(pr > pc, pick(beta[hl]) * kk * decay, 0.0))
            hqk_ref[wslot, hl * (ncb // 2) + gi] = (qk * decay).astype(BF16)
            g_rows[hl, gi] = g_row
            yield
        t_mats = None
        for t_mats in _tri_inverse_packed(a_mats):
            yield

        for (hl, gi), t_mat in zip(groups, t_mats):
            rhs = []
            for n in (2 * gi, 2 * gi + 1):
                sl = chunks[n]
                for e in range(2):
                    j = 2 * (n % 2) + e
                    g_last = g_rows[hl, gi][:, j * cs + cs - 1:(j + 1) * cs]
                    hgl_ref[wslot, jidx(hl, n, e)] = jnp.broadcast_to(jnp.exp(g_last), (1, LANES))
                    hkd_ref[wslot, jidx(hl, n, e)] = (k[hl][sl] * jnp.exp(g_last - gc[hl][e][sl])).T.astype(BF16)
                    kbeta = k[hl][sl] * beta[hl][e][sl]
                    rhs.append(jnp.concatenate([v[hl][e][sl] * beta[hl][e][sl], kbeta * egc[hl][e][sl]], axis=1))
            out = _dot(_block_rows(t_mat), jnp.concatenate(rhs, axis=0).astype(BF16))
            for n in (2 * gi, 2 * gi + 1):
                for e in range(2):
                    uw = out[(2 * (n % 2) + e) * cs:(2 * (n % 2) + e + 1) * cs]
                    hu_ref[wslot, jidx(hl, n, e)] = uw[:, :dh]
                    q_dec = q[hl][chunks[n]] * egc[hl][e][chunks[n]]
                    hwq_ref[wslot, jidx(hl, n, e)] = jnp.concatenate([uw[:, dh:], q_dec], axis=0).astype(BF16)
            yield

    def recurrence():
        nw = nw_ref[...]
        heads = [(hl, e) for hl in range(hps) for e in range(2)]
        state = {he: s_ref[2 * he[0] + he[1]] for he in heads}
        for n, sl in enumerate(chunks):
            ws = {}
            for hl, e in heads:
                ws[hl, e] = _dot(hwq_ref[rslot, jidx(hl, n, e)], state[hl, e].astype(BF16))
            yield
            for hl, e in heads:
                j = jidx(hl, n, e)
                v16 = (hu_ref[rslot, j] - ws[hl, e][:cs]).astype(BF16)
                lb = 2 * (n % 2) + e
                qkd = hqk_ref[rslot, hl * (ncb // 2) + n // 2][:, lb * cs:(lb + 1) * cs]
                o = ws[hl, e][cs:] + _dot(qkd, v16)
                state[hl, e] = state[hl, e] * hgl_ref[rslot, j] + _dot(hkd_ref[rslot, j], v16)
                o = o * lax.rsqrt(jnp.mean(o * o, axis=-1, keepdims=True) + EPS) * nw
                lanes = slice((2 * hl + e) * dh, (2 * hl + e + 1) * dh)
                o_ref[0, sl, lanes] = (o * _silu(z_ref[0, sl, lanes])).astype(o_ref.dtype)
            yield
        for (hl, e), st in state.items():
            s_ref[2 * hl + e] = st

    prep, rec = prepare(), recurrence()
    n_prep = 1 + hps * (ncb // 2) * 2 + 6
    per_stage = -(-n_prep // (2 * ncb))
    prep_live = rec_live = True
    while prep_live or rec_live:
        for _ in range(per_stage):
            if prep_live:
                prep_live = next(prep, "done") != "done"
        if rec_live:
            rec_live = next(rec, "done") != "done"


def _gdn_chunk(proj, col, row, norm_w, *, tt=GDN_TILE, hps=GDN_HEADS_PER_STEP):
    bsz, t, _ = proj.shape
    dh = GDN_HEAD_DIM
    qw, vw = hps * dh, 2 * hps * dh
    k_blk0 = GDN_QK_W // qw
    v_blk0 = 2 * GDN_QK_W // vw
    z_blk0 = GDN_CONV_W // vw
    ncb = tt // GDN_CHUNK
    nt = t // tt
    nj = hps * ncb * 2
    cur = lambda i: jnp.minimum(i, nt - 1)
    prev = lambda i: jnp.maximum(i - 1, 0)
    return pl.pallas_call(
        _gdn_chunk_kernel,
        grid=(bsz, GDN_QK_HEADS // hps, nt + 1),
        in_specs=[
            pl.BlockSpec((1, tt, qw), lambda b, h, i: (b, cur(i), h)),
            pl.BlockSpec((1, tt, qw), lambda b, h, i: (b, cur(i), k_blk0 + h)),
            pl.BlockSpec((1, tt, vw), lambda b, h, i: (b, cur(i), v_blk0 + h)),
            pl.BlockSpec((1, tt, vw), lambda b, h, i: (b, prev(i), z_blk0 + h)),
            pl.BlockSpec((1, tt, LANES), lambda b, h, i: (b, cur(i), 0)),
            pl.BlockSpec((1, ncb, GDN_V_HEADS // 2, 2 * GDN_CHUNK), lambda b, h, i: (b, cur(i), 0, 0)),
            pl.BlockSpec((1, dh), lambda b, h, i: (0, 0)),
        ],
        out_specs=pl.BlockSpec((1, tt, vw), lambda b, h, i: (b, prev(i), h)),
        out_shape=jax.ShapeDtypeStruct((bsz, t, GDN_V_W), BF16),
        scratch_shapes=[
            pltpu.VMEM((2 * hps, dh, dh), F32),
            pltpu.VMEM((2, nj, GDN_CHUNK, dh), F32),
            pltpu.VMEM((2, nj, 2 * GDN_CHUNK, dh), BF16),
            pltpu.VMEM((2, nj // PACK, GDN_CHUNK, PACK * GDN_CHUNK), BF16),
            pltpu.VMEM((2, nj, dh, GDN_CHUNK), BF16),
            pltpu.VMEM((2, nj, 1, LANES), F32),
        ],
        compiler_params=pltpu.CompilerParams(
            dimension_semantics=("parallel", "parallel", "arbitrary"), vmem_limit_bytes=VMEM_LIMIT),
        name="gdn_chunk_scan",
    )(proj, proj, proj, proj, col, row, norm_w.reshape(1, dh))


def _gdn_layer(x, mod, norm_g, w_in, conv_w, a_log, dt_bias, norm_w, w_out):
    n_in = w_in.shape[1]
    n_pad = -(-n_in // (7 * LANES)) * (7 * LANES)
    w_in_p = jnp.pad(w_in, ((0, 0), (0, n_pad - n_in))).astype(BF16)
    proj, col, row = _gdn_inproj(x, norm_g, mod, w_in_p, conv_w, a_log, dt_bias, tm=256, tn=4 * LANES)
    o = _gdn_chunk(proj, col, row, norm_w)
    return _outproj(o, w_out.astype(BF16), x, mod, tm=512)


NSA_HEADS = 16
NSA_GROUPS = 4
NSA_HPG = NSA_HEADS // NSA_GROUPS
NSA_HEAD_DIM = 64
NSA_CMP_LEN = 32
NSA_CMP_STRIDE = 16
NSA_SLC_LEN = 64
NSA_TOP_K = 8
NSA_WINDOW = 512
NSA_QTILE = 256
NSA_Q_W = NSA_HEADS * NSA_HEAD_DIM
NSA_KV_W = NSA_GROUPS * NSA_HEAD_DIM
REL_BUCKETS = 32
REL_MAX_DIST = 128
FEAT_LANE0 = NSA_HEAD_DIM
CONST_LANE0 = FEAT_LANE0 + 32
VT_PAD = 16
NSA_COL_Q = 0
NSA_COL_CMP = NSA_Q_W
NSA_COL_SEL = NSA_COL_CMP + 2 * NSA_KV_W
NSA_COL_WIN = NSA_COL_SEL + 2 * NSA_KV_W
NSA_COL_Z = NSA_COL_WIN + 2 * NSA_KV_W
NSA_COL_GATE = NSA_COL_Z + NSA_Q_W
NSA_PROJ_W = NSA_COL_GATE + LANES


def _nsa_column_perm():
    g, dh = NSA_GROUPS, NSA_HEAD_DIM
    kv0 = NSA_Q_W
    cols = list(range(NSA_Q_W))
    cols += [kv0 + i for i in range(2 * NSA_KV_W)]
    for br in (1, 2):
        for gi in range(g):
            cols += [kv0 + (2 * br) * NSA_KV_W + gi * dh + d for d in range(dh)]
            cols += [kv0 + (2 * br + 1) * NSA_KV_W + gi * dh + d for d in range(dh)]
    gate0 = kv0 + 6 * NSA_KV_W
    cols += [gate0 + 3 * NSA_HEADS + i for i in range(NSA_Q_W)]
    cols += [gate0 + i for i in range(3 * NSA_HEADS)] + [-1] * (LANES - 3 * NSA_HEADS)
    assert len(cols) == NSA_PROJ_W
    return np.asarray(cols, np.int32)


def _rel_bucket_table(n):
    d = np.arange(n)
    max_exact = REL_BUCKETS // 2
    nf = np.maximum(d, 1).astype(np.float64)
    large = max_exact + (np.log(nf / max_exact) / math.log(REL_MAX_DIST / max_exact)
                         * (REL_BUCKETS - max_exact)).astype(np.int32)
    large = np.minimum(large, REL_BUCKETS - 1)
    return np.where(d < max_exact, d, large).astype(np.int32)


def _nsa_tables(rel_bias, t):
    qb = NSA_QTILE
    bucket = _rel_bucket_table(t)
    assert np.all(bucket[qb + 1:] == REL_BUCKETS - 1)
    bvec = rel_bias[bucket].T
    far = rel_bias[REL_BUCKETS - 1]
    far_hi = far.astype(BF16)
    far_lo = (far - far_hi.astype(F32)).astype(BF16)
    far_sum = far_hi.astype(F32) + far_lo.astype(F32)
    r = np.arange(qb)[:, None]
    c = np.arange(qb)[None, :]
    d0 = r - c

    def toeplitz(w):
        n = 2 * qb - 1
        ext = jnp.pad(w[:, ::-1], ((0, 0), (0, 1)))
        skew = jnp.tile(ext, (1, qb))[:, :qb * n].reshape(w.shape[0], qb, n)
        return skew[:, :, qb - 1:]

    rel = bvec[:, :2 * qb] - far_sum[:, None]
    t0 = toeplitz(jnp.concatenate([jnp.full((NSA_HEADS, qb - 1), NEG_INF, F32), rel[:, :qb]], axis=1))
    t1 = toeplitz(rel[:, 1:])
    g, hpg = NSA_GROUPS, NSA_HPG
    none = jnp.full_like(t0, NEG_INF)
    near = jnp.stack([t1, t0, t0, none], axis=1).reshape(g, hpg, 2, 2, qb, qb)
    near = near.transpose(0, 2, 3, 5, 1, 4).reshape(g, 2, 2 * qb, hpg * qb)
    nb = t // NSA_CMP_STRIDE
    per_tile = qb // NSA_CMP_STRIDE
    back = 9
    far_d = back * NSA_CMP_STRIDE - (NSA_CMP_LEN - 1)
    assert np.all(bucket[far_d:] == REL_BUCKETS - 1)
    width = (qb - 1 + far_d) // NSA_CMP_STRIDE + 1
    dm = r - NSA_CMP_STRIDE * np.arange(width)[None, :] + far_d
    band = jnp.where(dm >= 0, bvec[:, np.maximum(dm, 0)], NEG_INF)
    tiles = []
    for i in range(t // qb):
        j0 = per_tile * i - back
        lo, hi = max(j0, 0), min(j0 + width, nb)
        tiles.append(jnp.concatenate([
            jnp.broadcast_to(far[:, None, None], (NSA_HEADS, qb, lo)),
            band[:, :, lo - j0:hi - j0],
            jnp.full((NSA_HEADS, qb, nb - hi), NEG_INF, F32)], axis=2))
    cmp_bias = jnp.stack(tiles, axis=0).reshape(t // qb, g, hpg, qb, nb).transpose(0, 1, 4, 2, 3)
    cmp_bias = cmp_bias.reshape(t // qb, g, nb, hpg * qb)
    qconst = jnp.zeros((g, 8, hpg, qb), F32)
    qconst = qconst.at[:, 0].set(jnp.broadcast_to(far_hi.astype(F32).reshape(g, hpg, 1), (g, hpg, qb)))
    qconst = qconst.at[:, 1].set(jnp.broadcast_to(far_lo.astype(F32).reshape(g, hpg, 1), (g, hpg, qb)))
    return near, cmp_bias, qconst.reshape(g, 8, hpg * qb)


def _overlap_t(t):
    n_cmp = (t - NSA_CMP_LEN) // NSA_CMP_STRIDE + 1
    n_slc = t // NSA_SLC_LEN
    c_start = np.arange(n_cmp)[:, None] * NSA_CMP_STRIDE
    s_start = np.arange(n_slc)[None, :] * NSA_SLC_LEN
    ov = np.clip(np.minimum(c_start + NSA_CMP_LEN, s_start + NSA_SLC_LEN) - np.maximum(c_start, s_start), 0, None)
    ov = ov.astype(np.float32) / NSA_CMP_LEN
    out = np.zeros((32, t // NSA_CMP_STRIDE), np.float32)
    out[:n_slc, :n_cmp] = ov.T
    return out


def _nsa_compress_kernel(x_ref, pos_ref, w1_ref, w2_ref, o_ref, xs_ref):
    t = x_ref.shape[1]
    nb = t // NSA_CMP_STRIDE
    nlt = xs_ref.shape[0]
    for c in range(nlt):
        xs_ref[c, 0:t, :] = x_ref[0, :, c * LANES:(c + 1) * LANES]
        xs_ref[c, t:t + NSA_CMP_STRIDE, :] = jnp.zeros((NSA_CMP_STRIDE, LANES), F32)
    acc = jnp.zeros((nb, w1_ref.shape[2]), F32)
    for l in range(NSA_CMP_LEN):
        xl = jnp.concatenate([xs_ref[c, pl.ds(l, nb, stride=NSA_CMP_STRIDE), :] for c in range(nlt)], axis=1)
        xl = xl + pos_ref[l:l + 1, :]
        acc = acc + _dot(xl.astype(BF16), w1_ref[l])
    hid = _silu(acc).astype(BF16)
    res = _dot(hid, w2_ref[...])
    for g in range(NSA_GROUPS):
        o_ref[0, g] = res[:, g * LANES:(g + 1) * LANES]


def _nsa_compress(proj, cmp_pos, cmp_w1, cmp_w2):
    bsz, t, _ = proj.shape
    g, dh = NSA_GROUPS, NSA_HEAD_DIM
    nb = t // NSA_CMP_STRIDE
    w = 2 * NSA_KV_W
    w1 = cmp_w1.reshape(2, NSA_CMP_LEN, dh, dh).astype(BF16)
    w2 = cmp_w2.astype(BF16)
    place = lambda blk, c0: jnp.pad(blk, [(0, 0)] * (blk.ndim - 1) + [(c0, w - dh - c0)])
    w1c = jnp.concatenate([place(w1[i], (gi * 2 + i) * dh) for i in range(2) for gi in range(g)], axis=1)
    w2c = jnp.concatenate([place(w2[i], (gi * 2 + i) * dh) for gi in range(g) for i in range(2)], axis=0)
    pos = jnp.broadcast_to(cmp_pos[:, :, None, :], (2, NSA_CMP_LEN, g, dh)).transpose(1, 0, 2, 3).reshape(NSA_CMP_LEN, w)
    return pl.pallas_call(
        _nsa_compress_kernel,
        grid=(bsz,),
        in_specs=[
            pl.BlockSpec((1, t, w), lambda b: (b, 0, NSA_COL_CMP // w)),
            pl.BlockSpec((NSA_CMP_LEN, w), lambda b: (0, 0)),
            pl.BlockSpec((NSA_CMP_LEN, w, w), lambda b: (0, 0, 0), pipeline_mode=pl.Buffered(1)),
            pl.BlockSpec((w, w), lambda b: (0, 0)),
        ],
        out_specs=pl.BlockSpec((1, g, nb, LANES), lambda b: (b, 0, 0, 0)),
        out_shape=jax.ShapeDtypeStruct((bsz, g, nb, LANES), F32),
        scratch_shapes=[pltpu.VMEM((w // LANES, t + NSA_CMP_STRIDE, LANES), F32)],
        compiler_params=pltpu.CompilerParams(dimension_semantics=("parallel",), vmem_limit_bytes=VMEM_LIMIT),
        name="nsa_compress",
    )(proj, pos, w1c, w2c)


def _nsa_attn_kernel(q_ref, kvs_ref, kvw_ref, kvc_ref, gate_ref, z_ref, cb_ref, near_ref, qc_ref, ovl_ref,
                     o_ref, ks_ref, vs_ref, kw_ref, vw_ref, gt_ref, ms_ref, accs_ref, mw_ref, accw_ref, sc_ref):
    qb, dh, hpg = NSA_QTILE, NSA_HEAD_DIM, NSA_HPG
    t = kvs_ref.shape[1]
    nblk = t // NSA_SLC_LEN
    cols = hpg * qb
    g = pl.program_id(1)
    i = pl.program_id(2)

    @pl.when(i == 0)
    def _():
        tok = lax.broadcasted_iota(jnp.int32, (t, LANES), 0)
        ln = lax.broadcasted_iota(jnp.int32, (t, LANES), 1)
        const = jnp.where((ln == CONST_LANE0) | (ln == CONST_LANE0 + 1), 1.0, 0.0)
        onehot = jnp.where(ln - FEAT_LANE0 == tok // NSA_SLC_LEN, 1.0, 0.0)
        ones_rows = jnp.where(lax.broadcasted_iota(jnp.int32, (VT_PAD, t), 0) == 0, 1.0, 0.0)
        kvs = kvs_ref[0]
        kvw = kvw_ref[0]
        ks_ref[...] = jnp.where(ln < dh, kvs, onehot + const).astype(BF16)
        kw_ref[...] = jnp.where(ln < dh, kvw, const).astype(BF16)
        vs_ref[...] = jnp.concatenate([ones_rows, kvs.T[dh:]], axis=0).astype(BF16)
        vw_ref[...] = jnp.concatenate([ones_rows, kvw.T[dh:]], axis=0).astype(BF16)

    q_t = (q_ref[0] * (dh ** -0.5)).T
    q_heads = jnp.concatenate([q_t[hh * dh:(hh + 1) * dh] for hh in range(hpg)], axis=1)

    def scores(branch, qa_t, start, nk, bias):
        sc = _dot(branch[0][pl.ds(pl.multiple_of(start, qb), nk), :], qa_t)
        return sc if bias is None else sc + bias

    def update(branch, start, nk, sc):
        _, vt_ref, m_ref, acc_ref = branch
        m_old = m_ref[...]
        m_new = jnp.maximum(m_old, jnp.max(sc, axis=0, keepdims=True))
        alpha = jnp.exp(m_old - m_new)
        pe = jnp.exp(sc - m_new).astype(BF16)
        acc_ref[...] = alpha * acc_ref[...] + _dot(vt_ref[:, pl.ds(pl.multiple_of(start, qb), nk)], pe)
        m_ref[...] = m_new

    sel = (ks_ref, vs_ref, ms_ref, accs_ref)
    win = (kw_ref, vw_ref, mw_ref, accw_ref)
    for m_ref, acc_ref in ((ms_ref, accs_ref), (mw_ref, accw_ref)):
        m_ref[...] = jnp.full(m_ref.shape, NEG_INF, F32)
        acc_ref[...] = jnp.zeros(acc_ref.shape, F32)

    nwt = NSA_WINDOW // qb
    assert nwt in (2, 4)
    pad_rows = jnp.zeros((LANES - CONST_LANE0 - 8, cols), F32)
    qa_win = jnp.concatenate([q_heads, jnp.zeros((32, cols), F32), qc_ref[0], pad_rows], axis=0).astype(BF16)

    kvc = kvc_ref[0, 0]
    lane_k = lax.broadcasted_iota(jnp.int32, kvc.shape, 1)
    kc16 = jnp.where(lane_k < dh, kvc, 0.0).astype(BF16)
    s = _dot(kc16, qa_win) + cb_ref[0, 0]

    kk = lax.broadcasted_iota(jnp.int32, (qb, cols), 0)
    rr = lax.broadcasted_iota(jnp.int32, (qb, cols), 1) % qb
    w4_start = jnp.maximum(i - nwt, 0) * qb
    sc_w4 = scores(win, qa_win, w4_start, qb, jnp.where((rr < kk) & (i >= nwt), 0.0, NEG_INF))

    row2 = lax.broadcasted_iota(jnp.int32, (2 * qb, cols), 0)
    if nwt == 4:
        w32_start = jnp.maximum(i - 3, 0) * qb
        sc_w32 = scores(win, qa_win, w32_start, 2 * qb,
                        jnp.where(row2 < (i - 1) * qb - w32_start, 0.0, NEG_INF))
    near_start = jnp.maximum(i - 1, 0) * qb
    near_bias = near_ref[0, jnp.where(i == 0, 1, 0)]
    sc_wn = scores(win, qa_win, near_start, 2 * qb, near_bias)

    s = jnp.exp(s - jnp.max(s, axis=0, keepdims=True))
    p = s / jnp.sum(s, axis=0, keepdims=True)
    tq_lane = i * qb + lax.broadcasted_iota(jnp.int32, (1, cols), 1) % qb
    p16 = (p * (tq_lane >= NSA_CMP_LEN - 1).astype(F32)).astype(BF16)
    o_cmp = _dot(kvc.T.astype(BF16), p16)
    ovl = ovl_ref[...].astype(BF16)
    imp = _dot(ovl, p16[:, 0:qb])
    for hh in range(1, hpg):
        imp = imp + _dot(ovl, p16[:, hh * qb:(hh + 1) * qb])

    update(win, w4_start, qb, sc_w4)

    blk = lax.broadcasted_iota(jnp.int32, (32, qb), 0)
    tq = i * qb + lax.broadcasted_iota(jnp.int32, (32, qb), 1)
    cur = tq // NSA_SLC_LEN
    forced = (blk == 0) | (blk == cur) | (blk == cur - 1)
    val = jnp.where(forced, jnp.inf, jnp.where(blk * NSA_SLC_LEN <= tq, imp, -jnp.inf))
    cnt = jnp.zeros((32, qb), jnp.int32)
    for s2 in range(nblk):
        other = val[s2:s2 + 1, :]
        cnt = cnt + ((other > val) | ((other == val) & (s2 < blk))).astype(jnp.int32)
    feat = jnp.where((cnt < min(NSA_TOP_K, nblk)) & (blk < nblk), 0.0, NEG_INF)
    qa = jnp.concatenate([q_heads, jnp.concatenate([feat] * hpg, axis=1), qc_ref[0], pad_rows],
                         axis=0).astype(BF16)

    if nwt == 4:
        update(win, w32_start, 2 * qb, sc_w32)
    sc_sn = scores(sel, qa, near_start, 2 * qb, near_bias)
    update(win, near_start, 2 * qb, sc_wn)
    update(sel, near_start, 2 * qb, sc_sn)

    n_far = jnp.maximum(i - 1, 0)
    n_pairs = (n_far + 1) // 2
    pair_start = lambda p: jnp.maximum(2 * p - n_far % 2, 0) * qb
    last_pair = jnp.maximum(n_pairs - 1, 0)
    sc_ref[...] = scores(sel, qa, 0, 2 * qb, jnp.where(row2 < (2 - n_far % 2) * qb, 0.0, NEG_INF))

    def sel_body(k, carry):
        sc_odd = scores(sel, qa, pair_start(2 * k + 1), 2 * qb, None)
        update(sel, pair_start(2 * k), 2 * qb, sc_ref[...])
        sc_ref[...] = scores(sel, qa, pair_start(jnp.minimum(2 * k + 2, last_pair)), 2 * qb, None)
        update(sel, pair_start(2 * k + 1), 2 * qb, sc_odd)
        return carry

    lax.fori_loop(0, n_pairs // 2, sel_body, 0)

    @pl.when(n_pairs % 2 == 1)
    def _():
        update(sel, pair_start(n_pairs - 1), 2 * qb, sc_ref[...])

    def finish(acc_ref):
        acc = acc_ref[...]
        return acc[VT_PAD:] / acc[0:1]

    o_slc = finish(accs_ref)
    o_win = finish(accw_ref)

    gt_ref[...] = jax.nn.sigmoid(gate_ref[0]).T
    outs = []
    for hh in range(hpg):
        sl = slice(hh * qb, (hh + 1) * qb)
        base = (g * hpg + hh) * 3
        gate = [gt_ref[pl.ds(base + br, 1), :] for br in range(3)]
        outs.append(gate[0] * o_cmp[dh:, sl] + gate[1] * o_slc[:, sl] + gate[2] * o_win[:, sl])
    out = jnp.concatenate(outs, axis=0).T
    o_ref[0] = (out * _silu(z_ref[0])).astype(o_ref.dtype)


def _nsa_attn(proj, kv_cmp, near, cmp_bias, qconst, ovl):
    bsz, t, _ = proj.shape
    qb, hpg = NSA_QTILE, NSA_HPG
    gw = hpg * NSA_HEAD_DIM
    nb = t // NSA_CMP_STRIDE
    cols = hpg * qb
    return pl.pallas_call(
        _nsa_attn_kernel,
        grid=(bsz, NSA_GROUPS, t // qb),
        in_specs=[
            pl.BlockSpec((1, qb, gw), lambda b, g, i: (b, i, NSA_COL_Q // gw + g)),
            pl.BlockSpec((1, t, LANES), lambda b, g, i: (b, 0, NSA_COL_SEL // LANES + g)),
            pl.BlockSpec((1, t, LANES), lambda b, g, i: (b, 0, NSA_COL_WIN // LANES + g)),
            pl.BlockSpec((1, 1, nb, LANES), lambda b, g, i: (b, g, 0, 0)),
            pl.BlockSpec((1, qb, LANES), lambda b, g, i: (b, i, NSA_COL_GATE // LANES)),
            pl.BlockSpec((1, qb, gw), lambda b, g, i: (b, i, NSA_COL_Z // gw + g)),
            pl.BlockSpec((1, 1, nb, cols), lambda b, g, i: (i, g, 0, 0)),
            pl.BlockSpec((1, 2, 2 * qb, cols), lambda b, g, i: (g, 0, 0, 0)),
            pl.BlockSpec((1, 8, cols), lambda b, g, i: (g, 0, 0)),
            pl.BlockSpec((32, nb), lambda b, g, i: (0, 0)),
        ],
        out_specs=pl.BlockSpec((1, qb, gw), lambda b, g, i: (b, i, g)),
        out_shape=jax.ShapeDtypeStruct((bsz, t, NSA_Q_W), BF16),
        scratch_shapes=[
            pltpu.VMEM((t, LANES), BF16), pltpu.VMEM((VT_PAD + NSA_HEAD_DIM, t), BF16),
            pltpu.VMEM((t, LANES), BF16), pltpu.VMEM((VT_PAD + NSA_HEAD_DIM, t), BF16),
            pltpu.VMEM((LANES, qb), F32),
            pltpu.VMEM((1, cols), F32), pltpu.VMEM((VT_PAD + NSA_HEAD_DIM, cols), F32),
            pltpu.VMEM((1, cols), F32), pltpu.VMEM((VT_PAD + NSA_HEAD_DIM, cols), F32),
            pltpu.VMEM((2 * qb, cols), F32),
        ],
        compiler_params=pltpu.CompilerParams(
            dimension_semantics=("parallel", "parallel", "arbitrary"), vmem_limit_bytes=VMEM_LIMIT),
        name="nsa_attention",
    )(proj, proj, proj, kv_cmp, proj, proj, cmp_bias, near, qconst, ovl)


def _nsa_layer(x, mod, norm_g, w_in, cmp_pos, cmp_w1, cmp_w2, rel_bias, w_out, final_g):
    t = x.shape[1]
    assert t // NSA_SLC_LEN <= 32 and NSA_COL_Z % (NSA_HPG * NSA_HEAD_DIM) == 0
    perm = _nsa_column_perm()
    cuts = [0] + [j for j in range(1, len(perm)) if perm[j] != perm[j - 1] + (perm[j - 1] >= 0)] + [len(perm)]
    runs = [(int(perm[a]), b - a) for a, b in zip(cuts[:-1], cuts[1:])]
    w16 = w_in.astype(BF16)
    w_in_p = jnp.concatenate([w16[:, s:s + n] if s >= 0 else jnp.zeros((w_in.shape[0], n), BF16)
                              for s, n in runs], axis=1)
    proj = _inproj(x, norm_g, mod, w_in_p, tm=256, tn=NSA_PROJ_W)
    kv_cmp = _nsa_compress(proj, cmp_pos, cmp_w1, cmp_w2)
    near, cmp_bias, qconst = _nsa_tables(rel_bias, t)
    o = _nsa_attn(proj, kv_cmp, near, cmp_bias, qconst, jnp.asarray(_overlap_t(t)))
    return _outproj(o, w_out.astype(BF16), x, mod, final_g, tm=512)


def kernel(x, c, ada_w, ada_b, norm_g, gdn_w_in, gdn_conv_w, gdn_a_log, gdn_dt_bias, gdn_norm_w, gdn_w_out,
           nsa_w_in, nsa_cmp_pos, nsa_cmp_w1, nsa_cmp_w2, nsa_w_out, rel_bias, final_g):
    bsz, t, d = x.shape
    mod = _modulation(c, ada_w, ada_b).reshape(ada_w.shape[0], bsz, 3, d)
    x = _gdn_layer(x, mod[0], norm_g[0], gdn_w_in[0], gdn_conv_w[0], gdn_a_log[0], gdn_dt_bias[0],
                   gdn_norm_w[0], gdn_w_out[0])
    return _nsa_layer(x, mod[1], norm_g[1], nsa_w_in[0], nsa_cmp_pos[0], nsa_cmp_w1[0], nsa_cmp_w2[0],
                      rel_bias, nsa_w_out[0], final_g)
```

```python
import functools
import math

import numpy as np
import jax
import jax.numpy as jnp
from jax import lax
from jax.experimental import pallas as pl
from jax.experimental.pallas import tpu as pltpu

F32 = jnp.float32
BF16 = jnp.bfloat16
HIGHEST = lax.Precision.HIGHEST

EPS = 1e-6
NEG_INF = -1e30
LANES = 128
VMEM_LIMIT = 56 * 1024 * 1024

GDN_QK_HEADS = 8
GDN_V_HEADS = 16
GDN_HEAD_DIM = 128
GDN_CONV = 4
GDN_CHUNK = 64
GDN_QK_W = GDN_QK_HEADS * GDN_HEAD_DIM
GDN_V_W = GDN_V_HEADS * GDN_HEAD_DIM
GDN_CONV_W = 2 * GDN_QK_W + GDN_V_W
GDN_TILE = 128
GDN_HEADS_PER_STEP = 8


def _silu(x):
    return x * jax.nn.sigmoid(x)


def _dot(a, b, **kw):
    return jnp.dot(a, b, preferred_element_type=F32, **kw)


def _dot_nt(a, b, **kw):
    return lax.dot_general(a, b, (((1,), (1,)), ((), ())), preferred_element_type=F32, **kw)


def _mod_kernel(c_ref, w_ref, b_ref, o_ref):
    cond = _silu(c_ref[...])
    o_ref[0] = _dot(cond, w_ref[0], precision=HIGHEST) + b_ref[0]


def _modulation(c, ada_w, ada_b):
    depth, d, d3 = ada_w.shape
    bsz = c.shape[0]
    return pl.pallas_call(
        _mod_kernel,
        grid=(depth, d3 // d),
        in_specs=[
            pl.BlockSpec((bsz, d), lambda i, j: (0, 0)),
            pl.BlockSpec((1, d, d), lambda i, j: (i, 0, j)),
            pl.BlockSpec((1, 1, d), lambda i, j: (i, 0, j)),
        ],
        out_specs=pl.BlockSpec((1, bsz, d), lambda i, j: (i, 0, j)),
        out_shape=jax.ShapeDtypeStruct((depth, bsz, d3), F32),
        name="adaln_mod",
    )(c, ada_w, ada_b.reshape(depth, 1, d3))


def _inproj_kernel(x_ref, g_ref, mod_ref, w_ref, o_ref, *, tn):
    x = x_ref[0]
    m = mod_ref[0]
    y = x * lax.rsqrt(jnp.mean(x * x, axis=-1, keepdims=True) + EPS) * g_ref[...]
    h = (y * (1.0 + m[1:2]) + m[0:1]).astype(BF16)
    for j in range(w_ref.shape[1] // tn):
        o_ref[0, :, j * tn:(j + 1) * tn] = _dot(h, w_ref[:, j * tn:(j + 1) * tn])


def _inproj(x, g, mod, w, *, tm, tn):
    bsz, t, d = x.shape
    n = w.shape[1]
    assert t % tm == 0 and n % tn == 0
    return pl.pallas_call(
        functools.partial(_inproj_kernel, tn=tn),
        grid=(bsz, t // tm),
        in_specs=[
            pl.BlockSpec((1, tm, d), lambda b, i: (b, i, 0)),
            pl.BlockSpec((1, d), lambda b, i: (0, 0)),
            pl.BlockSpec((1, 3, d), lambda b, i: (b, 0, 0)),
            pl.BlockSpec((d, n), lambda b, i: (0, 0), pipeline_mode=pl.Buffered(1)),
        ],
        out_specs=pl.BlockSpec((1, tm, n), lambda b, i: (b, i, 0)),
        out_shape=jax.ShapeDtypeStruct((bsz, t, n), F32),
        compiler_params=pltpu.CompilerParams(
            dimension_semantics=("parallel", "parallel"), vmem_limit_bytes=VMEM_LIMIT),
        name="norm_mod_inproj",
    )(x, g.reshape(1, d), mod, w)


def _gdn_inproj_kernel(x_ref, g_ref, mod_ref, w_ref, cw_ref, alog_ref, dtb_ref, o_ref, col_ref, row_ref,
                        ext_ref, tail_ref, *, tn):
    dh = GDN_HEAD_DIM
    tm = x_ref.shape[1]
    x = x_ref[0]
    m = mod_ref[0]
    y = x * lax.rsqrt(jnp.mean(x * x, axis=-1, keepdims=True) + EPS) * g_ref[...]
    h = (y * (1.0 + m[1:2]) + m[0:1]).astype(BF16)

    @pl.when(pl.program_id(1) == 0)
    def _():
        tail_ref[...] = jnp.zeros_like(tail_ref)

    def conv_tile(j):
        cols = slice(j * tn, (j + 1) * tn)
        ext_ref[0:8, :] = tail_ref[:, cols]
        ext_ref[8:8 + tm, :] = o_ref[0, :, cols]
        tail_ref[:, cols] = ext_ref[tm:tm + 8, :]
        w = cw_ref[:, cols]
        ext = ext_ref[...]
        acc = w[0:1] * ext
        for tap in range(1, GDN_CONV):
            acc = pltpu.roll(acc, 1, axis=0) + w[tap:tap + 1] * ext
        act = _silu(acc[8:8 + tm])
        if j * tn < 2 * GDN_QK_W:
            scale = dh ** -0.5 if j * tn < GDN_QK_W else 1.0
            heads = [act[:, c * dh:(c + 1) * dh] for c in range(tn // dh)]
            heads = [hd * (lax.rsqrt(jnp.sum(hd * hd, axis=-1, keepdims=True) + EPS) * scale) for hd in heads]
            act = jnp.concatenate(heads, axis=1)
        o_ref[0, :, cols] = act

    n = w_ref.shape[1]
    n_conv = GDN_CONV_W // tn
    bounds = [(j * tn, min((j + 1) * tn, n)) for j in range(-(-n // tn))]
    for j, (lo, hi) in enumerate(bounds):
        o_ref[0, :, lo:hi] = _dot(h, w_ref[:, lo:hi])
        if 1 <= j <= n_conv:
            conv_tile(j - 1)
    _gdn_gate_scalars(o_ref[0, :, n - LANES:n], alog_ref[...], dtb_ref[...], col_ref, row_ref)


def _gdn_inproj(x, g, mod, w, conv_w, a_log, dt_bias, *, tm, tn):
    bsz, t, d = x.shape
    n = w.shape[1]
    assert t % tm == 0 and GDN_CONV_W % tn == 0 and GDN_QK_W % tn == 0 and n > GDN_CONV_W
    pad = lambda u: jnp.zeros((1, LANES), F32).at[0, GDN_V_HEADS:2 * GDN_V_HEADS].set(u)
    ncb = tm // GDN_CHUNK
    return pl.pallas_call(
        functools.partial(_gdn_inproj_kernel, tn=tn),
        grid=(bsz, t // tm),
        in_specs=[
            pl.BlockSpec((1, tm, d), lambda b, i: (b, i, 0)),
            pl.BlockSpec((1, d), lambda b, i: (0, 0)),
            pl.BlockSpec((1, 3, d), lambda b, i: (b, 0, 0)),
            pl.BlockSpec((d, n), lambda b, i: (0, 0), pipeline_mode=pl.Buffered(1)),
            pl.BlockSpec((GDN_CONV, GDN_CONV_W), lambda b, i: (0, 0)),
            pl.BlockSpec((1, LANES), lambda b, i: (0, 0)),
            pl.BlockSpec((1, LANES), lambda b, i: (0, 0)),
        ],
        out_specs=[
            pl.BlockSpec((1, tm, n), lambda b, i: (b, i, 0)),
            pl.BlockSpec((1, tm, LANES), lambda b, i: (b, i, 0)),
            pl.BlockSpec((1, ncb, GDN_V_HEADS // 2, 2 * GDN_CHUNK), lambda b, i: (b, i, 0, 0)),
        ],
        out_shape=[
            jax.ShapeDtypeStruct((bsz, t, n), F32),
            jax.ShapeDtypeStruct((bsz, t, LANES), F32),
            jax.ShapeDtypeStruct((bsz, t // GDN_CHUNK, GDN_V_HEADS // 2, 2 * GDN_CHUNK), F32),
        ],
        scratch_shapes=[pltpu.VMEM((tm + 8, tn), F32), pltpu.VMEM((8, GDN_CONV_W), F32)],
        compiler_params=pltpu.CompilerParams(
            dimension_semantics=("parallel", "arbitrary"), vmem_limit_bytes=VMEM_LIMIT),
        name="gdn_norm_inproj_conv",
    )(x, g.reshape(1, d), mod, w, conv_w, pad(a_log), pad(dt_bias))


def _outproj_kernel(o_ref, w_ref, x_ref, mod_ref, *rest, final_norm):
    y = _dot(o_ref[0], w_ref[...])
    x = x_ref[0] + mod_ref[0][2:3] * y
    if final_norm:
        fg_ref, out_ref = rest
        x = x * lax.rsqrt(jnp.mean(x * x, axis=-1, keepdims=True) + EPS) * fg_ref[...]
    else:
        (out_ref,) = rest
    out_ref[0] = x


def _outproj(o, w, x, mod, final_g=None, *, tm):
    bsz, t, d = x.shape
    k = o.shape[-1]
    in_specs = [
        pl.BlockSpec((1, tm, k), lambda b, i: (b, i, 0)),
        pl.BlockSpec((k, d), lambda b, i: (0, 0), pipeline_mode=pl.Buffered(1)),
        pl.BlockSpec((1, tm, d), lambda b, i: (b, i, 0)),
        pl.BlockSpec((1, 3, d), lambda b, i: (b, 0, 0)),
    ]
    args = [o, w, x, mod]
    if final_g is not None:
        in_specs.append(pl.BlockSpec((1, d), lambda b, i: (0, 0)))
        args.append(final_g.reshape(1, d))
    return pl.pallas_call(
        functools.partial(_outproj_kernel, final_norm=final_g is not None),
        grid=(bsz, t // tm),
        in_specs=in_specs,
        out_specs=pl.BlockSpec((1, tm, d), lambda b, i: (b, i, 0)),
        out_shape=jax.ShapeDtypeStruct((bsz, t, d), F32),
        compiler_params=pltpu.CompilerParams(
            dimension_semantics=("parallel", "parallel"), vmem_limit_bytes=VMEM_LIMIT),
        name="outproj_residual",
    )(*args)


def _gdn_gate_scalars(ba, alog, dtb, col_ref, row_ref):
    cs, nh = GDN_CHUNK, GDN_V_HEADS
    lane = lax.broadcasted_iota(jnp.int32, ba.shape, 1)
    g = -jnp.exp(alog) * jax.nn.softplus(ba + dtb)
    vals = jnp.where(lane < nh, jax.nn.sigmoid(ba), g)
    r = lax.broadcasted_iota(jnp.int32, (cs, cs), 0)
    c = lax.broadcasted_iota(jnp.int32, (cs, cs), 1)
    tri = (r >= c).astype(F32)
    is_beta = lax.broadcasted_iota(jnp.int32, (cs, LANES), 1) < nh
    pr = lax.broadcasted_iota(jnp.int32, (nh // 2, LANES), 0)
    pc = lax.broadcasted_iota(jnp.int32, (nh // 2, LANES), 1)
    pick_even = (pc == nh + 2 * pr).astype(F32)
    for n in range(ba.shape[0] // cs):
        v = vals[n * cs:(n + 1) * cs]
        cum = _dot(tri, v, precision=HIGHEST)
        out = jnp.where(is_beta, v, cum)
        col_ref[0, n * cs:(n + 1) * cs, :] = out
        both = jnp.concatenate([out, pltpu.roll(out, LANES - 1, axis=1)], axis=0)
        row_ref[0, n] = _dot_nt(pick_even, both, precision=HIGHEST)


TRI_BASE = 8
PACK = 4


def _block_rows(p):
    n = p.shape[0]
    blk = lax.broadcasted_iota(jnp.int32, p.shape, 1) // n
    return jnp.concatenate([jnp.where(blk == j, p, 0.0) for j in range(PACK)], axis=0).astype(BF16)


def _tri_inverse_packed(mats):
    n = mats[0].shape[0]
    r = lax.broadcasted_iota(jnp.int32, mats[0].shape, 0)
    c = lax.broadcasted_iota(jnp.int32, mats[0].shape, 1) % n
    same = lambda s: (r // s) == (c // s)
    mm = lambda a, b: _dot(a.astype(BF16), _block_rows(b))
    diag = [jnp.where(same(TRI_BASE), a, 0.0) for a in mats]
    inv = [jnp.where(r == c, 1.0, 0.0) - d for d in diag]
    pw = diag
    k = 1
    while 2 * k < TRI_BASE:
        pw = [mm(m, m) for m in pw]
        inv = [p + mm(p, m) for p, m in zip(inv, pw)]
        k *= 2
        yield None
    s = TRI_BASE
    while s < n:
        sub = same(2 * s) & jnp.logical_not(same(s))
        left = [mm(p, jnp.where(sub, a, 0.0)) for p, a in zip(inv, mats)]
        inv = [p - mm(l, p) for p, l in zip(inv, left)]
        s *= 2
        yield None
    yield inv


def _gdn_chunk_kernel(q_ref, k_ref, v_ref, z_ref, col_ref, row_ref, nw_ref,
                      o_ref, s_ref, hu_ref, hwq_ref, hqk_ref, hkd_ref, hgl_ref):
    cs, dh, nh = GDN_CHUNK, GDN_HEAD_DIM, GDN_V_HEADS
    tt = q_ref.shape[1]
    hps = q_ref.shape[2] // dh
    ncb = tt // cs
    assert ncb % 2 == 0 and PACK == 4
    hg = pl.program_id(1)
    ti = pl.program_id(2)
    wslot = ti % 2
    rslot = 1 - wslot
    jidx = lambda hl, n, e: (hl * ncb + n) * 2 + e

    @pl.when(ti == 0)
    def _():
        s_ref[...] = jnp.zeros_like(s_ref)
        for h_ref in (hu_ref, hwq_ref, hqk_ref, hkd_ref, hgl_ref):
            h_ref[1] = jnp.zeros(h_ref.shape[1:], h_ref.dtype)

    chunks = [slice(n * cs, (n + 1) * cs) for n in range(ncb)]

    def prepare():
        q = [q_ref[0, :, hl * dh:(hl + 1) * dh] for hl in range(hps)]
        k = [k_ref[0, :, hl * dh:(hl + 1) * dh] for hl in range(hps)]
        v = [[v_ref[0, :, (2 * hl + e) * dh:(2 * hl + e + 1) * dh] for e in range(2)] for hl in range(hps)]
        lane = lax.broadcasted_iota(jnp.int32, (tt, LANES), 1)
        colv = col_ref[0]
        column = lambda idx: jnp.sum(jnp.where(lane == idx, colv, 0.0), axis=1, keepdims=True)
        hq = [hg * hps + hl for hl in range(hps)]
        beta = [[column(2 * h + e) for e in range(2)] for h in hq]
        gc = [[column(nh + 2 * h + e) for e in range(2)] for h in hq]
        egc = [[jnp.exp(x) for x in pair] for pair in gc]
        yield

        pr = lax.broadcasted_iota(jnp.int32, (cs, PACK * cs), 0)
        pb = lax.broadcasted_iota(jnp.int32, (cs, PACK * cs), 1) // cs
        pc = lax.broadcasted_iota(jnp.int32, (cs, PACK * cs), 1) % cs
        groups = [(hl, gi) for hl in range(hps) for gi in range(ncb // 2)]
        a_mats, g_rows = [], {}
        for hl, gi in groups:
            k16, q16 = k[hl].astype(BF16), q[hl].astype(BF16)
            pair = (chunks[2 * gi], chunks[2 * gi + 1])
            kdup = [jnp.concatenate([k16[sl], k16[sl]], axis=0) for sl in pair]
            kk = jnp.concatenate([_dot_nt(k16[sl], kd) for sl, kd in zip(pair, kdup)], axis=1)
            qk = jnp.concatenate([_dot_nt(q16[sl], kd) for sl, kd in zip(pair, kdup)], axis=1)
            pick = lambda cols: jnp.where(pb == 0, cols[0][pair[0]], jnp.where(
                pb == 1, cols[1][pair[0]], jnp.where(pb == 2, cols[0][pair[1]], cols[1][pair[1]])))
            g_row = jnp.concatenate([row_ref[0, 2 * gi + j, pl.ds(hq[hl], 1), :] for j in range(2)], axis=1)
            decay = jnp.exp(jnp.where(pr >= pc, pick(gc[hl]) - g_row, -jnp.inf))
            a_mats.append(jnp.where(pr > pc, pick(beta[hl]) * kk * decay, 0.0))
            hqk_ref[wslot, hl * (ncb // 2) + gi] = (qk * decay).astype(BF16)
            g_rows[hl, gi] = g_row
            yield
        t_mats = None
        for t_mats in _tri_inverse_packed(a_mats):
            yield

        for (hl, gi), t_mat in zip(groups, t_mats):
            rhs = []
            for n in (2 * gi, 2 * gi + 1):
                sl = chunks[n]
                for e in range(2):
                    j = 2 * (n % 2) + e
                    g_last = g_rows[hl, gi][:, j * cs + cs - 1:(j + 1) * cs]
                    hgl_ref[wslot, jidx(hl, n, e)] = jnp.broadcast_to(jnp.exp(g_last), (1, LANES))
                    hkd_ref[wslot, jidx(hl, n, e)] = (k[hl][sl] * jnp.exp(g_last - gc[hl][e][sl])).T.astype(BF16)
                    kbeta = k[hl][sl] * beta[hl][e][sl]
                    rhs.append(jnp.concatenate([v[hl][e][sl] * beta[hl][e][sl], kbeta * egc[hl][e][sl]], axis=1))
            out = _dot(_block_rows(t_mat), jnp.concatenate(rhs, axis=0).astype(BF16))
            for n in (2 * gi, 2 * gi + 1):
                for e in range(2):
                    uw = out[(2 * (n % 2) + e) * cs:(2 * (n % 2) + e + 1) * cs]
                    hu_ref[wslot, jidx(hl, n, e)] = uw[:, :dh]
                    q_dec = q[hl][chunks[n]] * egc[hl][e][chunks[n]]
                    hwq_ref[wslot, jidx(hl, n, e)] = jnp.concatenate([uw[:, dh:], q_dec], axis=0).astype(BF16)
            yield

    def recurrence():
        nw = nw_ref[...]
        heads = [(hl, e) for hl in range(hps) for e in range(2)]
        state = {he: s_ref[2 * he[0] + he[1]] for he in heads}
        for n, sl in enumerate(chunks):
            ws = {}
            for hl, e in heads:
                ws[hl, e] = _dot(hwq_ref[rslot, jidx(hl, n, e)], state[hl, e].astype(BF16))
            yield
            for hl, e in heads:
                j = jidx(hl, n, e)
                v16 = (hu_ref[rslot, j] - ws[hl, e][:cs]).astype(BF16)
                lb = 2 * (n % 2) + e
                qkd = hqk_ref[rslot, hl * (ncb // 2) + n // 2][:, lb * cs:(lb + 1) * cs]
                o = ws[hl, e][cs:] + _dot(qkd, v16)
                state[hl, e] = state[hl, e] * hgl_ref[rslot, j] + _dot(hkd_ref[rslot, j], v16)
                o = o * lax.rsqrt(jnp.mean(o * o, axis=-1, keepdims=True) + EPS) * nw
                lanes = slice((2 * hl + e) * dh, (2 * hl + e + 1) * dh)
                o_ref[0, sl, lanes] = (o * _silu(z_ref[0, sl, lanes])).astype(o_ref.dtype)
            yield
        for (hl, e), st in state.items():
            s_ref[2 * hl + e] = st

    prep, rec = prepare(), recurrence()
    n_prep = 1 + hps * (ncb // 2) * 2 + 6
    per_stage = -(-n_prep // (2 * ncb))
    prep_live = rec_live = True
    while prep_live or rec_live:
        for _ in range(per_stage):
            if prep_live:
                prep_live = next(prep, "done") != "done"
        if rec_live:
            rec_live = next(rec, "done") != "done"


def _gdn_chunk(proj, col, row, norm_w, *, tt=GDN_TILE, hps=GDN_HEADS_PER_STEP):
    bsz, t, _ = proj.shape
    dh = GDN_HEAD_DIM
    qw, vw = hps * dh, 2 * hps * dh
    k_blk0 = GDN_QK_W // qw
    v_blk0 = 2 * GDN_QK_W // vw
    z_blk0 = GDN_CONV_W // vw
    ncb = tt // GDN_CHUNK
    nt = t // tt
    nj = hps * ncb * 2
    cur = lambda i: jnp.minimum(i, nt - 1)
    prev = lambda i: jnp.maximum(i - 1, 0)
    return pl.pallas_call(
        _gdn_chunk_kernel,
        grid=(bsz, GDN_QK_HEADS // hps, nt + 1),
        in_specs=[
            pl.BlockSpec((1, tt, qw), lambda b, h, i: (b, cur(i), h)),
            pl.BlockSpec((1, tt, qw), lambda b, h, i: (b, cur(i), k_blk0 + h)),
            pl.BlockSpec((1, tt, vw), lambda b, h, i: (b, cur(i), v_blk0 + h)),
            pl.BlockSpec((1, tt, vw), lambda b, h, i: (b, prev(i), z_blk0 + h)),
            pl.BlockSpec((1, tt, LANES), lambda b, h, i: (b, cur(i), 0)),
            pl.BlockSpec((1, ncb, GDN_V_HEADS // 2, 2 * GDN_CHUNK), lambda b, h, i: (b, cur(i), 0, 0)),
            pl.BlockSpec((1, dh), lambda b, h, i: (0, 0)),
        ],
        out_specs=pl.BlockSpec((1, tt, vw), lambda b, h, i: (b, prev(i), h)),
        out_shape=jax.ShapeDtypeStruct((bsz, t, GDN_V_W), BF16),
        scratch_shapes=[
            pltpu.VMEM((2 * hps, dh, dh), F32),
            pltpu.VMEM((2, nj, GDN_CHUNK, dh), F32),
            pltpu.VMEM((2, nj, 2 * GDN_CHUNK, dh), BF16),
            pltpu.VMEM((2, nj // PACK, GDN_CHUNK, PACK * GDN_CHUNK), BF16),
            pltpu.VMEM((2, nj, dh, GDN_CHUNK), BF16),
            pltpu.VMEM((2, nj, 1, LANES), F32),
        ],
        compiler_params=pltpu.CompilerParams(
            dimension_semantics=("parallel", "parallel", "arbitrary"), vmem_limit_bytes=VMEM_LIMIT),
        name="gdn_chunk_scan",
    )(proj, proj, proj, proj, col, row, norm_w.reshape(1, dh))


def _gdn_layer(x, mod, norm_g, w_in, conv_w, a_log, dt_bias, norm_w, w_out):
    n_in = w_in.shape[1]
    n_pad = -(-n_in // (7 * LANES)) * (7 * LANES)
    w_in_p = jnp.pad(w_in, ((0, 0), (0, n_pad - n_in))).astype(BF16)
    proj, col, row = _gdn_inproj(x, norm_g, mod, w_in_p, conv_w, a_log, dt_bias, tm=256, tn=4 * LANES)
    o = _gdn_chunk(proj, col, row, norm_w)
    return _outproj(o, w_out.astype(BF16), x, mod, tm=512)


NSA_HEADS = 16
NSA_GROUPS = 4
NSA_HPG = NSA_HEADS // NSA_GROUPS
NSA_HEAD_DIM = 64
NSA_CMP_LEN = 32
NSA_CMP_STRIDE = 16
NSA_SLC_LEN = 64
NSA_TOP_K = 8
NSA_WINDOW = 512
NSA_QTILE = 256
NSA_Q_W = NSA_HEADS * NSA_HEAD_DIM
NSA_KV_W = NSA_GROUPS * NSA_HEAD_DIM
REL_BUCKETS = 32
REL_MAX_DIST = 128
FEAT_LANE0 = NSA_HEAD_DIM
CONST_LANE0 = FEAT_LANE0 + 32
VT_PAD = 16
NSA_COL_Q = 0
NSA_COL_CMP = NSA_Q_W
NSA_COL_SEL = NSA_COL_CMP + 2 * NSA_KV_W
NSA_COL_WIN = NSA_COL_SEL + 2 * NSA_KV_W
NSA_COL_Z = NSA_COL_WIN + 2 * NSA_KV_W
NSA_COL_GATE = NSA_COL_Z + NSA_Q_W
NSA_PROJ_W = NSA_COL_GATE + LANES


def _nsa_column_perm():
    g, dh = NSA_GROUPS, NSA_HEAD_DIM
    kv0 = NSA_Q_W
    cols = list(range(NSA_Q_W))
    cols += [kv0 + i for i in range(2 * NSA_KV_W)]
    for br in (1, 2):
        for gi in range(g):
            cols += [kv0 + (2 * br) * NSA_KV_W + gi * dh + d for d in range(dh)]
            cols += [kv0 + (2 * br + 1) * NSA_KV_W + gi * dh + d for d in range(dh)]
    gate0 = kv0 + 6 * NSA_KV_W
    cols += [gate0 + 3 * NSA_HEADS + i for i in range(NSA_Q_W)]
    cols += [gate0 + i for i in range(3 * NSA_HEADS)] + [-1] * (LANES - 3 * NSA_HEADS)
    assert len(cols) == NSA_PROJ_W
    return np.asarray(cols, np.int32)


def _rel_bucket_table(n):
    d = np.arange(n)
    max_exact = REL_BUCKETS // 2
    nf = np.maximum(d, 1).astype(np.float64)
    large = max_exact + (np.log(nf / max_exact) / math.log(REL_MAX_DIST / max_exact)
                         * (REL_BUCKETS - max_exact)).astype(np.int32)
    large = np.minimum(large, REL_BUCKETS - 1)
    return np.where(d < max_exact, d, large).astype(np.int32)


def _nsa_tables(rel_bias, t):
    qb = NSA_QTILE
    bucket = _rel_bucket_table(t)
    assert np.all(bucket[qb + 1:] == REL_BUCKETS - 1)
    bvec = rel_bias[bucket].T
    far = rel_bias[REL_BUCKETS - 1]
    far_hi = far.astype(BF16)
    far_lo = (far - far_hi.astype(F32)).astype(BF16)
    far_sum = far_hi.astype(F32) + far_lo.astype(F32)
    r = np.arange(qb)[:, None]
    c = np.arange(qb)[None, :]
    d0 = r - c

    def toeplitz(w):
        n = 2 * qb - 1
        ext = jnp.pad(w[:, ::-1], ((0, 0), (0, 1)))
        skew = jnp.tile(ext, (1, qb))[:, :qb * n].reshape(w.shape[0], qb, n)
        return skew[:, :, qb - 1:]

    rel = bvec[:, :2 * qb] - far_sum[:, None]
    t0 = toeplitz(jnp.concatenate([jnp.full((NSA_HEADS, qb - 1), NEG_INF, F32), rel[:, :qb]], axis=1))
    t1 = toeplitz(rel[:, 1:])
    g, hpg = NSA_GROUPS, NSA_HPG
    none = jnp.full_like(t0, NEG_INF)
    near = jnp.stack([t1, t0, t0, none], axis=1).reshape(g, hpg, 2, 2, qb, qb)
    near = near.transpose(0, 2, 3, 5, 1, 4).reshape(g, 2, 2 * qb, hpg * qb)
    nb = t // NSA_CMP_STRIDE
    per_tile = qb // NSA_CMP_STRIDE
    back = 9
    far_d = back * NSA_CMP_STRIDE - (NSA_CMP_LEN - 1)
    assert np.all(bucket[far_d:] == REL_BUCKETS - 1)
    width = (qb - 1 + far_d) // NSA_CMP_STRIDE + 1
    dm = r - NSA_CMP_STRIDE * np.arange(width)[None, :] + far_d
    band = jnp.where(dm >= 0, bvec[:, np.maximum(dm, 0)], NEG_INF)
    tiles = []
    for i in range(t // qb):
        j0 = per_tile * i - back
        lo, hi = max(j0, 0), min(j0 + width, nb)
        tiles.append(jnp.concatenate([
            jnp.broadcast_to(far[:, None, None], (NSA_HEADS, qb, lo)),
            band[:, :, lo - j0:hi - j0],
            jnp.full((NSA_HEADS, qb, nb - hi), NEG_INF, F32)], axis=2))
    cmp_bias = jnp.stack(tiles, axis=0).reshape(t // qb, g, hpg, qb, nb).transpose(0, 1, 4, 2, 3)
    cmp_bias = cmp_bias.reshape(t // qb, g, nb, hpg * qb)
    qconst = jnp.zeros((g, 8, hpg, qb), F32)
    qconst = qconst.at[:, 0].set(jnp.broadcast_to(far_hi.astype(F32).reshape(g, hpg, 1), (g, hpg, qb)))
    qconst = qconst.at[:, 1].set(jnp.broadcast_to(far_lo.astype(F32).reshape(g, hpg, 1), (g, hpg, qb)))
    return near, cmp_bias, qconst.reshape(g, 8, hpg * qb)


def _overlap_t(t):
    n_cmp = (t - NSA_CMP_LEN) // NSA_CMP_STRIDE + 1
    n_slc = t // NSA_SLC_LEN
    c_start = np.arange(n_cmp)[:, None] * NSA_CMP_STRIDE
    s_start = np.arange(n_slc)[None, :] * NSA_SLC_LEN
    ov = np.clip(np.minimum(c_start + NSA_CMP_LEN, s_start + NSA_SLC_LEN) - np.maximum(c_start, s_start), 0, None)
    ov = ov.astype(np.float32) / NSA_CMP_LEN
    out = np.zeros((32, t // NSA_CMP_STRIDE), np.float32)
    out[:n_slc, :n_cmp] = ov.T
    return out


def _nsa_compress_kernel(x_ref, pos_ref, w1_ref, w2_ref, o_ref, xs_ref):
    t = x_ref.shape[1]
    nb = t // NSA_CMP_STRIDE
    nlt = xs_ref.shape[0]
    for c in range(nlt):
        xs_ref[c, 0:t, :] = x_ref[0, :, c * LANES:(c + 1) * LANES]
        xs_ref[c, t:t + NSA_CMP_STRIDE, :] = jnp.zeros((NSA_CMP_STRIDE, LANES), F32)
    acc = jnp.zeros((nb, w1_ref.shape[2]), F32)
    for l in range(NSA_CMP_LEN):
        xl = jnp.concatenate([xs_ref[c, pl.ds(l, nb, stride=NSA_CMP_STRIDE), :] for c in range(nlt)], axis=1)
        xl = xl + pos_ref[l:l + 1, :]
        acc = acc + _dot(xl.astype(BF16), w1_ref[l])
    hid = _silu(acc).astype(BF16)
    res = _dot(hid, w2_ref[...])
    for g in range(NSA_GROUPS):
        o_ref[0, g] = res[:, g * LANES:(g + 1) * LANES]


def _nsa_compress(proj, cmp_pos, cmp_w1, cmp_w2):
    bsz, t, _ = proj.shape
    g, dh = NSA_GROUPS, NSA_HEAD_DIM
    nb = t // NSA_CMP_STRIDE
    w = 2 * NSA_KV_W
    w1 = cmp_w1.reshape(2, NSA_CMP_LEN, dh, dh).astype(BF16)
    w2 = cmp_w2.astype(BF16)
    place = lambda blk, c0: jnp.pad(blk, [(0, 0)] * (blk.ndim - 1) + [(c0, w - dh - c0)])
    w1c = jnp.concatenate([place(w1[i], (gi * 2 + i) * dh) for i in range(2) for gi in range(g)], axis=1)
    w2c = jnp.concatenate([place(w2[i], (gi * 2 + i) * dh) for gi in range(g) for i in range(2)], axis=0)
    pos = jnp.broadcast_to(cmp_pos[:, :, None, :], (2, NSA_CMP_LEN, g, dh)).transpose(1, 0, 2, 3).reshape(NSA_CMP_LEN, w)
    return pl.pallas_call(
        _nsa_compress_kernel,
        grid=(bsz,),
        in_specs=[
            pl.BlockSpec((1, t, w), lambda b: (b, 0, NSA_COL_CMP // w)),
            pl.BlockSpec((NSA_CMP_LEN, w), lambda b: (0, 0)),
            pl.BlockSpec((NSA_CMP_LEN, w, w), lambda b: (0, 0, 0), pipeline_mode=pl.Buffered(1)),
            pl.BlockSpec((w, w), lambda b: (0, 0)),
        ],
        out_specs=pl.BlockSpec((1, g, nb, LANES), lambda b: (b, 0, 0, 0)),
        out_shape=jax.ShapeDtypeStruct((bsz, g, nb, LANES), F32),
        scratch_shapes=[pltpu.VMEM((w // LANES, t + NSA_CMP_STRIDE, LANES), F32)],
        compiler_params=pltpu.CompilerParams(dimension_semantics=("parallel",), vmem_limit_bytes=VMEM_LIMIT),
        name="nsa_compress",
    )(proj, pos, w1c, w2c)


def _nsa_attn_kernel(q_ref, kvs_ref, kvw_ref, kvc_ref, gate_ref, z_ref, cb_ref, near_ref, qc_ref, ovl_ref,
                     o_ref, ks_ref, vs_ref, kw_ref, vw_ref, gt_ref, ms_ref, accs_ref, mw_ref, accw_ref, sc_ref):
    qb, dh, hpg = NSA_QTILE, NSA_HEAD_DIM, NSA_HPG
    t = kvs_ref.shape[1]
    nblk = t // NSA_SLC_LEN
    cols = hpg * qb
    g = pl.program_id(1)
    i = pl.program_id(2)

    @pl.when(i == 0)
    def _():
        tok = lax.broadcasted_iota(jnp.int32, (t, LANES), 0)
        ln = lax.broadcasted_iota(jnp.int32, (t, LANES), 1)
        const = jnp.where((ln == CONST_LANE0) | (ln == CONST_LANE0 + 1), 1.0, 0.0)
        onehot = jnp.where(ln - FEAT_LANE0 == tok // NSA_SLC_LEN, 1.0, 0.0)
        ones_rows = jnp.where(lax.broadcasted_iota(jnp.int32, (VT_PAD, t), 0) == 0, 1.0, 0.0)
        kvs = kvs_ref[0]
        kvw = kvw_ref[0]
        ks_ref[...] = jnp.where(ln < dh, kvs, onehot + const).astype(BF16)
        kw_ref[...] = jnp.where(ln < dh, kvw, const).astype(BF16)
        vs_ref[...] = jnp.concatenate([ones_rows, kvs.T[dh:]], axis=0).astype(BF16)
        vw_ref[...] = jnp.concatenate([ones_rows, kvw.T[dh:]], axis=0).astype(BF16)

    q_t = (q_ref[0] * (dh ** -0.5)).T
    q_heads = jnp.concatenate([q_t[hh * dh:(hh + 1) * dh] for hh in range(hpg)], axis=1)

    def scores(branch, qa_t, start, nk, bias):
        sc = _dot(branch[0][pl.ds(pl.multiple_of(start, qb), nk), :], qa_t)
        return sc if bias is None else sc + bias

    def update(branch, start, nk, sc):
        _, vt_ref, m_ref, acc_ref = branch
        m_old = m_ref[...]
        m_new = jnp.maximum(m_old, jnp.max(sc, axis=0, keepdims=True))
        alpha = jnp.exp(m_old - m_new)
        pe = jnp.exp(sc - m_new).astype(BF16)
        acc_ref[...] = alpha * acc_ref[...] + _dot(vt_ref[:, pl.ds(pl.multiple_of(start, qb), nk)], pe)
        m_ref[...] = m_new

    sel = (ks_ref, vs_ref, ms_ref, accs_ref)
    win = (kw_ref, vw_ref, mw_ref, accw_ref)
    for m_ref, acc_ref in ((ms_ref, accs_ref), (mw_ref, accw_ref)):
        m_ref[...] = jnp.full(m_ref.shape, NEG_INF, F32)
        acc_ref[...] = jnp.zeros(acc_ref.shape, F32)

    nwt = NSA_WINDOW // qb
    assert nwt in (2, 4)
    pad_rows = jnp.zeros((LANES - CONST_LANE0 - 8, cols), F32)
    qa_win = jnp.concatenate([q_heads, jnp.zeros((32, cols), F32), qc_ref[0], pad_rows], axis=0).astype(BF16)

    kvc = kvc_ref[0, 0]
    lane_k = lax.broadcasted_iota(jnp.int32, kvc.shape, 1)
    kc16 = jnp.where(lane_k < dh, kvc, 0.0).astype(BF16)
    s = _dot(kc16, qa_win) + cb_ref[0, 0]

    kk = lax.broadcasted_iota(jnp.int32, (qb, cols), 0)
    rr = lax.broadcasted_iota(jnp.int32, (qb, cols), 1) % qb
    w4_start = jnp.maximum(i - nwt, 0) * qb
    sc_w4 = scores(win, qa_win, w4_start, qb, jnp.where((rr < kk) & (i >= nwt), 0.0, NEG_INF))

    row2 = lax.broadcasted_iota(jnp.int32, (2 * qb, cols), 0)
    if nwt == 4:
        w32_start = jnp.maximum(i - 3, 0) * qb
        sc_w32 = scores(win, qa_win, w32_start, 2 * qb,
                        jnp.where(row2 < (i - 1) * qb - w32_start, 0.0, NEG_INF))
    near_start = jnp.maximum(i - 1, 0) * qb
    near_bias = near_ref[0, jnp.where(i == 0, 1, 0)]
    sc_wn = scores(win, qa_win, near_start, 2 * qb, near_bias)

    s = jnp.exp(s - jnp.max(s, axis=0, keepdims=True))
    p = s / jnp.sum(s, axis=0, keepdims=True)
    tq_lane = i * qb + lax.broadcasted_iota(jnp.int32, (1, cols), 1) % qb
    p16 = (p * (tq_lane >= NSA_CMP_LEN - 1).astype(F32)).astype(BF16)
    o_cmp = _dot(kvc.T.astype(BF16), p16)
    ovl = ovl_ref[...].astype(BF16)
    imp = _dot(ovl, p16[:, 0:qb])
    for hh in range(1, hpg):
        imp = imp + _dot(ovl, p16[:, hh * qb:(hh + 1) * qb])

    update(win, w4_start, qb, sc_w4)

    blk = lax.broadcasted_iota(jnp.int32, (32, qb), 0)
    tq = i * qb + lax.broadcasted_iota(jnp.int32, (32, qb), 1)
    cur = tq // NSA_SLC_LEN
    forced = (blk == 0) | (blk == cur) | (blk == cur - 1)
    val = jnp.where(forced, jnp.inf, jnp.where(blk * NSA_SLC_LEN <= tq, imp, -jnp.inf))
    cnt = jnp.zeros((32, qb), jnp.int32)
    for s2 in range(nblk):
        other = val[s2:s2 + 1, :]
        cnt = cnt + ((other > val) | ((other == val) & (s2 < blk))).astype(jnp.int32)
    feat = jnp.where((cnt < min(NSA_TOP_K, nblk)) & (blk < nblk), 0.0, NEG_INF)
    qa = jnp.concatenate([q_heads, jnp.concatenate([feat] * hpg, axis=1), qc_ref[0], pad_rows],
                         axis=0).astype(BF16)

    if nwt == 4:
        update(win, w32_start, 2 * qb, sc_w32)
    sc_sn = scores(sel, qa, near_start, 2 * qb, near_bias)
    update(win, near_start, 2 * qb, sc_wn)
    update(sel, near_start, 2 * qb, sc_sn)

    n_far = jnp.maximum(i - 1, 0)
    n_pairs = (n_far + 1) // 2
    pair_start = lambda p: jnp.maximum(2 * p - n_far % 2, 0) * qb
    last_pair = jnp.maximum(n_pairs - 1, 0)
    sc_ref[...] = scores(sel, qa, 0, 2 * qb, jnp.where(row2 < (2 - n_far % 2) * qb, 0.0, NEG_INF))

    def sel_body(k, carry):
        sc_odd = scores(sel, qa, pair_start(2 * k + 1), 2 * qb, None)
        update(sel, pair_start(2 * k), 2 * qb, sc_ref[...])
        sc_ref[...] = scores(sel, qa, pair_start(jnp.minimum(2 * k + 2, last_pair)), 2 * qb, None)
        update(sel, pair_start(2 * k + 1), 2 * qb, sc_odd)
        return carry

    lax.fori_loop(0, n_pairs // 2, sel_body, 0)

    @pl.when(n_pairs % 2 == 1)
    def _():
        update(sel, pair_start(n_pairs - 1), 2 * qb, sc_ref[...])

    def finish(acc_ref):
        acc = acc_ref[...]
        return acc[VT_PAD:] / acc[0:1]

    o_slc = finish(accs_ref)
    o_win = finish(accw_ref)

    gt_ref[...] = jax.nn.sigmoid(gate_ref[0]).T
    outs = []
    for hh in range(hpg):
        sl = slice(hh * qb, (hh + 1) * qb)
        base = (g * hpg + hh) * 3
        gate = [gt_ref[pl.ds(base + br, 1), :] for br in range(3)]
        outs.append(gate[0] * o_cmp[dh:, sl] + gate[1] * o_slc[:, sl] + gate[2] * o_win[:, sl])
    out = jnp.concatenate(outs, axis=0).T
    o_ref[0] = (out * _silu(z_ref[0])).astype(o_ref.dtype)


def _nsa_attn(proj, kv_cmp, near, cmp_bias, qconst, ovl):
    bsz, t, _ = proj.shape
    qb, hpg = NSA_QTILE, NSA_HPG
    gw = hpg * NSA_HEAD_DIM
    nb = t // NSA_CMP_STRIDE
    cols = hpg * qb
    return pl.pallas_call(
        _nsa_attn_kernel,
        grid=(bsz, NSA_GROUPS, t // qb),
        in_specs=[
            pl.BlockSpec((1, qb, gw), lambda b, g, i: (b, i, NSA_COL_Q // gw + g)),
            pl.BlockSpec((1, t, LANES), lambda b, g, i: (b, 0, NSA_COL_SEL // LANES + g)),
            pl.BlockSpec((1, t, LANES), lambda b, g, i: (b, 0, NSA_COL_WIN // LANES + g)),
            pl.BlockSpec((1, 1, nb, LANES), lambda b, g, i: (b, g, 0, 0)),
            pl.BlockSpec((1, qb, LANES), lambda b, g, i: (b, i, NSA_COL_GATE // LANES)),
            pl.BlockSpec((1, qb, gw), lambda b, g, i: (b, i, NSA_COL_Z // gw + g)),
            pl.BlockSpec((1, 1, nb, cols), lambda b, g, i: (i, g, 0, 0)),
            pl.BlockSpec((1, 2, 2 * qb, cols), lambda b, g, i: (g, 0, 0, 0)),
            pl.BlockSpec((1, 8, cols), lambda b, g, i: (g, 0, 0)),
            pl.BlockSpec((32, nb), lambda b, g, i: (0, 0)),
        ],
        out_specs=pl.BlockSpec((1, qb, gw), lambda b, g, i: (b, i, g)),
        out_shape=jax.ShapeDtypeStruct((bsz, t, NSA_Q_W), BF16),
        scratch_shapes=[
            pltpu.VMEM((t, LANES), BF16), pltpu.VMEM((VT_PAD + NSA_HEAD_DIM, t), BF16),
            pltpu.VMEM((t, LANES), BF16), pltpu.VMEM((VT_PAD + NSA_HEAD_DIM, t), BF16),
            pltpu.VMEM((LANES, qb), F32),
            pltpu.VMEM((1, cols), F32), pltpu.VMEM((VT_PAD + NSA_HEAD_DIM, cols), F32),
            pltpu.VMEM((1, cols), F32), pltpu.VMEM((VT_PAD + NSA_HEAD_DIM, cols), F32),
            pltpu.VMEM((2 * qb, cols), F32),
        ],
        compiler_params=pltpu.CompilerParams(
            dimension_semantics=("parallel", "parallel", "arbitrary"), vmem_limit_bytes=VMEM_LIMIT),
        name="nsa_attention",
    )(proj, proj, proj, kv_cmp, proj, proj, cmp_bias, near, qconst, ovl)


def _nsa_layer(x, mod, norm_g, w_in, cmp_pos, cmp_w1, cmp_w2, rel_bias, w_out, final_g):
    t = x.shape[1]
    assert t // NSA_SLC_LEN <= 32 and NSA_COL_Z % (NSA_HPG * NSA_HEAD_DIM) == 0
    perm = _nsa_column_perm()
    cuts = [0] + [j for j in range(1, len(perm)) if perm[j] != perm[j - 1] + (perm[j - 1] >= 0)] + [len(perm)]
    runs = [(int(perm[a]), b - a) for a, b in zip(cuts[:-1], cuts[1:])]
    w16 = w_in.astype(BF16)
    w_in_p = jnp.concatenate([w16[:, s:s + n] if s >= 0 else jnp.zeros((w_in.shape[0], n), BF16)
                              for s, n in runs], axis=1)
    proj = _inproj(x, norm_g, mod, w_in_p, tm=512, tn=NSA_PROJ_W)
    kv_cmp = _nsa_compress(proj, cmp_pos, cmp_w1, cmp_w2)
    near, cmp_bias, qconst = _nsa_tables(rel_bias, t)
    o = _nsa_attn(proj, kv_cmp, near, cmp_bias, qconst, jnp.asarray(_overlap_t(t)))
    return _outproj(o, w_out.astype(BF16), x, mod, final_g, tm=512)


def kernel(x, c, ada_w, ada_b, norm_g, gdn_w_in, gdn_conv_w, gdn_a_log, gdn_dt_bias, gdn_norm_w, gdn_w_out,
           nsa_w_in, nsa_cmp_pos, nsa_cmp_w1, nsa_cmp_w2, nsa_w_out, rel_bias, final_g):
    bsz, t, d = x.shape
    mod = _modulation(c, ada_w, ada_b).reshape(ada_w.shape[0], bsz, 3, d)
    x = _gdn_layer(x, mod[0], norm_g[0], gdn_w_in[0], gdn_conv_w[0], gdn_a_log[0], gdn_dt_bias[0],
                   gdn_norm_w[0], gdn_w_out[0])
    return _nsa_layer(x, mod[1], norm_g[1], nsa_w_in[0], nsa_cmp_pos[0], nsa_cmp_w1[0], nsa_cmp_w2[0],
                      rel_bias, nsa_w_out[0], final_g)
```

```python
import functools
import math

import numpy as np
import jax
import jax.numpy as jnp
from jax import lax
from jax.experimental import pallas as pl
from jax.experimental.pallas import tpu as pltpu

F32 = jnp.float32
BF16 = jnp.bfloat16
HIGHEST = lax.Precision.HIGHEST

EPS = 1e-6
NEG_INF = -1e30
LANES = 128
VMEM_LIMIT = 56 * 1024 * 1024

GDN_QK_HEADS = 8
GDN_V_HEADS = 16
GDN_HEAD_DIM = 128
GDN_CONV = 4
GDN_CHUNK = 64
GDN_QK_W = GDN_QK_HEADS * GDN_HEAD_DIM
GDN_V_W = GDN_V_HEADS * GDN_HEAD_DIM
GDN_CONV_W = 2 * GDN_QK_W + GDN_V_W
GDN_TILE = 128
GDN_HEADS_PER_STEP = 8
GDN_INPROJ_ROWS = 256
GDN_INPROJ_COLS = 4 * LANES
NSA_INPROJ_ROWS = 512
OUTPROJ_ROWS = 1024


def _silu(x):
    return x * jax.nn.sigmoid(x)


def _dot(a, b, **kw):
    return jnp.dot(a, b, preferred_element_type=F32, **kw)


def _dot_nt(a, b, **kw):
    return lax.dot_general(a, b, (((1,), (1,)), ((), ())), preferred_element_type=F32, **kw)


def _mod_kernel(c_ref, w_ref, b_ref, o_ref):
    cond = _silu(c_ref[...])
    o_ref[0] = _dot(cond, w_ref[0], precision=HIGHEST) + b_ref[0]


def _modulation(c, ada_w, ada_b):
    depth, d, d3 = ada_w.shape
    bsz = c.shape[0]
    return pl.pallas_call(
        _mod_kernel,
        grid=(depth, d3 // d),
        in_specs=[
            pl.BlockSpec((bsz, d), lambda i, j: (0, 0)),
            pl.BlockSpec((1, d, d), lambda i, j: (i, 0, j)),
            pl.BlockSpec((1, 1, d), lambda i, j: (i, 0, j)),
        ],
        out_specs=pl.BlockSpec((1, bsz, d), lambda i, j: (i, 0, j)),
        out_shape=jax.ShapeDtypeStruct((depth, bsz, d3), F32),
        name="adaln_mod",
    )(c, ada_w, ada_b.reshape(depth, 1, d3))


def _inproj_kernel(x_ref, g_ref, mod_ref, w_ref, o_ref, *, tn):
    x = x_ref[0]
    m = mod_ref[0]
    y = x * lax.rsqrt(jnp.mean(x * x, axis=-1, keepdims=True) + EPS) * g_ref[...]
    h = (y * (1.0 + m[1:2]) + m[0:1]).astype(BF16)
    for j in range(w_ref.shape[1] // tn):
        o_ref[0, :, j * tn:(j + 1) * tn] = _dot(h, w_ref[:, j * tn:(j + 1) * tn])


def _inproj(x, g, mod, w, *, tm, tn):
    bsz, t, d = x.shape
    n = w.shape[1]
    assert t % tm == 0 and n % tn == 0
    return pl.pallas_call(
        functools.partial(_inproj_kernel, tn=tn),
        grid=(bsz, t // tm),
        in_specs=[
            pl.BlockSpec((1, tm, d), lambda b, i: (b, i, 0)),
            pl.BlockSpec((1, d), lambda b, i: (0, 0)),
            pl.BlockSpec((1, 3, d), lambda b, i: (b, 0, 0)),
            pl.BlockSpec((d, n), lambda b, i: (0, 0), pipeline_mode=pl.Buffered(1)),
        ],
        out_specs=pl.BlockSpec((1, tm, n), lambda b, i: (b, i, 0)),
        out_shape=jax.ShapeDtypeStruct((bsz, t, n), F32),
        compiler_params=pltpu.CompilerParams(
            dimension_semantics=("parallel", "parallel"), vmem_limit_bytes=VMEM_LIMIT),
        name="norm_mod_inproj",
    )(x, g.reshape(1, d), mod, w)


def _gdn_inproj_kernel(x_ref, g_ref, mod_ref, w_ref, cw_ref, alog_ref, dtb_ref, o_ref, col_ref, row_ref,
                        ext_ref, tail_ref, *, tn):
    dh = GDN_HEAD_DIM
    tm = x_ref.shape[1]
    x = x_ref[0]
    m = mod_ref[0]
    y = x * lax.rsqrt(jnp.mean(x * x, axis=-1, keepdims=True) + EPS) * g_ref[...]
    h = (y * (1.0 + m[1:2]) + m[0:1]).astype(BF16)

    @pl.when(pl.program_id(1) == 0)
    def _():
        tail_ref[...] = jnp.zeros_like(tail_ref)

    def conv_tile(j):
        cols = slice(j * tn, (j + 1) * tn)
        ext_ref[0:8, :] = tail_ref[:, cols]
        ext_ref[8:8 + tm, :] = o_ref[0, :, cols]
        tail_ref[:, cols] = ext_ref[tm:tm + 8, :]
        w = cw_ref[:, cols]
        ext = ext_ref[...]
        acc = w[0:1] * ext
        for tap in range(1, GDN_CONV):
            acc = pltpu.roll(acc, 1, axis=0) + w[tap:tap + 1] * ext
        act = _silu(acc[8:8 + tm])
        if j * tn < 2 * GDN_QK_W:
            scale = dh ** -0.5 if j * tn < GDN_QK_W else 1.0
            heads = [act[:, c * dh:(c + 1) * dh] for c in range(tn // dh)]
            heads = [hd * (lax.rsqrt(jnp.sum(hd * hd, axis=-1, keepdims=True) + EPS) * scale) for hd in heads]
            act = jnp.concatenate(heads, axis=1)
        o_ref[0, :, cols] = act

    n = w_ref.shape[1]
    n_conv = GDN_CONV_W // tn
    bounds = [(j * tn, min((j + 1) * tn, n)) for j in range(-(-n // tn))]
    for j, (lo, hi) in enumerate(bounds):
        o_ref[0, :, lo:hi] = _dot(h, w_ref[:, lo:hi])
        if 1 <= j <= n_conv:
            conv_tile(j - 1)
    _gdn_gate_scalars(o_ref[0, :, n - LANES:n], alog_ref[...], dtb_ref[...], col_ref, row_ref)


def _gdn_inproj(x, g, mod, w, conv_w, a_log, dt_bias, *, tm, tn):
    bsz, t, d = x.shape
    n = w.shape[1]
    assert t % tm == 0 and GDN_CONV_W % tn == 0 and GDN_QK_W % tn == 0 and n > GDN_CONV_W
    pad = lambda u: jnp.zeros((1, LANES), F32).at[0, GDN_V_HEADS:2 * GDN_V_HEADS].set(u)
    ncb = tm // GDN_CHUNK
    return pl.pallas_call(
        functools.partial(_gdn_inproj_kernel, tn=tn),
        grid=(bsz, t // tm),
        in_specs=[
            pl.BlockSpec((1, tm, d), lambda b, i: (b, i, 0)),
            pl.BlockSpec((1, d), lambda b, i: (0, 0)),
            pl.BlockSpec((1, 3, d), lambda b, i: (b, 0, 0)),
            pl.BlockSpec((d, n), lambda b, i: (0, 0), pipeline_mode=pl.Buffered(1)),
            pl.BlockSpec((GDN_CONV, GDN_CONV_W), lambda b, i: (0, 0)),
            pl.BlockSpec((1, LANES), lambda b, i: (0, 0)),
            pl.BlockSpec((1, LANES), lambda b, i: (0, 0)),
        ],
        out_specs=[
            pl.BlockSpec((1, tm, n), lambda b, i: (b, i, 0)),
            pl.BlockSpec((1, tm, LANES), lambda b, i: (b, i, 0)),
            pl.BlockSpec((1, ncb, GDN_V_HEADS // 2, 2 * GDN_CHUNK), lambda b, i: (b, i, 0, 0)),
        ],
        out_shape=[
            jax.ShapeDtypeStruct((bsz, t, n), F32),
            jax.ShapeDtypeStruct((bsz, t, LANES), F32),
            jax.ShapeDtypeStruct((bsz, t // GDN_CHUNK, GDN_V_HEADS // 2, 2 * GDN_CHUNK), F32),
        ],
        scratch_shapes=[pltpu.VMEM((tm + 8, tn), F32), pltpu.VMEM((8, GDN_CONV_W), F32)],
        compiler_params=pltpu.CompilerParams(
            dimension_semantics=("parallel", "arbitrary"), vmem_limit_bytes=VMEM_LIMIT),
        name="gdn_norm_inproj_conv",
    )(x, g.reshape(1, d), mod, w, conv_w, pad(a_log), pad(dt_bias))


def _outproj_kernel(o_ref, w_ref, x_ref, mod_ref, *rest, final_norm):
    y = _dot(o_ref[0], w_ref[...])
    x = x_ref[0] + mod_ref[0][2:3] * y
    if final_norm:
        fg_ref, out_ref = rest
        x = x * lax.rsqrt(jnp.mean(x * x, axis=-1, keepdims=True) + EPS) * fg_ref[...]
    else:
        (out_ref,) = rest
    out_ref[0] = x


def _outproj(o, w, x, mod, final_g=None, *, tm):
    bsz, t, d = x.shape
    k = o.shape[-1]
    in_specs = [
        pl.BlockSpec((1, tm, k), lambda b, i: (b, i, 0)),
        pl.BlockSpec((k, d), lambda b, i: (0, 0), pipeline_mode=pl.Buffered(1)),
        pl.BlockSpec((1, tm, d), lambda b, i: (b, i, 0)),
        pl.BlockSpec((1, 3, d), lambda b, i: (b, 0, 0)),
    ]
    args = [o, w, x, mod]
    if final_g is not None:
        in_specs.append(pl.BlockSpec((1, d), lambda b, i: (0, 0)))
        args.append(final_g.reshape(1, d))
    return pl.pallas_call(
        functools.partial(_outproj_kernel, final_norm=final_g is not None),
        grid=(bsz, t // tm),
        in_specs=in_specs,
        out_specs=pl.BlockSpec((1, tm, d), lambda b, i: (b, i, 0)),
        out_shape=jax.ShapeDtypeStruct((bsz, t, d), F32),
        compiler_params=pltpu.CompilerParams(
            dimension_semantics=("parallel", "parallel"), vmem_limit_bytes=VMEM_LIMIT),
        name="outproj_residual",
    )(*args)


def _gdn_gate_scalars(ba, alog, dtb, col_ref, row_ref):
    cs, nh = GDN_CHUNK, GDN_V_HEADS
    lane = lax.broadcasted_iota(jnp.int32, ba.shape, 1)
    g = -jnp.exp(alog) * jax.nn.softplus(ba + dtb)
    vals = jnp.where(lane < nh, jax.nn.sigmoid(ba), g)
    r = lax.broadcasted_iota(jnp.int32, (cs, cs), 0)
    c = lax.broadcasted_iota(jnp.int32, (cs, cs), 1)
    tri = (r >= c).astype(F32)
    is_beta = lax.broadcasted_iota(jnp.int32, (cs, LANES), 1) < nh
    pr = lax.broadcasted_iota(jnp.int32, (nh // 2, LANES), 0)
    pc = lax.broadcasted_iota(jnp.int32, (nh // 2, LANES), 1)
    pick_even = (pc == nh + 2 * pr).astype(F32)
    for n in range(ba.shape[0] // cs):
        v = vals[n * cs:(n + 1) * cs]
        cum = _dot(tri, v, precision=HIGHEST)
        out = jnp.where(is_beta, v, cum)
        col_ref[0, n * cs:(n + 1) * cs, :] = out
        both = jnp.concatenate([out, pltpu.roll(out, LANES - 1, axis=1)], axis=0)
        row_ref[0, n] = _dot_nt(pick_even, both, precision=HIGHEST)


TRI_BASE = 8
PACK = 4


def _block_rows(p):
    n = p.shape[0]
    blk = lax.broadcasted_iota(jnp.int32, p.shape, 1) // n
    return jnp.concatenate([jnp.where(blk == j, p, 0.0) for j in range(PACK)], axis=0).astype(BF16)


def _tri_inverse_packed(mats):
    n = mats[0].shape[0]
    r = lax.broadcasted_iota(jnp.int32, mats[0].shape, 0)
    c = lax.broadcasted_iota(jnp.int32, mats[0].shape, 1) % n
    same = lambda s: (r // s) == (c // s)
    mm = lambda a, b: _dot(a.astype(BF16), _block_rows(b))
    diag = [jnp.where(same(TRI_BASE), a, 0.0) for a in mats]
    inv = [jnp.where(r == c, 1.0, 0.0) - d for d in diag]
    pw = diag
    k = 1
    while 2 * k < TRI_BASE:
        pw = [mm(m, m) for m in pw]
        inv = [p + mm(p, m) for p, m in zip(inv, pw)]
        k *= 2
        yield None
    s = TRI_BASE
    while s < n:
        sub = same(2 * s) & jnp.logical_not(same(s))
        left = [mm(p, jnp.where(sub, a, 0.0)) for p, a in zip(inv, mats)]
        inv = [p - mm(l, p) for p, l in zip(inv, left)]
        s *= 2
        yield None
    yield inv


def _gdn_chunk_kernel(q_ref, k_ref, v_ref, z_ref, col_ref, row_ref, nw_ref,
                      o_ref, s_ref, hu_ref, hwq_ref, hqk_ref, hkd_ref, hgl_ref):
    cs, dh, nh = GDN_CHUNK, GDN_HEAD_DIM, GDN_V_HEADS
    tt = q_ref.shape[1]
    hps = q_ref.shape[2] // dh
    ncb = tt // cs
    assert ncb % 2 == 0 and PACK == 4
    hg = pl.program_id(1)
    ti = pl.program_id(2)
    wslot = ti % 2
    rslot = 1 - wslot
    jidx = lambda hl, n, e: (hl * ncb + n) * 2 + e

    @pl.when(ti == 0)
    def _():
        s_ref[...] = jnp.zeros_like(s_ref)
        for h_ref in (hu_ref, hwq_ref, hqk_ref, hkd_ref, hgl_ref):
            h_ref[1] = jnp.zeros(h_ref.shape[1:], h_ref.dtype)

    chunks = [slice(n * cs, (n + 1) * cs) for n in range(ncb)]

    def prepare():
        q = [q_ref[0, :, hl * dh:(hl + 1) * dh] for hl in range(hps)]
        k = [k_ref[0, :, hl * dh:(hl + 1) * dh] for hl in range(hps)]
        v = [[v_ref[0, :, (2 * hl + e) * dh:(2 * hl + e + 1) * dh] for e in range(2)] for hl in range(hps)]
        lane = lax.broadcasted_iota(jnp.int32, (tt, LANES), 1)
        colv = col_ref[0]
        column = lambda idx: jnp.sum(jnp.where(lane == idx, colv, 0.0), axis=1, keepdims=True)
        hq = [hg * hps + hl for hl in range(hps)]
        beta = [[column(2 * h + e) for e in range(2)] for h in hq]
        gc = [[column(nh + 2 * h + e) for e in range(2)] for h in hq]
        egc = [[jnp.exp(x) for x in pair] for pair in gc]
        yield

        pr = lax.broadcasted_iota(jnp.int32, (cs, PACK * cs), 0)
        pb = lax.broadcasted_iota(jnp.int32, (cs, PACK * cs), 1) // cs
        pc = lax.broadcasted_iota(jnp.int32, (cs, PACK * cs), 1) % cs
        groups = [(hl, gi) for hl in range(hps) for gi in range(ncb // 2)]
        a_mats, g_rows = [], {}
        for hl, gi in groups:
            k16, q16 = k[hl].astype(BF16), q[hl].astype(BF16)
            pair = (chunks[2 * gi], chunks[2 * gi + 1])
            kdup = [jnp.concatenate([k16[sl], k16[sl]], axis=0) for sl in pair]
            kk = jnp.concatenate([_dot_nt(k16[sl], kd) for sl, kd in zip(pair, kdup)], axis=1)
            qk = jnp.concatenate([_dot_nt(q16[sl], kd) for sl, kd in zip(pair, kdup)], axis=1)
            pick = lambda cols: jnp.where(pb == 0, cols[0][pair[0]], jnp.where(
                pb == 1, cols[1][pair[0]], jnp.where(pb == 2, cols[0][pair[1]], cols[1][pair[1]])))
            g_row = jnp.concatenate([row_ref[0, 2 * gi + j, pl.ds(hq[hl], 1), :] for j in range(2)], axis=1)
            decay = jnp.exp(jnp.where(pr >= pc, pick(gc[hl]) - g_row, -jnp.inf))
            a_mats.append(jnp.where(pr > pc, pick(beta[hl]) * kk * decay, 0.0))
            hqk_ref[wslot, hl * (ncb // 2) + gi] = (qk * decay).astype(BF16)
            g_rows[hl, gi] = g_row
            yield
        t_mats = None
        for t_mats in _tri_inverse_packed(a_mats):
            yield

        for (hl, gi), t_mat in zip(groups, t_mats):
            rhs = []
            for n in (2 * gi, 2 * gi + 1):
                sl = chunks[n]
                for e in range(2):
                    j = 2 * (n % 2) + e
                    g_last = g_rows[hl, gi][:, j * cs + cs - 1:(j + 1) * cs]
                    hgl_ref[wslot, jidx(hl, n, e)] = jnp.broadcast_to(jnp.exp(g_last), (1, LANES))
                    hkd_ref[wslot, jidx(hl, n, e)] = (k[hl][sl] * jnp.exp(g_last - gc[hl][e][sl])).T.astype(BF16)
                    kbeta = k[hl][sl] * beta[hl][e][sl]
                    rhs.append(jnp.concatenate([v[hl][e][sl] * beta[hl][e][sl], kbeta * egc[hl][e][sl]], axis=1))
            out = _dot(_block_rows(t_mat), jnp.concatenate(rhs, axis=0).astype(BF16))
            for n in (2 * gi, 2 * gi + 1):
                for e in range(2):
                    uw = out[(2 * (n % 2) + e) * cs:(2 * (n % 2) + e + 1) * cs]
                    hu_ref[wslot, jidx(hl, n, e)] = uw[:, :dh]
                    q_dec = q[hl][chunks[n]] * egc[hl][e][chunks[n]]
                    hwq_ref[wslot, jidx(hl, n, e)] = jnp.concatenate([uw[:, dh:], q_dec], axis=0).astype(BF16)
            yield

    def recurrence():
        nw = nw_ref[...]
        heads = [(hl, e) for hl in range(hps) for e in range(2)]
        state = {he: s_ref[2 * he[0] + he[1]] for he in heads}
        for n, sl in enumerate(chunks):
            ws = {}
            for hl, e in heads:
                ws[hl, e] = _dot(hwq_ref[rslot, jidx(hl, n, e)], state[hl, e].astype(BF16))
            yield
            for hl, e in heads:
                j = jidx(hl, n, e)
                v16 = (hu_ref[rslot, j] - ws[hl, e][:cs]).astype(BF16)
                lb = 2 * (n % 2) + e
                qkd = hqk_ref[rslot, hl * (ncb // 2) + n // 2][:, lb * cs:(lb + 1) * cs]
                o = ws[hl, e][cs:] + _dot(qkd, v16)
                state[hl, e] = state[hl, e] * hgl_ref[rslot, j] + _dot(hkd_ref[rslot, j], v16)
                o = o * lax.rsqrt(jnp.mean(o * o, axis=-1, keepdims=True) + EPS) * nw
                lanes = slice((2 * hl + e) * dh, (2 * hl + e + 1) * dh)
                o_ref[0, sl, lanes] = (o * _silu(z_ref[0, sl, lanes])).astype(o_ref.dtype)
            yield
        for (hl, e), st in state.items():
            s_ref[2 * hl + e] = st

    prep, rec = prepare(), recurrence()
    n_prep = 1 + hps * (ncb // 2) * 2 + 6
    per_stage = -(-n_prep // (2 * ncb))
    prep_live = rec_live = True
    while prep_live or rec_live:
        for _ in range(per_stage):
            if prep_live:
                prep_live = next(prep, "done") != "done"
        if rec_live:
            rec_live = next(rec, "done") != "done"


def _gdn_chunk(proj, col, row, norm_w, *, tt=GDN_TILE, hps=GDN_HEADS_PER_STEP):
    bsz, t, _ = proj.shape
    dh = GDN_HEAD_DIM
    qw, vw = hps * dh, 2 * hps * dh
    k_blk0 = GDN_QK_W // qw
    v_blk0 = 2 * GDN_QK_W // vw
    z_blk0 = GDN_CONV_W // vw
    ncb = tt // GDN_CHUNK
    nt = t // tt
    nj = hps * ncb * 2
    cur = lambda i: jnp.minimum(i, nt - 1)
    prev = lambda i: jnp.maximum(i - 1, 0)
    return pl.pallas_call(
        _gdn_chunk_kernel,
        grid=(bsz, GDN_QK_HEADS // hps, nt + 1),
        in_specs=[
            pl.BlockSpec((1, tt, qw), lambda b, h, i: (b, cur(i), h)),
            pl.BlockSpec((1, tt, qw), lambda b, h, i: (b, cur(i), k_blk0 + h)),
            pl.BlockSpec((1, tt, vw), lambda b, h, i: (b, cur(i), v_blk0 + h)),
            pl.BlockSpec((1, tt, vw), lambda b, h, i: (b, prev(i), z_blk0 + h)),
            pl.BlockSpec((1, tt, LANES), lambda b, h, i: (b, cur(i), 0)),
            pl.BlockSpec((1, ncb, GDN_V_HEADS // 2, 2 * GDN_CHUNK), lambda b, h, i: (b, cur(i), 0, 0)),
            pl.BlockSpec((1, dh), lambda b, h, i: (0, 0)),
        ],
        out_specs=pl.BlockSpec((1, tt, vw), lambda b, h, i: (b, prev(i), h)),
        out_shape=jax.ShapeDtypeStruct((bsz, t, GDN_V_W), BF16),
        scratch_shapes=[
            pltpu.VMEM((2 * hps, dh, dh), F32),
            pltpu.VMEM((2, nj, GDN_CHUNK, dh), F32),
            pltpu.VMEM((2, nj, 2 * GDN_CHUNK, dh), BF16),
            pltpu.VMEM((2, nj // PACK, GDN_CHUNK, PACK * GDN_CHUNK), BF16),
            pltpu.VMEM((2, nj, dh, GDN_CHUNK), BF16),
            pltpu.VMEM((2, nj, 1, LANES), F32),
        ],
        compiler_params=pltpu.CompilerParams(
            dimension_semantics=("parallel", "parallel", "arbitrary"), vmem_limit_bytes=VMEM_LIMIT),
        name="gdn_chunk_scan",
    )(proj, proj, proj, proj, col, row, norm_w.reshape(1, dh))


def _gdn_layer(x, mod, norm_g, w_in, conv_w, a_log, dt_bias, norm_w, w_out):
    n_in = w_in.shape[1]
    n_pad = -(-n_in // (7 * LANES)) * (7 * LANES)
    w_in_p = jnp.pad(w_in, ((0, 0), (0, n_pad - n_in))).astype(BF16)
    proj, col, row = _gdn_inproj(x, norm_g, mod, w_in_p, conv_w, a_log, dt_bias,
                                 tm=GDN_INPROJ_ROWS, tn=GDN_INPROJ_COLS)
    o = _gdn_chunk(proj, col, row, norm_w)
    return _outproj(o, w_out.astype(BF16), x, mod, tm=OUTPROJ_ROWS)


NSA_HEADS = 16
NSA_GROUPS = 4
NSA_HPG = NSA_HEADS // NSA_GROUPS
NSA_HEAD_DIM = 64
NSA_CMP_LEN = 32
NSA_CMP_STRIDE = 16
NSA_SLC_LEN = 64
NSA_TOP_K = 8
NSA_WINDOW = 512
NSA_QTILE = 256
NSA_Q_W = NSA_HEADS * NSA_HEAD_DIM
NSA_KV_W = NSA_GROUPS * NSA_HEAD_DIM
REL_BUCKETS = 32
REL_MAX_DIST = 128
FEAT_LANE0 = NSA_HEAD_DIM
CONST_LANE0 = FEAT_LANE0 + 32
VT_PAD = 16
NSA_COL_Q = 0
NSA_COL_CMP = NSA_Q_W
NSA_COL_SEL = NSA_COL_CMP + 2 * NSA_KV_W
NSA_COL_WIN = NSA_COL_SEL + 2 * NSA_KV_W
NSA_COL_Z = NSA_COL_WIN + 2 * NSA_KV_W
NSA_COL_GATE = NSA_COL_Z + NSA_Q_W
NSA_PROJ_W = NSA_COL_GATE + LANES


def _nsa_column_perm():
    g, dh = NSA_GROUPS, NSA_HEAD_DIM
    kv0 = NSA_Q_W
    cols = list(range(NSA_Q_W))
    cols += [kv0 + i for i in range(2 * NSA_KV_W)]
    for br in (1, 2):
        for gi in range(g):
            cols += [kv0 + (2 * br) * NSA_KV_W + gi * dh + d for d in range(dh)]
            cols += [kv0 + (2 * br + 1) * NSA_KV_W + gi * dh + d for d in range(dh)]
    gate0 = kv0 + 6 * NSA_KV_W
    cols += [gate0 + 3 * NSA_HEADS + i for i in range(NSA_Q_W)]
    cols += [gate0 + i for i in range(3 * NSA_HEADS)] + [-1] * (LANES - 3 * NSA_HEADS)
    assert len(cols) == NSA_PROJ_W
    return np.asarray(cols, np.int32)


def _rel_bucket_table(n):
    d = np.arange(n)
    max_exact = REL_BUCKETS // 2
    nf = np.maximum(d, 1).astype(np.float64)
    large = max_exact + (np.log(nf / max_exact) / math.log(REL_MAX_DIST / max_exact)
                         * (REL_BUCKETS - max_exact)).astype(np.int32)
    large = np.minimum(large, REL_BUCKETS - 1)
    return np.where(d < max_exact, d, large).astype(np.int32)


def _nsa_tables(rel_bias, t):
    qb = NSA_QTILE
    bucket = _rel_bucket_table(t)
    assert np.all(bucket[qb + 1:] == REL_BUCKETS - 1)
    bvec = rel_bias[bucket].T
    far = rel_bias[REL_BUCKETS - 1]
    far_hi = far.astype(BF16)
    far_lo = (far - far_hi.astype(F32)).astype(BF16)
    far_sum = far_hi.astype(F32) + far_lo.astype(F32)
    r = np.arange(qb)[:, None]
    c = np.arange(qb)[None, :]
    d0 = r - c

    def toeplitz(w):
        n = 2 * qb - 1
        ext = jnp.pad(w[:, ::-1], ((0, 0), (0, 1)))
        skew = jnp.tile(ext, (1, qb))[:, :qb * n].reshape(w.shape[0], qb, n)
        return skew[:, :, qb - 1:]

    rel = bvec[:, :2 * qb] - far_sum[:, None]
    t0 = toeplitz(jnp.concatenate([jnp.full((NSA_HEADS, qb - 1), NEG_INF, F32), rel[:, :qb]], axis=1))
    t1 = toeplitz(rel[:, 1:])
    g, hpg = NSA_GROUPS, NSA_HPG
    none = jnp.full_like(t0, NEG_INF)
    near = jnp.stack([t1, t0, t0, none], axis=1).reshape(g, hpg, 2, 2, qb, qb)
    near = near.transpose(0, 2, 3, 5, 1, 4).reshape(g, 2, 2 * qb, hpg * qb)
    nb = t // NSA_CMP_STRIDE
    per_tile = qb // NSA_CMP_STRIDE
    back = 9
    far_d = back * NSA_CMP_STRIDE - (NSA_CMP_LEN - 1)
    assert np.all(bucket[far_d:] == REL_BUCKETS - 1)
    width = (qb - 1 + far_d) // NSA_CMP_STRIDE + 1
    dm = r - NSA_CMP_STRIDE * np.arange(width)[None, :] + far_d
    band = jnp.where(dm >= 0, bvec[:, np.maximum(dm, 0)], NEG_INF)
    tiles = []
    for i in range(t // qb):
        j0 = per_tile * i - back
        lo, hi = max(j0, 0), min(j0 + width, nb)
        tiles.append(jnp.concatenate([
            jnp.broadcast_to(far[:, None, None], (NSA_HEADS, qb, lo)),
            band[:, :, lo - j0:hi - j0],
            jnp.full((NSA_HEADS, qb, nb - hi), NEG_INF, F32)], axis=2))
    cmp_bias = jnp.stack(tiles, axis=0).reshape(t // qb, g, hpg, qb, nb).transpose(0, 1, 4, 2, 3)
    cmp_bias = cmp_bias.reshape(t // qb, g, nb, hpg * qb)
    qconst = jnp.zeros((g, 8, hpg, qb), F32)
    qconst = qconst.at[:, 0].set(jnp.broadcast_to(far_hi.astype(F32).reshape(g, hpg, 1), (g, hpg, qb)))
    qconst = qconst.at[:, 1].set(jnp.broadcast_to(far_lo.astype(F32).reshape(g, hpg, 1), (g, hpg, qb)))
    return near, cmp_bias, qconst.reshape(g, 8, hpg * qb)


def _overlap_t(t):
    n_cmp = (t - NSA_CMP_LEN) // NSA_CMP_STRIDE + 1
    n_slc = t // NSA_SLC_LEN
    c_start = np.arange(n_cmp)[:, None] * NSA_CMP_STRIDE
    s_start = np.arange(n_slc)[None, :] * NSA_SLC_LEN
    ov = np.clip(np.minimum(c_start + NSA_CMP_LEN, s_start + NSA_SLC_LEN) - np.maximum(c_start, s_start), 0, None)
    ov = ov.astype(np.float32) / NSA_CMP_LEN
    out = np.zeros((32, t // NSA_CMP_STRIDE), np.float32)
    out[:n_slc, :n_cmp] = ov.T
    return out


def _nsa_compress_kernel(x_ref, pos_ref, w1_ref, w2_ref, o_ref, xs_ref):
    t = x_ref.shape[1]
    nb = t // NSA_CMP_STRIDE
    nlt = xs_ref.shape[0]
    for c in range(nlt):
        xs_ref[c, 0:t, :] = x_ref[0, :, c * LANES:(c + 1) * LANES]
        xs_ref[c, t:t + NSA_CMP_STRIDE, :] = jnp.zeros((NSA_CMP_STRIDE, LANES), F32)
    acc = jnp.zeros((nb, w1_ref.shape[2]), F32)
    for l in range(NSA_CMP_LEN):
        xl = jnp.concatenate([xs_ref[c, pl.ds(l, nb, stride=NSA_CMP_STRIDE), :] for c in range(nlt)], axis=1)
        xl = xl + pos_ref[l:l + 1, :]
        acc = acc + _dot(xl.astype(BF16), w1_ref[l])
    hid = _silu(acc).astype(BF16)
    res = _dot(hid, w2_ref[...])
    for g in range(NSA_GROUPS):
        o_ref[0, g] = res[:, g * LANES:(g + 1) * LANES]


def _nsa_compress(proj, cmp_pos, cmp_w1, cmp_w2):
    bsz, t, _ = proj.shape
    g, dh = NSA_GROUPS, NSA_HEAD_DIM
    nb = t // NSA_CMP_STRIDE
    w = 2 * NSA_KV_W
    w1 = cmp_w1.reshape(2, NSA_CMP_LEN, dh, dh).astype(BF16)
    w2 = cmp_w2.astype(BF16)
    place = lambda blk, c0: jnp.pad(blk, [(0, 0)] * (blk.ndim - 1) + [(c0, w - dh - c0)])
    w1c = jnp.concatenate([place(w1[i], (gi * 2 + i) * dh) for i in range(2) for gi in range(g)], axis=1)
    w2c = jnp.concatenate([place(w2[i], (gi * 2 + i) * dh) for gi in range(g) for i in range(2)], axis=0)
    pos = jnp.broadcast_to(cmp_pos[:, :, None, :], (2, NSA_CMP_LEN, g, dh)).transpose(1, 0, 2, 3).reshape(NSA_CMP_LEN, w)
    return pl.pallas_call(
        _nsa_compress_kernel,
        grid=(bsz,),
        in_specs=[
            pl.BlockSpec((1, t, w), lambda b: (b, 0, NSA_COL_CMP // w)),
            pl.BlockSpec((NSA_CMP_LEN, w), lambda b: (0, 0)),
            pl.BlockSpec((NSA_CMP_LEN, w, w), lambda b: (0, 0, 0), pipeline_mode=pl.Buffered(1)),
            pl.BlockSpec((w, w), lambda b: (0, 0)),
        ],
        out_specs=pl.BlockSpec((1, g, nb, LANES), lambda b: (b, 0, 0, 0)),
        out_shape=jax.ShapeDtypeStruct((bsz, g, nb, LANES), F32),
        scratch_shapes=[pltpu.VMEM((w // LANES, t + NSA_CMP_STRIDE, LANES), F32)],
        compiler_params=pltpu.CompilerParams(dimension_semantics=("parallel",), vmem_limit_bytes=VMEM_LIMIT),
        name="nsa_compress",
    )(proj, pos, w1c, w2c)


def _nsa_attn_kernel(q_ref, kvs_ref, kvw_ref, kvc_ref, gate_ref, z_ref, cb_ref, near_ref, qc_ref, ovl_ref,
                     o_ref, ks_ref, vs_ref, kw_ref, vw_ref, gt_ref, ms_ref, accs_ref, mw_ref, accw_ref, sc_ref):
    qb, dh, hpg = NSA_QTILE, NSA_HEAD_DIM, NSA_HPG
    t = kvs_ref.shape[1]
    nblk = t // NSA_SLC_LEN
    cols = hpg * qb
    g = pl.program_id(1)
    i = pl.program_id(2)

    @pl.when(i == 0)
    def _():
        tok = lax.broadcasted_iota(jnp.int32, (t, LANES), 0)
        ln = lax.broadcasted_iota(jnp.int32, (t, LANES), 1)
        const = jnp.where((ln == CONST_LANE0) | (ln == CONST_LANE0 + 1), 1.0, 0.0)
        onehot = jnp.where(ln - FEAT_LANE0 == tok // NSA_SLC_LEN, 1.0, 0.0)
        ones_rows = jnp.where(lax.broadcasted_iota(jnp.int32, (VT_PAD, t), 0) == 0, 1.0, 0.0)
        kvs = kvs_ref[0]
        kvw = kvw_ref[0]
        ks_ref[...] = jnp.where(ln < dh, kvs, onehot + const).astype(BF16)
        kw_ref[...] = jnp.where(ln < dh, kvw, const).astype(BF16)
        vs_ref[...] = jnp.concatenate([ones_rows, kvs.T[dh:]], axis=0).astype(BF16)
        vw_ref[...] = jnp.concatenate([ones_rows, kvw.T[dh:]], axis=0).astype(BF16)
    q_t = (q_ref[0] * (dh ** -0.5)).T
    q_heads = jnp.concatenate([q_t[hh * dh:(hh + 1) * dh] for hh in range(hpg)], axis=1)

    def scores(branch, qa_t, start, nk, bias):
        sc = _dot(branch[0][pl.ds(pl.multiple_of(start, qb), nk), :], qa_t)
        return sc if bias is None else sc + bias

    def update(branch, start, nk, sc):
        _, vt_ref, m_ref, acc_ref = branch
        m_old = m_ref[...]
        m_new = jnp.maximum(m_old, jnp.max(sc, axis=0, keepdims=True))
        alpha = jnp.exp(m_old - m_new)
        pe = jnp.exp(sc - m_new).astype(BF16)
        acc_ref[...] = alpha * acc_ref[...] + _dot(vt_ref[:, pl.ds(pl.multiple_of(start, qb), nk)], pe)
        m_ref[...] = m_new

    sel = (ks_ref, vs_ref, ms_ref, accs_ref)
    win = (kw_ref, vw_ref, mw_ref, accw_ref)
    for m_ref, acc_ref in ((ms_ref, accs_ref), (mw_ref, accw_ref)):
        m_ref[...] = jnp.full(m_ref.shape, NEG_INF, F32)
        acc_ref[...] = jnp.zeros(acc_ref.shape, F32)

    nwt = NSA_WINDOW // qb
    assert nwt in (2, 4)
    pad_rows = jnp.zeros((LANES - CONST_LANE0 - 8, cols), F32)
    qa_win = jnp.concatenate([q_heads, jnp.zeros((32, cols), F32), qc_ref[0], pad_rows], axis=0).astype(BF16)

    kvc = kvc_ref[0, 0]
    lane_k = lax.broadcasted_iota(jnp.int32, kvc.shape, 1)
    kc16 = jnp.where(lane_k < dh, kvc, 0.0).astype(BF16)
    s = _dot(kc16, qa_win) + cb_ref[0, 0]

    kk = lax.broadcasted_iota(jnp.int32, (qb, cols), 0)
    rr = lax.broadcasted_iota(jnp.int32, (qb, cols), 1) % qb
    w4_start = jnp.maximum(i - nwt, 0) * qb
    sc_w4 = scores(win, qa_win, w4_start, qb, jnp.where((rr < kk) & (i >= nwt), 0.0, NEG_INF))

    row2 = lax.broadcasted_iota(jnp.int32, (2 * qb, cols), 0)
    if nwt == 4:
        w32_start = jnp.maximum(i - 3, 0) * qb
        sc_w32 = scores(win, qa_win, w32_start, 2 * qb,
                        jnp.where(row2 < (i - 1) * qb - w32_start, 0.0, NEG_INF))
    near_start = jnp.maximum(i - 1, 0) * qb
    near_bias = near_ref[0, jnp.where(i == 0, 1, 0)]
    sc_wn = scores(win, qa_win, near_start, 2 * qb, near_bias)

    s = jnp.exp(s - jnp.max(s, axis=0, keepdims=True))
    p = s / jnp.sum(s, axis=0, keepdims=True)
    tq_lane = i * qb + lax.broadcasted_iota(jnp.int32, (1, cols), 1) % qb
    p16 = (p * (tq_lane >= NSA_CMP_LEN - 1).astype(F32)).astype(BF16)
    o_cmp = _dot(kvc.T.astype(BF16), p16)
    ovl = ovl_ref[...].astype(BF16)
    imp = _dot(ovl, p16[:, 0:qb])
    for hh in range(1, hpg):
        imp = imp + _dot(ovl, p16[:, hh * qb:(hh + 1) * qb])

    update(win, w4_start, qb, sc_w4)

    blk = lax.broadcasted_iota(jnp.int32, (32, qb), 0)
    tq = i * qb + lax.broadcasted_iota(jnp.int32, (32, qb), 1)
    cur = tq // NSA_SLC_LEN
    forced = (blk == 0) | (blk == cur) | (blk == cur - 1)
    val = jnp.where(forced, jnp.inf, jnp.where(blk * NSA_SLC_LEN <= tq, imp, -jnp.inf))
    cnt = jnp.zeros((32, qb), jnp.int32)
    for s2 in range(nblk):
        other = val[s2:s2 + 1, :]
        cnt = cnt + ((other > val) | ((other == val) & (s2 < blk))).astype(jnp.int32)
    feat = jnp.where((cnt < min(NSA_TOP_K, nblk)) & (blk < nblk), 0.0, NEG_INF)
    qa = jnp.concatenate([q_heads, jnp.concatenate([feat] * hpg, axis=1), qc_ref[0], pad_rows],
                         axis=0).astype(BF16)

    if nwt == 4:
        update(win, w32_start, 2 * qb, sc_w32)
    sc_sn = scores(sel, qa, near_start, 2 * qb, near_bias)
    update(win, near_start, 2 * qb, sc_wn)
    update(sel, near_start, 2 * qb, sc_sn)

    n_far = jnp.maximum(i - 1, 0)
    n_pairs = (n_far + 1) // 2
    pair_start = lambda p: jnp.maximum(2 * p - n_far % 2, 0) * qb
    last_pair = jnp.maximum(n_pairs - 1, 0)
    sc_ref[...] = scores(sel, qa, 0, 2 * qb, jnp.where(row2 < (2 - n_far % 2) * qb, 0.0, NEG_INF))

    def sel_body(k, carry):
        sc_odd = scores(sel, qa, pair_start(2 * k + 1), 2 * qb, None)
        update(sel, pair_start(2 * k), 2 * qb, sc_ref[...])
        sc_ref[...] = scores(sel, qa, pair_start(jnp.minimum(2 * k + 2, last_pair)), 2 * qb, None)
        update(sel, pair_start(2 * k + 1), 2 * qb, sc_odd)
        return carry

    lax.fori_loop(0, n_pairs // 2, sel_body, 0)

    @pl.when(n_pairs % 2 == 1)
    def _():
        update(sel, pair_start(n_pairs - 1), 2 * qb, sc_ref[...])

    def finish(acc_ref):
        acc = acc_ref[...]
        return acc[VT_PAD:] / acc[0:1]

    o_slc = finish(accs_ref)
    o_win = finish(accw_ref)

    gt_ref[...] = jax.nn.sigmoid(gate_ref[0]).T
    outs = []
    for hh in range(hpg):
        sl = slice(hh * qb, (hh + 1) * qb)
        base = (g * hpg + hh) * 3
        gate = [gt_ref[pl.ds(base + br, 1), :] for br in range(3)]
        outs.append(gate[0] * o_cmp[dh:, sl] + gate[1] * o_slc[:, sl] + gate[2] * o_win[:, sl])
    out = jnp.concatenate(outs, axis=0).T
    o_ref[0] = (out * _silu(z_ref[0])).astype(o_ref.dtype)


def _nsa_attn(proj, kv_cmp, near, cmp_bias, qconst, ovl):
    bsz, t, _ = proj.shape
    qb, hpg = NSA_QTILE, NSA_HPG
    gw = hpg * NSA_HEAD_DIM
    nb = t // NSA_CMP_STRIDE
    cols = hpg * qb
    return pl.pallas_call(
        _nsa_attn_kernel,
        grid=(bsz, NSA_GROUPS, t // qb),
        in_specs=[
            pl.BlockSpec((1, qb, gw), lambda b, g, i: (b, i, NSA_COL_Q // gw + g)),
            pl.BlockSpec((1, t, LANES), lambda b, g, i: (b, 0, NSA_COL_SEL // LANES + g)),
            pl.BlockSpec((1, t, LANES), lambda b, g, i: (b, 0, NSA_COL_WIN // LANES + g)),
            pl.BlockSpec((1, 1, nb, LANES), lambda b, g, i: (b, g, 0, 0)),
            pl.BlockSpec((1, qb, LANES), lambda b, g, i: (b, i, NSA_COL_GATE // LANES)),
            pl.BlockSpec((1, qb, gw), lambda b, g, i: (b, i, NSA_COL_Z // gw + g)),
            pl.BlockSpec((1, 1, nb, cols), lambda b, g, i: (i, g, 0, 0)),
            pl.BlockSpec((1, 2, 2 * qb, cols), lambda b, g, i: (g, 0, 0, 0)),
            pl.BlockSpec((1, 8, cols), lambda b, g, i: (g, 0, 0)),
            pl.BlockSpec((32, nb), lambda b, g, i: (0, 0)),
        ],
        out_specs=pl.BlockSpec((1, qb, gw), lambda b, g, i: (b, i, g)),
        out_shape=jax.ShapeDtypeStruct((bsz, t, NSA_Q_W), BF16),
        scratch_shapes=[
            pltpu.VMEM((t, LANES), BF16), pltpu.VMEM((VT_PAD + NSA_HEAD_DIM, t), BF16),
            pltpu.VMEM((t, LANES), BF16), pltpu.VMEM((VT_PAD + NSA_HEAD_DIM, t), BF16),
            pltpu.VMEM((LANES, qb), F32),
            pltpu.VMEM((1, cols), F32), pltpu.VMEM((VT_PAD + NSA_HEAD_DIM, cols), F32),
            pltpu.VMEM((1, cols), F32), pltpu.VMEM((VT_PAD + NSA_HEAD_DIM, cols), F32),
            pltpu.VMEM((2 * qb, cols), F32),
        ],
        compiler_params=pltpu.CompilerParams(
            dimension_semantics=("parallel", "parallel", "arbitrary"), vmem_limit_bytes=VMEM_LIMIT),
        name="nsa_attention",
    )(proj, proj, proj, kv_cmp, proj, proj, cmp_bias, near, qconst, ovl)


def _nsa_layer(x, mod, norm_g, w_in, cmp_pos, cmp_w1, cmp_w2, rel_bias, w_out, final_g):
    t = x.shape[1]
    assert t // NSA_SLC_LEN <= 32 and NSA_COL_Z % (NSA_HPG * NSA_HEAD_DIM) == 0
    perm = _nsa_column_perm()
    cuts = [0] + [j for j in range(1, len(perm)) if perm[j] != perm[j - 1] + (perm[j - 1] >= 0)] + [len(perm)]
    runs = [(int(perm[a]), b - a) for a, b in zip(cuts[:-1], cuts[1:])]
    w16 = w_in.astype(BF16)
    w_in_p = jnp.concatenate([w16[:, s:s + n] if s >= 0 else jnp.zeros((w_in.shape[0], n), BF16)
                              for s, n in runs], axis=1)
    proj = _inproj(x, norm_g, mod, w_in_p, tm=NSA_INPROJ_ROWS, tn=NSA_PROJ_W)
    kv_cmp = _nsa_compress(proj, cmp_pos, cmp_w1, cmp_w2)
    near, cmp_bias, qconst = _nsa_tables(rel_bias, t)
    o = _nsa_attn(proj, kv_cmp, near, cmp_bias, qconst, jnp.asarray(_overlap_t(t)))
    return _outproj(o, w_out.astype(BF16), x, mod, final_g, tm=OUTPROJ_ROWS)


def kernel(x, c, ada_w, ada_b, norm_g, gdn_w_in, gdn_conv_w, gdn_a_log, gdn_dt_bias, gdn_norm_w, gdn_w_out,
           nsa_w_in, nsa_cmp_pos, nsa_cmp_w1, nsa_cmp_w2, nsa_w_out, rel_bias, final_g):
    bsz, t, d = x.shape
    mod = _modulation(c, ada_w, ada_b).reshape(ada_w.shape[0], bsz, 3, d)
    x = _gdn_layer(x, mod[0], norm_g[0], gdn_w_in[0], gdn_conv_w[0], gdn_a_log[0], gdn_dt_bias[0],
                   gdn_norm_w[0], gdn_w_out[0])
    return _nsa_layer(x, mod[1], norm_g[1], nsa_w_in[0], nsa_cmp_pos[0], nsa_cmp_w1[0], nsa_cmp_w2[0],
                      rel_bias, nsa_w_out[0], final_g)
```

```python
import functools
import math

import numpy as np
import jax
import jax.numpy as jnp
from jax import lax
from jax.experimental import pallas as pl
from jax.experimental.pallas import tpu as pltpu

F32 = jnp.float32
BF16 = jnp.bfloat16
HIGHEST = lax.Precision.HIGHEST

EPS = 1e-6
NEG_INF = -1e30
LANES = 128
VMEM_LIMIT = 56 * 1024 * 1024

GDN_QK_HEADS = 8
GDN_V_HEADS = 16
GDN_HEAD_DIM = 128
GDN_CONV = 4
GDN_CHUNK = 64
GDN_QK_W = GDN_QK_HEADS * GDN_HEAD_DIM
GDN_V_W = GDN_V_HEADS * GDN_HEAD_DIM
GDN_CONV_W = 2 * GDN_QK_W + GDN_V_W
GDN_TILE = 128
GDN_HEADS_PER_STEP = 8
GDN_INPROJ_ROWS = 256
GDN_INPROJ_COLS = 4 * LANES
NSA_INPROJ_ROWS = 512
OUTPROJ_ROWS = 1024


def _silu(x):
    return x * jax.nn.sigmoid(x)


def _dot(a, b, **kw):
    return jnp.dot(a, b, preferred_element_type=F32, **kw)


def _dot_nt(a, b, **kw):
    return lax.dot_general(a, b, (((1,), (1,)), ((), ())), preferred_element_type=F32, **kw)


def _mod_kernel(c_ref, w_ref, b_ref, o_ref):
    cond = _silu(c_ref[...])
    o_ref[0] = _dot(cond, w_ref[0], precision=HIGHEST) + b_ref[0]


def _modulation(c, ada_w, ada_b):
    depth, d, d3 = ada_w.shape
    bsz = c.shape[0]
    return pl.pallas_call(
        _mod_kernel,
        grid=(depth, d3 // d),
        in_specs=[
            pl.BlockSpec((bsz, d), lambda i, j: (0, 0)),
            pl.BlockSpec((1, d, d), lambda i, j: (i, 0, j)),
            pl.BlockSpec((1, 1, d), lambda i, j: (i, 0, j)),
        ],
        out_specs=pl.BlockSpec((1, bsz, d), lambda i, j: (i, 0, j)),
        out_shape=jax.ShapeDtypeStruct((depth, bsz, d3), F32),
        name="adaln_mod",
    )(c, ada_w, ada_b.reshape(depth, 1, d3))


def _inproj_kernel(x_ref, g_ref, mod_ref, w_ref, o_ref, *, tn):
    x = x_ref[0]
    m = mod_ref[0]
    y = x * lax.rsqrt(jnp.mean(x * x, axis=-1, keepdims=True) + EPS) * g_ref[...]
    h = (y * (1.0 + m[1:2]) + m[0:1]).astype(BF16)
    for j in range(w_ref.shape[1] // tn):
        o_ref[0, :, j * tn:(j + 1) * tn] = _dot(h, w_ref[:, j * tn:(j + 1) * tn])


def _inproj(x, g, mod, w, *, tm, tn):
    bsz, t, d = x.shape
    n = w.shape[1]
    assert t % tm == 0 and n % tn == 0
    return pl.pallas_call(
        functools.partial(_inproj_kernel, tn=tn),
        grid=(bsz, t // tm),
        in_specs=[
            pl.BlockSpec((1, tm, d), lambda b, i: (b, i, 0)),
            pl.BlockSpec((1, d), lambda b, i: (0, 0)),
            pl.BlockSpec((1, 3, d), lambda b, i: (b, 0, 0)),
            pl.BlockSpec((d, n), lambda b, i: (0, 0), pipeline_mode=pl.Buffered(1)),
        ],
        out_specs=pl.BlockSpec((1, tm, n), lambda b, i: (b, i, 0)),
        out_shape=jax.ShapeDtypeStruct((bsz, t, n), F32),
        compiler_params=pltpu.CompilerParams(
            dimension_semantics=("parallel", "parallel"), vmem_limit_bytes=VMEM_LIMIT),
        name="norm_mod_inproj",
    )(x, g.reshape(1, d), mod, w)


def _gdn_inproj_kernel(x_ref, g_ref, mod_ref, w_ref, cw_ref, alog_ref, dtb_ref, o_ref, col_ref, row_ref,
                        ext_ref, tail_ref, *, tn):
    dh = GDN_HEAD_DIM
    tm = x_ref.shape[1]
    x = x_ref[0]
    m = mod_ref[0]
    y = x * lax.rsqrt(jnp.mean(x * x, axis=-1, keepdims=True) + EPS) * g_ref[...]
    h = (y * (1.0 + m[1:2]) + m[0:1]).astype(BF16)

    @pl.when(pl.program_id(1) == 0)
    def _():
        tail_ref[...] = jnp.zeros_like(tail_ref)

    def conv_tile(j):
        cols = slice(j * tn, (j + 1) * tn)
        ext_ref[0:8, :] = tail_ref[:, cols]
        ext_ref[8:8 + tm, :] = o_ref[0, :, cols]
        tail_ref[:, cols] = ext_ref[tm:tm + 8, :]
        w = cw_ref[:, cols]
        ext = ext_ref[...]
        acc = w[0:1] * ext
        for tap in range(1, GDN_CONV):
            acc = pltpu.roll(acc, 1, axis=0) + w[tap:tap + 1] * ext
        act = _silu(acc[8:8 + tm])
        if j * tn < 2 * GDN_QK_W:
            scale = dh ** -0.5 if j * tn < GDN_QK_W else 1.0
            heads = [act[:, c * dh:(c + 1) * dh] for c in range(tn // dh)]
            heads = [hd * (lax.rsqrt(jnp.sum(hd * hd, axis=-1, keepdims=True) + EPS) * scale) for hd in heads]
            act = jnp.concatenate(heads, axis=1)
        o_ref[0, :, cols] = act

    n = w_ref.shape[1]
    n_conv = GDN_CONV_W // tn
    bounds = [(j * tn, min((j + 1) * tn, n)) for j in range(-(-n // tn))]
    for j, (lo, hi) in enumerate(bounds):
        o_ref[0, :, lo:hi] = _dot(h, w_ref[:, lo:hi])
        if 1 <= j <= n_conv:
            conv_tile(j - 1)
    _gdn_gate_scalars(o_ref[0, :, n - LANES:n], alog_ref[...], dtb_ref[...], col_ref, row_ref)


def _gdn_inproj(x, g, mod, w, conv_w, a_log, dt_bias, *, tm, tn):
    bsz, t, d = x.shape
    n = w.shape[1]
    assert t % tm == 0 and GDN_CONV_W % tn == 0 and GDN_QK_W % tn == 0 and n > GDN_CONV_W
    pad = lambda u: jnp.zeros((1, LANES), F32).at[0, GDN_V_HEADS:2 * GDN_V_HEADS].set(u)
    ncb = tm // GDN_CHUNK
    return pl.pallas_call(
        functools.partial(_gdn_inproj_kernel, tn=tn),
        grid=(bsz, t // tm),
        in_specs=[
            pl.BlockSpec((1, tm, d), lambda b, i: (b, i, 0)),
            pl.BlockSpec((1, d), lambda b, i: (0, 0)),
            pl.BlockSpec((1, 3, d), lambda b, i: (b, 0, 0)),
            pl.BlockSpec((d, n), lambda b, i: (0, 0), pipeline_mode=pl.Buffered(1)),
            pl.BlockSpec((GDN_CONV, GDN_CONV_W), lambda b, i: (0, 0)),
            pl.BlockSpec((1, LANES), lambda b, i: (0, 0)),
            pl.BlockSpec((1, LANES), lambda b, i: (0, 0)),
        ],
        out_specs=[
            pl.BlockSpec((1, tm, n), lambda b, i: (b, i, 0)),
            pl.BlockSpec((1, tm, LANES), lambda b, i: (b, i, 0)),
            pl.BlockSpec((1, ncb, GDN_V_HEADS // 2, 2 * GDN_CHUNK), lambda b, i: (b, i, 0, 0)),
        ],
        out_shape=[
            jax.ShapeDtypeStruct((bsz, t, n), F32),
            jax.ShapeDtypeStruct((bsz, t, LANES), F32),
            jax.ShapeDtypeStruct((bsz, t // GDN_CHUNK, GDN_V_HEADS // 2, 2 * GDN_CHUNK), F32),
        ],
        scratch_shapes=[pltpu.VMEM((tm + 8, tn), F32), pltpu.VMEM((8, GDN_CONV_W), F32)],
        compiler_params=pltpu.CompilerParams(
            dimension_semantics=("parallel", "arbitrary"), vmem_limit_bytes=VMEM_LIMIT),
        name="gdn_norm_inproj_conv",
    )(x, g.reshape(1, d), mod, w, conv_w, pad(a_log), pad(dt_bias))


def _outproj_kernel(o_ref, w_ref, x_ref, mod_ref, *rest, final_norm):
    y = _dot(o_ref[0], w_ref[...])
    x = x_ref[0] + mod_ref[0][2:3] * y
    if final_norm:
        fg_ref, out_ref = rest
        x = x * lax.rsqrt(jnp.mean(x * x, axis=-1, keepdims=True) + EPS) * fg_ref[...]
    else:
        (out_ref,) = rest
    out_ref[0] = x


def _outproj(o, w, x, mod, final_g=None, *, tm):
    bsz, t, d = x.shape
    k = o.shape[-1]
    in_specs = [
        pl.BlockSpec((1, tm, k), lambda b, i: (b, i, 0)),
        pl.BlockSpec((k, d), lambda b, i: (0, 0), pipeline_mode=pl.Buffered(1)),
        pl.BlockSpec((1, tm, d), lambda b, i: (b, i, 0)),
        pl.BlockSpec((1, 3, d), lambda b, i: (b, 0, 0)),
    ]
    args = [o, w, x, mod]
    if final_g is not None:
        in_specs.append(pl.BlockSpec((1, d), lambda b, i: (0, 0)))
        args.append(final_g.reshape(1, d))
    return pl.pallas_call(
        functools.partial(_outproj_kernel, final_norm=final_g is not None),
        grid=(bsz, t // tm),
        in_specs=in_specs,
        out_specs=pl.BlockSpec((1, tm, d), lambda b, i: (b, i, 0)),
        out_shape=jax.ShapeDtypeStruct((bsz, t, d), F32),
        compiler_params=pltpu.CompilerParams(
            dimension_semantics=("parallel", "parallel"), vmem_limit_bytes=VMEM_LIMIT),
        name="outproj_residual",
    )(*args)


def _gdn_gate_scalars(ba, alog, dtb, col_ref, row_ref):
    cs, nh = GDN_CHUNK, GDN_V_HEADS
    lane = lax.broadcasted_iota(jnp.int32, ba.shape, 1)
    g = -jnp.exp(alog) * jax.nn.softplus(ba + dtb)
    vals = jnp.where(lane < nh, jax.nn.sigmoid(ba), g)
    r = lax.broadcasted_iota(jnp.int32, (cs, cs), 0)
    c = lax.broadcasted_iota(jnp.int32, (cs, cs), 1)
    tri = (r >= c).astype(F32)
    is_beta = lax.broadcasted_iota(jnp.int32, (cs, LANES), 1) < nh
    pr = lax.broadcasted_iota(jnp.int32, (nh // 2, LANES), 0)
    pc = lax.broadcasted_iota(jnp.int32, (nh // 2, LANES), 1)
    pick_even = (pc == nh + 2 * pr).astype(F32)
    for n in range(ba.shape[0] // cs):
        v = vals[n * cs:(n + 1) * cs]
        cum = _dot(tri, v, precision=HIGHEST)
        out = jnp.where(is_beta, v, cum)
        col_ref[0, n * cs:(n + 1) * cs, :] = out
        both = jnp.concatenate([out, pltpu.roll(out, LANES - 1, axis=1)], axis=0)
        row_ref[0, n] = _dot_nt(pick_even, both, precision=HIGHEST)


TRI_BASE = 8
PACK = 4


def _block_rows(p):
    n = p.shape[0]
    blk = lax.broadcasted_iota(jnp.int32, p.shape, 1) // n
    return jnp.concatenate([jnp.where(blk == j, p, 0.0) for j in range(PACK)], axis=0).astype(BF16)


def _tri_inverse_packed(mats):
    n = mats[0].shape[0]
    r = lax.broadcasted_iota(jnp.int32, mats[0].shape, 0)
    c = lax.broadcasted_iota(jnp.int32, mats[0].shape, 1) % n
    same = lambda s: (r // s) == (c // s)
    mm = lambda a, b: _dot(a.astype(BF16), _block_rows(b))
    diag = [jnp.where(same(TRI_BASE), a, 0.0) for a in mats]
    inv = [jnp.where(r == c, 1.0, 0.0) - d for d in diag]
    pw = diag
    k = 1
    while 2 * k < TRI_BASE:
        pw = [mm(m, m) for m in pw]
        inv = [p + mm(p, m) for p, m in zip(inv, pw)]
        k *= 2
        yield None
    s = TRI_BASE
    while s < n:
        sub = same(2 * s) & jnp.logical_not(same(s))
        left = [mm(p, jnp.where(sub, a, 0.0)) for p, a in zip(inv, mats)]
        inv = [p - mm(l, p) for p, l in zip(inv, left)]
        s *= 2
        yield None
    yield inv


def _gdn_chunk_kernel(q_ref, k_ref, v_ref, z_ref, col_ref, row_ref, nw_ref,
                      o_ref, s_ref, hu_ref, hwq_ref, hqk_ref, hkd_ref, hgl_ref):
    cs, dh, nh = GDN_CHUNK, GDN_HEAD_DIM, GDN_V_HEADS
    tt = q_ref.shape[1]
    hps = q_ref.shape[2] // dh
    ncb = tt // cs
    assert ncb % 2 == 0 and PACK == 4
    hg = pl.program_id(1)
    ti = pl.program_id(2)
    wslot = ti % 2
    rslot = 1 - wslot
    jidx = lambda hl, n, e: (hl * ncb + n) * 2 + e

    @pl.when(ti == 0)
    def _():
        s_ref[...] = jnp.zeros_like(s_ref)
        for h_ref in (hu_ref, hwq_ref, hqk_ref, hkd_ref, hgl_ref):
            h_ref[1] = jnp.zeros(h_ref.shape[1:], h_ref.dtype)

    chunks = [slice(n * cs, (n + 1) * cs) for n in range(ncb)]

    def prepare():
        q = [q_ref[0, :, hl * dh:(hl + 1) * dh] for hl in range(hps)]
        k = [k_ref[0, :, hl * dh:(hl + 1) * dh] for hl in range(hps)]
        v = [[v_ref[0, :, (2 * hl + e) * dh:(2 * hl + e + 1) * dh] for e in range(2)] for hl in range(hps)]
        lane = lax.broadcasted_iota(jnp.int32, (tt, LANES), 1)
        colv = col_ref[0]
        column = lambda idx: jnp.sum(jnp.where(lane == idx, colv, 0.0), axis=1, keepdims=True)
        hq = [hg * hps + hl for hl in range(hps)]
        beta = [[column(2 * h + e) for e in range(2)] for h in hq]
        gc = [[column(nh + 2 * h + e) for e in range(2)] for h in hq]
        egc = [[jnp.exp(x) for x in pair] for pair in gc]
        yield

        pr = lax.broadcasted_iota(jnp.int32, (cs, PACK * cs), 0)
        pb = lax.broadcasted_iota(jnp.int32, (cs, PACK * cs), 1) // cs
        pc = lax.broadcasted_iota(jnp.int32, (cs, PACK * cs), 1) % cs
        groups = [(hl, gi) for hl in range(hps) for gi in range(ncb // 2)]
        a_mats, g_rows = [], {}
        for hl, gi in groups:
            k16, q16 = k[hl].astype(BF16), q[hl].astype(BF16)
            pair = (chunks[2 * gi], chunks[2 * gi + 1])
            kdup = [jnp.concatenate([k16[sl], k16[sl]], axis=0) for sl in pair]
            kk = jnp.concatenate([_dot_nt(k16[sl], kd) for sl, kd in zip(pair, kdup)], axis=1)
            qk = jnp.concatenate([_dot_nt(q16[sl], kd) for sl, kd in zip(pair, kdup)], axis=1)
            pick = lambda cols: jnp.where(pb == 0, cols[0][pair[0]], jnp.where(
                pb == 1, cols[1][pair[0]], jnp.where(pb == 2, cols[0][pair[1]], cols[1][pair[1]])))
            g_row = jnp.concatenate([row_ref[0, 2 * gi + j, pl.ds(hq[hl], 1), :] for j in range(2)], axis=1)
            decay = jnp.exp(jnp.where(pr >= pc, pick(gc[hl]) - g_row, -jnp.inf))
            a_mats.append(jnp.where(pr > pc, pick(beta[hl]) * kk * decay, 0.0))
            hqk_ref[wslot, hl * (ncb // 2) + gi] = (qk * decay).astype(BF16)
            g_rows[hl, gi] = g_row
            yield
        t_mats = None
        for t_mats in _tri_inverse_packed(a_mats):
            yield

        for (hl, gi), t_mat in zip(groups, t_mats):
            rhs = []
            for n in (2 * gi, 2 * gi + 1):
                sl = chunks[n]
                for e in range(2):
                    j = 2 * (n % 2) + e
                    g_last = g_rows[hl, gi][:, j * cs + cs - 1:(j + 1) * cs]
                    hgl_ref[wslot, jidx(hl, n, e)] = jnp.broadcast_to(jnp.exp(g_last), (1, LANES))
                    hkd_ref[wslot, jidx(hl, n, e)] = (k[hl][sl] * jnp.exp(g_last - gc[hl][e][sl])).T.astype(BF16)
                    kbeta = k[hl][sl] * beta[hl][e][sl]
                    rhs.append(jnp.concatenate([v[hl][e][sl] * beta[hl][e][sl], kbeta * egc[hl][e][sl]], axis=1))
            out = _dot(_block_rows(t_mat), jnp.concatenate(rhs, axis=0).astype(BF16))
            for n in (2 * gi, 2 * gi + 1):
                for e in range(2):
                    uw = out[(2 * (n % 2) + e) * cs:(2 * (n % 2) + e + 1) * cs]
                    hu_ref[wslot, jidx(hl, n, e)] = uw[:, :dh]
                    q_dec = q[hl][chunks[n]] * egc[hl][e][chunks[n]]
                    hwq_ref[wslot, jidx(hl, n, e)] = jnp.concatenate([uw[:, dh:], q_dec], axis=0).astype(BF16)
            yield

    def recurrence():
        nw = nw_ref[...]
        heads = [(hl, e) for hl in range(hps) for e in range(2)]
        state = {he: s_ref[2 * he[0] + he[1]] for he in heads}
        for n, sl in enumerate(chunks):
            ws = {}
            for hl, e in heads:
                ws[hl, e] = _dot(hwq_ref[rslot, jidx(hl, n, e)], state[hl, e].astype(BF16))
            yield
            for hl, e in heads:
                j = jidx(hl, n, e)
                v16 = (hu_ref[rslot, j] - ws[hl, e][:cs]).astype(BF16)
                lb = 2 * (n % 2) + e
                qkd = hqk_ref[rslot, hl * (ncb // 2) + n // 2][:, lb * cs:(lb + 1) * cs]
                o = ws[hl, e][cs:] + _dot(qkd, v16)
                state[hl, e] = state[hl, e] * hgl_ref[rslot, j] + _dot(hkd_ref[rslot, j], v16)
                o = o * lax.rsqrt(jnp.mean(o * o, axis=-1, keepdims=True) + EPS) * nw
                lanes = slice((2 * hl + e) * dh, (2 * hl + e + 1) * dh)
                o_ref[0, sl, lanes] = (o * _silu(z_ref[0, sl, lanes])).astype(o_ref.dtype)
            yield
        for (hl, e), st in state.items():
            s_ref[2 * hl + e] = st

    prep, rec = prepare(), recurrence()
    n_prep = 1 + hps * (ncb // 2) * 2 + 6
    per_stage = -(-n_prep // (2 * ncb))
    prep_live = rec_live = True
    while prep_live or rec_live:
        for _ in range(per_stage):
            if prep_live:
                prep_live = next(prep, "done") != "done"
        if rec_live:
            rec_live = next(rec, "done") != "done"


def _gdn_chunk(proj, col, row, norm_w, *, tt=GDN_TILE, hps=GDN_HEADS_PER_STEP):
    bsz, t, _ = proj.shape
    dh = GDN_HEAD_DIM
    qw, vw = hps * dh, 2 * hps * dh
    k_blk0 = GDN_QK_W // qw
    v_blk0 = 2 * GDN_QK_W // vw
    z_blk0 = GDN_CONV_W // vw
    ncb = tt // GDN_CHUNK
    nt = t // tt
    nj = hps * ncb * 2
    cur = lambda i: jnp.minimum(i, nt - 1)
    prev = lambda i: jnp.maximum(i - 1, 0)
    return pl.pallas_call(
        _gdn_chunk_kernel,
        grid=(bsz, GDN_QK_HEADS // hps, nt + 1),
        in_specs=[
            pl.BlockSpec((1, tt, qw), lambda b, h, i: (b, cur(i), h)),
            pl.BlockSpec((1, tt, qw), lambda b, h, i: (b, cur(i), k_blk0 + h)),
            pl.BlockSpec((1, tt, vw), lambda b, h, i: (b, cur(i), v_blk0 + h)),
            pl.BlockSpec((1, tt, vw), lambda b, h, i: (b, prev(i), z_blk0 + h)),
            pl.BlockSpec((1, tt, LANES), lambda b, h, i: (b, cur(i), 0)),
            pl.BlockSpec((1, ncb, GDN_V_HEADS // 2, 2 * GDN_CHUNK), lambda b, h, i: (b, cur(i), 0, 0)),
            pl.BlockSpec((1, dh), lambda b, h, i: (0, 0)),
        ],
        out_specs=pl.BlockSpec((1, tt, vw), lambda b, h, i: (b, prev(i), h)),
        out_shape=jax.ShapeDtypeStruct((bsz, t, GDN_V_W), BF16),
        scratch_shapes=[
            pltpu.VMEM((2 * hps, dh, dh), F32),
            pltpu.VMEM((2, nj, GDN_CHUNK, dh), F32),
            pltpu.VMEM((2, nj, 2 * GDN_CHUNK, dh), BF16),
            pltpu.VMEM((2, nj // PACK, GDN_CHUNK, PACK * GDN_CHUNK), BF16),
            pltpu.VMEM((2, nj, dh, GDN_CHUNK), BF16),
            pltpu.VMEM((2, nj, 1, LANES), F32),
        ],
        compiler_params=pltpu.CompilerParams(
            dimension_semantics=("parallel", "parallel", "arbitrary"), vmem_limit_bytes=VMEM_LIMIT),
        name="gdn_chunk_scan",
    )(proj, proj, proj, proj, col, row, norm_w.reshape(1, dh))


def _gdn_layer(x, mod, norm_g, w_in, conv_w, a_log, dt_bias, norm_w, w_out):
    n_in = w_in.shape[1]
    n_pad = -(-n_in // LANES) * LANES
    w_in_p = jnp.pad(w_in, ((0, 0), (0, n_pad - n_in))).astype(BF16)
    proj, col, row = _gdn_inproj(x, norm_g, mod, w_in_p, conv_w, a_log, dt_bias,
                                 tm=GDN_INPROJ_ROWS, tn=GDN_INPROJ_COLS)
    o = _gdn_chunk(proj, col, row, norm_w)
    return _outproj(o, w_out.astype(BF16), x, mod, tm=OUTPROJ_ROWS)


NSA_HEADS = 16
NSA_GROUPS = 4
NSA_HPG = NSA_HEADS // NSA_GROUPS
NSA_HEAD_DIM = 64
NSA_CMP_LEN = 32
NSA_CMP_STRIDE = 16
NSA_SLC_LEN = 64
NSA_TOP_K = 8
NSA_WINDOW = 512
NSA_QTILE = 256
NSA_Q_W = NSA_HEADS * NSA_HEAD_DIM
NSA_KV_W = NSA_GROUPS * NSA_HEAD_DIM
REL_BUCKETS = 32
REL_MAX_DIST = 128
FEAT_LANE0 = NSA_HEAD_DIM
CONST_LANE0 = FEAT_LANE0 + 32
VT_PAD = 16
NSA_COL_Q = 0
NSA_COL_CMP = NSA_Q_W
NSA_COL_SEL = NSA_COL_CMP + 2 * NSA_KV_W
NSA_COL_WIN = NSA_COL_SEL + 2 * NSA_KV_W
NSA_COL_Z = NSA_COL_WIN + 2 * NSA_KV_W
NSA_COL_GATE = NSA_COL_Z + NSA_Q_W
NSA_PROJ_W = NSA_COL_GATE + LANES


def _nsa_column_perm():
    g, dh = NSA_GROUPS, NSA_HEAD_DIM
    kv0 = NSA_Q_W
    cols = list(range(NSA_Q_W))
    cols += [kv0 + i for i in range(2 * NSA_KV_W)]
    for br in (1, 2):
        for gi in range(g):
            cols += [kv0 + (2 * br) * NSA_KV_W + gi * dh + d for d in range(dh)]
            cols += [kv0 + (2 * br + 1) * NSA_KV_W + gi * dh + d for d in range(dh)]
    gate0 = kv0 + 6 * NSA_KV_W
    cols += [gate0 + 3 * NSA_HEADS + i for i in range(NSA_Q_W)]
    cols += [gate0 + i for i in range(3 * NSA_HEADS)] + [-1] * (LANES - 3 * NSA_HEADS)
    assert len(cols) == NSA_PROJ_W
    return np.asarray(cols, np.int32)


def _rel_bucket_table(n):
    d = np.arange(n)
    max_exact = REL_BUCKETS // 2
    nf = np.maximum(d, 1).astype(np.float64)
    large = max_exact + (np.log(nf / max_exact) / math.log(REL_MAX_DIST / max_exact)
                         * (REL_BUCKETS - max_exact)).astype(np.int32)
    large = np.minimum(large, REL_BUCKETS - 1)
    return np.where(d < max_exact, d, large).astype(np.int32)


def _nsa_tables(rel_bias, t):
    qb = NSA_QTILE
    bucket = _rel_bucket_table(t)
    assert np.all(bucket[qb + 1:] == REL_BUCKETS - 1)
    bvec = rel_bias[bucket].T
    far = rel_bias[REL_BUCKETS - 1]
    far_hi = far.astype(BF16)
    far_lo = (far - far_hi.astype(F32)).astype(BF16)
    far_sum = far_hi.astype(F32) + far_lo.astype(F32)
    r = np.arange(qb)[:, None]
    c = np.arange(qb)[None, :]
    d0 = r - c

    def toeplitz(w):
        n = 2 * qb - 1
        ext = jnp.pad(w[:, ::-1], ((0, 0), (0, 1)))
        skew = jnp.tile(ext, (1, qb))[:, :qb * n].reshape(w.shape[0], qb, n)
        return skew[:, :, qb - 1:]

    rel = bvec[:, :2 * qb] - far_sum[:, None]
    t0 = toeplitz(jnp.concatenate([jnp.full((NSA_HEADS, qb - 1), NEG_INF, F32), rel[:, :qb]], axis=1))
    t1 = toeplitz(rel[:, 1:])
    g, hpg = NSA_GROUPS, NSA_HPG
    none = jnp.full_like(t0, NEG_INF)
    near = jnp.stack([t1, t0, t0, none], axis=1).reshape(g, hpg, 2, 2, qb, qb)
    near = near.transpose(0, 2, 3, 5, 1, 4).reshape(g, 2, 2 * qb, hpg * qb)
    nb = t // NSA_CMP_STRIDE
    per_tile = qb // NSA_CMP_STRIDE
    back = 9
    far_d = back * NSA_CMP_STRIDE - (NSA_CMP_LEN - 1)
    assert np.all(bucket[far_d:] == REL_BUCKETS - 1)
    width = (qb - 1 + far_d) // NSA_CMP_STRIDE + 1
    dm = r - NSA_CMP_STRIDE * np.arange(width)[None, :] + far_d
    band = jnp.where(dm >= 0, bvec[:, np.maximum(dm, 0)], NEG_INF)
    tiles = []
    for i in range(t // qb):
        j0 = per_tile * i - back
        lo, hi = max(j0, 0), min(j0 + width, nb)
        tiles.append(jnp.concatenate([
            jnp.broadcast_to(far[:, None, None], (NSA_HEADS, qb, lo)),
            band[:, :, lo - j0:hi - j0],
            jnp.full((NSA_HEADS, qb, nb - hi), NEG_INF, F32)], axis=2))
    cmp_bias = jnp.stack(tiles, axis=0).reshape(t // qb, g, hpg, qb, nb).transpose(0, 1, 4, 2, 3)
    cmp_bias = cmp_bias.reshape(t // qb, g, nb, hpg * qb)
    qconst = jnp.zeros((g, 8, hpg, qb), F32)
    qconst = qconst.at[:, 0].set(jnp.broadcast_to(far_hi.astype(F32).reshape(g, hpg, 1), (g, hpg, qb)))
    qconst = qconst.at[:, 1].set(jnp.broadcast_to(far_lo.astype(F32).reshape(g, hpg, 1), (g, hpg, qb)))
    return near, cmp_bias, qconst.reshape(g, 8, hpg * qb)


def _overlap_t(t):
    n_cmp = (t - NSA_CMP_LEN) // NSA_CMP_STRIDE + 1
    n_slc = t // NSA_SLC_LEN
    c_start = np.arange(n_cmp)[:, None] * NSA_CMP_STRIDE
    s_start = np.arange(n_slc)[None, :] * NSA_SLC_LEN
    ov = np.clip(np.minimum(c_start + NSA_CMP_LEN, s_start + NSA_SLC_LEN) - np.maximum(c_start, s_start), 0, None)
    ov = ov.astype(np.float32) / NSA_CMP_LEN
    out = np.zeros((32, t // NSA_CMP_STRIDE), np.float32)
    out[:n_slc, :n_cmp] = ov.T
    return out


def _nsa_compress_kernel(x_ref, pos_ref, w1_ref, w2_ref, o_ref, xs_ref):
    t = x_ref.shape[1]
    nb = t // NSA_CMP_STRIDE
    nlt = xs_ref.shape[0]
    for c in range(nlt):
        xs_ref[c, 0:t, :] = x_ref[0, :, c * LANES:(c + 1) * LANES]
        xs_ref[c, t:t + NSA_CMP_STRIDE, :] = jnp.zeros((NSA_CMP_STRIDE, LANES), F32)
    acc = jnp.zeros((nb, w1_ref.shape[2]), F32)
    for l in range(NSA_CMP_LEN):
        xl = jnp.concatenate([xs_ref[c, pl.ds(l, nb, stride=NSA_CMP_STRIDE), :] for c in range(nlt)], axis=1)
        xl = xl + pos_ref[l:l + 1, :]
        acc = acc + _dot(xl.astype(BF16), w1_ref[l])
    hid = _silu(acc).astype(BF16)
    res = _dot(hid, w2_ref[...])
    for g in range(NSA_GROUPS):
        o_ref[0, g] = res[:, g * LANES:(g + 1) * LANES]


def _nsa_compress(proj, cmp_pos, cmp_w1, cmp_w2):
    bsz, t, _ = proj.shape
    g, dh = NSA_GROUPS, NSA_HEAD_DIM
    nb = t // NSA_CMP_STRIDE
    w = 2 * NSA_KV_W
    w1 = cmp_w1.reshape(2, NSA_CMP_LEN, dh, dh).astype(BF16)
    w2 = cmp_w2.astype(BF16)
    place = lambda blk, c0: jnp.pad(blk, [(0, 0)] * (blk.ndim - 1) + [(c0, w - dh - c0)])
    w1c = jnp.concatenate([place(w1[i], (gi * 2 + i) * dh) for i in range(2) for gi in range(g)], axis=1)
    w2c = jnp.concatenate([place(w2[i], (gi * 2 + i) * dh) for gi in range(g) for i in range(2)], axis=0)
    pos = jnp.broadcast_to(cmp_pos[:, :, None, :], (2, NSA_CMP_LEN, g, dh)).transpose(1, 0, 2, 3).reshape(NSA_CMP_LEN, w)
    return pl.pallas_call(
        _nsa_compress_kernel,
        grid=(bsz,),
        in_specs=[
            pl.BlockSpec((1, t, w), lambda b: (b, 0, NSA_COL_CMP // w)),
            pl.BlockSpec((NSA_CMP_LEN, w), lambda b: (0, 0)),
            pl.BlockSpec((NSA_CMP_LEN, w, w), lambda b: (0, 0, 0), pipeline_mode=pl.Buffered(1)),
            pl.BlockSpec((w, w), lambda b: (0, 0)),
        ],
        out_specs=pl.BlockSpec((1, g, nb, LANES), lambda b: (b, 0, 0, 0)),
        out_shape=jax.ShapeDtypeStruct((bsz, g, nb, LANES), F32),
        scratch_shapes=[pltpu.VMEM((w // LANES, t + NSA_CMP_STRIDE, LANES), F32)],
        compiler_params=pltpu.CompilerParams(dimension_semantics=("parallel",), vmem_limit_bytes=VMEM_LIMIT),
        name="nsa_compress",
    )(proj, pos, w1c, w2c)


def _nsa_attn_kernel(q_ref, kvs_ref, kvw_ref, kvc_ref, gate_ref, z_ref, cb_ref, near_ref, qc_ref, ovl_ref,
                     o_ref, ks_ref, vs_ref, kw_ref, vw_ref, gt_ref, ms_ref, accs_ref, mw_ref, accw_ref, sc_ref):
    qb, dh, hpg = NSA_QTILE, NSA_HEAD_DIM, NSA_HPG
    t = kvs_ref.shape[1]
    nblk = t // NSA_SLC_LEN
    cols = hpg * qb
    g = pl.program_id(1)
    i = pl.program_id(2)

    @pl.when(i == 0)
    def _():
        tok = lax.broadcasted_iota(jnp.int32, (t, LANES), 0)
        ln = lax.broadcasted_iota(jnp.int32, (t, LANES), 1)
        const = jnp.where((ln == CONST_LANE0) | (ln == CONST_LANE0 + 1), 1.0, 0.0)
        onehot = jnp.where(ln - FEAT_LANE0 == tok // NSA_SLC_LEN, 1.0, 0.0)
        ones_rows = jnp.where(lax.broadcasted_iota(jnp.int32, (VT_PAD, t), 0) == 0, 1.0, 0.0)
        kvs = kvs_ref[0]
        kvw = kvw_ref[0]
        ks_ref[...] = jnp.where(ln < dh, kvs, onehot + const).astype(BF16)
        kw_ref[...] = jnp.where(ln < dh, kvw, const).astype(BF16)
        vs_ref[...] = jnp.concatenate([ones_rows, kvs.T[dh:]], axis=0).astype(BF16)
        vw_ref[...] = jnp.concatenate([ones_rows, kvw.T[dh:]], axis=0).astype(BF16)

    q_t = (q_ref[0] * (dh ** -0.5)).T
    q_heads = jnp.concatenate([q_t[hh * dh:(hh + 1) * dh] for hh in range(hpg)], axis=1)

    def scores(branch, qa_t, start, nk, bias):
        sc = _dot(branch[0][pl.ds(pl.multiple_of(start, qb), nk), :], qa_t)
        return sc if bias is None else sc + bias

    def update(branch, start, nk, sc):
        _, vt_ref, m_ref, acc_ref = branch
        m_old = m_ref[...]
        m_new = jnp.maximum(m_old, jnp.max(sc, axis=0, keepdims=True))
        alpha = jnp.exp(m_old - m_new)
        pe = jnp.exp(sc - m_new).astype(BF16)
        acc_ref[...] = alpha * acc_ref[...] + _dot(vt_ref[:, pl.ds(pl.multiple_of(start, qb), nk)], pe)
        m_ref[...] = m_new

    sel = (ks_ref, vs_ref, ms_ref, accs_ref)
    win = (kw_ref, vw_ref, mw_ref, accw_ref)
    for m_ref, acc_ref in ((ms_ref, accs_ref), (mw_ref, accw_ref)):
        m_ref[...] = jnp.full(m_ref.shape, NEG_INF, F32)
        acc_ref[...] = jnp.zeros(acc_ref.shape, F32)

    nwt = NSA_WINDOW // qb
    assert nwt in (2, 4)
    pad_rows = jnp.zeros((LANES - CONST_LANE0 - 8, cols), F32)
    qa_win = jnp.concatenate([q_heads, jnp.zeros((32, cols), F32), qc_ref[0], pad_rows], axis=0).astype(BF16)

    kvc = kvc_ref[0, 0]
    lane_k = lax.broadcasted_iota(jnp.int32, kvc.shape, 1)
    kc16 = jnp.where(lane_k < dh, kvc, 0.0).astype(BF16)
    s = _dot(kc16, qa_win) + cb_ref[0, 0]

    kk = lax.broadcasted_iota(jnp.int32, (qb, cols), 0)
    rr = lax.broadcasted_iota(jnp.int32, (qb, cols), 1) % qb
    wold_start = jnp.maximum(i - nwt, 0) * qb
    sc_wold = scores(win, qa_win, wold_start, qb, jnp.where((rr < kk) & (i >= nwt), 0.0, NEG_INF))

    row2 = lax.broadcasted_iota(jnp.int32, (2 * qb, cols), 0)
    if nwt == 4:
        w32_start = jnp.maximum(i - 3, 0) * qb
        sc_w32 = scores(win, qa_win, w32_start, 2 * qb,
                        jnp.where(row2 < (i - 1) * qb - w32_start, 0.0, NEG_INF))
    near_start = jnp.maximum(i - 1, 0) * qb
    near_bias = near_ref[0, jnp.where(i == 0, 1, 0)]
    sc_wn = scores(win, qa_win, near_start, 2 * qb, near_bias)

    s = jnp.exp(s - jnp.max(s, axis=0, keepdims=True))
    p = s / jnp.sum(s, axis=0, keepdims=True)
    tq_lane = i * qb + lax.broadcasted_iota(jnp.int32, (1, cols), 1) % qb
    p16 = (p * (tq_lane >= NSA_CMP_LEN - 1).astype(F32)).astype(BF16)
    o_cmp = _dot(kvc.T.astype(BF16), p16)
    ovl = ovl_ref[...].astype(BF16)
    imp = _dot(ovl, p16[:, 0:qb])
    for hh in range(1, hpg):
        imp = imp + _dot(ovl, p16[:, hh * qb:(hh + 1) * qb])

    update(win, wold_start, qb, sc_wold)

    blk = lax.broadcasted_iota(jnp.int32, (32, qb), 0)
    tq = i * qb + lax.broadcasted_iota(jnp.int32, (32, qb), 1)
    cur = tq // NSA_SLC_LEN
    forced = (blk == 0) | (blk == cur) | (blk == cur - 1)
    val = jnp.where(forced, jnp.inf, jnp.where(blk * NSA_SLC_LEN <= tq, imp, -jnp.inf))
    cnt = jnp.zeros((32, qb), jnp.int32)
    for s2 in range(nblk):
        other = val[s2:s2 + 1, :]
        cnt = cnt + ((other > val) | ((other == val) & (s2 < blk))).astype(jnp.int32)
    feat = jnp.where((cnt < min(NSA_TOP_K, nblk)) & (blk < nblk), 0.0, NEG_INF)
    qa = jnp.concatenate([q_heads, jnp.concatenate([feat] * hpg, axis=1), qc_ref[0], pad_rows],
                         axis=0).astype(BF16)

    if nwt == 4:
        update(win, w32_start, 2 * qb, sc_w32)
    sc_sn = scores(sel, qa, near_start, 2 * qb, near_bias)
    update(win, near_start, 2 * qb, sc_wn)
    update(sel, near_start, 2 * qb, sc_sn)

    n_far = jnp.maximum(i - 1, 0)
    n_pairs = (n_far + 1) // 2
    pair_start = lambda p: jnp.maximum(2 * p - n_far % 2, 0) * qb
    last_pair = jnp.maximum(n_pairs - 1, 0)
    sc_ref[...] = scores(sel, qa, 0, 2 * qb, jnp.where(row2 < (2 - n_far % 2) * qb, 0.0, NEG_INF))

    def sel_body(k, carry):
        sc_odd = scores(sel, qa, pair_start(2 * k + 1), 2 * qb, None)
        update(sel, pair_start(2 * k), 2 * qb, sc_ref[...])
        sc_ref[...] = scores(sel, qa, pair_start(jnp.minimum(2 * k + 2, last_pair)), 2 * qb, None)
        update(sel, pair_start(2 * k + 1), 2 * qb, sc_odd)
        return carry

    lax.fori_loop(0, n_pairs // 2, sel_body, 0)

    @pl.when(n_pairs % 2 == 1)
    def _():
        update(sel, pair_start(n_pairs - 1), 2 * qb, sc_ref[...])

    def finish(acc_ref):
        acc = acc_ref[...]
        return acc[VT_PAD:] / acc[0:1]

    o_slc = finish(accs_ref)
    o_win = finish(accw_ref)

    gt_ref[...] = jax.nn.sigmoid(gate_ref[0]).T
    outs = []
    for hh in range(hpg):
        sl = slice(hh * qb, (hh + 1) * qb)
        base = (g * hpg + hh) * 3
        gate = [gt_ref[pl.ds(base + br, 1), :] for br in range(3)]
        outs.append(gate[0] * o_cmp[dh:, sl] + gate[1] * o_slc[:, sl] + gate[2] * o_win[:, sl])
    out = jnp.concatenate(outs, axis=0).T
    o_ref[0] = (out * _silu(z_ref[0])).astype(o_ref.dtype)


def _nsa_attn(proj, kv_cmp, near, cmp_bias, qconst, ovl):
    bsz, t, _ = proj.shape
    qb, hpg = NSA_QTILE, NSA_HPG
    gw = hpg * NSA_HEAD_DIM
    nb = t // NSA_CMP_STRIDE
    cols = hpg * qb
    return pl.pallas_call(
        _nsa_attn_kernel,
        grid=(bsz, NSA_GROUPS, t // qb),
        in_specs=[
            pl.BlockSpec((1, qb, gw), lambda b, g, i: (b, i, NSA_COL_Q // gw + g)),
            pl.BlockSpec((1, t, LANES), lambda b, g, i: (b, 0, NSA_COL_SEL // LANES + g)),
            pl.BlockSpec((1, t, LANES), lambda b, g, i: (b, 0, NSA_COL_WIN // LANES + g)),
            pl.BlockSpec((1, 1, nb, LANES), lambda b, g, i: (b, g, 0, 0)),
            pl.BlockSpec((1, qb, LANES), lambda b, g, i: (b, i, NSA_COL_GATE // LANES)),
            pl.BlockSpec((1, qb, gw), lambda b, g, i: (b, i, NSA_COL_Z // gw + g)),
            pl.BlockSpec((1, 1, nb, cols), lambda b, g, i: (i, g, 0, 0)),
            pl.BlockSpec((1, 2, 2 * qb, cols), lambda b, g, i: (g, 0, 0, 0)),
            pl.BlockSpec((1, 8, cols), lambda b, g, i: (g, 0, 0)),
            pl.BlockSpec((32, nb), lambda b, g, i: (0, 0)),
        ],
        out_specs=pl.BlockSpec((1, qb, gw), lambda b, g, i: (b, i, g)),
        out_shape=jax.ShapeDtypeStruct((bsz, t, NSA_Q_W), BF16),
        scratch_shapes=[
            pltpu.VMEM((t, LANES), BF16), pltpu.VMEM((VT_PAD + NSA_HEAD_DIM, t), BF16),
            pltpu.VMEM((t, LANES), BF16), pltpu.VMEM((VT_PAD + NSA_HEAD_DIM, t), BF16),
            pltpu.VMEM((LANES, qb), F32),
            pltpu.VMEM((1, cols), F32), pltpu.VMEM((VT_PAD + NSA_HEAD_DIM, cols), F32),
            pltpu.VMEM((1, cols), F32), pltpu.VMEM((VT_PAD + NSA_HEAD_DIM, cols), F32),
            pltpu.VMEM((2 * qb, cols), F32),
        ],
        compiler_params=pltpu.CompilerParams(
            dimension_semantics=("parallel", "parallel", "arbitrary"), vmem_limit_bytes=VMEM_LIMIT),
        name="nsa_attention",
    )(proj, proj, proj, kv_cmp, proj, proj, cmp_bias, near, qconst, ovl)


def _nsa_layer(x, mod, norm_g, w_in, cmp_pos, cmp_w1, cmp_w2, rel_bias, w_out, final_g):
    t = x.shape[1]
    assert t // NSA_SLC_LEN <= 32 and NSA_COL_Z % (NSA_HPG * NSA_HEAD_DIM) == 0
    perm = _nsa_column_perm()
    cuts = [0] + [j for j in range(1, len(perm)) if perm[j] != perm[j - 1] + (perm[j - 1] >= 0)] + [len(perm)]
    runs = [(int(perm[a]), b - a) for a, b in zip(cuts[:-1], cuts[1:])]
    w16 = w_in.astype(BF16)
    w_in_p = jnp.concatenate([w16[:, s:s + n] if s >= 0 else jnp.zeros((w_in.shape[0], n), BF16)
                              for s, n in runs], axis=1)
    proj = _inproj(x, norm_g, mod, w_in_p, tm=NSA_INPROJ_ROWS, tn=NSA_PROJ_W)
    kv_cmp = _nsa_compress(proj, cmp_pos, cmp_w1, cmp_w2)
    near, cmp_bias, qconst = _nsa_tables(rel_bias, t)
    o = _nsa_attn(proj, kv_cmp, near, cmp_bias, qconst, jnp.asarray(_overlap_t(t)))
    return _outproj(o, w_out.astype(BF16), x, mod, final_g, tm=OUTPROJ_ROWS)


def kernel(x, c, ada_w, ada_b, norm_g, gdn_w_in, gdn_conv_w, gdn_a_log, gdn_dt_bias, gdn_norm_w, gdn_w_out,
           nsa_w_in, nsa_cmp_pos, nsa_cmp_w1, nsa_cmp_w2, nsa_w_out, rel_bias, final_g):
    bsz, t, d = x.shape
    mod = _modulation(c, ada_w, ada_b).reshape(ada_w.shape[0], bsz, 3, d)
    x = _gdn_layer(x, mod[0], norm_g[0], gdn_w_in[0], gdn_conv_w[0], gdn_a_log[0], gdn_dt_bias[0],
                   gdn_norm_w[0], gdn_w_out[0])
    return _nsa_layer(x, mod[1], norm_g[1], nsa_w_in[0], nsa_cmp_pos[0], nsa_cmp_w1[0], nsa_cmp_w2[0],
                      rel_bias, nsa_w_out[0], final_g)
```

```python
import functools
import math

import numpy as np
import jax
import jax.numpy as jnp
from jax import lax
from jax.experimental import pallas as pl
from jax.experimental.pallas import tpu as pltpu

F32 = jnp.float32
BF16 = jnp.bfloat16
HIGHEST = lax.Precision.HIGHEST

EPS = 1e-6
NEG_INF = -1e30
LANES = 128
VMEM_LIMIT = 56 * 1024 * 1024

GDN_QK_HEADS = 8
GDN_V_HEADS = 16
GDN_HEAD_DIM = 128
GDN_CONV = 4
GDN_CHUNK = 64
GDN_QK_W = GDN_QK_HEADS * GDN_HEAD_DIM
GDN_V_W = GDN_V_HEADS * GDN_HEAD_DIM
GDN_CONV_W = 2 * GDN_QK_W + GDN_V_W
GDN_TILE = 128
GDN_HEADS_PER_STEP = 8
GDN_INPROJ_ROWS = 256
GDN_INPROJ_COLS = 4 * LANES
NSA_INPROJ_ROWS = 512
OUTPROJ_ROWS = 1024


def _silu(x):
    return x * jax.nn.sigmoid(x)


def _dot(a, b, **kw):
    return jnp.dot(a, b, preferred_element_type=F32, **kw)


def _dot_nt(a, b, **kw):
    return lax.dot_general(a, b, (((1,), (1,)), ((), ())), preferred_element_type=F32, **kw)


def _mod_kernel(c_ref, w_ref, b_ref, o_ref):
    cond = _silu(c_ref[...])
    o_ref[0] = _dot(cond, w_ref[0], precision=HIGHEST) + b_ref[0]


def _modulation(c, ada_w, ada_b):
    depth, d, d3 = ada_w.shape
    bsz = c.shape[0]
    return pl.pallas_call(
        _mod_kernel,
        grid=(depth, d3 // d),
        in_specs=[
            pl.BlockSpec((bsz, d), lambda i, j: (0, 0)),
            pl.BlockSpec((1, d, d), lambda i, j: (i, 0, j)),
            pl.BlockSpec((1, 1, d), lambda i, j: (i, 0, j)),
        ],
        out_specs=pl.BlockSpec((1, bsz, d), lambda i, j: (i, 0, j)),
        out_shape=jax.ShapeDtypeStruct((depth, bsz, d3), F32),
        name="adaln_mod",
    )(c, ada_w, ada_b.reshape(depth, 1, d3))


def _inproj_kernel(x_ref, g_ref, mod_ref, w_ref, o_ref, *, tn):
    x = x_ref[0]
    m = mod_ref[0]
    y = x * lax.rsqrt(jnp.mean(x * x, axis=-1, keepdims=True) + EPS) * g_ref[...]
    h = (y * (1.0 + m[1:2]) + m[0:1]).astype(BF16)
    for j in range(w_ref.shape[1] // tn):
        o_ref[0, :, j * tn:(j + 1) * tn] = _dot(h, w_ref[:, j * tn:(j + 1) * tn])


def _inproj(x, g, mod, w, *, tm, tn):
    bsz, t, d = x.shape
    n = w.shape[1]
    assert t % tm == 0 and n % tn == 0
    return pl.pallas_call(
        functools.partial(_inproj_kernel, tn=tn),
        grid=(bsz, t // tm),
        in_specs=[
            pl.BlockSpec((1, tm, d), lambda b, i: (b, i, 0)),
            pl.BlockSpec((1, d), lambda b, i: (0, 0)),
            pl.BlockSpec((1, 3, d), lambda b, i: (b, 0, 0)),
            pl.BlockSpec((d, n), lambda b, i: (0, 0), pipeline_mode=pl.Buffered(1)),
        ],
        out_specs=pl.BlockSpec((1, tm, n), lambda b, i: (b, i, 0)),
        out_shape=jax.ShapeDtypeStruct((bsz, t, n), F32),
        compiler_params=pltpu.CompilerParams(
            dimension_semantics=("parallel", "parallel"), vmem_limit_bytes=VMEM_LIMIT),
        name="norm_mod_inproj",
    )(x, g.reshape(1, d), mod, w)


def _gdn_inproj_kernel(x_ref, g_ref, mod_ref, w_ref, cw_ref, alog_ref, dtb_ref, o_ref, col_ref, row_ref,
                        ext_ref, tail_ref, *, tn):
    dh = GDN_HEAD_DIM
    tm = x_ref.shape[1]
    x = x_ref[0]
    m = mod_ref[0]
    y = x * lax.rsqrt(jnp.mean(x * x, axis=-1, keepdims=True) + EPS) * g_ref[...]
    h = (y * (1.0 + m[1:2]) + m[0:1]).astype(BF16)

    @pl.when(pl.program_id(1) == 0)
    def _():
        tail_ref[...] = jnp.zeros_like(tail_ref)

    def conv_tile(j):
        cols = slice(j * tn, (j + 1) * tn)
        ext_ref[0:8, :] = tail_ref[:, cols]
        ext_ref[8:8 + tm, :] = o_ref[0, :, cols]
        tail_ref[:, cols] = ext_ref[tm:tm + 8, :]
        w = cw_ref[:, cols]
        ext = ext_ref[...]
        acc = w[0:1] * ext
        for tap in range(1, GDN_CONV):
            acc = pltpu.roll(acc, 1, axis=0) + w[tap:tap + 1] * ext
        act = _silu(acc[8:8 + tm])
        if j * tn < 2 * GDN_QK_W:
            scale = dh ** -0.5 if j * tn < GDN_QK_W else 1.0
            heads = [act[:, c * dh:(c + 1) * dh] for c in range(tn // dh)]
            heads = [hd * (lax.rsqrt(jnp.sum(hd * hd, axis=-1, keepdims=True) + EPS) * scale) for hd in heads]
            act = jnp.concatenate(heads, axis=1)
        o_ref[0, :, cols] = act

    n = w_ref.shape[1]
    n_conv = GDN_CONV_W // tn
    bounds = [(j * tn, min((j + 1) * tn, n)) for j in range(-(-n // tn))]
    for j, (lo, hi) in enumerate(bounds):
        o_ref[0, :, lo:hi] = _dot(h, w_ref[:, lo:hi])
        if 1 <= j <= n_conv:
            conv_tile(j - 1)
    _gdn_gate_scalars(o_ref[0, :, n - LANES:n], alog_ref[...], dtb_ref[...], col_ref, row_ref)


def _gdn_inproj(x, g, mod, w, conv_w, a_log, dt_bias, *, tm, tn):
    bsz, t, d = x.shape
    n = w.shape[1]
    assert t % tm == 0 and GDN_CONV_W % tn == 0 and GDN_QK_W % tn == 0 and n > GDN_CONV_W
    pad = lambda u: jnp.zeros((1, LANES), F32).at[0, GDN_V_HEADS:2 * GDN_V_HEADS].set(u)
    ncb = tm // GDN_CHUNK
    return pl.pallas_call(
        functools.partial(_gdn_inproj_kernel, tn=tn),
        grid=(bsz, t // tm),
        in_specs=[
            pl.BlockSpec((1, tm, d), lambda b, i: (b, i, 0)),
            pl.BlockSpec((1, d), lambda b, i: (0, 0)),
            pl.BlockSpec((1, 3, d), lambda b, i: (b, 0, 0)),
            pl.BlockSpec((d, n), lambda b, i: (0, 0), pipeline_mode=pl.Buffered(1)),
            pl.BlockSpec((GDN_CONV, GDN_CONV_W), lambda b, i: (0, 0)),
            pl.BlockSpec((1, LANES), lambda b, i: (0, 0)),
            pl.BlockSpec((1, LANES), lambda b, i: (0, 0)),
        ],
        out_specs=[
            pl.BlockSpec((1, tm, n), lambda b, i: (b, i, 0)),
            pl.BlockSpec((1, tm, LANES), lambda b, i: (b, i, 0)),
            pl.BlockSpec((1, ncb, GDN_V_HEADS // 2, 2 * GDN_CHUNK), lambda b, i: (b, i, 0, 0)),
        ],
        out_shape=[
            jax.ShapeDtypeStruct((bsz, t, n), F32),
            jax.ShapeDtypeStruct((bsz, t, LANES), F32),
            jax.ShapeDtypeStruct((bsz, t // GDN_CHUNK, GDN_V_HEADS // 2, 2 * GDN_CHUNK), F32),
        ],
        scratch_shapes=[pltpu.VMEM((tm + 8, tn), F32), pltpu.VMEM((8, GDN_CONV_W), F32)],
        compiler_params=pltpu.CompilerParams(
            dimension_semantics=("parallel", "arbitrary"), vmem_limit_bytes=VMEM_LIMIT),
        name="gdn_norm_inproj_conv",
    )(x, g.reshape(1, d), mod, w, conv_w, pad(a_log), pad(dt_bias))


def _outproj_kernel(o_ref, w_ref, x_ref, mod_ref, *rest, final_norm):
    y = _dot(o_ref[0], w_ref[...])
    x = x_ref[0] + mod_ref[0][2:3] * y
    if final_norm:
        fg_ref, out_ref = rest
        x = x * lax.rsqrt(jnp.mean(x * x, axis=-1, keepdims=True) + EPS) * fg_ref[...]
    else:
        (out_ref,) = rest
    out_ref[0] = x


def _outproj(o, w, x, mod, final_g=None, *, tm):
    bsz, t, d = x.shape
    k = o.shape[-1]
    in_specs = [
        pl.BlockSpec((1, tm, k), lambda b, i: (b, i, 0)),
        pl.BlockSpec((k, d), lambda b, i: (0, 0), pipeline_mode=pl.Buffered(1)),
        pl.BlockSpec((1, tm, d), lambda b, i: (b, i, 0)),
        pl.BlockSpec((1, 3, d), lambda b, i: (b, 0, 0)),
    ]
    args = [o, w, x, mod]
    if final_g is not None:
        in_specs.append(pl.BlockSpec((1, d), lambda b, i: (0, 0)))
        args.append(final_g.reshape(1, d))
    return pl.pallas_call(
        functools.partial(_outproj_kernel, final_norm=final_g is not None),
        grid=(bsz, t // tm),
        in_specs=in_specs,
        out_specs=pl.BlockSpec((1, tm, d), lambda b, i: (b, i, 0)),
        out_shape=jax.ShapeDtypeStruct((bsz, t, d), F32),
        compiler_params=pltpu.CompilerParams(
            dimension_semantics=("parallel", "parallel"), vmem_limit_bytes=VMEM_LIMIT),
        name="outproj_residual",
    )(*args)


def _gdn_gate_scalars(ba, alog, dtb, col_ref, row_ref):
    cs, nh = GDN_CHUNK, GDN_V_HEADS
    lane = lax.broadcasted_iota(jnp.int32, ba.shape, 1)
    g = -jnp.exp(alog) * jax.nn.softplus(ba + dtb)
    vals = jnp.where(lane < nh, jax.nn.sigmoid(ba), g)
    r = lax.broadcasted_iota(jnp.int32, (cs, cs), 0)
    c = lax.broadcasted_iota(jnp.int32, (cs, cs), 1)
    tri = (r >= c).astype(F32)
    is_beta = lax.broadcasted_iota(jnp.int32, (cs, LANES), 1) < nh
    pr = lax.broadcasted_iota(jnp.int32, (nh // 2, LANES), 0)
    pc = lax.broadcasted_iota(jnp.int32, (nh // 2, LANES), 1)
    pick_even = (pc == nh + 2 * pr).astype(F32)
    for n in range(ba.shape[0] // cs):
        v = vals[n * cs:(n + 1) * cs]
        cum = _dot(tri, v, precision=HIGHEST)
        out = jnp.where(is_beta, v, cum)
        col_ref[0, n * cs:(n + 1) * cs, :] = out
        both = jnp.concatenate([out, pltpu.roll(out, LANES - 1, axis=1)], axis=0)
        row_ref[0, n] = _dot_nt(pick_even, both, precision=HIGHEST)


TRI_BASE = 8
PACK = 4


def _block_rows(p):
    n = p.shape[0]
    blk = lax.broadcasted_iota(jnp.int32, p.shape, 1) // n
    return jnp.concatenate([jnp.where(blk == j, p, 0.0) for j in range(PACK)], axis=0).astype(BF16)


def _tri_inverse_packed(mats):
    n = mats[0].shape[0]
    r = lax.broadcasted_iota(jnp.int32, mats[0].shape, 0)
    c = lax.broadcasted_iota(jnp.int32, mats[0].shape, 1) % n
    same = lambda s: (r // s) == (c // s)
    mm = lambda a, b: _dot(a.astype(BF16), _block_rows(b))
    diag = [jnp.where(same(TRI_BASE), a, 0.0) for a in mats]
    inv = [jnp.where(r == c, 1.0, 0.0) - d for d in diag]
    pw = diag
    k = 1
    while 2 * k < TRI_BASE:
        pw = [mm(m, m) for m in pw]
        inv = [p + mm(p, m) for p, m in zip(inv, pw)]
        k *= 2
        yield None
    s = TRI_BASE
    while s < n:
        sub = same(2 * s) & jnp.logical_not(same(s))
        left = [mm(p, jnp.where(sub, a, 0.0)) for p, a in zip(inv, mats)]
        inv = [p - mm(l, p) for p, l in zip(inv, left)]
        s *= 2
        yield None
    yield inv


def _gdn_chunk_kernel(q_ref, k_ref, v_ref, z_ref, col_ref, row_ref, nw_ref,
                      o_ref, s_ref, hu_ref, hwq_ref, hqk_ref, hkd_ref, hgl_ref):
    cs, dh, nh = GDN_CHUNK, GDN_HEAD_DIM, GDN_V_HEADS
    tt = q_ref.shape[1]
    hps = q_ref.shape[2] // dh
    ncb = tt // cs
    assert ncb % 2 == 0 and PACK == 4
    hg = pl.program_id(1)
    ti = pl.program_id(2)
    wslot = ti % 2
    rslot = 1 - wslot
    jidx = lambda hl, n, e: (hl * ncb + n) * 2 + e

    @pl.when(ti == 0)
    def _():
        s_ref[...] = jnp.zeros_like(s_ref)
        for h_ref in (hu_ref, hwq_ref, hqk_ref, hkd_ref, hgl_ref):
            h_ref[1] = jnp.zeros(h_ref.shape[1:], h_ref.dtype)

    chunks = [slice(n * cs, (n + 1) * cs) for n in range(ncb)]

    def prepare():
        q = [q_ref[0, :, hl * dh:(hl + 1) * dh] for hl in range(hps)]
        k = [k_ref[0, :, hl * dh:(hl + 1) * dh] for hl in range(hps)]
        v = [[v_ref[0, :, (2 * hl + e) * dh:(2 * hl + e + 1) * dh] for e in range(2)] for hl in range(hps)]
        lane = lax.broadcasted_iota(jnp.int32, (tt, LANES), 1)
        colv = col_ref[0]
        column = lambda idx: jnp.sum(jnp.where(lane == idx, colv, 0.0), axis=1, keepdims=True)
        hq = [hg * hps + hl for hl in range(hps)]
        beta = [[column(2 * h + e) for e in range(2)] for h in hq]
        gc = [[column(nh + 2 * h + e) for e in range(2)] for h in hq]
        egc = [[jnp.exp(x) for x in pair] for pair in gc]
        yield

        pr = lax.broadcasted_iota(jnp.int32, (cs, PACK * cs), 0)
        pb = lax.broadcasted_iota(jnp.int32, (cs, PACK * cs), 1) // cs
        pc = lax.broadcasted_iota(jnp.int32, (cs, PACK * cs), 1) % cs
        groups = [(hl, gi) for hl in range(hps) for gi in range(ncb // 2)]
        a_mats, g_rows = [], {}
        for hl, gi in groups:
            k16, q16 = k[hl].astype(BF16), q[hl].astype(BF16)
            pair = (chunks[2 * gi], chunks[2 * gi + 1])
            kdup = [jnp.concatenate([k16[sl], k16[sl]], axis=0) for sl in pair]
            kk = jnp.concatenate([_dot_nt(k16[sl], kd) for sl, kd in zip(pair, kdup)], axis=1)
            qk = jnp.concatenate([_dot_nt(q16[sl], kd) for sl, kd in zip(pair, kdup)], axis=1)
            pick = lambda cols: jnp.where(pb == 0, cols[0][pair[0]], jnp.where(
                pb == 1, cols[1][pair[0]], jnp.where(pb == 2, cols[0][pair[1]], cols[1][pair[1]])))
            g_row = jnp.concatenate([row_ref[0, 2 * gi + j, pl.ds(hq[hl], 1), :] for j in range(2)], axis=1)
            decay = jnp.exp(jnp.where(pr >= pc, pick(gc[hl]) - g_row, -jnp.inf))
            a_mats.append(jnp.where(pr > pc, pick(beta[hl]) * kk * decay, 0.0))
            hqk_ref[wslot, hl * (ncb // 2) + gi] = (qk * decay).astype(BF16)
            g_rows[hl, gi] = g_row
            yield
        t_mats = None
        for t_mats in _tri_inverse_packed(a_mats):
            yield

        for (hl, gi), t_mat in zip(groups, t_mats):
            rhs = []
            for n in (2 * gi, 2 * gi + 1):
                sl = chunks[n]
                for e in range(2):
                    j = 2 * (n % 2) + e
                    g_last = g_rows[hl, gi][:, j * cs + cs - 1:(j + 1) * cs]
                    hgl_ref[wslot, jidx(hl, n, e)] = jnp.broadcast_to(jnp.exp(g_last), (1, LANES))
                    hkd_ref[wslot, jidx(hl, n, e)] = (k[hl][sl] * jnp.exp(g_last - gc[hl][e][sl])).T.astype(BF16)
                    kbeta = k[hl][sl] * beta[hl][e][sl]
                    rhs.append(jnp.concatenate([v[hl][e][sl] * beta[hl][e][sl], kbeta * egc[hl][e][sl]], axis=1))
            out = _dot(_block_rows(t_mat), jnp.concatenate(rhs, axis=0).astype(BF16))
            for n in (2 * gi, 2 * gi + 1):
                for e in range(2):
                    uw = out[(2 * (n % 2) + e) * cs:(2 * (n % 2) + e + 1) * cs]
                    hu_ref[wslot, jidx(hl, n, e)] = uw[:, :dh]
                    q_dec = q[hl][chunks[n]] * egc[hl][e][chunks[n]]
                    hwq_ref[wslot, jidx(hl, n, e)] = jnp.concatenate([uw[:, dh:], q_dec], axis=0).astype(BF16)
            yield

    def recurrence():
        nw = nw_ref[...]
        heads = [(hl, e) for hl in range(hps) for e in range(2)]
        state = {he: s_ref[2 * he[0] + he[1]] for he in heads}
        for n, sl in enumerate(chunks):
            ws = {}
            for hl, e in heads:
                ws[hl, e] = _dot(hwq_ref[rslot, jidx(hl, n, e)], state[hl, e].astype(BF16))
            yield
            for hl, e in heads:
                j = jidx(hl, n, e)
                v16 = (hu_ref[rslot, j] - ws[hl, e][:cs]).astype(BF16)
                lb = 2 * (n % 2) + e
                qkd = hqk_ref[rslot, hl * (ncb // 2) + n // 2][:, lb * cs:(lb + 1) * cs]
                o = ws[hl, e][cs:] + _dot(qkd, v16)
                state[hl, e] = state[hl, e] * hgl_ref[rslot, j] + _dot(hkd_ref[rslot, j], v16)
                o = o * lax.rsqrt(jnp.mean(o * o, axis=-1, keepdims=True) + EPS) * nw
                lanes = slice((2 * hl + e) * dh, (2 * hl + e + 1) * dh)
                o_ref[0, sl, lanes] = (o * _silu(z_ref[0, sl, lanes])).astype(o_ref.dtype)
            yield
        for (hl, e), st in state.items():
            s_ref[2 * hl + e] = st

    prep, rec = prepare(), recurrence()
    n_prep = 1 + hps * (ncb // 2) * 2 + 6
    per_stage = -(-n_prep // (2 * ncb))
    prep_live = rec_live = True
    while prep_live or rec_live:
        for _ in range(per_stage):
            if prep_live:
                prep_live = next(prep, "done") != "done"
        if rec_live:
            rec_live = next(rec, "done") != "done"


def _gdn_chunk(proj, col, row, norm_w, *, tt=GDN_TILE, hps=GDN_HEADS_PER_STEP):
    bsz, t, _ = proj.shape
    dh = GDN_HEAD_DIM
    qw, vw = hps * dh, 2 * hps * dh
    k_blk0 = GDN_QK_W // qw
    v_blk0 = 2 * GDN_QK_W // vw
    z_blk0 = GDN_CONV_W // vw
    ncb = tt // GDN_CHUNK
    nt = t // tt
    nj = hps * ncb * 2
    cur = lambda i: jnp.minimum(i, nt - 1)
    prev = lambda i: jnp.maximum(i - 1, 0)
    return pl.pallas_call(
        _gdn_chunk_kernel,
        grid=(bsz, GDN_QK_HEADS // hps, nt + 1),
        in_specs=[
            pl.BlockSpec((1, tt, qw), lambda b, h, i: (b, cur(i), h)),
            pl.BlockSpec((1, tt, qw), lambda b, h, i: (b, cur(i), k_blk0 + h)),
            pl.BlockSpec((1, tt, vw), lambda b, h, i: (b, cur(i), v_blk0 + h)),
            pl.BlockSpec((1, tt, vw), lambda b, h, i: (b, prev(i), z_blk0 + h)),
            pl.BlockSpec((1, tt, LANES), lambda b, h, i: (b, cur(i), 0)),
            pl.BlockSpec((1, ncb, GDN_V_HEADS // 2, 2 * GDN_CHUNK), lambda b, h, i: (b, cur(i), 0, 0)),
            pl.BlockSpec((1, dh), lambda b, h, i: (0, 0)),
        ],
        out_specs=pl.BlockSpec((1, tt, vw), lambda b, h, i: (b, prev(i), h)),
        out_shape=jax.ShapeDtypeStruct((bsz, t, GDN_V_W), BF16),
        scratch_shapes=[
            pltpu.VMEM((2 * hps, dh, dh), F32),
            pltpu.VMEM((2, nj, GDN_CHUNK, dh), F32),
            pltpu.VMEM((2, nj, 2 * GDN_CHUNK, dh), BF16),
            pltpu.VMEM((2, nj // PACK, GDN_CHUNK, PACK * GDN_CHUNK), BF16),
            pltpu.VMEM((2, nj, dh, GDN_CHUNK), BF16),
            pltpu.VMEM((2, nj, 1, LANES), F32),
        ],
        compiler_params=pltpu.CompilerParams(
            dimension_semantics=("parallel", "parallel", "arbitrary"), vmem_limit_bytes=VMEM_LIMIT),
        name="gdn_chunk_scan",
    )(proj, proj, proj, proj, col, row, norm_w.reshape(1, dh))


def _gdn_layer(x, mod, norm_g, w_in, conv_w, a_log, dt_bias, norm_w, w_out):
    n_in = w_in.shape[1]
    n_pad = -(-n_in // LANES) * LANES
    w_in_p = jnp.pad(w_in, ((0, 0), (0, n_pad - n_in))).astype(BF16)
    proj, col, row = _gdn_inproj(x, norm_g, mod, w_in_p, conv_w, a_log, dt_bias,
                                 tm=GDN_INPROJ_ROWS, tn=GDN_INPROJ_COLS)
    o = _gdn_chunk(proj, col, row, norm_w)
    return _outproj(o, w_out.astype(BF16), x, mod, tm=OUTPROJ_ROWS)


NSA_HEADS = 16
NSA_GROUPS = 4
NSA_HPG = NSA_HEADS // NSA_GROUPS
NSA_HEAD_DIM = 64
NSA_CMP_LEN = 32
NSA_CMP_STRIDE = 16
NSA_SLC_LEN = 64
NSA_TOP_K = 8
NSA_WINDOW = 512
NSA_QTILE = 256
NSA_Q_W = NSA_HEADS * NSA_HEAD_DIM
NSA_KV_W = NSA_GROUPS * NSA_HEAD_DIM
REL_BUCKETS = 32
REL_MAX_DIST = 128
FEAT_LANE0 = NSA_HEAD_DIM
CONST_LANE0 = FEAT_LANE0 + 32
VT_PAD = 16
NSA_COL_Q = 0
NSA_COL_CMP = NSA_Q_W
NSA_COL_SEL = NSA_COL_CMP + 2 * NSA_KV_W
NSA_COL_WIN = NSA_COL_SEL + 2 * NSA_KV_W
NSA_COL_Z = NSA_COL_WIN + 2 * NSA_KV_W
NSA_COL_GATE = NSA_COL_Z + NSA_Q_W
NSA_PROJ_W = NSA_COL_GATE + LANES


def _nsa_column_perm():
    g, dh = NSA_GROUPS, NSA_HEAD_DIM
    kv0 = NSA_Q_W
    cols = list(range(NSA_Q_W))
    cols += [kv0 + i for i in range(2 * NSA_KV_W)]
    for br in (1, 2):
        for gi in range(g):
            cols += [kv0 + (2 * br) * NSA_KV_W + gi * dh + d for d in range(dh)]
            cols += [kv0 + (2 * br + 1) * NSA_KV_W + gi * dh + d for d in range(dh)]
    gate0 = kv0 + 6 * NSA_KV_W
    cols += [gate0 + 3 * NSA_HEADS + i for i in range(NSA_Q_W)]
    cols += [gate0 + i for i in range(3 * NSA_HEADS)] + [-1] * (LANES - 3 * NSA_HEADS)
    assert len(cols) == NSA_PROJ_W
    return np.asarray(cols, np.int32)


def _rel_bucket_table(n):
    d = np.arange(n)
    max_exact = REL_BUCKETS // 2
    nf = np.maximum(d, 1).astype(np.float64)
    large = max_exact + (np.log(nf / max_exact) / math.log(REL_MAX_DIST / max_exact)
                         * (REL_BUCKETS - max_exact)).astype(np.int32)
    large = np.minimum(large, REL_BUCKETS - 1)
    return np.where(d < max_exact, d, large).astype(np.int32)


def _nsa_tables(rel_bias, t):
    qb = NSA_QTILE
    bucket = _rel_bucket_table(t)
    assert np.all(bucket[qb + 1:] == REL_BUCKETS - 1)
    bvec = rel_bias[bucket].T
    far = rel_bias[REL_BUCKETS - 1]
    far_hi = far.astype(BF16)
    far_lo = (far - far_hi.astype(F32)).astype(BF16)
    far_sum = far_hi.astype(F32) + far_lo.astype(F32)
    r = np.arange(qb)[:, None]
    c = np.arange(qb)[None, :]
    d0 = r - c

    def toeplitz(w):
        n = 2 * qb - 1
        ext = jnp.pad(w[:, ::-1], ((0, 0), (0, 1)))
        skew = jnp.tile(ext, (1, qb))[:, :qb * n].reshape(w.shape[0], qb, n)
        return skew[:, :, qb - 1:]

    rel = bvec[:, :2 * qb] - far_sum[:, None]
    t0 = toeplitz(jnp.concatenate([jnp.full((NSA_HEADS, qb - 1), NEG_INF, F32), rel[:, :qb]], axis=1))
    t1 = toeplitz(rel[:, 1:])
    g, hpg = NSA_GROUPS, NSA_HPG
    none = jnp.full_like(t0, NEG_INF)
    near = jnp.stack([t1, t0, t0, none], axis=1).reshape(g, hpg, 2, 2, qb, qb)
    near = near.transpose(0, 2, 3, 5, 1, 4).reshape(g, 2, 2 * qb, hpg * qb)
    nb = t // NSA_CMP_STRIDE
    per_tile = qb // NSA_CMP_STRIDE
    back = 9
    far_d = back * NSA_CMP_STRIDE - (NSA_CMP_LEN - 1)
    assert np.all(bucket[far_d:] == REL_BUCKETS - 1)
    width = (qb - 1 + far_d) // NSA_CMP_STRIDE + 1
    dm = r - NSA_CMP_STRIDE * np.arange(width)[None, :] + far_d
    band = jnp.where(dm >= 0, bvec[:, np.maximum(dm, 0)], NEG_INF)
    tiles = []
    for i in range(t // qb):
        j0 = per_tile * i - back
        lo, hi = max(j0, 0), min(j0 + width, nb)
        tiles.append(jnp.concatenate([
            jnp.broadcast_to(far[:, None, None], (NSA_HEADS, qb, lo)),
            band[:, :, lo - j0:hi - j0],
            jnp.full((NSA_HEADS, qb, nb - hi), NEG_INF, F32)], axis=2))
    cmp_bias = jnp.stack(tiles, axis=0).reshape(t // qb, g, hpg, qb, nb).transpose(0, 1, 4, 2, 3)
    cmp_bias = cmp_bias.reshape(t // qb, g, nb, hpg * qb)
    qconst = jnp.zeros((g, 8, hpg, qb), F32)
    qconst = qconst.at[:, 0].set(jnp.broadcast_to(far_hi.astype(F32).reshape(g, hpg, 1), (g, hpg, qb)))
    qconst = qconst.at[:, 1].set(jnp.broadcast_to(far_lo.astype(F32).reshape(g, hpg, 1), (g, hpg, qb)))
    return near, cmp_bias, qconst.reshape(g, 8, hpg * qb)


def _overlap_t(t):
    n_cmp = (t - NSA_CMP_LEN) // NSA_CMP_STRIDE + 1
    n_slc = t // NSA_SLC_LEN
    c_start = np.arange(n_cmp)[:, None] * NSA_CMP_STRIDE
    s_start = np.arange(n_slc)[None, :] * NSA_SLC_LEN
    ov = np.clip(np.minimum(c_start + NSA_CMP_LEN, s_start + NSA_SLC_LEN) - np.maximum(c_start, s_start), 0, None)
    ov = ov.astype(np.float32) / NSA_CMP_LEN
    out = np.zeros((32, t // NSA_CMP_STRIDE), np.float32)
    out[:n_slc, :n_cmp] = ov.T
    return out


def _nsa_compress_kernel(x_ref, pos_ref, w1_ref, w2_ref, o_ref, xs_ref):
    t = x_ref.shape[1]
    nb = t // NSA_CMP_STRIDE
    nlt = xs_ref.shape[0]
    for c in range(nlt):
        xs_ref[c, 0:t, :] = x_ref[0, :, c * LANES:(c + 1) * LANES]
        xs_ref[c, t:t + NSA_CMP_STRIDE, :] = jnp.zeros((NSA_CMP_STRIDE, LANES), F32)
    acc = jnp.zeros((nb, w1_ref.shape[2]), F32)
    for l in range(NSA_CMP_LEN):
        xl = jnp.concatenate([xs_ref[c, pl.ds(l, nb, stride=NSA_CMP_STRIDE), :] for c in range(nlt)], axis=1)
        xl = xl + pos_ref[l:l + 1, :]
        acc = acc + _dot(xl.astype(BF16), w1_ref[l])
    hid = _silu(acc).astype(BF16)
    res = _dot(hid, w2_ref[...])
    for g in range(NSA_GROUPS):
        o_ref[0, g] = res[:, g * LANES:(g + 1) * LANES]


def _nsa_compress(proj, cmp_pos, cmp_w1, cmp_w2):
    bsz, t, _ = proj.shape
    g, dh = NSA_GROUPS, NSA_HEAD_DIM
    nb = t // NSA_CMP_STRIDE
    w = 2 * NSA_KV_W
    w1 = cmp_w1.reshape(2, NSA_CMP_LEN, dh, dh).astype(BF16)
    w2 = cmp_w2.astype(BF16)
    place = lambda blk, c0: jnp.pad(blk, [(0, 0)] * (blk.ndim - 1) + [(c0, w - dh - c0)])
    w1c = jnp.concatenate([place(w1[i], (gi * 2 + i) * dh) for i in range(2) for gi in range(g)], axis=1)
    w2c = jnp.concatenate([place(w2[i], (gi * 2 + i) * dh) for gi in range(g) for i in range(2)], axis=0)
    pos = jnp.broadcast_to(cmp_pos[:, :, None, :], (2, NSA_CMP_LEN, g, dh)).transpose(1, 0, 2, 3).reshape(NSA_CMP_LEN, w)
    return pl.pallas_call(
        _nsa_compress_kernel,
        grid=(bsz,),
        in_specs=[
            pl.BlockSpec((1, t, w), lambda b: (b, 0, NSA_COL_CMP // w)),
            pl.BlockSpec((NSA_CMP_LEN, w), lambda b: (0, 0)),
            pl.BlockSpec((NSA_CMP_LEN, w, w), lambda b: (0, 0, 0), pipeline_mode=pl.Buffered(1)),
            pl.BlockSpec((w, w), lambda b: (0, 0)),
        ],
        out_specs=pl.BlockSpec((1, g, nb, LANES), lambda b: (b, 0, 0, 0)),
        out_shape=jax.ShapeDtypeStruct((bsz, g, nb, LANES), F32),
        scratch_shapes=[pltpu.VMEM((w // LANES, t + NSA_CMP_STRIDE, LANES), F32)],
        compiler_params=pltpu.CompilerParams(dimension_semantics=("parallel",), vmem_limit_bytes=VMEM_LIMIT),
        name="nsa_compress",
    )(proj, pos, w1c, w2c)


def _nsa_attn_kernel(q_ref, kvs_ref, kvw_ref, kvc_ref, gate_ref, z_ref, cb_ref, near_ref, qc_ref, ovl_ref,
                     o_ref, ks_ref, vs_ref, kw_ref, vw_ref, gt_ref, ms_ref, accs_ref, mw_ref, accw_ref, sc_ref):
    qb, dh, hpg = NSA_QTILE, NSA_HEAD_DIM, NSA_HPG
    t = kvs_ref.shape[1]
    nblk = t // NSA_SLC_LEN
    cols = hpg * qb
    g = pl.program_id(0)
    i = pl.program_id(2)

    @pl.when(i == 0)
    def _():
        tok = lax.broadcasted_iota(jnp.int32, (t, LANES), 0)
        ln = lax.broadcasted_iota(jnp.int32, (t, LANES), 1)
        const = jnp.where((ln == CONST_LANE0) | (ln == CONST_LANE0 + 1), 1.0, 0.0)
        onehot = jnp.where(ln - FEAT_LANE0 == tok // NSA_SLC_LEN, 1.0, 0.0)
        ones_rows = jnp.where(lax.broadcasted_iota(jnp.int32, (VT_PAD, t), 0) == 0, 1.0, 0.0)
        kvs = kvs_ref[0]
        kvw = kvw_ref[0]
        ks_ref[...] = jnp.where(ln < dh, kvs, onehot + const).astype(BF16)
        kw_ref[...] = jnp.where(ln < dh, kvw, const).astype(BF16)
        vs_ref[...] = jnp.concatenate([ones_rows, kvs.T[dh:]], axis=0).astype(BF16)
        vw_ref[...] = jnp.concatenate([ones_rows, kvw.T[dh:]], axis=0).astype(BF16)

    q_t = (q_ref[0] * (dh ** -0.5)).T
    q_heads = jnp.concatenate([q_t[hh * dh:(hh + 1) * dh] for hh in range(hpg)], axis=1)

    def scores(branch, qa_t, start, nk, bias):
        sc = _dot(branch[0][pl.ds(pl.multiple_of(start, qb), nk), :], qa_t)
        return sc if bias is None else sc + bias

    def update(branch, start, nk, sc):
        _, vt_ref, m_ref, acc_ref = branch
        m_old = m_ref[...]
        m_new = jnp.maximum(m_old, jnp.max(sc, axis=0, keepdims=True))
        alpha = jnp.exp(m_old - m_new)
        pe = jnp.exp(sc - m_new).astype(BF16)
        acc_ref[...] = alpha * acc_ref[...] + _dot(vt_ref[:, pl.ds(pl.multiple_of(start, qb), nk)], pe)
        m_ref[...] = m_new

    sel = (ks_ref, vs_ref, ms_ref, accs_ref)
    win = (kw_ref, vw_ref, mw_ref, accw_ref)
    for m_ref, acc_ref in ((ms_ref, accs_ref), (mw_ref, accw_ref)):
        m_ref[...] = jnp.full(m_ref.shape, NEG_INF, F32)
        acc_ref[...] = jnp.zeros(acc_ref.shape, F32)

    nwt = NSA_WINDOW // qb
    assert nwt in (2, 4)
    pad_rows = jnp.zeros((LANES - CONST_LANE0 - 8, cols), F32)
    qa_win = jnp.concatenate([q_heads, jnp.zeros((32, cols), F32), qc_ref[0], pad_rows], axis=0).astype(BF16)

    kvc = kvc_ref[0, 0]
    lane_k = lax.broadcasted_iota(jnp.int32, kvc.shape, 1)
    kc16 = jnp.where(lane_k < dh, kvc, 0.0).astype(BF16)
    s = _dot(kc16, qa_win) + cb_ref[0, 0]

    kk = lax.broadcasted_iota(jnp.int32, (qb, cols), 0)
    rr = lax.broadcasted_iota(jnp.int32, (qb, cols), 1) % qb
    wold_start = jnp.maximum(i - nwt, 0) * qb
    sc_wold = scores(win, qa_win, wold_start, qb, jnp.where((rr < kk) & (i >= nwt), 0.0, NEG_INF))

    row2 = lax.broadcasted_iota(jnp.int32, (2 * qb, cols), 0)
    if nwt == 4:
        w32_start = jnp.maximum(i - 3, 0) * qb
        sc_w32 = scores(win, qa_win, w32_start, 2 * qb,
                        jnp.where(row2 < (i - 1) * qb - w32_start, 0.0, NEG_INF))
    near_start = jnp.maximum(i - 1, 0) * qb
    near_bias = near_ref[0, jnp.where(i == 0, 1, 0)]
    sc_wn = scores(win, qa_win, near_start, 2 * qb, near_bias)

    s = jnp.exp(s - jnp.max(s, axis=0, keepdims=True))
    p = s / jnp.sum(s, axis=0, keepdims=True)
    tq_lane = i * qb + lax.broadcasted_iota(jnp.int32, (1, cols), 1) % qb
    p16 = (p * (tq_lane >= NSA_CMP_LEN - 1).astype(F32)).astype(BF16)
    o_cmp = _dot(kvc.T.astype(BF16), p16)
    ovl = ovl_ref[...].astype(BF16)
    imp = _dot(ovl, p16[:, 0:qb])
    for hh in range(1, hpg):
        imp = imp + _dot(ovl, p16[:, hh * qb:(hh + 1) * qb])

    update(win, wold_start, qb, sc_wold)

    blk = lax.broadcasted_iota(jnp.int32, (32, qb), 0)
    tq = i * qb + lax.broadcasted_iota(jnp.int32, (32, qb), 1)
    cur = tq // NSA_SLC_LEN
    forced = (blk == 0) | (blk == cur) | (blk == cur - 1)
    val = jnp.where(forced, jnp.inf, jnp.where(blk * NSA_SLC_LEN <= tq, imp, -jnp.inf))
    cnt = jnp.zeros((32, qb), jnp.int32)
    for s2 in range(nblk):
        other = val[s2:s2 + 1, :]
        cnt = cnt + ((other > val) | ((other == val) & (s2 < blk))).astype(jnp.int32)
    feat = jnp.where((cnt < min(NSA_TOP_K, nblk)) & (blk < nblk), 0.0, NEG_INF)
    qa = jnp.concatenate([q_heads, jnp.concatenate([feat] * hpg, axis=1), qc_ref[0], pad_rows],
                         axis=0).astype(BF16)

    if nwt == 4:
        update(win, w32_start, 2 * qb, sc_w32)
    sc_sn = scores(sel, qa, near_start, 2 * qb, near_bias)
    update(win, near_start, 2 * qb, sc_wn)
    update(sel, near_start, 2 * qb, sc_sn)

    n_far = jnp.maximum(i - 1, 0)
    n_pairs = (n_far + 1) // 2
    pair_start = lambda p: jnp.maximum(2 * p - n_far % 2, 0) * qb
    last_pair = jnp.maximum(n_pairs - 1, 0)
    sc_ref[...] = scores(sel, qa, 0, 2 * qb, jnp.where(row2 < (2 - n_far % 2) * qb, 0.0, NEG_INF))

    def sel_body(k, carry):
        sc_odd = scores(sel, qa, pair_start(2 * k + 1), 2 * qb, None)
        update(sel, pair_start(2 * k), 2 * qb, sc_ref[...])
        sc_ref[...] = scores(sel, qa, pair_start(jnp.minimum(2 * k + 2, last_pair)), 2 * qb, None)
        update(sel, pair_start(2 * k + 1), 2 * qb, sc_odd)
        return carry

    lax.fori_loop(0, n_pairs // 2, sel_body, 0)

    @pl.when(n_pairs % 2 == 1)
    def _():
        update(sel, pair_start(n_pairs - 1), 2 * qb, sc_ref[...])

    def finish(acc_ref):
        acc = acc_ref[...]
        return acc[VT_PAD:] / acc[0:1]

    o_slc = finish(accs_ref)
    o_win = finish(accw_ref)

    gt_ref[...] = jax.nn.sigmoid(gate_ref[0]).T
    outs = []
    for hh in range(hpg):
        sl = slice(hh * qb, (hh + 1) * qb)
        base = (g * hpg + hh) * 3
        gate = [gt_ref[pl.ds(base + br, 1), :] for br in range(3)]
        outs.append(gate[0] * o_cmp[dh:, sl] + gate[1] * o_slc[:, sl] + gate[2] * o_win[:, sl])
    out = jnp.concatenate(outs, axis=0).T
    o_ref[0] = (out * _silu(z_ref[0])).astype(o_ref.dtype)


def _nsa_attn(proj, kv_cmp, near, cmp_bias, qconst, ovl):
    bsz, t, _ = proj.shape
    qb, hpg = NSA_QTILE, NSA_HPG
    gw = hpg * NSA_HEAD_DIM
    nb = t // NSA_CMP_STRIDE
    cols = hpg * qb
    return pl.pallas_call(
        _nsa_attn_kernel,
        grid=(NSA_GROUPS, bsz, t // qb),
        in_specs=[
            pl.BlockSpec((1, qb, gw), lambda g, b, i: (b, i, NSA_COL_Q // gw + g)),
            pl.BlockSpec((1, t, LANES), lambda g, b, i: (b, 0, NSA_COL_SEL // LANES + g)),
            pl.BlockSpec((1, t, LANES), lambda g, b, i: (b, 0, NSA_COL_WIN // LANES + g)),
            pl.BlockSpec((1, 1, nb, LANES), lambda g, b, i: (b, g, 0, 0)),
            pl.BlockSpec((1, qb, LANES), lambda g, b, i: (b, i, NSA_COL_GATE // LANES)),
            pl.BlockSpec((1, qb, gw), lambda g, b, i: (b, i, NSA_COL_Z // gw + g)),
            pl.BlockSpec((1, 1, nb, cols), lambda g, b, i: (i, g, 0, 0)),
            pl.BlockSpec((1, 2, 2 * qb, cols), lambda g, b, i: (g, 0, 0, 0)),
            pl.BlockSpec((1, 8, cols), lambda g, b, i: (g, 0, 0)),
            pl.BlockSpec((32, nb), lambda g, b, i: (0, 0)),
        ],
        out_specs=pl.BlockSpec((1, qb, gw), lambda g, b, i: (b, i, g)),
        out_shape=jax.ShapeDtypeStruct((bsz, t, NSA_Q_W), BF16),
        scratch_shapes=[
            pltpu.VMEM((t, LANES), BF16), pltpu.VMEM((VT_PAD + NSA_HEAD_DIM, t), BF16),
            pltpu.VMEM((t, LANES), BF16), pltpu.VMEM((VT_PAD + NSA_HEAD_DIM, t), BF16),
            pltpu.VMEM((LANES, qb), F32),
            pltpu.VMEM((1, cols), F32), pltpu.VMEM((VT_PAD + NSA_HEAD_DIM, cols), F32),
            pltpu.VMEM((1, cols), F32), pltpu.VMEM((VT_PAD + NSA_HEAD_DIM, cols), F32),
            pltpu.VMEM((2 * qb, cols), F32),
        ],
        compiler_params=pltpu.CompilerParams(
            dimension_semantics=("parallel", "parallel", "arbitrary"), vmem_limit_bytes=VMEM_LIMIT),
        name="nsa_attention",
    )(proj, proj, proj, kv_cmp, proj, proj, cmp_bias, near, qconst, ovl)


def _nsa_layer(x, mod, norm_g, w_in, cmp_pos, cmp_w1, cmp_w2, rel_bias, w_out, final_g):
    t = x.shape[1]
    assert t // NSA_SLC_LEN <= 32 and NSA_COL_Z % (NSA_HPG * NSA_HEAD_DIM) == 0
    perm = _nsa_column_perm()
    cuts = [0] + [j for j in range(1, len(perm)) if perm[j] != perm[j - 1] + (perm[j - 1] >= 0)] + [len(perm)]
    runs = [(int(perm[a]), b - a) for a, b in zip(cuts[:-1], cuts[1:])]
    w16 = w_in.astype(BF16)
    w_in_p = jnp.concatenate([w16[:, s:s + n] if s >= 0 else jnp.zeros((w_in.shape[0], n), BF16)
                              for s, n in runs], axis=1)
    proj = _inproj(x, norm_g, mod, w_in_p, tm=NSA_INPROJ_ROWS, tn=NSA_PROJ_W)
    kv_cmp = _nsa_compress(proj, cmp_pos, cmp_w1, cmp_w2)
    near, cmp_bias, qconst = _nsa_tables(rel_bias, t)
    o = _nsa_attn(proj, kv_cmp, near, cmp_bias, qconst, jnp.asarray(_overlap_t(t)))
    return _outproj(o, w_out.astype(BF16), x, mod, final_g, tm=OUTPROJ_ROWS)


def kernel(x, c, ada_w, ada_b, norm_g, gdn_w_in, gdn_conv_w, gdn_a_log, gdn_dt_bias, gdn_norm_w, gdn_w_out,
           nsa_w_in, nsa_cmp_pos, nsa_cmp_w1, nsa_cmp_w2, nsa_w_out, rel_bias, final_g):
    bsz, t, d = x.shape
    mod = _modulation(c, ada_w, ada_b).reshape(ada_w.shape[0], bsz, 3, d)
    x = _gdn_layer(x, mod[0], norm_g[0], gdn_w_in[0], gdn_conv_w[0], gdn_a_log[0], gdn_dt_bias[0],
                   gdn_norm_w[0], gdn_w_out[0])
    return _nsa_layer(x, mod[1], norm_g[1], nsa_w_in[0], nsa_cmp_pos[0], nsa_cmp_w1[0], nsa_cmp_w2[0],
                      rel_bias, nsa_w_out[0], final_g)
```

```python
import functools
import math

import numpy as np
import jax
import jax.numpy as jnp
from jax import lax
from jax.experimental import pallas as pl
from jax.experimental.pallas import tpu as pltpu

F32 = jnp.float32
BF16 = jnp.bfloat16
HIGHEST = lax.Precision.HIGHEST

EPS = 1e-6
NEG_INF = -1e30
LANES = 128
VMEM_LIMIT = 56 * 1024 * 1024

GDN_QK_HEADS = 8
GDN_V_HEADS = 16
GDN_HEAD_DIM = 128
GDN_CONV = 4
GDN_CHUNK = 64
GDN_QK_W = GDN_QK_HEADS * GDN_HEAD_DIM
GDN_V_W = GDN_V_HEADS * GDN_HEAD_DIM
GDN_CONV_W = 2 * GDN_QK_W + GDN_V_W
GDN_TILE = 128
GDN_HEADS_PER_STEP = 8
GDN_INPROJ_ROWS = 256
GDN_INPROJ_COLS = 4 * LANES
NSA_INPROJ_ROWS = 512
OUTPROJ_ROWS = 1024


def _silu(x):
    return x * jax.nn.sigmoid(x)


def _dot(a, b, **kw):
    return jnp.dot(a, b, preferred_element_type=F32, **kw)


def _dot_nt(a, b, **kw):
    return lax.dot_general(a, b, (((1,), (1,)), ((), ())), preferred_element_type=F32, **kw)


def _mod_kernel(c_ref, w_ref, b_ref, o_ref):
    cond = _silu(c_ref[...])
    o_ref[0] = _dot(cond, w_ref[0], precision=HIGHEST) + b_ref[0]


def _modulation(c, ada_w, ada_b):
    depth, d, d3 = ada_w.shape
    bsz = c.shape[0]
    return pl.pallas_call(
        _mod_kernel,
        grid=(depth, d3 // d),
        in_specs=[
            pl.BlockSpec((bsz, d), lambda i, j: (0, 0)),
            pl.BlockSpec((1, d, d), lambda i, j: (i, 0, j)),
            pl.BlockSpec((1, 1, d), lambda i, j: (i, 0, j)),
        ],
        out_specs=pl.BlockSpec((1, bsz, d), lambda i, j: (i, 0, j)),
        out_shape=jax.ShapeDtypeStruct((depth, bsz, d3), F32),
        name="adaln_mod",
    )(c, ada_w, ada_b.reshape(depth, 1, d3))


def _inproj_kernel(x_ref, g_ref, mod_ref, w_ref, o_ref, *, tn):
    x = x_ref[0]
    m = mod_ref[0]
    y = x * lax.rsqrt(jnp.mean(x * x, axis=-1, keepdims=True) + EPS) * g_ref[...]
    h = (y * (1.0 + m[1:2]) + m[0:1]).astype(BF16)
    for j in range(w_ref.shape[1] // tn):
        o_ref[0, :, j * tn:(j + 1) * tn] = _dot(h, w_ref[:, j * tn:(j + 1) * tn])


def _inproj(x, g, mod, w, *, tm, tn):
    bsz, t, d = x.shape
    n = w.shape[1]
    assert t % tm == 0 and n % tn == 0
    return pl.pallas_call(
        functools.partial(_inproj_kernel, tn=tn),
        grid=(bsz, t // tm),
        in_specs=[
            pl.BlockSpec((1, tm, d), lambda b, i: (b, i, 0)),
            pl.BlockSpec((1, d), lambda b, i: (0, 0)),
            pl.BlockSpec((1, 3, d), lambda b, i: (b, 0, 0)),
            pl.BlockSpec((d, n), lambda b, i: (0, 0), pipeline_mode=pl.Buffered(1)),
        ],
        out_specs=pl.BlockSpec((1, tm, n), lambda b, i: (b, i, 0)),
        out_shape=jax.ShapeDtypeStruct((bsz, t, n), F32),
        compiler_params=pltpu.CompilerParams(
            dimension_semantics=("parallel", "parallel"), vmem_limit_bytes=VMEM_LIMIT),
        name="norm_mod_inproj",
    )(x, g.reshape(1, d), mod, w)


def _gdn_inproj_kernel(x_ref, g_ref, mod_ref, w_ref, cw_ref, alog_ref, dtb_ref, o_ref, col_ref, row_ref,
                        ext_ref, tail_ref, *, tn):
    dh = GDN_HEAD_DIM
    tm = x_ref.shape[1]
    x = x_ref[0]
    m = mod_ref[0]
    y = x * lax.rsqrt(jnp.mean(x * x, axis=-1, keepdims=True) + EPS) * g_ref[...]
    h = (y * (1.0 + m[1:2]) + m[0:1]).astype(BF16)

    @pl.when(pl.program_id(1) == 0)
    def _():
        tail_ref[...] = jnp.zeros_like(tail_ref)

    def conv_tile(j):
        cols = slice(j * tn, (j + 1) * tn)
        ext_ref[0:8, :] = tail_ref[:, cols]
        ext_ref[8:8 + tm, :] = o_ref[0, :, cols]
        tail_ref[:, cols] = ext_ref[tm:tm + 8, :]
        w = cw_ref[:, cols]
        ext = ext_ref[...]
        acc = w[0:1] * ext
        for tap in range(1, GDN_CONV):
            acc = pltpu.roll(acc, 1, axis=0) + w[tap:tap + 1] * ext
        act = _silu(acc[8:8 + tm])
        if j * tn < 2 * GDN_QK_W:
            scale = dh ** -0.5 if j * tn < GDN_QK_W else 1.0
            heads = [act[:, c * dh:(c + 1) * dh] for c in range(tn // dh)]
            heads = [hd * (lax.rsqrt(jnp.sum(hd * hd, axis=-1, keepdims=True) + EPS) * scale) for hd in heads]
            act = jnp.concatenate(heads, axis=1)
        o_ref[0, :, cols] = act

    n = w_ref.shape[1]
    n_conv = GDN_CONV_W // tn
    bounds = [(j * tn, min((j + 1) * tn, n)) for j in range(-(-n // tn))]
    for j, (lo, hi) in enumerate(bounds):
        o_ref[0, :, lo:hi] = _dot(h, w_ref[:, lo:hi])
        if 1 <= j <= n_conv:
            conv_tile(j - 1)
    _gdn_gate_scalars(o_ref[0, :, n - LANES:n], alog_ref[...], dtb_ref[...], col_ref, row_ref)


def _gdn_inproj(x, g, mod, w, conv_w, a_log, dt_bias, *, tm, tn):
    bsz, t, d = x.shape
    n = w.shape[1]
    assert t % tm == 0 and GDN_CONV_W % tn == 0 and GDN_QK_W % tn == 0 and n > GDN_CONV_W
    pad = lambda u: jnp.zeros((1, LANES), F32).at[0, GDN_V_HEADS:2 * GDN_V_HEADS].set(u)
    ncb = tm // GDN_CHUNK
    return pl.pallas_call(
        functools.partial(_gdn_inproj_kernel, tn=tn),
        grid=(bsz, t // tm),
        in_specs=[
            pl.BlockSpec((1, tm, d), lambda b, i: (b, i, 0)),
            pl.BlockSpec((1, d), lambda b, i: (0, 0)),
            pl.BlockSpec((1, 3, d), lambda b, i: (b, 0, 0)),
            pl.BlockSpec((d, n), lambda b, i: (0, 0), pipeline_mode=pl.Buffered(1)),
            pl.BlockSpec((GDN_CONV, GDN_CONV_W), lambda b, i: (0, 0)),
            pl.BlockSpec((1, LANES), lambda b, i: (0, 0)),
            pl.BlockSpec((1, LANES), lambda b, i: (0, 0)),
        ],
        out_specs=[
            pl.BlockSpec((1, tm, n), lambda b, i: (b, i, 0)),
            pl.BlockSpec((1, tm, LANES), lambda b, i: (b, i, 0)),
            pl.BlockSpec((1, ncb, GDN_V_HEADS // 2, 2 * GDN_CHUNK), lambda b, i: (b, i, 0, 0)),
        ],
        out_shape=[
            jax.ShapeDtypeStruct((bsz, t, n), F32),
            jax.ShapeDtypeStruct((bsz, t, LANES), F32),
            jax.ShapeDtypeStruct((bsz, t // GDN_CHUNK, GDN_V_HEADS // 2, 2 * GDN_CHUNK), F32),
        ],
        scratch_shapes=[pltpu.VMEM((tm + 8, tn), F32), pltpu.VMEM((8, GDN_CONV_W), F32)],
        compiler_params=pltpu.CompilerParams(
            dimension_semantics=("parallel", "arbitrary"), vmem_limit_bytes=VMEM_LIMIT),
        name="gdn_norm_inproj_conv",
    )(x, g.reshape(1, d), mod, w, conv_w, pad(a_log), pad(dt_bias))


def _outproj_kernel(o_ref, w_ref, x_ref, mod_ref, *rest, final_norm):
    y = _dot(o_ref[0], w_ref[...])
    x = x_ref[0] + mod_ref[0][2:3] * y
    if final_norm:
        fg_ref, out_ref = rest
        x = x * lax.rsqrt(jnp.mean(x * x, axis=-1, keepdims=True) + EPS) * fg_ref[...]
    else:
        (out_ref,) = rest
    out_ref[0] = x


def _outproj(o, w, x, mod, final_g=None, *, tm):
    bsz, t, d = x.shape
    k = o.shape[-1]
    in_specs = [
        pl.BlockSpec((1, tm, k), lambda b, i: (b, i, 0)),
        pl.BlockSpec((k, d), lambda b, i: (0, 0), pipeline_mode=pl.Buffered(1)),
        pl.BlockSpec((1, tm, d), lambda b, i: (b, i, 0)),
        pl.BlockSpec((1, 3, d), lambda b, i: (b, 0, 0)),
    ]
    args = [o, w, x, mod]
    if final_g is not None:
        in_specs.append(pl.BlockSpec((1, d), lambda b, i: (0, 0)))
        args.append(final_g.reshape(1, d))
    return pl.pallas_call(
        functools.partial(_outproj_kernel, final_norm=final_g is not None),
        grid=(bsz, t // tm),
        in_specs=in_specs,
        out_specs=pl.BlockSpec((1, tm, d), lambda b, i: (b, i, 0)),
        out_shape=jax.ShapeDtypeStruct((bsz, t, d), F32),
        compiler_params=pltpu.CompilerParams(
            dimension_semantics=("parallel", "parallel"), vmem_limit_bytes=VMEM_LIMIT),
        name="outproj_residual",
    )(*args)


def _gdn_gate_scalars(ba, alog, dtb, col_ref, row_ref):
    cs, nh = GDN_CHUNK, GDN_V_HEADS
    lane = lax.broadcasted_iota(jnp.int32, ba.shape, 1)
    g = -jnp.exp(alog) * jax.nn.softplus(ba + dtb)
    vals = jnp.where(lane < nh, jax.nn.sigmoid(ba), g)
    r = lax.broadcasted_iota(jnp.int32, (cs, cs), 0)
    c = lax.broadcasted_iota(jnp.int32, (cs, cs), 1)
    tri = (r >= c).astype(F32)
    is_beta = lax.broadcasted_iota(jnp.int32, (cs, LANES), 1) < nh
    pr = lax.broadcasted_iota(jnp.int32, (nh // 2, LANES), 0)
    pc = lax.broadcasted_iota(jnp.int32, (nh // 2, LANES), 1)
    pick_even = (pc == nh + 2 * pr).astype(F32)
    for n in range(ba.shape[0] // cs):
        v = vals[n * cs:(n + 1) * cs]
        cum = _dot(tri, v, precision=HIGHEST)
        out = jnp.where(is_beta, v, cum)
        col_ref[0, n * cs:(n + 1) * cs, :] = out
        both = jnp.concatenate([out, pltpu.roll(out, LANES - 1, axis=1)], axis=0)
        row_ref[0, n] = _dot_nt(pick_even, both, precision=HIGHEST)


TRI_BASE = 8
PACK = 4


def _block_rows(p):
    n = p.shape[0]
    blk = lax.broadcasted_iota(jnp.int32, p.shape, 1) // n
    return jnp.concatenate([jnp.where(blk == j, p, 0.0) for j in range(PACK)], axis=0).astype(BF16)


def _tri_inverse_packed(mats):
    n = mats[0].shape[0]
    r = lax.broadcasted_iota(jnp.int32, mats[0].shape, 0)
    c = lax.broadcasted_iota(jnp.int32, mats[0].shape, 1) % n
    same = lambda s: (r // s) == (c // s)
    mm = lambda a, b: _dot(a.astype(BF16), _block_rows(b))
    diag = [jnp.where(same(TRI_BASE), a, 0.0) for a in mats]
    inv = [jnp.where(r == c, 1.0, 0.0) - d for d in diag]
    pw = diag
    k = 1
    while 2 * k < TRI_BASE:
        pw = [mm(m, m) for m in pw]
        inv = [p + mm(p, m) for p, m in zip(inv, pw)]
        k *= 2
        yield None
    s = TRI_BASE
    while s < n:
        sub = same(2 * s) & jnp.logical_not(same(s))
        left = [mm(p, jnp.where(sub, a, 0.0)) for p, a in zip(inv, mats)]
        inv = [p - mm(l, p) for p, l in zip(inv, left)]
        s *= 2
        yield None
    yield inv


def _gdn_chunk_kernel(q_ref, k_ref, v_ref, z_ref, col_ref, row_ref, nw_ref,
                      o_ref, s_ref, hu_ref, hwq_ref, hqk_ref, hkd_ref, hgl_ref):
    cs, dh, nh = GDN_CHUNK, GDN_HEAD_DIM, GDN_V_HEADS
    tt = q_ref.shape[1]
    hps = q_ref.shape[2] // dh
    ncb = tt // cs
    assert ncb % 2 == 0 and PACK == 4
    hg = pl.program_id(1)
    ti = pl.program_id(2)
    wslot = ti % 2
    rslot = 1 - wslot
    jidx = lambda hl, n, e: (hl * ncb + n) * 2 + e

    @pl.when(ti == 0)
    def _():
        s_ref[...] = jnp.zeros_like(s_ref)
        for h_ref in (hu_ref, hwq_ref, hqk_ref, hkd_ref, hgl_ref):
            h_ref[1] = jnp.zeros(h_ref.shape[1:], h_ref.dtype)

    chunks = [slice(n * cs, (n + 1) * cs) for n in range(ncb)]

    def prepare():
        q = [q_ref[0, :, hl * dh:(hl + 1) * dh] for hl in range(hps)]
        k = [k_ref[0, :, hl * dh:(hl + 1) * dh] for hl in range(hps)]
        v = [[v_ref[0, :, (2 * hl + e) * dh:(2 * hl + e + 1) * dh] for e in range(2)] for hl in range(hps)]
        lane = lax.broadcasted_iota(jnp.int32, (tt, LANES), 1)
        colv = col_ref[0]
        column = lambda idx: jnp.sum(jnp.where(lane == idx, colv, 0.0), axis=1, keepdims=True)
        hq = [hg * hps + hl for hl in range(hps)]
        beta = [[column(2 * h + e) for e in range(2)] for h in hq]
        gc = [[column(nh + 2 * h + e) for e in range(2)] for h in hq]
        egc = [[jnp.exp(x) for x in pair] for pair in gc]
        yield

        pr = lax.broadcasted_iota(jnp.int32, (cs, PACK * cs), 0)
        pb = lax.broadcasted_iota(jnp.int32, (cs, PACK * cs), 1) // cs
        pc = lax.broadcasted_iota(jnp.int32, (cs, PACK * cs), 1) % cs
        groups = [(hl, gi) for hl in range(hps) for gi in range(ncb // 2)]
        a_mats, g_rows = [], {}
        for hl, gi in groups:
            k16, q16 = k[hl].astype(BF16), q[hl].astype(BF16)
            pair = (chunks[2 * gi], chunks[2 * gi + 1])
            kdup = [jnp.concatenate([k16[sl], k16[sl]], axis=0) for sl in pair]
            kk = jnp.concatenate([_dot_nt(k16[sl], kd) for sl, kd in zip(pair, kdup)], axis=1)
            qk = jnp.concatenate([_dot_nt(q16[sl], kd) for sl, kd in zip(pair, kdup)], axis=1)
            pick = lambda cols: jnp.where(pb == 0, cols[0][pair[0]], jnp.where(
                pb == 1, cols[1][pair[0]], jnp.where(pb == 2, cols[0][pair[1]], cols[1][pair[1]])))
            g_row = jnp.concatenate([row_ref[0, 2 * gi + j, pl.ds(hq[hl], 1), :] for j in range(2)], axis=1)
            decay = jnp.exp(jnp.where(pr >= pc, pick(gc[hl]) - g_row, -jnp.inf))
            a_mats.append(jnp.where(pr > pc, pick(beta[hl]) * kk * decay, 0.0))
            hqk_ref[wslot, hl * (ncb // 2) + gi] = (qk * decay).astype(BF16)
            g_rows[hl, gi] = g_row
            yield
        t_mats = None
        for t_mats in _tri_inverse_packed(a_mats):
            yield

        for (hl, gi), t_mat in zip(groups, t_mats):
            rhs = []
            for n in (2 * gi, 2 * gi + 1):
                sl = chunks[n]
                for e in range(2):
                    j = 2 * (n % 2) + e
                    g_last = g_rows[hl, gi][:, j * cs + cs - 1:(j + 1) * cs]
                    hgl_ref[wslot, jidx(hl, n, e)] = jnp.broadcast_to(jnp.exp(g_last), (1, LANES))
                    hkd_ref[wslot, jidx(hl, n, e)] = (k[hl][sl] * jnp.exp(g_last - gc[hl][e][sl])).T.astype(BF16)
                    kbeta = k[hl][sl] * beta[hl][e][sl]
                    rhs.append(jnp.concatenate([v[hl][e][sl] * beta[hl][e][sl], kbeta * egc[hl][e][sl]], axis=1))
            out = _dot(_block_rows(t_mat), jnp.concatenate(rhs, axis=0).astype(BF16))
            for n in (2 * gi, 2 * gi + 1):
                for e in range(2):
                    uw = out[(2 * (n % 2) + e) * cs:(2 * (n % 2) + e + 1) * cs]
                    hu_ref[wslot, jidx(hl, n, e)] = uw[:, :dh]
                    q_dec = q[hl][chunks[n]] * egc[hl][e][chunks[n]]
                    hwq_ref[wslot, jidx(hl, n, e)] = jnp.concatenate([uw[:, dh:], q_dec], axis=0).astype(BF16)
            yield

    def recurrence():
        nw = nw_ref[...]
        heads = [(hl, e) for hl in range(hps) for e in range(2)]
        state = {he: s_ref[2 * he[0] + he[1]] for he in heads}
        for n, sl in enumerate(chunks):
            ws = {}
            for hl, e in heads:
                ws[hl, e] = _dot(hwq_ref[rslot, jidx(hl, n, e)], state[hl, e].astype(BF16))
            yield
            for hl, e in heads:
                j = jidx(hl, n, e)
                v16 = (hu_ref[rslot, j] - ws[hl, e][:cs]).astype(BF16)
                lb = 2 * (n % 2) + e
                qkd = hqk_ref[rslot, hl * (ncb // 2) + n // 2][:, lb * cs:(lb + 1) * cs]
                o = ws[hl, e][cs:] + _dot(qkd, v16)
                state[hl, e] = state[hl, e] * hgl_ref[rslot, j] + _dot(hkd_ref[rslot, j], v16)
                o = o * lax.rsqrt(jnp.mean(o * o, axis=-1, keepdims=True) + EPS) * nw
                lanes = slice((2 * hl + e) * dh, (2 * hl + e + 1) * dh)
                o_ref[0, sl, lanes] = (o * _silu(z_ref[0, sl, lanes])).astype(o_ref.dtype)
            yield
        for (hl, e), st in state.items():
            s_ref[2 * hl + e] = st

    prep, rec = prepare(), recurrence()
    n_prep = 1 + hps * (ncb // 2) * 2 + 6
    per_stage = -(-n_prep // (2 * ncb))
    prep_live = rec_live = True
    while prep_live or rec_live:
        for _ in range(per_stage):
            if prep_live:
                prep_live = next(prep, "done") != "done"
        if rec_live:
            rec_live = next(rec, "done") != "done"


def _gdn_chunk(proj, col, row, norm_w, *, tt=GDN_TILE, hps=GDN_HEADS_PER_STEP):
    bsz, t, _ = proj.shape
    dh = GDN_HEAD_DIM
    qw, vw = hps * dh, 2 * hps * dh
    k_blk0 = GDN_QK_W // qw
    v_blk0 = 2 * GDN_QK_W // vw
    z_blk0 = GDN_CONV_W // vw
    ncb = tt // GDN_CHUNK
    nt = t // tt
    nj = hps * ncb * 2
    cur = lambda i: jnp.minimum(i, nt - 1)
    prev = lambda i: jnp.maximum(i - 1, 0)
    return pl.pallas_call(
        _gdn_chunk_kernel,
        grid=(bsz, GDN_QK_HEADS // hps, nt + 1),
        in_specs=[
            pl.BlockSpec((1, tt, qw), lambda b, h, i: (b, cur(i), h)),
            pl.BlockSpec((1, tt, qw), lambda b, h, i: (b, cur(i), k_blk0 + h)),
            pl.BlockSpec((1, tt, vw), lambda b, h, i: (b, cur(i), v_blk0 + h)),
            pl.BlockSpec((1, tt, vw), lambda b, h, i: (b, prev(i), z_blk0 + h)),
            pl.BlockSpec((1, tt, LANES), lambda b, h, i: (b, cur(i), 0)),
            pl.BlockSpec((1, ncb, GDN_V_HEADS // 2, 2 * GDN_CHUNK), lambda b, h, i: (b, cur(i), 0, 0)),
            pl.BlockSpec((1, dh), lambda b, h, i: (0, 0)),
        ],
        out_specs=pl.BlockSpec((1, tt, vw), lambda b, h, i: (b, prev(i), h)),
        out_shape=jax.ShapeDtypeStruct((bsz, t, GDN_V_W), BF16),
        scratch_shapes=[
            pltpu.VMEM((2 * hps, dh, dh), F32),
            pltpu.VMEM((2, nj, GDN_CHUNK, dh), F32),
            pltpu.VMEM((2, nj, 2 * GDN_CHUNK, dh), BF16),
            pltpu.VMEM((2, nj // PACK, GDN_CHUNK, PACK * GDN_CHUNK), BF16),
            pltpu.VMEM((2, nj, dh, GDN_CHUNK), BF16),
            pltpu.VMEM((2, nj, 1, LANES), F32),
        ],
        compiler_params=pltpu.CompilerParams(
            dimension_semantics=("parallel", "parallel", "arbitrary"), vmem_limit_bytes=VMEM_LIMIT),
        name="gdn_chunk_scan",
    )(proj, proj, proj, proj, col, row, norm_w.reshape(1, dh))


def _gdn_layer(x, mod, norm_g, w_in, conv_w, a_log, dt_bias, norm_w, w_out):
    n_in = w_in.shape[1]
    n_pad = -(-n_in // LANES) * LANES
    w_in_p = jnp.pad(w_in, ((0, 0), (0, n_pad - n_in))).astype(BF16)
    proj, col, row = _gdn_inproj(x, norm_g, mod, w_in_p, conv_w, a_log, dt_bias,
                                 tm=GDN_INPROJ_ROWS, tn=GDN_INPROJ_COLS)
    o = _gdn_chunk(proj, col, row, norm_w)
    return _outproj(o, w_out.astype(BF16), x, mod, tm=OUTPROJ_ROWS)


NSA_HEADS = 16
NSA_GROUPS = 4
NSA_HPG = NSA_HEADS // NSA_GROUPS
NSA_HEAD_DIM = 64
NSA_CMP_LEN = 32
NSA_CMP_STRIDE = 16
NSA_SLC_LEN = 64
NSA_TOP_K = 8
NSA_WINDOW = 512
NSA_QTILE = 256
NSA_Q_W = NSA_HEADS * NSA_HEAD_DIM
NSA_KV_W = NSA_GROUPS * NSA_HEAD_DIM
REL_BUCKETS = 32
REL_MAX_DIST = 128
FEAT_LANE0 = NSA_HEAD_DIM
CONST_LANE0 = FEAT_LANE0 + 32
VT_PAD = 16
NSA_COL_Q = 0
NSA_COL_CMP = NSA_Q_W
NSA_COL_SEL = NSA_COL_CMP + 2 * NSA_KV_W
NSA_COL_WIN = NSA_COL_SEL + 2 * NSA_KV_W
NSA_COL_Z = NSA_COL_WIN + 2 * NSA_KV_W
NSA_COL_GATE = NSA_COL_Z + NSA_Q_W
NSA_PROJ_W = NSA_COL_GATE + LANES


def _nsa_column_perm():
    g, dh = NSA_GROUPS, NSA_HEAD_DIM
    kv0 = NSA_Q_W
    cols = list(range(NSA_Q_W))
    cols += [kv0 + i for i in range(2 * NSA_KV_W)]
    for br in (1, 2):
        for gi in range(g):
            cols += [kv0 + (2 * br) * NSA_KV_W + gi * dh + d for d in range(dh)]
            cols += [kv0 + (2 * br + 1) * NSA_KV_W + gi * dh + d for d in range(dh)]
    gate0 = kv0 + 6 * NSA_KV_W
    cols += [gate0 + 3 * NSA_HEADS + i for i in range(NSA_Q_W)]
    cols += [gate0 + i for i in range(3 * NSA_HEADS)] + [-1] * (LANES - 3 * NSA_HEADS)
    assert len(cols) == NSA_PROJ_W
    return np.asarray(cols, np.int32)


def _rel_bucket_table(n):
    d = np.arange(n)
    max_exact = REL_BUCKETS // 2
    nf = np.maximum(d, 1).astype(np.float64)
    large = max_exact + (np.log(nf / max_exact) / math.log(REL_MAX_DIST / max_exact)
                         * (REL_BUCKETS - max_exact)).astype(np.int32)
    large = np.minimum(large, REL_BUCKETS - 1)
    return np.where(d < max_exact, d, large).astype(np.int32)


def _nsa_tables(rel_bias, t):
    qb = NSA_QTILE
    bucket = _rel_bucket_table(t)
    assert np.all(bucket[qb + 1:] == REL_BUCKETS - 1)
    bvec = rel_bias[bucket].T
    far = rel_bias[REL_BUCKETS - 1]
    far_hi = far.astype(BF16)
    far_lo = (far - far_hi.astype(F32)).astype(BF16)
    far_sum = far_hi.astype(F32) + far_lo.astype(F32)
    r = np.arange(qb)[:, None]
    c = np.arange(qb)[None, :]
    d0 = r - c

    def toeplitz(w):
        n = 2 * qb - 1
        ext = jnp.pad(w[:, ::-1], ((0, 0), (0, 1)))
        skew = jnp.tile(ext, (1, qb))[:, :qb * n].reshape(w.shape[0], qb, n)
        return skew[:, :, qb - 1:]

    rel = bvec[:, :2 * qb] - far_sum[:, None]
    t0 = toeplitz(jnp.concatenate([jnp.full((NSA_HEADS, qb - 1), NEG_INF, F32), rel[:, :qb]], axis=1)[:, ::-1])
    t1 = toeplitz(rel[:, 1:][:, ::-1])
    g, hpg = NSA_GROUPS, NSA_HPG
    none = jnp.full_like(t0, NEG_INF)
    near = jnp.stack([jnp.concatenate([t1, t0], axis=1), jnp.concatenate([t0, none], axis=1)], axis=1)
    near = near.reshape(g, hpg, 2, 2 * qb, qb)
    nb = t // NSA_CMP_STRIDE
    per_tile = qb // NSA_CMP_STRIDE
    back = 9
    far_d = back * NSA_CMP_STRIDE - (NSA_CMP_LEN - 1)
    assert np.all(bucket[far_d:] == REL_BUCKETS - 1)
    width = (qb - 1 + far_d) // NSA_CMP_STRIDE + 1
    dm = c - NSA_CMP_STRIDE * np.arange(width)[:, None] + far_d
    band = jnp.where(dm >= 0, bvec[:, np.maximum(dm, 0)], NEG_INF)
    tiles = []
    for i in range(t // qb):
        j0 = per_tile * i - back
        lo, hi = max(j0, 0), min(j0 + width, nb)
        tiles.append(jnp.concatenate([
            jnp.broadcast_to(far[:, None, None], (NSA_HEADS, lo, qb)),
            band[:, lo - j0:hi - j0, :],
            jnp.full((NSA_HEADS, nb - hi, qb), NEG_INF, F32)], axis=1))
    cmp_bias = jnp.stack(tiles, axis=0).reshape(t // qb, g, hpg, nb, qb)
    qconst = jnp.zeros((g, 8, hpg, qb), F32)
    qconst = qconst.at[:, 0].set(jnp.broadcast_to(far_hi.astype(F32).reshape(g, hpg, 1), (g, hpg, qb)))
    qconst = qconst.at[:, 1].set(jnp.broadcast_to(far_lo.astype(F32).reshape(g, hpg, 1), (g, hpg, qb)))
    return near, cmp_bias, qconst.reshape(g, 8, hpg * qb)


def _overlap_t(t):
    n_cmp = (t - NSA_CMP_LEN) // NSA_CMP_STRIDE + 1
    n_slc = t // NSA_SLC_LEN
    c_start = np.arange(n_cmp)[:, None] * NSA_CMP_STRIDE
    s_start = np.arange(n_slc)[None, :] * NSA_SLC_LEN
    ov = np.clip(np.minimum(c_start + NSA_CMP_LEN, s_start + NSA_SLC_LEN) - np.maximum(c_start, s_start), 0, None)
    ov = ov.astype(np.float32) / NSA_CMP_LEN
    out = np.zeros((32, t // NSA_CMP_STRIDE), np.float32)
    out[:n_slc, :n_cmp] = ov.T
    return out


def _nsa_compress_kernel(x_ref, pos_ref, w1_ref, w2_ref, o_ref, xs_ref):
    t = x_ref.shape[1]
    nb = t // NSA_CMP_STRIDE
    nlt = xs_ref.shape[0]
    for c in range(nlt):
        xs_ref[c, 0:t, :] = x_ref[0, :, c * LANES:(c + 1) * LANES]
        xs_ref[c, t:t + NSA_CMP_STRIDE, :] = jnp.zeros((NSA_CMP_STRIDE, LANES), F32)
    acc = jnp.zeros((nb, w1_ref.shape[2]), F32)
    for l in range(NSA_CMP_LEN):
        xl = jnp.concatenate([xs_ref[c, pl.ds(l, nb, stride=NSA_CMP_STRIDE), :] for c in range(nlt)], axis=1)
        xl = xl + pos_ref[l:l + 1, :]
        acc = acc + _dot(xl.astype(BF16), w1_ref[l])
    hid = _silu(acc).astype(BF16)
    res = _dot(hid, w2_ref[...])
    for g in range(NSA_GROUPS):
        o_ref[0, g] = res[:, g * LANES:(g + 1) * LANES]


def _nsa_compress(proj, cmp_pos, cmp_w1, cmp_w2):
    bsz, t, _ = proj.shape
    g, dh = NSA_GROUPS, NSA_HEAD_DIM
    nb = t // NSA_CMP_STRIDE
    w = 2 * NSA_KV_W
    w1 = cmp_w1.reshape(2, NSA_CMP_LEN, dh, dh).astype(BF16)
    w2 = cmp_w2.astype(BF16)
    place = lambda blk, c0: jnp.pad(blk, [(0, 0)] * (blk.ndim - 1) + [(c0, w - dh - c0)])
    w1c = jnp.concatenate([place(w1[i], (gi * 2 + i) * dh) for i in range(2) for gi in range(g)], axis=1)
    w2c = jnp.concatenate([place(w2[i], (gi * 2 + i) * dh) for gi in range(g) for i in range(2)], axis=0)
    pos = jnp.broadcast_to(cmp_pos[:, :, None, :], (2, NSA_CMP_LEN, g, dh)).transpose(1, 0, 2, 3).reshape(NSA_CMP_LEN, w)
    return pl.pallas_call(
        _nsa_compress_kernel,
        grid=(bsz,),
        in_specs=[
            pl.BlockSpec((1, t, w), lambda b: (b, 0, NSA_COL_CMP // w)),
            pl.BlockSpec((NSA_CMP_LEN, w), lambda b: (0, 0)),
            pl.BlockSpec((NSA_CMP_LEN, w, w), lambda b: (0, 0, 0), pipeline_mode=pl.Buffered(1)),
            pl.BlockSpec((w, w), lambda b: (0, 0)),
        ],
        out_specs=pl.BlockSpec((1, g, nb, LANES), lambda b: (b, 0, 0, 0)),
        out_shape=jax.ShapeDtypeStruct((bsz, g, nb, LANES), F32),
        scratch_shapes=[pltpu.VMEM((w // LANES, t + NSA_CMP_STRIDE, LANES), F32)],
        compiler_params=pltpu.CompilerParams(dimension_semantics=("parallel",), vmem_limit_bytes=VMEM_LIMIT),
        name="nsa_compress",
    )(proj, pos, w1c, w2c)


def _nsa_attn_kernel(q_ref, kvs_ref, kvw_ref, kvc_ref, gate_ref, z_ref, cb_ref, near_ref, qc_ref, ovl_ref,
                     o_ref, ks_ref, vs_ref, kw_ref, vw_ref, gt_ref, ms_ref, accs_ref, mw_ref, accw_ref, sc_ref):
    qb, dh, hpg = NSA_QTILE, NSA_HEAD_DIM, NSA_HPG
    t = kvs_ref.shape[1]
    nblk = t // NSA_SLC_LEN
    cols = hpg * qb
    g = pl.program_id(1)
    i = pl.program_id(2)

    @pl.when(i == 0)
    def _():
        tok = lax.broadcasted_iota(jnp.int32, (t, LANES), 0)
        ln = lax.broadcasted_iota(jnp.int32, (t, LANES), 1)
        const = jnp.where((ln == CONST_LANE0) | (ln == CONST_LANE0 + 1), 1.0, 0.0)
        onehot = jnp.where(ln - FEAT_LANE0 == tok // NSA_SLC_LEN, 1.0, 0.0)
        ones_rows = jnp.where(lax.broadcasted_iota(jnp.int32, (VT_PAD, t), 0) == 0, 1.0, 0.0)
        kvs = kvs_ref[0]
        kvw = kvw_ref[0]
        ks_ref[...] = jnp.where(ln < dh, kvs, onehot + const).astype(BF16)
        kw_ref[...] = jnp.where(ln < dh, kvw, const).astype(BF16)
        vs_ref[...] = jnp.concatenate([ones_rows, kvs.T[dh:]], axis=0).astype(BF16)
        vw_ref[...] = jnp.concatenate([ones_rows, kvw.T[dh:]], axis=0).astype(BF16)

    q_t = (q_ref[0] * (dh ** -0.5)).T
    q_heads = jnp.concatenate([q_t[hh * dh:(hh + 1) * dh] for hh in range(hpg)], axis=1)

    def scores(branch, qa_t, start, nk, bias):
        sc = _dot(branch[0][pl.ds(pl.multiple_of(start, qb), nk), :], qa_t)
        return sc if bias is None else sc + bias

    def update(branch, start, nk, sc):
        _, vt_ref, m_ref, acc_ref = branch
        m_old = m_ref[...]
        m_new = jnp.maximum(m_old, jnp.max(sc, axis=0, keepdims=True))
        alpha = jnp.exp(m_old - m_new)
        pe = jnp.exp(sc - m_new).astype(BF16)
        acc_ref[...] = alpha * acc_ref[...] + _dot(vt_ref[:, pl.ds(pl.multiple_of(start, qb), nk)], pe)
        m_ref[...] = m_new

    sel = (ks_ref, vs_ref, ms_ref, accs_ref)
    win = (kw_ref, vw_ref, mw_ref, accw_ref)
    for m_ref, acc_ref in ((ms_ref, accs_ref), (mw_ref, accw_ref)):
        m_ref[...] = jnp.full(m_ref.shape, NEG_INF, F32)
        acc_ref[...] = jnp.zeros(acc_ref.shape, F32)

    nwt = NSA_WINDOW // qb
    assert nwt in (2, 4)
    pad_rows = jnp.zeros((LANES - CONST_LANE0 - 8, cols), F32)
    qa_win = jnp.concatenate([q_heads, jnp.zeros((32, cols), F32), qc_ref[0], pad_rows], axis=0).astype(BF16)

    kvc = kvc_ref[0, 0]
    lane_k = lax.broadcasted_iota(jnp.int32, kvc.shape, 1)
    kc16 = jnp.where(lane_k < dh, kvc, 0.0).astype(BF16)
    s = _dot(kc16, qa_win) + jnp.concatenate([cb_ref[0, 0, hh] for hh in range(hpg)], axis=1)

    kk = lax.broadcasted_iota(jnp.int32, (qb, cols), 0)
    rr = lax.broadcasted_iota(jnp.int32, (qb, cols), 1) % qb
    wold_start = jnp.maximum(i - nwt, 0) * qb
    sc_wold = scores(win, qa_win, wold_start, qb, jnp.where((rr < kk) & (i >= nwt), 0.0, NEG_INF))

    row2 = lax.broadcasted_iota(jnp.int32, (2 * qb, cols), 0)
    if nwt == 4:
        w32_start = jnp.maximum(i - 3, 0) * qb
        sc_w32 = scores(win, qa_win, w32_start, 2 * qb,
                        jnp.where(row2 < (i - 1) * qb - w32_start, 0.0, NEG_INF))
    near_start = jnp.maximum(i - 1, 0) * qb
    variant = jnp.where(i == 0, 1, 0)
    near_bias = jnp.concatenate([near_ref[0, hh, variant] for hh in range(hpg)], axis=1)
    sc_wn = scores(win, qa_win, near_start, 2 * qb, near_bias)

    s = jnp.exp(s - jnp.max(s, axis=0, keepdims=True))
    p = s / jnp.sum(s, axis=0, keepdims=True)
    tq_lane = i * qb + lax.broadcasted_iota(jnp.int32, (1, cols), 1) % qb
    p16 = (p * (tq_lane >= NSA_CMP_LEN - 1).astype(F32)).astype(BF16)
    o_cmp = _dot(kvc.T.astype(BF16), p16)
    ovl = ovl_ref[...].astype(BF16)
    imp = _dot(ovl, p16[:, 0:qb])
    for hh in range(1, hpg):
        imp = imp + _dot(ovl, p16[:, hh * qb:(hh + 1) * qb])

    update(win, wold_start, qb, sc_wold)

    blk = lax.broadcasted_iota(jnp.int32, (32, qb), 0)
    tq = i * qb + lax.broadcasted_iota(jnp.int32, (32, qb), 1)
    cur = tq // NSA_SLC_LEN
    forced = (blk == 0) | (blk == cur) | (blk == cur - 1)
    val = jnp.where(forced, jnp.inf, jnp.where(blk * NSA_SLC_LEN <= tq, imp, -jnp.inf))
    cnt = jnp.zeros((32, qb), jnp.int32)
    for s2 in range(nblk):
        other = val[s2:s2 + 1, :]
        cnt = cnt + ((other > val) | ((other == val) & (s2 < blk))).astype(jnp.int32)
    feat = jnp.where((cnt < min(NSA_TOP_K, nblk)) & (blk < nblk), 0.0, NEG_INF)
    qa = jnp.concatenate([q_heads, jnp.concatenate([feat] * hpg, axis=1), qc_ref[0], pad_rows],
                         axis=0).astype(BF16)

    if nwt == 4:
        update(win, w32_start, 2 * qb, sc_w32)
    sc_sn = scores(sel, qa, near_start, 2 * qb, near_bias)
    update(win, near_start, 2 * qb, sc_wn)
    update(sel, near_start, 2 * qb, sc_sn)

    n_far = jnp.maximum(i - 1, 0)
    n_pairs = (n_far + 1) // 2
    pair_start = lambda p: jnp.maximum(2 * p - n_far % 2, 0) * qb
    last_pair = jnp.maximum(n_pairs - 1, 0)
    sc_ref[...] = scores(sel, qa, 0, 2 * qb, jnp.where(row2 < (2 - n_far % 2) * qb, 0.0, NEG_INF))

    def sel_body(k, carry):
        sc_odd = scores(sel, qa, pair_start(2 * k + 1), 2 * qb, None)
        update(sel, pair_start(2 * k), 2 * qb, sc_ref[...])
        sc_ref[...] = scores(sel, qa, pair_start(jnp.minimum(2 * k + 2, last_pair)), 2 * qb, None)
        update(sel, pair_start(2 * k + 1), 2 * qb, sc_odd)
        return carry

    lax.fori_loop(0, n_pairs // 2, sel_body, 0)

    @pl.when(n_pairs % 2 == 1)
    def _():
        update(sel, pair_start(n_pairs - 1), 2 * qb, sc_ref[...])

    def finish(acc_ref):
        acc = acc_ref[...]
        return acc[VT_PAD:] / acc[0:1]

    o_slc = finish(accs_ref)
    o_win = finish(accw_ref)

    gt_ref[...] = jax.nn.sigmoid(gate_ref[0]).T
    outs = []
    for hh in range(hpg):
        sl = slice(hh * qb, (hh + 1) * qb)
        base = (g * hpg + hh) * 3
        gate = [gt_ref[pl.ds(base + br, 1), :] for br in range(3)]
        outs.append(gate[0] * o_cmp[dh:, sl] + gate[1] * o_slc[:, sl] + gate[2] * o_win[:, sl])
    out = jnp.concatenate(outs, axis=0).T
    o_ref[0] = (out * _silu(z_ref[0])).astype(o_ref.dtype)


def _nsa_attn(proj, kv_cmp, near, cmp_bias, qconst, ovl):
    bsz, t, _ = proj.shape
    qb, hpg = NSA_QTILE, NSA_HPG
    gw = hpg * NSA_HEAD_DIM
    nb = t // NSA_CMP_STRIDE
    cols = hpg * qb
    return pl.pallas_call(
        _nsa_attn_kernel,
        grid=(bsz, NSA_GROUPS, t // qb),
        in_specs=[
            pl.BlockSpec((1, qb, gw), lambda b, g, i: (b, i, NSA_COL_Q // gw + g)),
            pl.BlockSpec((1, t, LANES), lambda b, g, i: (b, 0, NSA_COL_SEL // LANES + g)),
            pl.BlockSpec((1, t, LANES), lambda b, g, i: (b, 0, NSA_COL_WIN // LANES + g)),
            pl.BlockSpec((1, 1, nb, LANES), lambda b, g, i: (b, g, 0, 0)),
            pl.BlockSpec((1, qb, LANES), lambda b, g, i: (b, i, NSA_COL_GATE // LANES)),
            pl.BlockSpec((1, qb, gw), lambda b, g, i: (b, i, NSA_COL_Z // gw + g)),
            pl.BlockSpec((1, 1, hpg, nb, qb), lambda b, g, i: (i, g, 0, 0, 0)),
            pl.BlockSpec((1, hpg, 2, 2 * qb, qb), lambda b, g, i: (g, 0, 0, 0, 0)),
            pl.BlockSpec((1, 8, cols), lambda b, g, i: (g, 0, 0)),
            pl.BlockSpec((32, nb), lambda b, g, i: (0, 0)),
        ],
        out_specs=pl.BlockSpec((1, qb, gw), lambda b, g, i: (b, i, g)),
        out_shape=jax.ShapeDtypeStruct((bsz, t, NSA_Q_W), BF16),
        scratch_shapes=[
            pltpu.VMEM((t, LANES), BF16), pltpu.VMEM((VT_PAD + NSA_HEAD_DIM, t), BF16),
            pltpu.VMEM((t, LANES), BF16), pltpu.VMEM((VT_PAD + NSA_HEAD_DIM, t), BF16),
            pltpu.VMEM((LANES, qb), F32),
            pltpu.VMEM((1, cols), F32), pltpu.VMEM((VT_PAD + NSA_HEAD_DIM, cols), F32),
            pltpu.VMEM((1, cols), F32), pltpu.VMEM((VT_PAD + NSA_HEAD_DIM, cols), F32),
            pltpu.VMEM((2 * qb, cols), F32),
        ],
        compiler_params=pltpu.CompilerParams(
            dimension_semantics=("parallel", "parallel", "arbitrary"), vmem_limit_bytes=VMEM_LIMIT),
        name="nsa_attention",
    )(proj, proj, proj, kv_cmp, proj, proj, cmp_bias, near, qconst, ovl)


def _nsa_layer(x, mod, norm_g, w_in, cmp_pos, cmp_w1, cmp_w2, rel_bias, w_out, final_g):
    t = x.shape[1]
    assert t // NSA_SLC_LEN <= 32 and NSA_COL_Z % (NSA_HPG * NSA_HEAD_DIM) == 0
    perm = _nsa_column_perm()
    cuts = [0] + [j for j in range(1, len(perm)) if perm[j] != perm[j - 1] + (perm[j - 1] >= 0)] + [len(perm)]
    runs = [(int(perm[a]), b - a) for a, b in zip(cuts[:-1], cuts[1:])]
    w16 = w_in.astype(BF16)
    w_in_p = jnp.concatenate([w16[:, s:s + n] if s >= 0 else jnp.zeros((w_in.shape[0], n), BF16)
                              for s, n in runs], axis=1)
    proj = _inproj(x, norm_g, mod, w_in_p, tm=NSA_INPROJ_ROWS, tn=NSA_PROJ_W)
    kv_cmp = _nsa_compress(proj, cmp_pos, cmp_w1, cmp_w2)
    near, cmp_bias, qconst = _nsa_tables(rel_bias, t)
    o = _nsa_attn(proj, kv_cmp, near, cmp_bias, qconst, jnp.asarray(_overlap_t(t)))
    return _outproj(o, w_out.astype(BF16), x, mod, final_g, tm=OUTPROJ_ROWS)


def kernel(x, c, ada_w, ada_b, norm_g, gdn_w_in, gdn_conv_w, gdn_a_log, gdn_dt_bias, gdn_norm_w, gdn_w_out,
           nsa_w_in, nsa_cmp_pos, nsa_cmp_w1, nsa_cmp_w2, nsa_w_out, rel_bias, final_g):
    bsz, t, d = x.shape
    mod = _modulation(c, ada_w, ada_b).reshape(ada_w.shape[0], bsz, 3, d)
    x = _gdn_layer(x, mod[0], norm_g[0], gdn_w_in[0], gdn_conv_w[0], gdn_a_log[0], gdn_dt_bias[0],
                   gdn_norm_w[0], gdn_w_out[0])
    return _nsa_layer(x, mod[1], norm_g[1], nsa_w_in[0], nsa_cmp_pos[0], nsa_cmp_w1[0], nsa_cmp_w2[0],
                      rel_bias, nsa_w_out[0], final_g)
```

```python
import functools
import math

import numpy as np
import jax
import jax.numpy as jnp
from jax import lax
from jax.experimental import pallas as pl
from jax.experimental.pallas import tpu as pltpu

F32 = jnp.float32
BF16 = jnp.bfloat16
HIGHEST = lax.Precision.HIGHEST

EPS = 1e-6
NEG_INF = -1e30
LANES = 128
VMEM_LIMIT = 56 * 1024 * 1024

GDN_QK_HEADS = 8
GDN_V_HEADS = 16
GDN_HEAD_DIM = 128
GDN_CONV = 4
GDN_CHUNK = 64
GDN_QK_W = GDN_QK_HEADS * GDN_HEAD_DIM
GDN_V_W = GDN_V_HEADS * GDN_HEAD_DIM
GDN_CONV_W = 2 * GDN_QK_W + GDN_V_W
GDN_TILE = 128
GDN_HEADS_PER_STEP = 8
GDN_INPROJ_ROWS = 256
GDN_INPROJ_COLS = 4 * LANES
NSA_INPROJ_ROWS = 512
OUTPROJ_ROWS = 1024


def _silu(x):
    return x * jax.nn.sigmoid(x)


def _dot(a, b, **kw):
    return jnp.dot(a, b, preferred_element_type=F32, **kw)


def _dot_nt(a, b, **kw):
    return lax.dot_general(a, b, (((1,), (1,)), ((), ())), preferred_element_type=F32, **kw)


def _mod_kernel(c_ref, w_ref, b_ref, o_ref):
    cond = _silu(c_ref[...])
    o_ref[0] = _dot(cond, w_ref[0], precision=HIGHEST) + b_ref[0]


def _modulation(c, ada_w, ada_b):
    depth, d, d3 = ada_w.shape
    bsz = c.shape[0]
    return pl.pallas_call(
        _mod_kernel,
        grid=(depth, d3 // d),
        in_specs=[
            pl.BlockSpec((bsz, d), lambda i, j: (0, 0)),
            pl.BlockSpec((1, d, d), lambda i, j: (i, 0, j)),
            pl.BlockSpec((1, 1, d), lambda i, j: (i, 0, j)),
        ],
        out_specs=pl.BlockSpec((1, bsz, d), lambda i, j: (i, 0, j)),
        out_shape=jax.ShapeDtypeStruct((depth, bsz, d3), F32),
        name="adaln_mod",
    )(c, ada_w, ada_b.reshape(depth, 1, d3))


def _inproj_kernel(x_ref, g_ref, mod_ref, w_ref, o_ref, *, tn):
    x = x_ref[0]
    m = mod_ref[0]
    y = x * lax.rsqrt(jnp.mean(x * x, axis=-1, keepdims=True) + EPS) * g_ref[...]
    h = (y * (1.0 + m[1:2]) + m[0:1]).astype(BF16)
    for j in range(w_ref.shape[1] // tn):
        o_ref[0, :, j * tn:(j + 1) * tn] = _dot(h, w_ref[:, j * tn:(j + 1) * tn])


def _inproj(x, g, mod, w, *, tm, tn):
    bsz, t, d = x.shape
    n = w.shape[1]
    assert t % tm == 0 and n % tn == 0
    return pl.pallas_call(
        functools.partial(_inproj_kernel, tn=tn),
        grid=(bsz, t // tm),
        in_specs=[
            pl.BlockSpec((1, tm, d), lambda b, i: (b, i, 0)),
            pl.BlockSpec((1, d), lambda b, i: (0, 0)),
            pl.BlockSpec((1, 3, d), lambda b, i: (b, 0, 0)),
            pl.BlockSpec((d, n), lambda b, i: (0, 0), pipeline_mode=pl.Buffered(1)),
        ],
        out_specs=pl.BlockSpec((1, tm, n), lambda b, i: (b, i, 0)),
        out_shape=jax.ShapeDtypeStruct((bsz, t, n), F32),
        compiler_params=pltpu.CompilerParams(
            dimension_semantics=("parallel", "parallel"), vmem_limit_bytes=VMEM_LIMIT),
        name="norm_mod_inproj",
    )(x, g.reshape(1, d), mod, w)


def _gdn_inproj_kernel(x_ref, g_ref, mod_ref, w_ref, wba_ref, cw_ref, alog_ref, dtb_ref, o_ref, col_ref, row_ref,
                        ext_ref, tail_ref, *, tn):
    dh = GDN_HEAD_DIM
    tm = x_ref.shape[1]
    x = x_ref[0]
    m = mod_ref[0]
    y = x * lax.rsqrt(jnp.mean(x * x, axis=-1, keepdims=True) + EPS) * g_ref[...]
    h = (y * (1.0 + m[1:2]) + m[0:1]).astype(BF16)

    @pl.when(pl.program_id(1) == 0)
    def _():
        tail_ref[...] = jnp.zeros_like(tail_ref)

    def conv_tile(j):
        cols = slice(j * tn, (j + 1) * tn)
        ext_ref[0:8, :] = tail_ref[:, cols]
        ext_ref[8:8 + tm, :] = o_ref[0, :, cols]
        tail_ref[:, cols] = ext_ref[tm:tm + 8, :]
        w = cw_ref[:, cols]
        ext = ext_ref[...]
        acc = w[0:1] * ext
        for tap in range(1, GDN_CONV):
            acc = pltpu.roll(acc, 1, axis=0) + w[tap:tap + 1] * ext
        act = _silu(acc[8:8 + tm])
        if j * tn < 2 * GDN_QK_W:
            scale = dh ** -0.5 if j * tn < GDN_QK_W else 1.0
            heads = [act[:, c * dh:(c + 1) * dh] for c in range(tn // dh)]
            heads = [hd * (lax.rsqrt(jnp.sum(hd * hd, axis=-1, keepdims=True) + EPS) * scale) for hd in heads]
            act = jnp.concatenate(heads, axis=1)
        o_ref[0, :, cols] = act

    n = w_ref.shape[1]
    n_conv = GDN_CONV_W // tn
    bounds = [(j * tn, min((j + 1) * tn, n)) for j in range(-(-n // tn))]
    for j, (lo, hi) in enumerate(bounds):
        o_ref[0, :, lo:hi] = _dot(h, w_ref[:, lo:hi])
        if 1 <= j <= n_conv:
            conv_tile(j - 1)
    _gdn_gate_scalars(_dot(h, wba_ref[...]), alog_ref[...], dtb_ref[...], col_ref, row_ref)


def _gdn_inproj(x, g, mod, w, w_ba, conv_w, a_log, dt_bias, *, tm, tn):
    bsz, t, d = x.shape
    n = w.shape[1]
    assert t % tm == 0 and GDN_CONV_W % tn == 0 and GDN_QK_W % tn == 0 and n % tn == 0 and n > GDN_CONV_W
    pad = lambda u: jnp.zeros((1, LANES), F32).at[0, GDN_V_HEADS:2 * GDN_V_HEADS].set(u)
    ncb = tm // GDN_CHUNK
    return pl.pallas_call(
        functools.partial(_gdn_inproj_kernel, tn=tn),
        grid=(bsz, t // tm),
        in_specs=[
            pl.BlockSpec((1, tm, d), lambda b, i: (b, i, 0)),
            pl.BlockSpec((1, d), lambda b, i: (0, 0)),
            pl.BlockSpec((1, 3, d), lambda b, i: (b, 0, 0)),
            pl.BlockSpec((d, n), lambda b, i: (0, 0), pipeline_mode=pl.Buffered(1)),
            pl.BlockSpec((d, LANES), lambda b, i: (0, 0)),
            pl.BlockSpec((GDN_CONV, GDN_CONV_W), lambda b, i: (0, 0)),
            pl.BlockSpec((1, LANES), lambda b, i: (0, 0)),
            pl.BlockSpec((1, LANES), lambda b, i: (0, 0)),
        ],
        out_specs=[
            pl.BlockSpec((1, tm, n), lambda b, i: (b, i, 0)),
            pl.BlockSpec((1, tm, LANES), lambda b, i: (b, i, 0)),
            pl.BlockSpec((1, ncb, GDN_V_HEADS // 2, 2 * GDN_CHUNK), lambda b, i: (b, i, 0, 0)),
        ],
        out_shape=[
            jax.ShapeDtypeStruct((bsz, t, n), F32),
            jax.ShapeDtypeStruct((bsz, t, LANES), F32),
            jax.ShapeDtypeStruct((bsz, t // GDN_CHUNK, GDN_V_HEADS // 2, 2 * GDN_CHUNK), F32),
        ],
        scratch_shapes=[pltpu.VMEM((tm + 8, tn), F32), pltpu.VMEM((8, GDN_CONV_W), F32)],
        compiler_params=pltpu.CompilerParams(
            dimension_semantics=("parallel", "arbitrary"), vmem_limit_bytes=VMEM_LIMIT),
        name="gdn_norm_inproj_conv",
    )(x, g.reshape(1, d), mod, w, w_ba, conv_w, pad(a_log), pad(dt_bias))


def _outproj_kernel(o_ref, w_ref, x_ref, mod_ref, *rest, final_norm):
    y = _dot(o_ref[0], w_ref[...])
    x = x_ref[0] + mod_ref[0][2:3] * y
    if final_norm:
        fg_ref, out_ref = rest
        x = x * lax.rsqrt(jnp.mean(x * x, axis=-1, keepdims=True) + EPS) * fg_ref[...]
    else:
        (out_ref,) = rest
    out_ref[0] = x


def _outproj(o, w, x, mod, final_g=None, *, tm):
    bsz, t, d = x.shape
    k = o.shape[-1]
    in_specs = [
        pl.BlockSpec((1, tm, k), lambda b, i: (b, i, 0)),
        pl.BlockSpec((k, d), lambda b, i: (0, 0), pipeline_mode=pl.Buffered(1)),
        pl.BlockSpec((1, tm, d), lambda b, i: (b, i, 0)),
        pl.BlockSpec((1, 3, d), lambda b, i: (b, 0, 0)),
    ]
    args = [o, w, x, mod]
    if final_g is not None:
        in_specs.append(pl.BlockSpec((1, d), lambda b, i: (0, 0)))
        args.append(final_g.reshape(1, d))
    return pl.pallas_call(
        functools.partial(_outproj_kernel, final_norm=final_g is not None),
        grid=(bsz, t // tm),
        in_specs=in_specs,
        out_specs=pl.BlockSpec((1, tm, d), lambda b, i: (b, i, 0)),
        out_shape=jax.ShapeDtypeStruct((bsz, t, d), F32),
        compiler_params=pltpu.CompilerParams(
            dimension_semantics=("parallel", "parallel"), vmem_limit_bytes=VMEM_LIMIT),
        name="outproj_residual",
    )(*args)


def _gdn_gate_scalars(ba, alog, dtb, col_ref, row_ref):
    cs, nh = GDN_CHUNK, GDN_V_HEADS
    lane = lax.broadcasted_iota(jnp.int32, ba.shape, 1)
    g = -jnp.exp(alog) * jax.nn.softplus(ba + dtb)
    vals = jnp.where(lane < nh, jax.nn.sigmoid(ba), g)
    r = lax.broadcasted_iota(jnp.int32, (cs, cs), 0)
    c = lax.broadcasted_iota(jnp.int32, (cs, cs), 1)
    tri = (r >= c).astype(F32)
    is_beta = lax.broadcasted_iota(jnp.int32, (cs, LANES), 1) < nh
    pr = lax.broadcasted_iota(jnp.int32, (nh // 2, LANES), 0)
    pc = lax.broadcasted_iota(jnp.int32, (nh // 2, LANES), 1)
    pick_even = (pc == nh + 2 * pr).astype(F32)
    for n in range(ba.shape[0] // cs):
        v = vals[n * cs:(n + 1) * cs]
        cum = _dot(tri, v, precision=HIGHEST)
        out = jnp.where(is_beta, v, cum)
        col_ref[0, n * cs:(n + 1) * cs, :] = out
        both = jnp.concatenate([out, pltpu.roll(out, LANES - 1, axis=1)], axis=0)
        row_ref[0, n] = _dot_nt(pick_even, both, precision=HIGHEST)


TRI_BASE = 8
PACK = 4


def _block_rows(p):
    n = p.shape[0]
    blk = lax.broadcasted_iota(jnp.int32, p.shape, 1) // n
    return jnp.concatenate([jnp.where(blk == j, p, 0.0) for j in range(PACK)], axis=0).astype(BF16)


def _tri_inverse_packed(mats):
    n = mats[0].shape[0]
    r = lax.broadcasted_iota(jnp.int32, mats[0].shape, 0)
    c = lax.broadcasted_iota(jnp.int32, mats[0].shape, 1) % n
    same = lambda s: (r // s) == (c // s)
    mm = lambda a, b: _dot(a.astype(BF16), _block_rows(b))
    diag = [jnp.where(same(TRI_BASE), a, 0.0) for a in mats]
    inv = [jnp.where(r == c, 1.0, 0.0) - d for d in diag]
    pw = diag
    k = 1
    while 2 * k < TRI_BASE:
        pw = [mm(m, m) for m in pw]
        inv = [p + mm(p, m) for p, m in zip(inv, pw)]
        k *= 2
        yield None
    s = TRI_BASE
    while s < n:
        sub = same(2 * s) & jnp.logical_not(same(s))
        left = [mm(p, jnp.where(sub, a, 0.0)) for p, a in zip(inv, mats)]
        inv = [p - mm(l, p) for p, l in zip(inv, left)]
        s *= 2
        yield None
    yield inv


def _gdn_chunk_kernel(q_ref, k_ref, v_ref, z_ref, col_ref, row_ref, nw_ref,
                      o_ref, s_ref, hu_ref, hwq_ref, hqk_ref, hkd_ref, hgl_ref):
    cs, dh, nh = GDN_CHUNK, GDN_HEAD_DIM, GDN_V_HEADS
    tt = q_ref.shape[1]
    hps = q_ref.shape[2] // dh
    ncb = tt // cs
    assert ncb % 2 == 0 and PACK == 4
    hg = pl.program_id(1)
    ti = pl.program_id(2)
    wslot = ti % 2
    rslot = 1 - wslot
    jidx = lambda hl, n, e: (hl * ncb + n) * 2 + e

    @pl.when(ti == 0)
    def _():
        s_ref[...] = jnp.zeros_like(s_ref)
        for h_ref in (hu_ref, hwq_ref, hqk_ref, hkd_ref, hgl_ref):
            h_ref[1] = jnp.zeros(h_ref.shape[1:], h_ref.dtype)

    chunks = [slice(n * cs, (n + 1) * cs) for n in range(ncb)]

    def prepare():
        q = [q_ref[0, :, hl * dh:(hl + 1) * dh] for hl in range(hps)]
        k = [k_ref[0, :, hl * dh:(hl + 1) * dh] for hl in range(hps)]
        v = [[v_ref[0, :, (2 * hl + e) * dh:(2 * hl + e + 1) * dh] for e in range(2)] for hl in range(hps)]
        lane = lax.broadcasted_iota(jnp.int32, (tt, LANES), 1)
        colv = col_ref[0]
        column = lambda idx: jnp.sum(jnp.where(lane == idx, colv, 0.0), axis=1, keepdims=True)
        hq = [hg * hps + hl for hl in range(hps)]
        beta = [[column(2 * h + e) for e in range(2)] for h in hq]
        gc = [[column(nh + 2 * h + e) for e in range(2)] for h in hq]
        egc = [[jnp.exp(x) for x in pair] for pair in gc]
        yield

        pr = lax.broadcasted_iota(jnp.int32, (cs, PACK * cs), 0)
        pb = lax.broadcasted_iota(jnp.int32, (cs, PACK * cs), 1) // cs
        pc = lax.broadcasted_iota(jnp.int32, (cs, PACK * cs), 1) % cs
        groups = [(hl, gi) for hl in range(hps) for gi in range(ncb // 2)]
        a_mats, g_rows = [], {}
        for hl, gi in groups:
            k16, q16 = k[hl].astype(BF16), q[hl].astype(BF16)
            pair = (chunks[2 * gi], chunks[2 * gi + 1])
            kdup = [jnp.concatenate([k16[sl], k16[sl]], axis=0) for sl in pair]
            kk = jnp.concatenate([_dot_nt(k16[sl], kd) for sl, kd in zip(pair, kdup)], axis=1)
            qk = jnp.concatenate([_dot_nt(q16[sl], kd) for sl, kd in zip(pair, kdup)], axis=1)
            pick = lambda cols: jnp.where(pb == 0, cols[0][pair[0]], jnp.where(
                pb == 1, cols[1][pair[0]], jnp.where(pb == 2, cols[0][pair[1]], cols[1][pair[1]])))
            g_row = jnp.concatenate([row_ref[0, 2 * gi + j, pl.ds(hq[hl], 1), :] for j in range(2)], axis=1)
            decay = jnp.exp(jnp.where(pr >= pc, pick(gc[hl]) - g_row, -jnp.inf))
            a_mats.append(jnp.where(pr > pc, pick(beta[hl]) * kk * decay, 0.0))
            hqk_ref[wslot, hl * (ncb // 2) + gi] = (qk * decay).astype(BF16)
            g_rows[hl, gi] = g_row
            yield
        t_mats = None
        for t_mats in _tri_inverse_packed(a_mats):
            yield

        for (hl, gi), t_mat in zip(groups, t_mats):
            rhs = []
            for n in (2 * gi, 2 * gi + 1):
                sl = chunks[n]
                for e in range(2):
                    j = 2 * (n % 2) + e
                    g_last = g_rows[hl, gi][:, j * cs + cs - 1:(j + 1) * cs]
                    hgl_ref[wslot, jidx(hl, n, e)] = jnp.broadcast_to(jnp.exp(g_last), (1, LANES))
                    hkd_ref[wslot, jidx(hl, n, e)] = (k[hl][sl] * jnp.exp(g_last - gc[hl][e][sl])).T.astype(BF16)
                    kbeta = k[hl][sl] * beta[hl][e][sl]
                    rhs.append(jnp.concatenate([v[hl][e][sl] * beta[hl][e][sl], kbeta * egc[hl][e][sl]], axis=1))
            out = _dot(_block_rows(t_mat), jnp.concatenate(rhs, axis=0).astype(BF16))
            for n in (2 * gi, 2 * gi + 1):
                for e in range(2):
                    uw = out[(2 * (n % 2) + e) * cs:(2 * (n % 2) + e + 1) * cs]
                    hu_ref[wslot, jidx(hl, n, e)] = uw[:, :dh]
                    q_dec = q[hl][chunks[n]] * egc[hl][e][chunks[n]]
                    hwq_ref[wslot, jidx(hl, n, e)] = jnp.concatenate([uw[:, dh:], q_dec], axis=0).astype(BF16)
            yield

    def recurrence():
        nw = nw_ref[...]
        heads = [(hl, e) for hl in range(hps) for e in range(2)]
        state = {he: s_ref[2 * he[0] + he[1]] for he in heads}
        for n, sl in enumerate(chunks):
            ws = {}
            for hl, e in heads:
                ws[hl, e] = _dot(hwq_ref[rslot, jidx(hl, n, e)], state[hl, e].astype(BF16))
            yield
            for hl, e in heads:
                j = jidx(hl, n, e)
                v16 = (hu_ref[rslot, j] - ws[hl, e][:cs]).astype(BF16)
                lb = 2 * (n % 2) + e
                qkd = hqk_ref[rslot, hl * (ncb // 2) + n // 2][:, lb * cs:(lb + 1) * cs]
                o = ws[hl, e][cs:] + _dot(qkd, v16)
                state[hl, e] = state[hl, e] * hgl_ref[rslot, j] + _dot(hkd_ref[rslot, j], v16)
                o = o * lax.rsqrt(jnp.mean(o * o, axis=-1, keepdims=True) + EPS) * nw
                lanes = slice((2 * hl + e) * dh, (2 * hl + e + 1) * dh)
                o_ref[0, sl, lanes] = (o * _silu(z_ref[0, sl, lanes])).astype(o_ref.dtype)
            yield
        for (hl, e), st in state.items():
            s_ref[2 * hl + e] = st

    prep, rec = prepare(), recurrence()
    n_prep = 1 + hps * (ncb // 2) * 2 + 6
    per_stage = -(-n_prep // (2 * ncb))
    prep_live = rec_live = True
    while prep_live or rec_live:
        for _ in range(per_stage):
            if prep_live:
                prep_live = next(prep, "done") != "done"
        if rec_live:
            rec_live = next(rec, "done") != "done"


def _gdn_chunk(proj, col, row, norm_w, *, tt=GDN_TILE, hps=GDN_HEADS_PER_STEP):
    bsz, t, _ = proj.shape
    dh = GDN_HEAD_DIM
    qw, vw = hps * dh, 2 * hps * dh
    k_blk0 = GDN_QK_W // qw
    v_blk0 = 2 * GDN_QK_W // vw
    z_blk0 = GDN_CONV_W // vw
    ncb = tt // GDN_CHUNK
    nt = t // tt
    nj = hps * ncb * 2
    cur = lambda i: jnp.minimum(i, nt - 1)
    prev = lambda i: jnp.maximum(i - 1, 0)
    return pl.pallas_call(
        _gdn_chunk_kernel,
        grid=(bsz, GDN_QK_HEADS // hps, nt + 1),
        in_specs=[
            pl.BlockSpec((1, tt, qw), lambda b, h, i: (b, cur(i), h)),
            pl.BlockSpec((1, tt, qw), lambda b, h, i: (b, cur(i), k_blk0 + h)),
            pl.BlockSpec((1, tt, vw), lambda b, h, i: (b, cur(i), v_blk0 + h)),
            pl.BlockSpec((1, tt, vw), lambda b, h, i: (b, prev(i), z_blk0 + h)),
            pl.BlockSpec((1, tt, LANES), lambda b, h, i: (b, cur(i), 0)),
            pl.BlockSpec((1, ncb, GDN_V_HEADS // 2, 2 * GDN_CHUNK), lambda b, h, i: (b, cur(i), 0, 0)),
            pl.BlockSpec((1, dh), lambda b, h, i: (0, 0)),
        ],
        out_specs=pl.BlockSpec((1, tt, vw), lambda b, h, i: (b, prev(i), h)),
        out_shape=jax.ShapeDtypeStruct((bsz, t, GDN_V_W), BF16),
        scratch_shapes=[
            pltpu.VMEM((2 * hps, dh, dh), F32),
            pltpu.VMEM((2, nj, GDN_CHUNK, dh), F32),
            pltpu.VMEM((2, nj, 2 * GDN_CHUNK, dh), BF16),
            pltpu.VMEM((2, nj // PACK, GDN_CHUNK, PACK * GDN_CHUNK), BF16),
            pltpu.VMEM((2, nj, dh, GDN_CHUNK), BF16),
            pltpu.VMEM((2, nj, 1, LANES), F32),
        ],
        compiler_params=pltpu.CompilerParams(
            dimension_semantics=("parallel", "parallel", "arbitrary"), vmem_limit_bytes=VMEM_LIMIT),
        name="gdn_chunk_scan",
    )(proj, proj, proj, proj, col, row, norm_w.reshape(1, dh))


def _gdn_layer(x, mod, norm_g, w_in, conv_w, a_log, dt_bias, norm_w, w_out):
    n_main = GDN_CONV_W + GDN_V_W
    w_main = w_in[:, :n_main].astype(BF16)
    w_ba = jnp.pad(w_in[:, n_main:], ((0, 0), (0, LANES - (w_in.shape[1] - n_main)))).astype(BF16)
    proj, col, row = _gdn_inproj(x, norm_g, mod, w_main, w_ba, conv_w, a_log, dt_bias,
                                 tm=GDN_INPROJ_ROWS, tn=GDN_INPROJ_COLS)
    o = _gdn_chunk(proj, col, row, norm_w)
    return _outproj(o, w_out.astype(BF16), x, mod, tm=OUTPROJ_ROWS)


NSA_HEADS = 16
NSA_GROUPS = 4
NSA_HPG = NSA_HEADS // NSA_GROUPS
NSA_HEAD_DIM = 64
NSA_CMP_LEN = 32
NSA_CMP_STRIDE = 16
NSA_SLC_LEN = 64
NSA_TOP_K = 8
NSA_WINDOW = 512
NSA_QTILE = 256
NSA_Q_W = NSA_HEADS * NSA_HEAD_DIM
NSA_KV_W = NSA_GROUPS * NSA_HEAD_DIM
REL_BUCKETS = 32
REL_MAX_DIST = 128
FEAT_LANE0 = NSA_HEAD_DIM
CONST_LANE0 = FEAT_LANE0 + 32
VT_PAD = 16
NSA_COL_Q = 0
NSA_COL_CMP = NSA_Q_W
NSA_COL_SEL = NSA_COL_CMP + 2 * NSA_KV_W
NSA_COL_WIN = NSA_COL_SEL + 2 * NSA_KV_W
NSA_COL_Z = NSA_COL_WIN + 2 * NSA_KV_W
NSA_COL_GATE = NSA_COL_Z + NSA_Q_W
NSA_PROJ_W = NSA_COL_GATE + LANES


def _nsa_column_perm():
    g, dh = NSA_GROUPS, NSA_HEAD_DIM
    kv0 = NSA_Q_W
    cols = list(range(NSA_Q_W))
    cols += [kv0 + i for i in range(2 * NSA_KV_W)]
    for br in (1, 2):
        for gi in range(g):
            cols += [kv0 + (2 * br) * NSA_KV_W + gi * dh + d for d in range(dh)]
            cols += [kv0 + (2 * br + 1) * NSA_KV_W + gi * dh + d for d in range(dh)]
    gate0 = kv0 + 6 * NSA_KV_W
    cols += [gate0 + 3 * NSA_HEADS + i for i in range(NSA_Q_W)]
    cols += [gate0 + i for i in range(3 * NSA_HEADS)] + [-1] * (LANES - 3 * NSA_HEADS)
    assert len(cols) == NSA_PROJ_W
    return np.asarray(cols, np.int32)


def _rel_bucket_table(n):
    d = np.arange(n)
    max_exact = REL_BUCKETS // 2
    nf = np.maximum(d, 1).astype(np.float64)
    large = max_exact + (np.log(nf / max_exact) / math.log(REL_MAX_DIST / max_exact)
                         * (REL_BUCKETS - max_exact)).astype(np.int32)
    large = np.minimum(large, REL_BUCKETS - 1)
    return np.where(d < max_exact, d, large).astype(np.int32)


def _nsa_tables(rel_bias, t):
    qb = NSA_QTILE
    bucket = _rel_bucket_table(t)
    assert np.all(bucket[qb + 1:] == REL_BUCKETS - 1)
    bvec = rel_bias[bucket].T
    far = rel_bias[REL_BUCKETS - 1]
    far_hi = far.astype(BF16)
    far_lo = (far - far_hi.astype(F32)).astype(BF16)
    far_sum = far_hi.astype(F32) + far_lo.astype(F32)
    r = np.arange(qb)[:, None]
    c = np.arange(qb)[None, :]
    d0 = r - c

    def toeplitz(w):
        n = 2 * qb - 1
        ext = jnp.pad(w[:, ::-1], ((0, 0), (0, 1)))
        skew = jnp.tile(ext, (1, qb))[:, :qb * n].reshape(w.shape[0], qb, n)
        return skew[:, :, qb - 1:]

    rel = bvec[:, :2 * qb] - far_sum[:, None]
    t0 = toeplitz(jnp.concatenate([jnp.full((NSA_HEADS, qb - 1), NEG_INF, F32), rel[:, :qb]], axis=1)[:, ::-1])
    t1 = toeplitz(rel[:, 1:][:, ::-1])
    g, hpg = NSA_GROUPS, NSA_HPG
    none = jnp.full_like(t0, NEG_INF)
    near = jnp.stack([jnp.concatenate([t1, t0], axis=1), jnp.concatenate([t0, none], axis=1)], axis=1)
    near = near.reshape(g, hpg, 2, 2 * qb, qb)
    nb = t // NSA_CMP_STRIDE
    per_tile = qb // NSA_CMP_STRIDE
    back = 9
    far_d = back * NSA_CMP_STRIDE - (NSA_CMP_LEN - 1)
    assert np.all(bucket[far_d:] == REL_BUCKETS - 1)
    width = (qb - 1 + far_d) // NSA_CMP_STRIDE + 1
    rows = []
    for m in range(width):
        start = far_d - NSA_CMP_STRIDE * m
        lo = max(start, 0)
        rows.append(jnp.concatenate([jnp.full((NSA_HEADS, lo - start), NEG_INF, F32), bvec[:, lo:start + qb]], axis=1))
    band = jnp.stack(rows, axis=1)
    tiles = []
    for i in range(t // qb):
        j0 = per_tile * i - back
        lo, hi = max(j0, 0), min(j0 + width, nb)
        tiles.append(jnp.concatenate([
            jnp.broadcast_to(far[:, None, None], (NSA_HEADS, lo, qb)),
            band[:, lo - j0:hi - j0, :],
            jnp.full((NSA_HEADS, nb - hi, qb), NEG_INF, F32)], axis=1))
    cmp_bias = jnp.stack(tiles, axis=0).reshape(t // qb, g, hpg, nb, qb)
    qconst = jnp.zeros((g, 8, hpg, qb), F32)
    qconst = qconst.at[:, 0].set(jnp.broadcast_to(far_hi.astype(F32).reshape(g, hpg, 1), (g, hpg, qb)))
    qconst = qconst.at[:, 1].set(jnp.broadcast_to(far_lo.astype(F32).reshape(g, hpg, 1), (g, hpg, qb)))
    return near, cmp_bias, qconst.reshape(g, 8, hpg * qb)


def _overlap_t(t):
    n_cmp = (t - NSA_CMP_LEN) // NSA_CMP_STRIDE + 1
    n_slc = t // NSA_SLC_LEN
    c_start = np.arange(n_cmp)[:, None] * NSA_CMP_STRIDE
    s_start = np.arange(n_slc)[None, :] * NSA_SLC_LEN
    ov = np.clip(np.minimum(c_start + NSA_CMP_LEN, s_start + NSA_SLC_LEN) - np.maximum(c_start, s_start), 0, None)
    ov = ov.astype(np.float32) / NSA_CMP_LEN
    out = np.zeros((32, t // NSA_CMP_STRIDE), np.float32)
    out[:n_slc, :n_cmp] = ov.T
    return out


def _nsa_compress_kernel(x_ref, pos_ref, w1_ref, w2_ref, o_ref, xs_ref):
    t = x_ref.shape[1]
    nb = t // NSA_CMP_STRIDE
    nlt = xs_ref.shape[0]
    for c in range(nlt):
        xs_ref[c, 0:t, :] = x_ref[0, :, c * LANES:(c + 1) * LANES]
        xs_ref[c, t:t + NSA_CMP_STRIDE, :] = jnp.zeros((NSA_CMP_STRIDE, LANES), F32)
    acc = jnp.zeros((nb, w1_ref.shape[2]), F32)
    for l in range(NSA_CMP_LEN):
        xl = jnp.concatenate([xs_ref[c, pl.ds(l, nb, stride=NSA_CMP_STRIDE), :] for c in range(nlt)], axis=1)
        xl = xl + pos_ref[l:l + 1, :]
        acc = acc + _dot(xl.astype(BF16), w1_ref[l])
    hid = _silu(acc).astype(BF16)
    res = _dot(hid, w2_ref[...])
    for g in range(NSA_GROUPS):
        o_ref[0, g] = res[:, g * LANES:(g + 1) * LANES]


def _nsa_compress(proj, cmp_pos, cmp_w1, cmp_w2):
    bsz, t, _ = proj.shape
    g, dh = NSA_GROUPS, NSA_HEAD_DIM
    nb = t // NSA_CMP_STRIDE
    w = 2 * NSA_KV_W
    w1 = cmp_w1.reshape(2, NSA_CMP_LEN, dh, dh).astype(BF16)
    w2 = cmp_w2.astype(BF16)
    place = lambda blk, c0: jnp.pad(blk, [(0, 0)] * (blk.ndim - 1) + [(c0, w - dh - c0)])
    w1c = jnp.concatenate([place(w1[i], (gi * 2 + i) * dh) for i in range(2) for gi in range(g)], axis=1)
    w2c = jnp.concatenate([place(w2[i], (gi * 2 + i) * dh) for gi in range(g) for i in range(2)], axis=0)
    pos = jnp.broadcast_to(cmp_pos[:, :, None, :], (2, NSA_CMP_LEN, g, dh)).transpose(1, 0, 2, 3).reshape(NSA_CMP_LEN, w)
    return pl.pallas_call(
        _nsa_compress_kernel,
        grid=(bsz,),
        in_specs=[
            pl.BlockSpec((1, t, w), lambda b: (b, 0, NSA_COL_CMP // w)),
            pl.BlockSpec((NSA_CMP_LEN, w), lambda b: (0, 0)),
            pl.BlockSpec((NSA_CMP_LEN, w, w), lambda b: (0, 0, 0), pipeline_mode=pl.Buffered(1)),
            pl.BlockSpec((w, w), lambda b: (0, 0)),
        ],
        out_specs=pl.BlockSpec((1, g, nb, LANES), lambda b: (b, 0, 0, 0)),
        out_shape=jax.ShapeDtypeStruct((bsz, g, nb, LANES), F32),
        scratch_shapes=[pltpu.VMEM((w // LANES, t + NSA_CMP_STRIDE, LANES), F32)],
        compiler_params=pltpu.CompilerParams(dimension_semantics=("parallel",), vmem_limit_bytes=VMEM_LIMIT),
        name="nsa_compress",
    )(proj, pos, w1c, w2c)


def _nsa_attn_kernel(q_ref, kvs_ref, kvw_ref, kvc_ref, gate_ref, z_ref, cb_ref, near_ref, qc_ref, ovl_ref,
                     o_ref, ks_ref, vs_ref, kw_ref, vw_ref, gt_ref, ms_ref, accs_ref, mw_ref, accw_ref, sc_ref):
    qb, dh, hpg = NSA_QTILE, NSA_HEAD_DIM, NSA_HPG
    t = kvs_ref.shape[1]
    nblk = t // NSA_SLC_LEN
    cols = hpg * qb
    g = pl.program_id(1)
    i = pl.program_id(2)

    @pl.when(i == 0)
    def _():
        tok = lax.broadcasted_iota(jnp.int32, (t, LANES), 0)
        ln = lax.broadcasted_iota(jnp.int32, (t, LANES), 1)
        const = jnp.where((ln == CONST_LANE0) | (ln == CONST_LANE0 + 1), 1.0, 0.0)
        onehot = jnp.where(ln - FEAT_LANE0 == tok // NSA_SLC_LEN, 1.0, 0.0)
        ones_rows = jnp.where(lax.broadcasted_iota(jnp.int32, (VT_PAD, t), 0) == 0, 1.0, 0.0)
        kvs = kvs_ref[0]
        kvw = kvw_ref[0]
        ks_ref[...] = jnp.where(ln < dh, kvs, onehot + const).astype(BF16)
        kw_ref[...] = jnp.where(ln < dh, kvw, const).astype(BF16)
        vs_ref[...] = jnp.concatenate([ones_rows, kvs.T[dh:]], axis=0).astype(BF16)
        vw_ref[...] = jnp.concatenate([ones_rows, kvw.T[dh:]], axis=0).astype(BF16)

    q_t = (q_ref[0] * (dh ** -0.5)).T
    q_heads = jnp.concatenate([q_t[hh * dh:(hh + 1) * dh] for hh in range(hpg)], axis=1)

    def scores(branch, qa_t, start, nk, bias):
        sc = _dot(branch[0][pl.ds(pl.multiple_of(start, qb), nk), :], qa_t)
        return sc if bias is None else sc + bias

    def update(branch, start, nk, sc):
        _, vt_ref, m_ref, acc_ref = branch
        m_old = m_ref[...]
        m_new = jnp.maximum(m_old, jnp.max(sc, axis=0, keepdims=True))
        alpha = jnp.exp(m_old - m_new)
        pe = jnp.exp(sc - m_new).astype(BF16)
        acc_ref[...] = alpha * acc_ref[...] + _dot(vt_ref[:, pl.ds(pl.multiple_of(start, qb), nk)], pe)
        m_ref[...] = m_new

    sel = (ks_ref, vs_ref, ms_ref, accs_ref)
    win = (kw_ref, vw_ref, mw_ref, accw_ref)
    for m_ref, acc_ref in ((ms_ref, accs_ref), (mw_ref, accw_ref)):
        m_ref[...] = jnp.full(m_ref.shape, NEG_INF, F32)
        acc_ref[...] = jnp.zeros(acc_ref.shape, F32)

    nwt = NSA_WINDOW // qb
    assert nwt in (2, 4)
    pad_rows = jnp.zeros((LANES - CONST_LANE0 - 8, cols), F32)
    qa_win = jnp.concatenate([q_heads, jnp.zeros((32, cols), F32), qc_ref[0], pad_rows], axis=0).astype(BF16)

    kvc = kvc_ref[0, 0]
    lane_k = lax.broadcasted_iota(jnp.int32, kvc.shape, 1)
    kc16 = jnp.where(lane_k < dh, kvc, 0.0).astype(BF16)
    s = _dot(kc16, qa_win) + jnp.concatenate([cb_ref[0, 0, hh] for hh in range(hpg)], axis=1)

    kk = lax.broadcasted_iota(jnp.int32, (qb, cols), 0)
    rr = lax.broadcasted_iota(jnp.int32, (qb, cols), 1) % qb
    wold_start = jnp.maximum(i - nwt, 0) * qb
    sc_wold = scores(win, qa_win, wold_start, qb, jnp.where((rr < kk) & (i >= nwt), 0.0, NEG_INF))

    row2 = lax.broadcasted_iota(jnp.int32, (2 * qb, cols), 0)
    if nwt == 4:
        w32_start = jnp.maximum(i - 3, 0) * qb
        sc_w32 = scores(win, qa_win, w32_start, 2 * qb,
                        jnp.where(row2 < (i - 1) * qb - w32_start, 0.0, NEG_INF))
    near_start = jnp.maximum(i - 1, 0) * qb
    variant = jnp.where(i == 0, 1, 0)
    near_bias = jnp.concatenate([near_ref[0, hh, variant] for hh in range(hpg)], axis=1)
    sc_wn = scores(win, qa_win, near_start, 2 * qb, near_bias)

    s = jnp.exp(s - jnp.max(s, axis=0, keepdims=True))
    p = s / jnp.sum(s, axis=0, keepdims=True)
    tq_lane = i * qb + lax.broadcasted_iota(jnp.int32, (1, cols), 1) % qb
    p16 = (p * (tq_lane >= NSA_CMP_LEN - 1).astype(F32)).astype(BF16)
    o_cmp = _dot(kvc.T.astype(BF16), p16)
    ovl = ovl_ref[...].astype(BF16)
    imp = _dot(ovl, p16[:, 0:qb])
    for hh in range(1, hpg):
        imp = imp + _dot(ovl, p16[:, hh * qb:(hh + 1) * qb])

    update(win, wold_start, qb, sc_wold)

    blk = lax.broadcasted_iota(jnp.int32, (32, qb), 0)
    tq = i * qb + lax.broadcasted_iota(jnp.int32, (32, qb), 1)
    cur = tq // NSA_SLC_LEN
    forced = (blk == 0) | (blk == cur) | (blk == cur - 1)
    val = jnp.where(forced, jnp.inf, jnp.where(blk * NSA_SLC_LEN <= tq, imp, -jnp.inf))
    cnt = jnp.zeros((32, qb), jnp.int32)
    for s2 in range(nblk):
        other = val[s2:s2 + 1, :]
        cnt = cnt + ((other > val) | ((other == val) & (s2 < blk))).astype(jnp.int32)
    feat = jnp.where((cnt < min(NSA_TOP_K, nblk)) & (blk < nblk), 0.0, NEG_INF)
    qa = jnp.concatenate([q_heads, jnp.concatenate([feat] * hpg, axis=1), qc_ref[0], pad_rows],
                         axis=0).astype(BF16)

    if nwt == 4:
        update(win, w32_start, 2 * qb, sc_w32)
    sc_sn = scores(sel, qa, near_start, 2 * qb, near_bias)
    update(win, near_start, 2 * qb, sc_wn)
    update(sel, near_start, 2 * qb, sc_sn)

    n_far = jnp.maximum(i - 1, 0)
    n_pairs = (n_far + 1) // 2
    pair_start = lambda p: jnp.maximum(2 * p - n_far % 2, 0) * qb
    last_pair = jnp.maximum(n_pairs - 1, 0)
    sc_ref[...] = scores(sel, qa, 0, 2 * qb, jnp.where(row2 < (2 - n_far % 2) * qb, 0.0, NEG_INF))

    def sel_body(k, carry):
        sc_odd = scores(sel, qa, pair_start(2 * k + 1), 2 * qb, None)
        update(sel, pair_start(2 * k), 2 * qb, sc_ref[...])
        sc_ref[...] = scores(sel, qa, pair_start(jnp.minimum(2 * k + 2, last_pair)), 2 * qb, None)
        update(sel, pair_start(2 * k + 1), 2 * qb, sc_odd)
        return carry

    lax.fori_loop(0, n_pairs // 2, sel_body, 0)

    @pl.when(n_pairs % 2 == 1)
    def _():
        update(sel, pair_start(n_pairs - 1), 2 * qb, sc_ref[...])

    def finish(acc_ref):
        acc = acc_ref[...]
        return acc[VT_PAD:] / acc[0:1]

    o_slc = finish(accs_ref)
    o_win = finish(accw_ref)

    gt_ref[...] = jax.nn.sigmoid(gate_ref[0]).T
    outs = []
    for hh in range(hpg):
        sl = slice(hh * qb, (hh + 1) * qb)
        base = (g * hpg + hh) * 3
        gate = [gt_ref[pl.ds(base + br, 1), :] for br in range(3)]
        outs.append(gate[0] * o_cmp[dh:, sl] + gate[1] * o_slc[:, sl] + gate[2] * o_win[:, sl])
    out = jnp.concatenate(outs, axis=0).T
    o_ref[0] = (out * _silu(z_ref[0])).astype(o_ref.dtype)


def _nsa_attn(proj, kv_cmp, near, cmp_bias, qconst, ovl):
    bsz, t, _ = proj.shape
    qb, hpg = NSA_QTILE, NSA_HPG
    gw = hpg * NSA_HEAD_DIM
    nb = t // NSA_CMP_STRIDE
    cols = hpg * qb
    return pl.pallas_call(
        _nsa_attn_kernel,
        grid=(bsz, NSA_GROUPS, t // qb),
        in_specs=[
            pl.BlockSpec((1, qb, gw), lambda b, g, i: (b, i, NSA_COL_Q // gw + g)),
            pl.BlockSpec((1, t, LANES), lambda b, g, i: (b, 0, NSA_COL_SEL // LANES + g)),
            pl.BlockSpec((1, t, LANES), lambda b, g, i: (b, 0, NSA_COL_WIN // LANES + g)),
            pl.BlockSpec((1, 1, nb, LANES), lambda b, g, i: (b, g, 0, 0)),
            pl.BlockSpec((1, qb, LANES), lambda b, g, i: (b, i, NSA_COL_GATE // LANES)),
            pl.BlockSpec((1, qb, gw), lambda b, g, i: (b, i, NSA_COL_Z // gw + g)),
            pl.BlockSpec((1, 1, hpg, nb, qb), lambda b, g, i: (i, g, 0, 0, 0)),
            pl.BlockSpec((1, hpg, 2, 2 * qb, qb), lambda b, g, i: (g, 0, 0, 0, 0)),
            pl.BlockSpec((1, 8, cols), lambda b, g, i: (g, 0, 0)),
            pl.BlockSpec((32, nb), lambda b, g, i: (0, 0)),
        ],
        out_specs=pl.BlockSpec((1, qb, gw), lambda b, g, i: (b, i, g)),
        out_shape=jax.ShapeDtypeStruct((bsz, t, NSA_Q_W), BF16),
        scratch_shapes=[
            pltpu.VMEM((t, LANES), BF16), pltpu.VMEM((VT_PAD + NSA_HEAD_DIM, t), BF16),
            pltpu.VMEM((t, LANES), BF16), pltpu.VMEM((VT_PAD + NSA_HEAD_DIM, t), BF16),
            pltpu.VMEM((LANES, qb), F32),
            pltpu.VMEM((1, cols), F32), pltpu.VMEM((VT_PAD + NSA_HEAD_DIM, cols), F32),
            pltpu.VMEM((1, cols), F32), pltpu.VMEM((VT_PAD + NSA_HEAD_DIM, cols), F32),
            pltpu.VMEM((2 * qb, cols), F32),
        ],
        compiler_params=pltpu.CompilerParams(
            dimension_semantics=("parallel", "parallel", "arbitrary"), vmem_limit_bytes=VMEM_LIMIT),
        name="nsa_attention",
    )(proj, proj, proj, kv_cmp, proj, proj, cmp_bias, near, qconst, ovl)


def _nsa_layer(x, mod, norm_g, w_in, cmp_pos, cmp_w1, cmp_w2, rel_bias, w_out, final_g):
    t = x.shape[1]
    assert t // NSA_SLC_LEN <= 32 and NSA_COL_Z % (NSA_HPG * NSA_HEAD_DIM) == 0
    perm = _nsa_column_perm()
    cuts = [0] + [j for j in range(1, len(perm)) if perm[j] != perm[j - 1] + (perm[j - 1] >= 0)] + [len(perm)]
    runs = [(int(perm[a]), b - a) for a, b in zip(cuts[:-1], cuts[1:])]
    w16 = w_in.astype(BF16)
    w_in_p = jnp.concatenate([w16[:, s:s + n] if s >= 0 else jnp.zeros((w_in.shape[0], n), BF16)
                              for s, n in runs], axis=1)
    proj = _inproj(x, norm_g, mod, w_in_p, tm=NSA_INPROJ_ROWS, tn=NSA_PROJ_W)
    kv_cmp = _nsa_compress(proj, cmp_pos, cmp_w1, cmp_w2)
    near, cmp_bias, qconst = _nsa_tables(rel_bias, t)
    o = _nsa_attn(proj, kv_cmp, near, cmp_bias, qconst, jnp.asarray(_overlap_t(t)))
    return _outproj(o, w_out.astype(BF16), x, mod, final_g, tm=OUTPROJ_ROWS)


def kernel(x, c, ada_w, ada_b, norm_g, gdn_w_in, gdn_conv_w, gdn_a_log, gdn_dt_bias, gdn_norm_w, gdn_w_out,
           nsa_w_in, nsa_cmp_pos, nsa_cmp_w1, nsa_cmp_w2, nsa_w_out, rel_bias, final_g):
    bsz, t, d = x.shape
    mod = _modulation(c, ada_w, ada_b).reshape(ada_w.shape[0], bsz, 3, d)
    x = _gdn_layer(x, mod[0], norm_g[0], gdn_w_in[0], gdn_conv_w[0], gdn_a_log[0], gdn_dt_bias[0],
                   gdn_norm_w[0], gdn_w_out[0])
    return _nsa_layer(x, mod[1], norm_g[1], nsa_w_in[0], nsa_cmp_pos[0], nsa_cmp_w1[0], nsa_cmp_w2[0],
                      rel_bias, nsa_w_out[0], final_g)
```

```python
import functools
import math

import numpy as np
import jax
import jax.numpy as jnp
from jax import lax
from jax.experimental import pallas as pl
from jax.experimental.pallas import tpu as pltpu

F32 = jnp.float32
BF16 = jnp.bfloat16
HIGHEST = lax.Precision.HIGHEST

EPS = 1e-6
NEG_INF = -1e30
LANES = 128
VMEM_LIMIT = 56 * 1024 * 1024

GDN_QK_HEADS = 8
GDN_V_HEADS = 16
GDN_HEAD_DIM = 128
GDN_CONV = 4
GDN_CHUNK = 64
GDN_QK_W = GDN_QK_HEADS * GDN_HEAD_DIM
GDN_V_W = GDN_V_HEADS * GDN_HEAD_DIM
GDN_CONV_W = 2 * GDN_QK_W + GDN_V_W
GDN_TILE = 128
GDN_HEADS_PER_STEP = 8
GDN_INPROJ_ROWS = 256
GDN_INPROJ_COLS = 4 * LANES
NSA_INPROJ_ROWS = 512
OUTPROJ_ROWS = 1024


def _silu(x):
    return x * jax.nn.sigmoid(x)


def _dot(a, b, **kw):
    return jnp.dot(a, b, preferred_element_type=F32, **kw)


def _dot_nt(a, b, **kw):
    return lax.dot_general(a, b, (((1,), (1,)), ((), ())), preferred_element_type=F32, **kw)


def _mod_kernel(c_ref, w_ref, b_ref, o_ref):
    cond = _silu(c_ref[...])
    o_ref[0] = _dot(cond, w_ref[0], precision=HIGHEST) + b_ref[0]


def _modulation(c, ada_w, ada_b):
    depth, d, d3 = ada_w.shape
    bsz = c.shape[0]
    return pl.pallas_call(
        _mod_kernel,
        grid=(depth, d3 // d),
        in_specs=[
            pl.BlockSpec((bsz, d), lambda i, j: (0, 0)),
            pl.BlockSpec((1, d, d), lambda i, j: (i, 0, j)),
            pl.BlockSpec((1, 1, d), lambda i, j: (i, 0, j)),
        ],
        out_specs=pl.BlockSpec((1, bsz, d), lambda i, j: (i, 0, j)),
        out_shape=jax.ShapeDtypeStruct((depth, bsz, d3), F32),
        name="adaln_mod",
    )(c, ada_w, ada_b.reshape(depth, 1, d3))


def _inproj_kernel(x_ref, g_ref, mod_ref, w_ref, o_ref, *, tn):
    x = x_ref[0]
    m = mod_ref[0]
    y = x * lax.rsqrt(jnp.mean(x * x, axis=-1, keepdims=True) + EPS) * g_ref[...]
    h = (y * (1.0 + m[1:2]) + m[0:1]).astype(BF16)
    for j in range(w_ref.shape[1] // tn):
        o_ref[0, :, j * tn:(j + 1) * tn] = _dot(h, w_ref[:, j * tn:(j + 1) * tn])


def _inproj(x, g, mod, w, *, tm, tn):
    bsz, t, d = x.shape
    n = w.shape[1]
    assert t % tm == 0 and n % tn == 0
    return pl.pallas_call(
        functools.partial(_inproj_kernel, tn=tn),
        grid=(bsz, t // tm),
        in_specs=[
            pl.BlockSpec((1, tm, d), lambda b, i: (b, i, 0)),
            pl.BlockSpec((1, d), lambda b, i: (0, 0)),
            pl.BlockSpec((1, 3, d), lambda b, i: (b, 0, 0)),
            pl.BlockSpec((d, n), lambda b, i: (0, 0), pipeline_mode=pl.Buffered(1)),
        ],
        out_specs=pl.BlockSpec((1, tm, n), lambda b, i: (b, i, 0)),
        out_shape=jax.ShapeDtypeStruct((bsz, t, n), F32),
        compiler_params=pltpu.CompilerParams(
            dimension_semantics=("parallel", "parallel"), vmem_limit_bytes=VMEM_LIMIT),
        name="norm_mod_inproj",
    )(x, g.reshape(1, d), mod, w)


def _gdn_inproj_kernel(x_ref, g_ref, mod_ref, w_ref, cw_ref, alog_ref, dtb_ref, o_ref, col_ref, row_ref,
                        ext_ref, tail_ref, *, tn):
    dh = GDN_HEAD_DIM
    tm = x_ref.shape[1]
    x = x_ref[0]
    m = mod_ref[0]
    y = x * lax.rsqrt(jnp.mean(x * x, axis=-1, keepdims=True) + EPS) * g_ref[...]
    h = (y * (1.0 + m[1:2]) + m[0:1]).astype(BF16)

    @pl.when(pl.program_id(1) == 0)
    def _():
        tail_ref[...] = jnp.zeros_like(tail_ref)

    def conv_tile(j):
        cols = slice(j * tn, (j + 1) * tn)
        ext_ref[0:8, :] = tail_ref[:, cols]
        ext_ref[8:8 + tm, :] = o_ref[0, :, cols]
        tail_ref[:, cols] = ext_ref[tm:tm + 8, :]
        w = cw_ref[:, cols]
        ext = ext_ref[...]
        acc = w[0:1] * ext
        for tap in range(1, GDN_CONV):
            acc = pltpu.roll(acc, 1, axis=0) + w[tap:tap + 1] * ext
        act = _silu(acc[8:8 + tm])
        if j * tn < 2 * GDN_QK_W:
            scale = dh ** -0.5 if j * tn < GDN_QK_W else 1.0
            heads = [act[:, c * dh:(c + 1) * dh] for c in range(tn // dh)]
            heads = [hd * (lax.rsqrt(jnp.sum(hd * hd, axis=-1, keepdims=True) + EPS) * scale) for hd in heads]
            act = jnp.concatenate(heads, axis=1)
        o_ref[0, :, cols] = act

    n = w_ref.shape[1]
    n_conv = GDN_CONV_W // tn
    bounds = [(j * tn, min((j + 1) * tn, n)) for j in range(-(-n // tn))]
    for j, (lo, hi) in enumerate(bounds):
        o_ref[0, :, lo:hi] = _dot(h, w_ref[:, lo:hi])
        if 1 <= j <= n_conv:
            conv_tile(j - 1)
    _gdn_gate_scalars(o_ref[0, :, n - LANES:n], alog_ref[...], dtb_ref[...], col_ref, row_ref)


def _gdn_inproj(x, g, mod, w, conv_w, a_log, dt_bias, *, tm, tn):
    bsz, t, d = x.shape
    n = w.shape[1]
    assert t % tm == 0 and GDN_CONV_W % tn == 0 and GDN_QK_W % tn == 0 and n > GDN_CONV_W
    pad = lambda u: jnp.zeros((1, LANES), F32).at[0, GDN_V_HEADS:2 * GDN_V_HEADS].set(u)
    ncb = tm // GDN_CHUNK
    return pl.pallas_call(
        functools.partial(_gdn_inproj_kernel, tn=tn),
        grid=(bsz, t // tm),
        in_specs=[
            pl.BlockSpec((1, tm, d), lambda b, i: (b, i, 0)),
            pl.BlockSpec((1, d), lambda b, i: (0, 0)),
            pl.BlockSpec((1, 3, d), lambda b, i: (b, 0, 0)),
            pl.BlockSpec((d, n), lambda b, i: (0, 0), pipeline_mode=pl.Buffered(1)),
            pl.BlockSpec((GDN_CONV, GDN_CONV_W), lambda b, i: (0, 0)),
            pl.BlockSpec((1, LANES), lambda b, i: (0, 0)),
            pl.BlockSpec((1, LANES), lambda b, i: (0, 0)),
        ],
        out_specs=[
            pl.BlockSpec((1, tm, n), lambda b, i: (b, i, 0)),
            pl.BlockSpec((1, tm, LANES), lambda b, i: (b, i, 0)),
            pl.BlockSpec((1, ncb, GDN_V_HEADS // 2, 2 * GDN_CHUNK), lambda b, i: (b, i, 0, 0)),
        ],
        out_shape=[
            jax.ShapeDtypeStruct((bsz, t, n), F32),
            jax.ShapeDtypeStruct((bsz, t, LANES), F32),
            jax.ShapeDtypeStruct((bsz, t // GDN_CHUNK, GDN_V_HEADS // 2, 2 * GDN_CHUNK), F32),
        ],
        scratch_shapes=[pltpu.VMEM((tm + 8, tn), F32), pltpu.VMEM((8, GDN_CONV_W), F32)],
        compiler_params=pltpu.CompilerParams(
            dimension_semantics=("parallel", "arbitrary"), vmem_limit_bytes=VMEM_LIMIT),
        name="gdn_norm_inproj_conv",
    )(x, g.reshape(1, d), mod, w, conv_w, pad(a_log), pad(dt_bias))


def _outproj_kernel(o_ref, w_ref, x_ref, mod_ref, *rest, final_norm):
    y = _dot(o_ref[0], w_ref[...])
    x = x_ref[0] + mod_ref[0][2:3] * y
    if final_norm:
        fg_ref, out_ref = rest
        x = x * lax.rsqrt(jnp.mean(x * x, axis=-1, keepdims=True) + EPS) * fg_ref[...]
    else:
        (out_ref,) = rest
    out_ref[0] = x


def _outproj(o, w, x, mod, final_g=None, *, tm):
    bsz, t, d = x.shape
    k = o.shape[-1]
    in_specs = [
        pl.BlockSpec((1, tm, k), lambda b, i: (b, i, 0)),
        pl.BlockSpec((k, d), lambda b, i: (0, 0), pipeline_mode=pl.Buffered(1)),
        pl.BlockSpec((1, tm, d), lambda b, i: (b, i, 0)),
        pl.BlockSpec((1, 3, d), lambda b, i: (b, 0, 0)),
    ]
    args = [o, w, x, mod]
    if final_g is not None:
        in_specs.append(pl.BlockSpec((1, d), lambda b, i: (0, 0)))
        args.append(final_g.reshape(1, d))
    return pl.pallas_call(
        functools.partial(_outproj_kernel, final_norm=final_g is not None),
        grid=(bsz, t // tm),
        in_specs=in_specs,
        out_specs=pl.BlockSpec((1, tm, d), lambda b, i: (b, i, 0)),
        out_shape=jax.ShapeDtypeStruct((bsz, t, d), F32),
        compiler_params=pltpu.CompilerParams(
            dimension_semantics=("parallel", "parallel"), vmem_limit_bytes=VMEM_LIMIT),
        name="outproj_residual",
    )(*args)


def _gdn_gate_scalars(ba, alog, dtb, col_ref, row_ref):
    cs, nh = GDN_CHUNK, GDN_V_HEADS
    lane = lax.broadcasted_iota(jnp.int32, ba.shape, 1)
    g = -jnp.exp(alog) * jax.nn.softplus(ba + dtb)
    vals = jnp.where(lane < nh, jax.nn.sigmoid(ba), g)
    r = lax.broadcasted_iota(jnp.int32, (cs, cs), 0)
    c = lax.broadcasted_iota(jnp.int32, (cs, cs), 1)
    tri = (r >= c).astype(F32)
    is_beta = lax.broadcasted_iota(jnp.int32, (cs, LANES), 1) < nh
    pr = lax.broadcasted_iota(jnp.int32, (nh // 2, LANES), 0)
    pc = lax.broadcasted_iota(jnp.int32, (nh // 2, LANES), 1)
    pick_even = (pc == nh + 2 * pr).astype(F32)
    for n in range(ba.shape[0] // cs):
        v = vals[n * cs:(n + 1) * cs]
        cum = _dot(tri, v, precision=HIGHEST)
        out = jnp.where(is_beta, v, cum)
        col_ref[0, n * cs:(n + 1) * cs, :] = out
        both = jnp.concatenate([out, pltpu.roll(out, LANES - 1, axis=1)], axis=0)
        row_ref[0, n] = _dot_nt(pick_even, both, precision=HIGHEST)


TRI_BASE = 8
PACK = 4


def _block_rows(p):
    n = p.shape[0]
    blk = lax.broadcasted_iota(jnp.int32, p.shape, 1) // n
    return jnp.concatenate([jnp.where(blk == j, p, 0.0) for j in range(PACK)], axis=0).astype(BF16)


def _tri_inverse_packed(mats):
    n = mats[0].shape[0]
    r = lax.broadcasted_iota(jnp.int32, mats[0].shape, 0)
    c = lax.broadcasted_iota(jnp.int32, mats[0].shape, 1) % n
    same = lambda s: (r // s) == (c // s)
    mm = lambda a, b: _dot(a.astype(BF16), _block_rows(b))
    diag = [jnp.where(same(TRI_BASE), a, 0.0) for a in mats]
    inv = [jnp.where(r == c, 1.0, 0.0) - d for d in diag]
    pw = diag
    k = 1
    while 2 * k < TRI_BASE:
        pw = [mm(m, m) for m in pw]
        inv = [p + mm(p, m) for p, m in zip(inv, pw)]
        k *= 2
        yield None
    s = TRI_BASE
    while s < n:
        sub = same(2 * s) & jnp.logical_not(same(s))
        left = [mm(p, jnp.where(sub, a, 0.0)) for p, a in zip(inv, mats)]
        inv = [p - mm(l, p) for p, l in zip(inv, left)]
        s *= 2
        yield None
    yield inv


def _gdn_chunk_kernel(q_ref, k_ref, v_ref, z_ref, col_ref, row_ref, nw_ref,
                      o_ref, s_ref, hu_ref, hwq_ref, hqk_ref, hkd_ref, hgl_ref):
    cs, dh, nh = GDN_CHUNK, GDN_HEAD_DIM, GDN_V_HEADS
    tt = q_ref.shape[1]
    hps = q_ref.shape[2] // dh
    ncb = tt // cs
    assert ncb % 2 == 0 and PACK == 4
    hg = pl.program_id(1)
    ti = pl.program_id(2)
    wslot = ti % 2
    rslot = 1 - wslot
    jidx = lambda hl, n, e: (hl * ncb + n) * 2 + e

    @pl.when(ti == 0)
    def _():
        s_ref[...] = jnp.zeros_like(s_ref)
        for h_ref in (hu_ref, hwq_ref, hqk_ref, hkd_ref, hgl_ref):
            h_ref[1] = jnp.zeros(h_ref.shape[1:], h_ref.dtype)

    chunks = [slice(n * cs, (n + 1) * cs) for n in range(ncb)]

    def prepare():
        q = [q_ref[0, :, hl * dh:(hl + 1) * dh] for hl in range(hps)]
        k = [k_ref[0, :, hl * dh:(hl + 1) * dh] for hl in range(hps)]
        v = [[v_ref[0, :, (2 * hl + e) * dh:(2 * hl + e + 1) * dh] for e in range(2)] for hl in range(hps)]
        lane = lax.broadcasted_iota(jnp.int32, (tt, LANES), 1)
        colv = col_ref[0]
        column = lambda idx: jnp.sum(jnp.where(lane == idx, colv, 0.0), axis=1, keepdims=True)
        hq = [hg * hps + hl for hl in range(hps)]
        beta = [[column(2 * h + e) for e in range(2)] for h in hq]
        gc = [[column(nh + 2 * h + e) for e in range(2)] for h in hq]
        egc = [[jnp.exp(x) for x in pair] for pair in gc]
        yield

        pr = lax.broadcasted_iota(jnp.int32, (cs, PACK * cs), 0)
        pb = lax.broadcasted_iota(jnp.int32, (cs, PACK * cs), 1) // cs
        pc = lax.broadcasted_iota(jnp.int32, (cs, PACK * cs), 1) % cs
        groups = [(hl, gi) for hl in range(hps) for gi in range(ncb // 2)]
        a_mats, g_rows = [], {}
        for hl, gi in groups:
            k16, q16 = k[hl].astype(BF16), q[hl].astype(BF16)
            pair = (chunks[2 * gi], chunks[2 * gi + 1])
            kdup = [jnp.concatenate([k16[sl], k16[sl]], axis=0) for sl in pair]
            kk = jnp.concatenate([_dot_nt(k16[sl], kd) for sl, kd in zip(pair, kdup)], axis=1)
            qk = jnp.concatenate([_dot_nt(q16[sl], kd) for sl, kd in zip(pair, kdup)], axis=1)
            pick = lambda cols: jnp.where(pb == 0, cols[0][pair[0]], jnp.where(
                pb == 1, cols[1][pair[0]], jnp.where(pb == 2, cols[0][pair[1]], cols[1][pair[1]])))
            g_row = jnp.concatenate([row_ref[0, 2 * gi + j, pl.ds(hq[hl], 1), :] for j in range(2)], axis=1)
            decay = jnp.exp(jnp.where(pr >= pc, pick(gc[hl]) - g_row, -jnp.inf))
            a_mats.append(jnp.where(pr > pc, pick(beta[hl]) * kk * decay, 0.0))
            hqk_ref[wslot, hl * (ncb // 2) + gi] = (qk * decay).astype(BF16)
            g_rows[hl, gi] = g_row
            yield
        t_mats = None
        for t_mats in _tri_inverse_packed(a_mats):
            yield

        for (hl, gi), t_mat in zip(groups, t_mats):
            rhs = []
            for n in (2 * gi, 2 * gi + 1):
                sl = chunks[n]
                for e in range(2):
                    j = 2 * (n % 2) + e
                    g_last = g_rows[hl, gi][:, j * cs + cs - 1:(j + 1) * cs]
                    hgl_ref[wslot, jidx(hl, n, e)] = jnp.broadcast_to(jnp.exp(g_last), (1, LANES))
                    hkd_ref[wslot, jidx(hl, n, e)] = (k[hl][sl] * jnp.exp(g_last - gc[hl][e][sl])).T.astype(BF16)
                    kbeta = k[hl][sl] * beta[hl][e][sl]
                    rhs.append(jnp.concatenate([v[hl][e][sl] * beta[hl][e][sl], kbeta * egc[hl][e][sl]], axis=1))
            out = _dot(_block_rows(t_mat), jnp.concatenate(rhs, axis=0).astype(BF16))
            for n in (2 * gi, 2 * gi + 1):
                for e in range(2):
                    uw = out[(2 * (n % 2) + e) * cs:(2 * (n % 2) + e + 1) * cs]
                    hu_ref[wslot, jidx(hl, n, e)] = uw[:, :dh]
                    q_dec = q[hl][chunks[n]] * egc[hl][e][chunks[n]]
                    hwq_ref[wslot, jidx(hl, n, e)] = jnp.concatenate([uw[:, dh:], q_dec], axis=0).astype(BF16)
            yield

    def recurrence():
        nw = nw_ref[...]
        heads = [(hl, e) for hl in range(hps) for e in range(2)]
        state = {he: s_ref[2 * he[0] + he[1]] for he in heads}
        for n, sl in enumerate(chunks):
            ws = {}
            for hl, e in heads:
                ws[hl, e] = _dot(hwq_ref[rslot, jidx(hl, n, e)], state[hl, e].astype(BF16))
            yield
            for hl, e in heads:
                j = jidx(hl, n, e)
                v16 = (hu_ref[rslot, j] - ws[hl, e][:cs]).astype(BF16)
                lb = 2 * (n % 2) + e
                qkd = hqk_ref[rslot, hl * (ncb // 2) + n // 2][:, lb * cs:(lb + 1) * cs]
                o = ws[hl, e][cs:] + _dot(qkd, v16)
                state[hl, e] = state[hl, e] * hgl_ref[rslot, j] + _dot(hkd_ref[rslot, j], v16)
                o = o * lax.rsqrt(jnp.mean(o * o, axis=-1, keepdims=True) + EPS) * nw
                lanes = slice((2 * hl + e) * dh, (2 * hl + e + 1) * dh)
                o_ref[0, sl, lanes] = (o * _silu(z_ref[0, sl, lanes])).astype(o_ref.dtype)
            yield
        for (hl, e), st in state.items():
            s_ref[2 * hl + e] = st

    prep, rec = prepare(), recurrence()
    per_stage = 2
    prep_live = rec_live = True
    while prep_live or rec_live:
        for _ in range(per_stage):
            if prep_live:
                prep_live = next(prep, "done") != "done"
        if rec_live:
            rec_live = next(rec, "done") != "done"


def _gdn_chunk(proj, col, row, norm_w, *, tt=GDN_TILE, hps=GDN_HEADS_PER_STEP):
    bsz, t, _ = proj.shape
    dh = GDN_HEAD_DIM
    qw, vw = hps * dh, 2 * hps * dh
    k_blk0 = GDN_QK_W // qw
    v_blk0 = 2 * GDN_QK_W // vw
    z_blk0 = GDN_CONV_W // vw
    ncb = tt // GDN_CHUNK
    nt = t // tt
    nj = hps * ncb * 2
    cur = lambda i: jnp.minimum(i, nt - 1)
    prev = lambda i: jnp.maximum(i - 1, 0)
    return pl.pallas_call(
        _gdn_chunk_kernel,
        grid=(bsz, GDN_QK_HEADS // hps, nt + 1),
        in_specs=[
            pl.BlockSpec((1, tt, qw), lambda b, h, i: (b, cur(i), h)),
            pl.BlockSpec((1, tt, qw), lambda b, h, i: (b, cur(i), k_blk0 + h)),
            pl.BlockSpec((1, tt, vw), lambda b, h, i: (b, cur(i), v_blk0 + h)),
            pl.BlockSpec((1, tt, vw), lambda b, h, i: (b, prev(i), z_blk0 + h)),
            pl.BlockSpec((1, tt, LANES), lambda b, h, i: (b, cur(i), 0)),
            pl.BlockSpec((1, ncb, GDN_V_HEADS // 2, 2 * GDN_CHUNK), lambda b, h, i: (b, cur(i), 0, 0)),
            pl.BlockSpec((1, dh), lambda b, h, i: (0, 0)),
        ],
        out_specs=pl.BlockSpec((1, tt, vw), lambda b, h, i: (b, prev(i), h)),
        out_shape=jax.ShapeDtypeStruct((bsz, t, GDN_V_W), BF16),
        scratch_shapes=[
            pltpu.VMEM((2 * hps, dh, dh), F32),
            pltpu.VMEM((2, nj, GDN_CHUNK, dh), F32),
            pltpu.VMEM((2, nj, 2 * GDN_CHUNK, dh), BF16),
            pltpu.VMEM((2, nj // PACK, GDN_CHUNK, PACK * GDN_CHUNK), BF16),
            pltpu.VMEM((2, nj, dh, GDN_CHUNK), BF16),
            pltpu.VMEM((2, nj, 1, LANES), F32),
        ],
        compiler_params=pltpu.CompilerParams(
            dimension_semantics=("parallel", "parallel", "arbitrary"), vmem_limit_bytes=VMEM_LIMIT),
        name="gdn_chunk_scan",
    )(proj, proj, proj, proj, col, row, norm_w.reshape(1, dh))


def _gdn_layer(x, mod, norm_g, w_in, conv_w, a_log, dt_bias, norm_w, w_out):
    n_in = w_in.shape[1]
    n_pad = -(-n_in // LANES) * LANES
    w_in_p = jnp.pad(w_in, ((0, 0), (0, n_pad - n_in))).astype(BF16)
    proj, col, row = _gdn_inproj(x, norm_g, mod, w_in_p, conv_w, a_log, dt_bias,
                                 tm=GDN_INPROJ_ROWS, tn=GDN_INPROJ_COLS)
    o = _gdn_chunk(proj, col, row, norm_w)
    return _outproj(o, w_out.astype(BF16), x, mod, tm=OUTPROJ_ROWS)


NSA_HEADS = 16
NSA_GROUPS = 4
NSA_HPG = NSA_HEADS // NSA_GROUPS
NSA_HEAD_DIM = 64
NSA_CMP_LEN = 32
NSA_CMP_STRIDE = 16
NSA_SLC_LEN = 64
NSA_TOP_K = 8
NSA_WINDOW = 512
NSA_QTILE = 256
NSA_Q_W = NSA_HEADS * NSA_HEAD_DIM
NSA_KV_W = NSA_GROUPS * NSA_HEAD_DIM
REL_BUCKETS = 32
REL_MAX_DIST = 128
FEAT_LANE0 = NSA_HEAD_DIM
CONST_LANE0 = FEAT_LANE0 + 32
VT_PAD = 16
NSA_COL_Q = 0
NSA_COL_CMP = NSA_Q_W
NSA_COL_SEL = NSA_COL_CMP + 2 * NSA_KV_W
NSA_COL_WIN = NSA_COL_SEL + 2 * NSA_KV_W
NSA_COL_Z = NSA_COL_WIN + 2 * NSA_KV_W
NSA_COL_GATE = NSA_COL_Z + NSA_Q_W
NSA_PROJ_W = NSA_COL_GATE + LANES


def _nsa_column_perm():
    g, dh = NSA_GROUPS, NSA_HEAD_DIM
    kv0 = NSA_Q_W
    cols = list(range(NSA_Q_W))
    cols += [kv0 + i for i in range(2 * NSA_KV_W)]
    for br in (1, 2):
        for gi in range(g):
            cols += [kv0 + (2 * br) * NSA_KV_W + gi * dh + d for d in range(dh)]
            cols += [kv0 + (2 * br + 1) * NSA_KV_W + gi * dh + d for d in range(dh)]
    gate0 = kv0 + 6 * NSA_KV_W
    cols += [gate0 + 3 * NSA_HEADS + i for i in range(NSA_Q_W)]
    cols += [gate0 + i for i in range(3 * NSA_HEADS)] + [-1] * (LANES - 3 * NSA_HEADS)
    assert len(cols) == NSA_PROJ_W
    return np.asarray(cols, np.int32)


def _rel_bucket_table(n):
    d = np.arange(n)
    max_exact = REL_BUCKETS // 2
    nf = np.maximum(d, 1).astype(np.float64)
    large = max_exact + (np.log(nf / max_exact) / math.log(REL_MAX_DIST / max_exact)
                         * (REL_BUCKETS - max_exact)).astype(np.int32)
    large = np.minimum(large, REL_BUCKETS - 1)
    return np.where(d < max_exact, d, large).astype(np.int32)


def _nsa_tables(rel_bias, t):
    qb = NSA_QTILE
    bucket = _rel_bucket_table(t)
    assert np.all(bucket[qb + 1:] == REL_BUCKETS - 1)
    bvec = rel_bias[bucket].T
    far = rel_bias[REL_BUCKETS - 1]
    far_hi = far.astype(BF16)
    far_lo = (far - far_hi.astype(F32)).astype(BF16)
    far_sum = far_hi.astype(F32) + far_lo.astype(F32)
    r = np.arange(qb)[:, None]
    c = np.arange(qb)[None, :]
    d0 = r - c

    def toeplitz(w):
        n = 2 * qb - 1
        ext = jnp.pad(w[:, ::-1], ((0, 0), (0, 1)))
        skew = jnp.tile(ext, (1, qb))[:, :qb * n].reshape(w.shape[0], qb, n)
        return skew[:, :, qb - 1:]

    rel = bvec[:, :2 * qb] - far_sum[:, None]
    t0 = toeplitz(jnp.concatenate([jnp.full((NSA_HEADS, qb - 1), NEG_INF, F32), rel[:, :qb]], axis=1)[:, ::-1])
    t1 = toeplitz(rel[:, 1:][:, ::-1])
    g, hpg = NSA_GROUPS, NSA_HPG
    none = jnp.full_like(t0, NEG_INF)
    near = jnp.stack([jnp.concatenate([t1, t0], axis=1), jnp.concatenate([t0, none], axis=1)], axis=1)
    near = near.reshape(g, hpg, 2, 2 * qb, qb)
    nb = t // NSA_CMP_STRIDE
    per_tile = qb // NSA_CMP_STRIDE
    back = 9
    far_d = back * NSA_CMP_STRIDE - (NSA_CMP_LEN - 1)
    assert np.all(bucket[far_d:] == REL_BUCKETS - 1)
    width = (qb - 1 + far_d) // NSA_CMP_STRIDE + 1
    dm = c - NSA_CMP_STRIDE * np.arange(width)[:, None] + far_d
    band = jnp.where(dm >= 0, bvec[:, np.maximum(dm, 0)], NEG_INF)
    tiles = []
    for i in range(t // qb):
        j0 = per_tile * i - back
        lo, hi = max(j0, 0), min(j0 + width, nb)
        tiles.append(jnp.concatenate([
            jnp.broadcast_to(far[:, None, None], (NSA_HEADS, lo, qb)),
            band[:, lo - j0:hi - j0, :],
            jnp.full((NSA_HEADS, nb - hi, qb), NEG_INF, F32)], axis=1))
    cmp_bias = jnp.stack(tiles, axis=0).reshape(t // qb, g, hpg, nb, qb)
    qconst = jnp.zeros((g, 8, hpg, qb), F32)
    qconst = qconst.at[:, 0].set(jnp.broadcast_to(far_hi.astype(F32).reshape(g, hpg, 1), (g, hpg, qb)))
    qconst = qconst.at[:, 1].set(jnp.broadcast_to(far_lo.astype(F32).reshape(g, hpg, 1), (g, hpg, qb)))
    return near, cmp_bias, qconst.reshape(g, 8, hpg * qb)


def _overlap_t(t):
    n_cmp = (t - NSA_CMP_LEN) // NSA_CMP_STRIDE + 1
    n_slc = t // NSA_SLC_LEN
    c_start = np.arange(n_cmp)[:, None] * NSA_CMP_STRIDE
    s_start = np.arange(n_slc)[None, :] * NSA_SLC_LEN
    ov = np.clip(np.minimum(c_start + NSA_CMP_LEN, s_start + NSA_SLC_LEN) - np.maximum(c_start, s_start), 0, None)
    ov = ov.astype(np.float32) / NSA_CMP_LEN
    out = np.zeros((32, t // NSA_CMP_STRIDE), np.float32)
    out[:n_slc, :n_cmp] = ov.T
    return out


def _nsa_compress_kernel(x_ref, pos_ref, w1_ref, w2_ref, o_ref, xs_ref):
    t = x_ref.shape[1]
    nb = t // NSA_CMP_STRIDE
    nlt = xs_ref.shape[0]
    for c in range(nlt):
        xs_ref[c, 0:t, :] = x_ref[0, :, c * LANES:(c + 1) * LANES]
        xs_ref[c, t:t + NSA_CMP_STRIDE, :] = jnp.zeros((NSA_CMP_STRIDE, LANES), F32)
    acc = jnp.zeros((nb, w1_ref.shape[2]), F32)
    for l in range(NSA_CMP_LEN):
        xl = jnp.concatenate([xs_ref[c, pl.ds(l, nb, stride=NSA_CMP_STRIDE), :] for c in range(nlt)], axis=1)
        xl = xl + pos_ref[l:l + 1, :]
        acc = acc + _dot(xl.astype(BF16), w1_ref[l])
    hid = _silu(acc).astype(BF16)
    res = _dot(hid, w2_ref[...])
    for g in range(NSA_GROUPS):
        o_ref[0, g] = res[:, g * LANES:(g + 1) * LANES]


def _nsa_compress(proj, cmp_pos, cmp_w1, cmp_w2):
    bsz, t, _ = proj.shape
    g, dh = NSA_GROUPS, NSA_HEAD_DIM
    nb = t // NSA_CMP_STRIDE
    w = 2 * NSA_KV_W
    w1 = cmp_w1.reshape(2, NSA_CMP_LEN, dh, dh).astype(BF16)
    w2 = cmp_w2.astype(BF16)
    place = lambda blk, c0: jnp.pad(blk, [(0, 0)] * (blk.ndim - 1) + [(c0, w - dh - c0)])
    w1c = jnp.concatenate([place(w1[i], (gi * 2 + i) * dh) for i in range(2) for gi in range(g)], axis=1)
    w2c = jnp.concatenate([place(w2[i], (gi * 2 + i) * dh) for gi in range(g) for i in range(2)], axis=0)
    pos = jnp.broadcast_to(cmp_pos[:, :, None, :], (2, NSA_CMP_LEN, g, dh)).transpose(1, 0, 2, 3).reshape(NSA_CMP_LEN, w)
    return pl.pallas_call(
        _nsa_compress_kernel,
        grid=(bsz,),
        in_specs=[
            pl.BlockSpec((1, t, w), lambda b: (b, 0, NSA_COL_CMP // w)),
            pl.BlockSpec((NSA_CMP_LEN, w), lambda b: (0, 0)),
            pl.BlockSpec((NSA_CMP_LEN, w, w), lambda b: (0, 0, 0), pipeline_mode=pl.Buffered(1)),
            pl.BlockSpec((w, w), lambda b: (0, 0)),
        ],
        out_specs=pl.BlockSpec((1, g, nb, LANES), lambda b: (b, 0, 0, 0)),
        out_shape=jax.ShapeDtypeStruct((bsz, g, nb, LANES), F32),
        scratch_shapes=[pltpu.VMEM((w // LANES, t + NSA_CMP_STRIDE, LANES), F32)],
        compiler_params=pltpu.CompilerParams(dimension_semantics=("parallel",), vmem_limit_bytes=VMEM_LIMIT),
        name="nsa_compress",
    )(proj, pos, w1c, w2c)


def _nsa_attn_kernel(q_ref, kvs_ref, kvw_ref, kvc_ref, gate_ref, z_ref, cb_ref, near_ref, qc_ref, ovl_ref,
                     o_ref, ks_ref, vs_ref, kw_ref, vw_ref, gt_ref, ms_ref, accs_ref, mw_ref, accw_ref, sc_ref):
    qb, dh, hpg = NSA_QTILE, NSA_HEAD_DIM, NSA_HPG
    t = kvs_ref.shape[1]
    nblk = t // NSA_SLC_LEN
    cols = hpg * qb
    g = pl.program_id(1)
    i = pl.program_id(2)

    @pl.when(i == 0)
    def _():
        tok = lax.broadcasted_iota(jnp.int32, (t, LANES), 0)
        ln = lax.broadcasted_iota(jnp.int32, (t, LANES), 1)
        const = jnp.where((ln == CONST_LANE0) | (ln == CONST_LANE0 + 1), 1.0, 0.0)
        onehot = jnp.where(ln - FEAT_LANE0 == tok // NSA_SLC_LEN, 1.0, 0.0)
        ones_rows = jnp.where(lax.broadcasted_iota(jnp.int32, (VT_PAD, t), 0) == 0, 1.0, 0.0)
        kvs = kvs_ref[0]
        kvw = kvw_ref[0]
        ks_ref[...] = jnp.where(ln < dh, kvs, onehot + const).astype(BF16)
        kw_ref[...] = jnp.where(ln < dh, kvw, const).astype(BF16)
        vs_ref[...] = jnp.concatenate([ones_rows, kvs.T[dh:]], axis=0).astype(BF16)
        vw_ref[...] = jnp.concatenate([ones_rows, kvw.T[dh:]], axis=0).astype(BF16)

    q_t = (q_ref[0] * (dh ** -0.5)).T
    q_heads = jnp.concatenate([q_t[hh * dh:(hh + 1) * dh] for hh in range(hpg)], axis=1)

    def scores(branch, qa_t, start, nk, bias):
        sc = _dot(branch[0][pl.ds(pl.multiple_of(start, qb), nk), :], qa_t)
        return sc if bias is None else sc + bias

    def update(branch, start, nk, sc):
        _, vt_ref, m_ref, acc_ref = branch
        m_old = m_ref[...]
        m_new = jnp.maximum(m_old, jnp.max(sc, axis=0, keepdims=True))
        alpha = jnp.exp(m_old - m_new)
        pe = jnp.exp(sc - m_new).astype(BF16)
        acc_ref[...] = alpha * acc_ref[...] + _dot(vt_ref[:, pl.ds(pl.multiple_of(start, qb), nk)], pe)
        m_ref[...] = m_new

    sel = (ks_ref, vs_ref, ms_ref, accs_ref)
    win = (kw_ref, vw_ref, mw_ref, accw_ref)
    for m_ref, acc_ref in ((ms_ref, accs_ref), (mw_ref, accw_ref)):
        m_ref[...] = jnp.full(m_ref.shape, NEG_INF, F32)
        acc_ref[...] = jnp.zeros(acc_ref.shape, F32)

    nwt = NSA_WINDOW // qb
    assert nwt in (2, 4)
    pad_rows = jnp.zeros((LANES - CONST_LANE0 - 8, cols), F32)
    qa_win = jnp.concatenate([q_heads, jnp.zeros((32, cols), F32), qc_ref[0], pad_rows], axis=0).astype(BF16)

    kvc = kvc_ref[0, 0]
    lane_k = lax.broadcasted_iota(jnp.int32, kvc.shape, 1)
    kc16 = jnp.where(lane_k < dh, kvc, 0.0).astype(BF16)
    s = _dot(kc16, qa_win) + jnp.concatenate([cb_ref[0, 0, hh] for hh in range(hpg)], axis=1)

    kk = lax.broadcasted_iota(jnp.int32, (qb, cols), 0)
    rr = lax.broadcasted_iota(jnp.int32, (qb, cols), 1) % qb
    wold_start = jnp.maximum(i - nwt, 0) * qb
    sc_wold = scores(win, qa_win, wold_start, qb, jnp.where((rr < kk) & (i >= nwt), 0.0, NEG_INF))

    row2 = lax.broadcasted_iota(jnp.int32, (2 * qb, cols), 0)
    if nwt == 4:
        w32_start = jnp.maximum(i - 3, 0) * qb
        sc_w32 = scores(win, qa_win, w32_start, 2 * qb,
                        jnp.where(row2 < (i - 1) * qb - w32_start, 0.0, NEG_INF))
    near_start = jnp.maximum(i - 1, 0) * qb
    variant = jnp.where(i == 0, 1, 0)
    near_bias = jnp.concatenate([near_ref[0, hh, variant] for hh in range(hpg)], axis=1)
    sc_wn = scores(win, qa_win, near_start, 2 * qb, near_bias)

    s = jnp.exp(s - jnp.max(s, axis=0, keepdims=True))
    p = s / jnp.sum(s, axis=0, keepdims=True)
    tq_lane = i * qb + lax.broadcasted_iota(jnp.int32, (1, cols), 1) % qb
    p16 = (p * (tq_lane >= NSA_CMP_LEN - 1).astype(F32)).astype(BF16)
    o_cmp = _dot(kvc.T.astype(BF16), p16)
    ovl = ovl_ref[...].astype(BF16)
    imp = _dot(ovl, p16[:, 0:qb])
    for hh in range(1, hpg):
        imp = imp + _dot(ovl, p16[:, hh * qb:(hh + 1) * qb])

    update(win, wold_start, qb, sc_wold)

    blk = lax.broadcasted_iota(jnp.int32, (32, qb), 0)
    tq = i * qb + lax.broadcasted_iota(jnp.int32, (32, qb), 1)
    cur = tq // NSA_SLC_LEN
    forced = (blk == 0) | (blk == cur) | (blk == cur - 1)
    val = jnp.where(forced, jnp.inf, jnp.where(blk * NSA_SLC_LEN <= tq, imp, -jnp.inf))
    cnt = jnp.zeros((32, qb), jnp.int32)
    for s2 in range(nblk):
        other = val[s2:s2 + 1, :]
        cnt = cnt + ((other > val) | ((other == val) & (s2 < blk))).astype(jnp.int32)
    feat = jnp.where((cnt < min(NSA_TOP_K, nblk)) & (blk < nblk), 0.0, NEG_INF)
    qa = jnp.concatenate([q_heads, jnp.concatenate([feat] * hpg, axis=1), qc_ref[0], pad_rows],
                         axis=0).astype(BF16)

    if nwt == 4:
        update(win, w32_start, 2 * qb, sc_w32)
    sc_sn = scores(sel, qa, near_start, 2 * qb, near_bias)
    update(win, near_start, 2 * qb, sc_wn)
    update(sel, near_start, 2 * qb, sc_sn)

    n_far = jnp.maximum(i - 1, 0)
    n_pairs = (n_far + 1) // 2
    pair_start = lambda p: jnp.maximum(2 * p - n_far % 2, 0) * qb
    last_pair = jnp.maximum(n_pairs - 1, 0)
    sc_ref[...] = scores(sel, qa, 0, 2 * qb, jnp.where(row2 < (2 - n_far % 2) * qb, 0.0, NEG_INF))

    def sel_body(k, carry):
        sc_odd = scores(sel, qa, pair_start(2 * k + 1), 2 * qb, None)
        update(sel, pair_start(2 * k), 2 * qb, sc_ref[...])
        sc_ref[...] = scores(sel, qa, pair_start(jnp.minimum(2 * k + 2, last_pair)), 2 * qb, None)
        update(sel, pair_start(2 * k + 1), 2 * qb, sc_odd)
        return carry

    lax.fori_loop(0, n_pairs // 2, sel_body, 0)

    @pl.when(n_pairs % 2 == 1)
    def _():
        update(sel, pair_start(n_pairs - 1), 2 * qb, sc_ref[...])

    def finish(acc_ref):
        acc = acc_ref[...]
        return acc[VT_PAD:] / acc[0:1]

    o_slc = finish(accs_ref)
    o_win = finish(accw_ref)

    gt_ref[...] = jax.nn.sigmoid(gate_ref[0]).T
    outs = []
    for hh in range(hpg):
        sl = slice(hh * qb, (hh + 1) * qb)
        base = (g * hpg + hh) * 3
        gate = [gt_ref[pl.ds(base + br, 1), :] for br in range(3)]
        outs.append(gate[0] * o_cmp[dh:, sl] + gate[1] * o_slc[:, sl] + gate[2] * o_win[:, sl])
    out = jnp.concatenate(outs, axis=0).T
    o_ref[0] = (out * _silu(z_ref[0])).astype(o_ref.dtype)


def _nsa_attn(proj, kv_cmp, near, cmp_bias, qconst, ovl):
    bsz, t, _ = proj.shape
    qb, hpg = NSA_QTILE, NSA_HPG
    gw = hpg * NSA_HEAD_DIM
    nb = t // NSA_CMP_STRIDE
    cols = hpg * qb
    return pl.pallas_call(
        _nsa_attn_kernel,
        grid=(bsz, NSA_GROUPS, t // qb),
        in_specs=[
            pl.BlockSpec((1, qb, gw), lambda b, g, i: (b, i, NSA_COL_Q // gw + g)),
            pl.BlockSpec((1, t, LANES), lambda b, g, i: (b, 0, NSA_COL_SEL // LANES + g)),
            pl.BlockSpec((1, t, LANES), lambda b, g, i: (b, 0, NSA_COL_WIN // LANES + g)),
            pl.BlockSpec((1, 1, nb, LANES), lambda b, g, i: (b, g, 0, 0)),
            pl.BlockSpec((1, qb, LANES), lambda b, g, i: (b, i, NSA_COL_GATE // LANES)),
            pl.BlockSpec((1, qb, gw), lambda b, g, i: (b, i, NSA_COL_Z // gw + g)),
            pl.BlockSpec((1, 1, hpg, nb, qb), lambda b, g, i: (i, g, 0, 0, 0)),
            pl.BlockSpec((1, hpg, 2, 2 * qb, qb), lambda b, g, i: (g, 0, 0, 0, 0)),
            pl.BlockSpec((1, 8, cols), lambda b, g, i: (g, 0, 0)),
            pl.BlockSpec((32, nb), lambda b, g, i: (0, 0)),
        ],
        out_specs=pl.BlockSpec((1, qb, gw), lambda b, g, i: (b, i, g)),
        out_shape=jax.ShapeDtypeStruct((bsz, t, NSA_Q_W), BF16),
        scratch_shapes=[
            pltpu.VMEM((t, LANES), BF16), pltpu.VMEM((VT_PAD + NSA_HEAD_DIM, t), BF16),
            pltpu.VMEM((t, LANES), BF16), pltpu.VMEM((VT_PAD + NSA_HEAD_DIM, t), BF16),
            pltpu.VMEM((LANES, qb), F32),
            pltpu.VMEM((1, cols), F32), pltpu.VMEM((VT_PAD + NSA_HEAD_DIM, cols), F32),
            pltpu.VMEM((1, cols), F32), pltpu.VMEM((VT_PAD + NSA_HEAD_DIM, cols), F32),
            pltpu.VMEM((2 * qb, cols), F32),
        ],
        compiler_params=pltpu.CompilerParams(
            dimension_semantics=("parallel", "parallel", "arbitrary"), vmem_limit_bytes=VMEM_LIMIT),
        name="nsa_attention",
    )(proj, proj, proj, kv_cmp, proj, proj, cmp_bias, near, qconst, ovl)


def _nsa_layer(x, mod, norm_g, w_in, cmp_pos, cmp_w1, cmp_w2, rel_bias, w_out, final_g):
    t = x.shape[1]
    assert t // NSA_SLC_LEN <= 32 and NSA_COL_Z % (NSA_HPG * NSA_HEAD_DIM) == 0
    perm = _nsa_column_perm()
    cuts = [0] + [j for j in range(1, len(perm)) if perm[j] != perm[j - 1] + (perm[j - 1] >= 0)] + [len(perm)]
    runs = [(int(perm[a]), b - a) for a, b in zip(cuts[:-1], cuts[1:])]
    w16 = w_in.astype(BF16)
    w_in_p = jnp.concatenate([w16[:, s:s + n] if s >= 0 else jnp.zeros((w_in.shape[0], n), BF16)
                              for s, n in runs], axis=1)
    proj = _inproj(x, norm_g, mod, w_in_p, tm=NSA_INPROJ_ROWS, tn=NSA_PROJ_W)
    kv_cmp = _nsa_compress(proj, cmp_pos, cmp_w1, cmp_w2)
    near, cmp_bias, qconst = _nsa_tables(rel_bias, t)
    o = _nsa_attn(proj, kv_cmp, near, cmp_bias, qconst, jnp.asarray(_overlap_t(t)))
    return _outproj(o, w_out.astype(BF16), x, mod, final_g, tm=OUTPROJ_ROWS)


def kernel(x, c, ada_w, ada_b, norm_g, gdn_w_in, gdn_conv_w, gdn_a_log, gdn_dt_bias, gdn_norm_w, gdn_w_out,
           nsa_w_in, nsa_cmp_pos, nsa_cmp_w1, nsa_cmp_w2, nsa_w_out, rel_bias, final_g):
    bsz, t, d = x.shape
    mod = _modulation(c, ada_w, ada_b).reshape(ada_w.shape[0], bsz, 3, d)
    x = _gdn_layer(x, mod[0], norm_g[0], gdn_w_in[0], gdn_conv_w[0], gdn_a_log[0], gdn_dt_bias[0],
                   gdn_norm_w[0], gdn_w_out[0])
    return _nsa_layer(x, mod[1], norm_g[1], nsa_w_in[0], nsa_cmp_pos[0], nsa_cmp_w1[0], nsa_cmp_w2[0],
                      rel_bias, nsa_w_out[0], final_g)
```
